```python
import math
import jax, jax.numpy as jnp
from jax import lax
import numpy as np

D_MODEL = 2048
BATCH = 4
SEQ = 2048
DEPTH = 1
DEC_BATCH = 128
DEC_SEQ = 1
PAST_LEN = 16384
PAGE_SIZE = 128

D_MIX = D_MODEL
D_SSM = D_MIX // 2
D_POOL = D_MIX - D_SSM
SSM_GROUP = 16
N_SSM_GROUPS = D_SSM // SSM_GROUP
SSM_STATE = 64
POOL_WINDOWS = (2, 4, 8, 16)
N_POOL_GROUPS = len(POOL_WINDOWS)
POOL_GROUP = D_POOL // N_POOL_GROUPS
POOL_HIST = max(POOL_WINDOWS) - 1
D_FF = 5632
FFN_RES = 0.5
N_SUBLAYERS = 3
N_MOD = 3
EPS = 1e-6
DT_MIN = 1e-3
DT_MAX = 1e-1

kernel_name = "s5_pool_hybrid_adaln_decode_step"


def rmsnorm(x, g):
    xf = x.astype(jnp.float32)
    y = xf * lax.rsqrt(jnp.mean(xf * xf, axis=-1, keepdims=True) + EPS)
    return (y * g.astype(jnp.float32)).astype(x.dtype)


def modulate(h, shift, scale):
    return h * (1 + scale) + shift


def swiglu(h, w_gate, w_up, w_down):
    return (jax.nn.silu(h @ w_gate) * (h @ w_up)) @ w_down


def s5_discretise(lam_re, lam_im, log_dt, b_re, b_im):
    lr = jnp.minimum(lam_re.astype(jnp.float32), -1e-4)
    li = lam_im.astype(jnp.float32)
    dt = jnp.exp(log_dt.astype(jnp.float32))[:, None]
    mag = jnp.exp(lr * dt)
    ang = li * dt
    a_re = mag * jnp.cos(ang)
    a_im = mag * jnp.sin(ang)
    den = lr * lr + li * li
    num_re = a_re - 1.0
    f_re = (num_re * lr + a_im * li) / den
    f_im = (a_im * lr - num_re * li) / den
    br = b_re.astype(jnp.float32)
    bi = b_im.astype(jnp.float32)
    bbar_re = f_re[..., None] * br - f_im[..., None] * bi
    bbar_im = f_re[..., None] * bi + f_im[..., None] * br
    return a_re, a_im, bbar_re, bbar_im


def s5_combine(e1, e2):
    a1r, a1i, b1r, b1i = e1
    a2r, a2i, b2r, b2i = e2
    ar = a2r * a1r - a2i * a1i
    ai = a2r * a1i + a2i * a1r
    br = a2r * b1r - a2i * b1i + b2r
    bi = a2r * b1i + a2i * b1r + b2i
    return (ar, ai, br, bi)


def s5_mixer(u, s0_re, s0_im, lam_re, lam_im, log_dt, b_re, b_im, c_re, c_im, d_skip, glu_w, glu_b):
    a_re, a_im, bbar_re, bbar_im = s5_discretise(lam_re, lam_im, log_dt, b_re, b_im)
    bu_re = jnp.einsum('nlgh,gph->nlgp', u, bbar_re)
    bu_im = jnp.einsum('nlgh,gph->nlgp', u, bbar_im)
    ar = jnp.broadcast_to(a_re, bu_re.shape)
    ai = jnp.broadcast_to(a_im, bu_re.shape)
    pa_re, pa_im, sb_re, sb_im = lax.associative_scan(s5_combine, (ar, ai, bu_re, bu_im), axis=1)
    x0r = s0_re.astype(jnp.float32)[:, None]
    x0i = s0_im.astype(jnp.float32)[:, None]
    s_re = pa_re * x0r - pa_im * x0i + sb_re
    s_im = pa_re * x0i + pa_im * x0r + sb_im
    y = (jnp.einsum('nlgp,ghp->nlgh', s_re, c_re.astype(jnp.float32))
         - jnp.einsum('nlgp,ghp->nlgh', s_im, c_im.astype(jnp.float32))
         + d_skip.astype(jnp.float32) * u)
    gy = jax.nn.gelu(y)
    out = gy * jax.nn.sigmoid(jnp.einsum('nlgh,ghk->nlgk', gy, glu_w.astype(jnp.float32))
                              + glu_b.astype(jnp.float32))
    return out, s_re[:, -1], s_im[:, -1]


def pool_mixer(v_buf, start, pool_w, pool_b, pool_scale):
    n, lb, _ = v_buf.shape
    pos = start - POOL_HIST + jnp.arange(lb)
    vf = v_buf.astype(jnp.float32).reshape(n, lb, N_POOL_GROUPS, POOL_GROUP)
    cs = jnp.cumsum(vf, axis=1)
    outs = []
    for gi, w in enumerate(POOL_WINDOWS):
        csg = cs[:, :, gi]
        prev = jnp.pad(csg, ((0, 0), (w, 0), (0, 0)))[:, :lb]
        cnt = jnp.clip(pos + 1, 1, w).astype(jnp.float32)[None, :, None]
        outs.append((csg - prev) / cnt - vf[:, :, gi])
    z = jnp.stack(outs, axis=2)[:, POOL_HIST:]
    z = jnp.einsum('nlgc,gcd->nlgd', z, pool_w.astype(jnp.float32)) + pool_b.astype(jnp.float32)
    return z.reshape(n, lb - POOL_HIST, D_POOL) * pool_scale.astype(jnp.float32)


def decoder_layer(x, c, ssm_re0, ssm_im0, pool_past, start, ada_w, ada_b,
                  ffn1_norm, ffn1_w_gate, ffn1_w_up, ffn1_w_down, mix_norm, w_in,
                  lam_re, lam_im, log_dt, b_re, b_im, c_re, c_im, d_skip, glu_w, glu_b,
                  pool_w, pool_b, pool_scale, w_out,
                  ffn2_norm, ffn2_w_gate, ffn2_w_up, ffn2_w_down):
    n, l, _ = x.shape
    mod = (jax.nn.silu(c) @ ada_w + ada_b).reshape(n, N_SUBLAYERS, N_MOD, 1, D_MODEL)
    h = modulate(rmsnorm(x, ffn1_norm), mod[:, 0, 0], mod[:, 0, 1])
    x = x + FFN_RES * mod[:, 0, 2] * swiglu(h, ffn1_w_gate, ffn1_w_up, ffn1_w_down)
    h = modulate(rmsnorm(x, mix_norm), mod[:, 1, 0], mod[:, 1, 1])
    proj = h @ w_in
    u = proj[..., :D_SSM].astype(jnp.float32).reshape(n, l, N_SSM_GROUPS, SSM_GROUP)
    v = proj[..., D_SSM:]
    y_ssm, new_re, new_im = s5_mixer(u, ssm_re0, ssm_im0, lam_re, lam_im, log_dt,
                                     b_re, b_im, c_re, c_im, d_skip, glu_w, glu_b)
    pool_buf = jnp.concatenate([pool_past.astype(v.dtype), v], axis=1)
    y_pool = pool_mixer(pool_buf, start, pool_w, pool_b, pool_scale)
    mixed = jnp.concatenate([y_ssm.reshape(n, l, D_SSM), y_pool], axis=-1).astype(x.dtype)
    x = x + mod[:, 1, 2] * (mixed @ w_out)
    h = modulate(rmsnorm(x, ffn2_norm), mod[:, 2, 0], mod[:, 2, 1])
    x = x + FFN_RES * mod[:, 2, 2] * swiglu(h, ffn2_w_gate, ffn2_w_up, ffn2_w_down)
    return x, new_re, new_im, pool_buf[:, -POOL_HIST:]


def setup_inputs(seed: int = 0) -> dict:
    key = jax.random.key(seed)
    ks = iter(jax.random.split(key, 48))
    f32 = jnp.float32

    def nrm(shape, s):
        return jax.random.normal(next(ks), shape, f32) * s

    L, G, P, H = DEPTH, N_SSM_GROUPS, SSM_STATE, SSM_GROUP
    inp = {}
    inp['x_prompt'] = nrm((BATCH, SEQ, D_MODEL), 1.0)
    inp['x_sample'] = nrm((DEC_BATCH, DEC_SEQ, D_MODEL), 1.0)
    inp['state_ssm_re'] = nrm((L, DEC_BATCH, G, P), 0.3)
    inp['state_ssm_im'] = nrm((L, DEC_BATCH, G, P), 0.3)
    inp['state_pool'] = nrm((L, DEC_BATCH, POOL_HIST, D_POOL), 1.0)
    inp['c_prompt'] = nrm((BATCH, D_MODEL), 1.0)
    inp['c_sample'] = nrm((DEC_BATCH, D_MODEL), 1.0)
    inp['ada_w'] = nrm((L, D_MODEL, N_SUBLAYERS * N_MOD * D_MODEL), 0.5 * D_MODEL ** -0.5)
    inp['ada_b'] = nrm((L, N_SUBLAYERS * N_MOD * D_MODEL), 0.02)
    inp['ffn1_norm'] = 1.0 + nrm((L, D_MODEL), 0.02)
    inp['ffn1_w_gate'] = nrm((L, D_MODEL, D_FF), D_MODEL ** -0.5)
    inp['ffn1_w_up'] = nrm((L, D_MODEL, D_FF), D_MODEL ** -0.5)
    inp['ffn1_w_down'] = nrm((L, D_FF, D_MODEL), D_FF ** -0.5)
    inp['mix_norm'] = 1.0 + nrm((L, D_MODEL), 0.02)
    inp['w_in'] = nrm((L, D_MODEL, D_MIX), D_MODEL ** -0.5)
    inp['ssm_lambda_re'] = -0.5 + nrm((L, G, P), 0.01)
    inp['ssm_lambda_im'] = jnp.pi * jnp.arange(P, dtype=f32) + nrm((L, G, P), 0.01)
    inp['ssm_log_dt'] = jax.random.uniform(next(ks), (L, G), f32,
                                           minval=math.log(DT_MIN), maxval=math.log(DT_MAX))
    inp['ssm_b_re'] = nrm((L, G, P, H), (2 * H) ** -0.5)
    inp['ssm_b_im'] = nrm((L, G, P, H), (2 * H) ** -0.5)
    inp['ssm_c_re'] = nrm((L, G, H, P), (2 * P) ** -0.5)
    inp['ssm_c_im'] = nrm((L, G, H, P), (2 * P) ** -0.5)
    inp['ssm_d'] = nrm((L, G, H), 0.5)
    inp['ssm_glu_w'] = nrm((L, G, H, H), H ** -0.5)
    inp['ssm_glu_b'] = nrm((L, G, H), 0.02)
    inp['pool_w'] = nrm((L, N_POOL_GROUPS, POOL_GROUP, POOL_GROUP), POOL_GROUP ** -0.5)
    inp['pool_b'] = nrm((L, N_POOL_GROUPS, POOL_GROUP), 0.02)
    inp['pool_scale'] = 1.0 + nrm((L, D_POOL), 0.02)
    inp['w_out'] = nrm((L, D_MIX, D_MODEL), D_MIX ** -0.5)
    inp['ffn2_norm'] = 1.0 + nrm((L, D_MODEL), 0.02)
    inp['ffn2_w_gate'] = nrm((L, D_MODEL, D_FF), D_MODEL ** -0.5)
    inp['ffn2_w_up'] = nrm((L, D_MODEL, D_FF), D_MODEL ** -0.5)
    inp['ffn2_w_down'] = nrm((L, D_FF, D_MODEL), D_FF ** -0.5)
    inp['final_norm'] = 1.0 + nrm((D_MODEL,), 0.02)
    return inp


def reference(x_prompt, x_sample, state_ssm_re, state_ssm_im, state_pool, c_prompt, c_sample,
              ada_w, ada_b, ffn1_norm, ffn1_w_gate, ffn1_w_up, ffn1_w_down, mix_norm, w_in,
              ssm_lambda_re, ssm_lambda_im, ssm_log_dt, ssm_b_re, ssm_b_im, ssm_c_re, ssm_c_im,
              ssm_d, ssm_glu_w, ssm_glu_b, pool_w, pool_b, pool_scale, w_out,
              ffn2_norm, ffn2_w_gate, ffn2_w_up, ffn2_w_down, final_norm):

    def trunk(x, c, ssm_re0, ssm_im0, pool_past, start):
        new_re, new_im, new_pool = [], [], []
        for l in range(DEPTH):
            x, r, i, p = decoder_layer(
                x, c, ssm_re0[l], ssm_im0[l], pool_past[l], start, ada_w[l], ada_b[l],
                ffn1_norm[l], ffn1_w_gate[l], ffn1_w_up[l], ffn1_w_down[l], mix_norm[l], w_in[l],
                ssm_lambda_re[l], ssm_lambda_im[l], ssm_log_dt[l], ssm_b_re[l], ssm_b_im[l],
                ssm_c_re[l], ssm_c_im[l], ssm_d[l], ssm_glu_w[l], ssm_glu_b[l],
                pool_w[l], pool_b[l], pool_scale[l], w_out[l],
                ffn2_norm[l], ffn2_w_gate[l], ffn2_w_up[l], ffn2_w_down[l])
            new_re.append(r)
            new_im.append(i)
            new_pool.append(p)
        return (rmsnorm(x, final_norm), jnp.stack(new_re), jnp.stack(new_im), jnp.stack(new_pool))

    nb = x_prompt.shape[0]
    zero_ssm = jnp.zeros((DEPTH, nb, N_SSM_GROUPS, SSM_STATE), jnp.float32)
    zero_pool = jnp.zeros((DEPTH, nb, POOL_HIST, D_POOL), x_prompt.dtype)
    y_prompt, ssm_re_p, ssm_im_p, pool_p = trunk(x_prompt, c_prompt, zero_ssm, zero_ssm, zero_pool, 0)
    y_sample, ssm_re_s, ssm_im_s, pool_s = trunk(x_sample, c_sample, state_ssm_re, state_ssm_im,
                                                 state_pool, PAST_LEN)
    return (y_prompt, y_sample, ssm_re_p, ssm_im_p, pool_p, ssm_re_s, ssm_im_s, pool_s)
```

```python
import functools

import jax
import jax.numpy as jnp
from jax import lax
from jax.experimental import pallas as pl
from jax.experimental.pallas import tpu as pltpu

F32 = jnp.float32
BF16 = jnp.bfloat16

D_MODEL = 2048
D_FF = 5632
D_SSM = 1024
D_POOL = 1024
SSM_GROUP = 16
SSM_STATE = 64
N_SSM_GROUPS = 64
POOL_WINDOWS = (2, 4, 8, 16)
POOL_GROUP = 256
POOL_HIST = 15
N_MOD_COLS = 9 * D_MODEL
EPS = 1e-6
FFN_RES = 0.5

LANES = 128
SUBLANES = 8
VMEM_LIMIT = 58 * 1024 * 1024

OCT = LANES // SSM_GROUP
N_OCT = N_SSM_GROUPS // OCT
T_CHUNK = 8
OCT_STATE = OCT * SSM_STATE
CHUNK_K = T_CHUNK * LANES


def _cparams(n_axes):
    return pltpu.CompilerParams(dimension_semantics=("arbitrary",) * n_axes,
                                vmem_limit_bytes=VMEM_LIMIT)


def _dot(a, b):
    return jnp.dot(a, b, preferred_element_type=F32)


ADALN_TN = 1024


def _adaln_kernel(c_ref, w_ref, b_ref, o_ref):
    c = c_ref[...]
    sc = (c * jax.nn.sigmoid(c)).astype(BF16)
    o_ref[...] = _dot(sc, w_ref[...].astype(BF16)) + b_ref[...]


def _adaln(c_all, ada_w, ada_b):
    rows = c_all.shape[0]
    return pl.pallas_call(
        _adaln_kernel,
        grid=(N_MOD_COLS // ADALN_TN,),
        in_specs=[pl.BlockSpec((rows, D_MODEL), lambda j: (0, 0)),
                  pl.BlockSpec((D_MODEL, ADALN_TN), lambda j: (0, j)),
                  pl.BlockSpec((1, ADALN_TN), lambda j: (0, j))],
        out_specs=pl.BlockSpec((rows, ADALN_TN), lambda j: (0, j)),
        out_shape=jax.ShapeDtypeStruct((rows, N_MOD_COLS), F32),
        compiler_params=_cparams(1),
        name="adaln",
    )(c_all, ada_w, ada_b.reshape(1, N_MOD_COLS))


def _norm_mod(x, g, shift, scale):
    ms = jnp.mean(x * x, axis=-1, keepdims=True)
    y = x * lax.rsqrt(ms + EPS) * g
    return y * (1.0 + scale) + shift


def _mod_specs(per_row, tm, seq_of_block, sub):
    specs = []
    for m in range(3):
        col = 3 * sub + m
        if per_row:
            specs.append(pl.BlockSpec((tm, D_MODEL), lambda i, *_, col=col: (i, col)))
        else:
            specs.append(pl.BlockSpec((None, 1, D_MODEL),
                                      lambda i, *_, col=col: (seq_of_block(i), 0, col)))
    return specs


def _ffn_kernel(x_ref, g_ref, shift_ref, scale_ref, gate_ref, wg_ref, wu_ref, wd_ref, fg_ref,
                o_ref, h_ref, *, row_chunk, final_norm):
    j = pl.program_id(1)
    tm = x_ref.shape[0]
    n_chunks = tm // row_chunk

    def rows_of(r):
        return pl.ds(pl.multiple_of(r * row_chunk, row_chunk), row_chunk)

    @pl.when(j == 0)
    def _():
        def prologue(r, carry):
            rows = rows_of(r)
            sh = shift_ref[...] if shift_ref.shape[0] == 1 else shift_ref[rows, :]
            sc = scale_ref[...] if scale_ref.shape[0] == 1 else scale_ref[rows, :]
            h_ref[rows, :] = _norm_mod(x_ref[rows, :], g_ref[...], sh, sc).astype(BF16)
            o_ref[rows, :] = jnp.zeros((row_chunk, D_MODEL), F32)
            return carry
        lax.fori_loop(0, n_chunks, prologue, 0)

    def body(r, carry):
        rows = rows_of(r)
        h = h_ref[rows, :]
        g = _dot(h, wg_ref[...])
        u = _dot(h, wu_ref[...])
        a = (g * jax.nn.sigmoid(g) * u).astype(BF16)
        o_ref[rows, :] += _dot(a, wd_ref[...])
        return carry
    lax.fori_loop(0, n_chunks, body, 0)

    @pl.when(j == pl.num_programs(1) - 1)
    def _():
        def epilogue(r, carry):
            rows = rows_of(r)
            gt = gate_ref[...] if gate_ref.shape[0] == 1 else gate_ref[rows, :]
            y = x_ref[rows, :] + FFN_RES * gt * o_ref[rows, :]
            if final_norm:
                ms = jnp.mean(y * y, axis=-1, keepdims=True)
                y = y * lax.rsqrt(ms + EPS) * fg_ref[...]
            o_ref[rows, :] = y
            return carry
        lax.fori_loop(0, n_chunks, epilogue, 0)


def _ffn(x, norm_g, mod, sub, wg, wu, wd, final_g, *, tm, tf, row_chunk, per_row, final_norm):
    rows = x.shape[0]
    blocks_per_seq = 0 if per_row else (rows // mod.shape[0]) // tm
    seq_of_block = (lambda i: i // blocks_per_seq) if not per_row else None
    kern = functools.partial(_ffn_kernel, row_chunk=row_chunk, final_norm=final_norm)
    return pl.pallas_call(
        kern,
        grid=(rows // tm, D_FF // tf),
        in_specs=[pl.BlockSpec((tm, D_MODEL), lambda i, j: (i, 0)),
                  pl.BlockSpec((1, D_MODEL), lambda i, j: (0, 0)),
                  *_mod_specs(per_row, tm, seq_of_block, sub),
                  pl.BlockSpec((D_MODEL, tf), lambda i, j: (0, j)),
                  pl.BlockSpec((D_MODEL, tf), lambda i, j: (0, j)),
                  pl.BlockSpec((tf, D_MODEL), lambda i, j: (j, 0)),
                  pl.BlockSpec((1, D_MODEL), lambda i, j: (0, 0))],
        out_specs=pl.BlockSpec((tm, D_MODEL), lambda i, j: (i, 0)),
        out_shape=jax.ShapeDtypeStruct((rows, D_MODEL), F32),
        scratch_shapes=[pltpu.VMEM((tm, D_MODEL), BF16)],
        compiler_params=_cparams(2),
        name="ffn",
    )(x, norm_g.reshape(1, D_MODEL), mod, mod, mod, wg, wu, wd, final_g.reshape(1, D_MODEL))


def _proj_kernel(x_ref, g_ref, shift_ref, scale_ref, w_ref, u_ref, v_ref):
    h = _norm_mod(x_ref[...], g_ref[...], shift_ref[...], scale_ref[...]).astype(BF16)
    p = _dot(h, w_ref[...])
    u_ref[...] = p[:, :D_SSM].reshape(u_ref.shape)
    v_ref[...] = p[:, D_SSM:]


def _proj(x, norm_g, mod, w_in, *, tm, per_row, n_seq):
    rows = x.shape[0]
    if per_row:
        seq_of_block = None
        u_shape = (rows, D_SSM)
        u_spec = pl.BlockSpec((tm, D_SSM), lambda i: (i, 0))
    else:
        seq_len = rows // n_seq
        blocks_per_seq = seq_len // tm
        seq_of_block = lambda i: i // blocks_per_seq
        u_shape = (seq_len // T_CHUNK, n_seq, T_CHUNK, D_SSM)
        u_spec = pl.BlockSpec((tm // T_CHUNK, None, T_CHUNK, D_SSM),
                              lambda i: (i % blocks_per_seq, i // blocks_per_seq, 0, 0))
    shift_spec, scale_spec, _ = _mod_specs(per_row, tm, seq_of_block, 1)
    return pl.pallas_call(
        _proj_kernel,
        grid=(rows // tm,),
        in_specs=[pl.BlockSpec((tm, D_MODEL), lambda i: (i, 0)),
                  pl.BlockSpec((1, D_MODEL), lambda i: (0, 0)),
                  shift_spec, scale_spec,
                  pl.BlockSpec((D_MODEL, D_MODEL), lambda i: (0, 0))],
        out_specs=[u_spec, pl.BlockSpec((tm, D_POOL), lambda i: (i, 0))],
        out_shape=[jax.ShapeDtypeStruct(u_shape, F32),
                   jax.ShapeDtypeStruct((rows, D_POOL), F32)],
        compiler_params=_cparams(1),
        name="proj",
    )(x, norm_g.reshape(1, D_MODEL), mod, mod, w_in)


def _discretise(lam_re, lam_im, log_dt):
    lr = jnp.minimum(lam_re, -1e-4)
    li = lam_im
    dt = jnp.exp(log_dt)
    mag = jnp.exp(lr * dt)
    ang = li * dt
    a_re = mag * jnp.cos(ang)
    a_im = mag * jnp.sin(ang)
    den = lr * lr + li * li
    num_re = a_re - 1.0
    f_re = (num_re * lr + a_im * li) / den
    f_im = (a_im * lr - num_re * li) / den
    return a_re, a_im, f_re, f_im


def _s5_prep_kernel(lam_re1, lam_im1, ldt1, c_re1, c_im1,
                    lam_re2, lam_im2, ldt2, b_re2, b_im2, gluw_t,
                    toep_ref, wz_ref, m_ref, apow_ref, gluw_ref):
    r1 = lax.broadcasted_iota(jnp.int32, (OCT_STATE, LANES), 0) // SSM_STATE
    l1 = lax.broadcasted_iota(jnp.int32, (OCT_STATE, LANES), 1) // SSM_GROUP
    mask1 = r1 == l1
    r2 = lax.broadcasted_iota(jnp.int32, (LANES, OCT_STATE), 0) // SSM_GROUP
    l2 = lax.broadcasted_iota(jnp.int32, (LANES, OCT_STATE), 1) // SSM_STATE
    mask2 = r2 == l2

    a1_re, a1_im, _, _ = _discretise(lam_re1[...], lam_im1[...], ldt1[...])
    a2_re, a2_im, f_re, f_im = _discretise(lam_re2[...], lam_im2[...], ldt2[...])
    br, bi = b_re2[...], b_im2[...]
    bbar_re = jnp.where(mask2, f_re * br - f_im * bi, 0.0)
    bbar_im = jnp.where(mask2, f_re * bi + f_im * br, 0.0)
    cr = jnp.where(mask1, c_re1[...], 0.0)
    ci = jnp.where(mask1, c_im1[...], 0.0)

    p1_re, p1_im = jnp.ones_like(a1_re), jnp.zeros_like(a1_im)
    p2_re, p2_im = jnp.ones_like(a2_re), jnp.zeros_like(a2_im)
    kk = []
    zero_blk = jnp.zeros((LANES, LANES), F32)
    for j in range(T_CHUNK + 1):
        ca_re = cr * p1_re - ci * p1_im
        ca_im = cr * p1_im + ci * p1_re
        m_ref[0:OCT_STATE, j * LANES:(j + 1) * LANES] = ca_re.astype(BF16)
        m_ref[OCT_STATE:2 * OCT_STATE, j * LANES:(j + 1) * LANES] = (-ca_im).astype(BF16)
        if j < T_CHUNK:
            t = T_CHUNK - 1 - j
            ba_re = bbar_re * p2_re - bbar_im * p2_im
            ba_im = bbar_re * p2_im + bbar_im * p2_re
            wz_ref[t * LANES:(t + 1) * LANES, 0:OCT_STATE] = ba_re.astype(BF16)
            wz_ref[t * LANES:(t + 1) * LANES, OCT_STATE:2 * OCT_STATE] = ba_im.astype(BF16)
            kk.append(jnp.dot(bbar_re, ca_re, preferred_element_type=F32,
                              precision=lax.Precision.HIGHEST)
                      - jnp.dot(bbar_im, ca_im, preferred_element_type=F32,
                                precision=lax.Precision.HIGHEST))
        if j == 1:
            apow_ref[0:1, :] = p2_re
            apow_ref[1:2, :] = p2_im
        if j == T_CHUNK:
            apow_ref[2:3, :] = p2_re
            apow_ref[3:4, :] = p2_im
        p1_re, p1_im = p1_re * a1_re - p1_im * a1_im, p1_re * a1_im + p1_im * a1_re
        p2_re, p2_im = p2_re * a2_re - p2_im * a2_im, p2_re * a2_im + p2_im * a2_re

    for t in range(T_CHUNK):
        for t2 in range(T_CHUNK):
            blk = kk[t2 - t] if t2 >= t else zero_blk
            toep_ref[t * LANES:(t + 1) * LANES, t2 * LANES:(t2 + 1) * LANES] = blk.astype(BF16)

    rg = lax.broadcasted_iota(jnp.int32, (LANES, LANES), 0) // SSM_GROUP
    lg = lax.broadcasted_iota(jnp.int32, (LANES, LANES), 1) // SSM_GROUP
    gluw_ref[...] = jnp.where(rg == lg, gluw_t[...], 0.0).astype(BF16)


def _s5_prep(lam_re, lam_im, log_dt, b_re, b_im, c_re, c_im, glu_w):
    G, P, H = N_SSM_GROUPS, SSM_STATE, SSM_GROUP
    col = lambda a: a.reshape(N_OCT, OCT_STATE, 1)
    row = lambda a: a.reshape(N_OCT, 1, OCT_STATE)
    ldt = jnp.broadcast_to(log_dt[:, None], (G, P))

    def lay1(a_gph):
        a = a_gph.reshape(N_OCT, OCT, P, 1, H)
        return jnp.broadcast_to(a, (N_OCT, OCT, P, OCT, H)).reshape(N_OCT, OCT_STATE, LANES)

    def lay2(a_ghp):
        a = a_ghp.reshape(N_OCT, OCT, H, 1, P)
        return jnp.broadcast_to(a, (N_OCT, OCT, H, OCT, P)).reshape(N_OCT, LANES, OCT_STATE)

    c_re1 = lay1(jnp.swapaxes(c_re, 1, 2))
    c_im1 = lay1(jnp.swapaxes(c_im, 1, 2))
    b_re2 = lay2(jnp.swapaxes(b_re, 1, 2))
    b_im2 = lay2(jnp.swapaxes(b_im, 1, 2))
    gw = glu_w.reshape(N_OCT, OCT, H, 1, H)
    gluw_t = jnp.broadcast_to(gw, (N_OCT, OCT, H, OCT, H)).reshape(N_OCT, LANES, LANES)

    def spec(shape):
        return pl.BlockSpec((None,) + shape, lambda o: (o,) + (0,) * len(shape))

    m_cols = (T_CHUNK + 1) * LANES
    return pl.pallas_call(
        _s5_prep_kernel,
        grid=(N_OCT,),
        in_specs=[spec((OCT_STATE, 1))] * 3 + [spec((OCT_STATE, LANES))] * 2
                 + [spec((1, OCT_STATE))] * 3 + [spec((LANES, OCT_STATE))] * 2
                 + [spec((LANES, LANES))],
        out_specs=[spec((CHUNK_K, CHUNK_K)), spec((CHUNK_K, 2 * OCT_STATE)),
                   spec((2 * OCT_STATE, m_cols)), spec((4, OCT_STATE)), spec((LANES, LANES))],
        out_shape=[jax.ShapeDtypeStruct((N_OCT, CHUNK_K, CHUNK_K), BF16),
                   jax.ShapeDtypeStruct((N_OCT, CHUNK_K, 2 * OCT_STATE), BF16),
                   jax.ShapeDtypeStruct((N_OCT, 2 * OCT_STATE, m_cols), BF16),
                   jax.ShapeDtypeStruct((N_OCT, 4, OCT_STATE), F32),
                   jax.ShapeDtypeStruct((N_OCT, LANES, LANES), BF16)],
        compiler_params=_cparams(1),
        name="s5_prep",
    )(col(lam_re), col(lam_im), col(ldt), c_re1, c_im1,
      row(lam_re), row(lam_im), row(ldt), b_re2, b_im2, gluw_t)


def _glu_out(y, gluw, glub):
    gy = jax.nn.gelu(y, approximate=True)
    return gy * jax.nn.sigmoid(_dot(gy.astype(BF16), gluw) + glub)


def _s5_prompt_kernel(x_ref, toep_ref, wz_ref, m_ref, apow_ref, d_ref, gluw_ref, glub_ref,
                      y_ref, sre_ref, sim_ref, xr_ref, z_ref, yv_ref, *, n_seq, row_chunk):
    n_rows = x_ref.shape[0] // T_CHUNK
    n_blk = n_rows // row_chunk

    for t in range(T_CHUNK):
        xr_ref[:, t * LANES:(t + 1) * LANES] = x_ref[pl.ds(t, n_rows, stride=T_CHUNK), :].astype(BF16)

    def local(b, carry):
        rows = pl.ds(pl.multiple_of(b * row_chunk, row_chunk), row_chunk)
        xr = xr_ref[rows, :]
        z_ref[rows, :] = _dot(xr, wz_ref[...])
        yv_ref[rows, :] = _dot(xr, toep_ref[...])
        return carry
    lax.fori_loop(0, n_blk, local, 0)

    are, aim = apow_ref[2:3, :], apow_ref[3:4, :]
    lo = lax.broadcasted_iota(jnp.int32, (SUBLANES, OCT_STATE), 0) < n_seq

    def step(k, carry):
        pre, pim = carry
        rows = pl.ds(pl.multiple_of(k * SUBLANES, SUBLANES), SUBLANES)
        zre = z_ref[rows, 0:OCT_STATE]
        zim = z_ref[rows, OCT_STATE:2 * OCT_STATE]
        w1re = are * pre - aim * pim + zre
        w1im = are * pim + aim * pre + zim
        r1re = pltpu.roll(w1re, n_seq, axis=0)
        r1im = pltpu.roll(w1im, n_seq, axis=0)
        w2re = are * r1re - aim * r1im + zre
        w2im = are * r1im + aim * r1re + zim
        z_ref[rows, 0:OCT_STATE] = jnp.where(lo, pre, r1re)
        z_ref[rows, OCT_STATE:2 * OCT_STATE] = jnp.where(lo, pim, r1im)
        nre = jnp.where(lo, pltpu.roll(w2re, n_seq, axis=0), w2re)
        nim = jnp.where(lo, pltpu.roll(w2im, n_seq, axis=0), w2im)
        return nre, nim

    zeros = jnp.zeros((SUBLANES, OCT_STATE), F32)
    fre, fim = lax.fori_loop(0, n_rows // SUBLANES, step, (zeros, zeros))
    sre_ref[...] = fre[0:n_seq, :]
    sim_ref[...] = fim[0:n_seq, :]

    def readout(b, carry):
        rows = pl.ds(pl.multiple_of(b * row_chunk, row_chunk), row_chunk)
        y = yv_ref[rows, :] + _dot(z_ref[rows, :].astype(BF16), m_ref[:, LANES:])
        for t in range(T_CHUNK):
            tok = pl.ds(b * (row_chunk * T_CHUNK) + t, row_chunk, stride=T_CHUNK)
            yt = y[:, t * LANES:(t + 1) * LANES] + d_ref[...] * x_ref[tok, :]
            y_ref[tok, :] = _glu_out(yt, gluw_ref[...], glub_ref[...])
        return carry
    lax.fori_loop(0, n_blk, readout, 0)


def _s5_prompt(u_flat, ops, d_skip, glu_b, *, n_seq):
    toep, wz, m, apow, gluw = ops
    rows = u_flat.shape[0]
    n_rows = rows // T_CHUNK
    oct_spec = lambda shape: pl.BlockSpec((None,) + shape, lambda o: (o, 0, 0))
    kern = functools.partial(_s5_prompt_kernel, n_seq=n_seq, row_chunk=256)
    return pl.pallas_call(
        kern,
        grid=(N_OCT,),
        in_specs=[pl.BlockSpec((rows, LANES), lambda o: (0, o)),
                  oct_spec(toep.shape[1:]), oct_spec(wz.shape[1:]), oct_spec(m.shape[1:]),
                  oct_spec(apow.shape[1:]), oct_spec((1, LANES)), oct_spec((LANES, LANES)),
                  oct_spec((1, LANES))],
        out_specs=[pl.BlockSpec((rows, LANES), lambda o: (0, o)),
                   pl.BlockSpec((n_seq, OCT_STATE), lambda o: (0, o)),
                   pl.BlockSpec((n_seq, OCT_STATE), lambda o: (0, o))],
        out_shape=[jax.ShapeDtypeStruct((rows, D_SSM), F32),
                   jax.ShapeDtypeStruct((n_seq, N_OCT * OCT_STATE), F32),
                   jax.ShapeDtypeStruct((n_seq, N_OCT * OCT_STATE), F32)],
        scratch_shapes=[pltpu.VMEM((n_rows, CHUNK_K), BF16),
                        pltpu.VMEM((n_rows, 2 * OCT_STATE), F32),
                        pltpu.VMEM((n_rows, CHUNK_K), F32)],
        compiler_params=_cparams(1),
        name="s5_prompt",
    )(u_flat, toep, wz, m, apow, d_skip.reshape(N_OCT, 1, LANES), gluw,
      glu_b.reshape(N_OCT, 1, LANES))


def _s5_sample_kernel(u_ref, s0re_ref, s0im_ref, wz_ref, m_ref, apow_ref, d_ref, gluw_ref,
                      glub_ref, y_ref, sre_ref, sim_ref):
    u = u_ref[...]
    z = _dot(u.astype(BF16), wz_ref[...])
    are, aim = apow_ref[0:1, :], apow_ref[1:2, :]
    s0re, s0im = s0re_ref[...], s0im_ref[...]
    nre = are * s0re - aim * s0im + z[:, 0:OCT_STATE]
    nim = are * s0im + aim * s0re + z[:, OCT_STATE:2 * OCT_STATE]
    sre_ref[...] = nre
    sim_ref[...] = nim
    y = (_dot(nre.astype(BF16), m_ref[0:OCT_STATE, :])
         + _dot(nim.astype(BF16), m_ref[OCT_STATE:2 * OCT_STATE, :])
         + d_ref[...] * u)
    y_ref[...] = _glu_out(y, gluw_ref[...], glub_ref[...])


def _s5_sample(u, s0_re, s0_im, ops, d_skip, glu_b):
    _, wz, m, apow, gluw = ops
    rows = u.shape[0]
    oct_spec = lambda shape: pl.BlockSpec((None,) + shape, lambda o: (o, 0, 0))
    return pl.pallas_call(
        _s5_sample_kernel,
        grid=(N_OCT,),
        in_specs=[pl.BlockSpec((rows, LANES), lambda o: (0, o)),
                  pl.BlockSpec((rows, OCT_STATE), lambda o: (0, o)),
                  pl.BlockSpec((rows, OCT_STATE), lambda o: (0, o)),
                  pl.BlockSpec((None, LANES, 2 * OCT_STATE), lambda o: (o, T_CHUNK - 1, 0)),
                  pl.BlockSpec((None, 2 * OCT_STATE, LANES), lambda o: (o, 0, 0)),
                  oct_spec(apow.shape[1:]), oct_spec((1, LANES)), oct_spec((LANES, LANES)),
                  oct_spec((1, LANES))],
        out_specs=[pl.BlockSpec((rows, LANES), lambda o: (0, o)),
                   pl.BlockSpec((rows, OCT_STATE), lambda o: (0, o)),
                   pl.BlockSpec((rows, OCT_STATE), lambda o: (0, o))],
        out_shape=[jax.ShapeDtypeStruct((rows, D_SSM), F32),
                   jax.ShapeDtypeStruct((rows, N_OCT * OCT_STATE), F32),
                   jax.ShapeDtypeStruct((rows, N_OCT * OCT_STATE), F32)],
        compiler_params=_cparams(1),
        name="s5_sample",
    )(u, s0_re, s0_im, wz, m, apow, d_skip.reshape(N_OCT, 1, LANES), gluw,
      glu_b.reshape(N_OCT, 1, LANES))


def _pool_linear(z, gi, pw_ref, pb_ref, ps_ref):
    lanes = slice(gi * POOL_GROUP, (gi + 1) * POOL_GROUP)
    return (_dot(z.astype(BF16), pw_ref[gi]) + pb_ref[:, lanes]) * ps_ref[:, lanes]


def _pool_prompt_kernel(v_ref, pw_ref, pb_ref, ps_ref, y_ref):
    n = v_ref.shape[0]
    pos = lax.broadcasted_iota(jnp.int32, (n, POOL_GROUP), 0)
    for gi, w in enumerate(POOL_WINDOWS):
        lanes = slice(gi * POOL_GROUP, (gi + 1) * POOL_GROUP)
        v = v_ref[:, lanes]
        s, k = v, 1
        while k < w:
            s = s + jnp.where(pos >= k, pltpu.roll(s, k, axis=0), 0.0)
            k *= 2
        cnt = jnp.clip(pos + 1, 1, w).astype(F32)
        y_ref[:, lanes] = _pool_linear(s / cnt - v, gi, pw_ref, pb_ref, ps_ref)


def _pool_prompt(v, pool_w, pool_b, pool_scale, *, n_seq):
    rows = v.shape[0]
    seq_len = rows // n_seq
    return pl.pallas_call(
        _pool_prompt_kernel,
        grid=(n_seq,),
        in_specs=[pl.BlockSpec((seq_len, D_POOL), lambda n: (n, 0)),
                  pl.BlockSpec(pool_w.shape, lambda n: (0, 0, 0)),
                  pl.BlockSpec((1, D_POOL), lambda n: (0, 0)),
                  pl.BlockSpec((1, D_POOL), lambda n: (0, 0))],
        out_specs=pl.BlockSpec((seq_len, D_POOL), lambda n: (n, 0)),
        out_shape=jax.ShapeDtypeStruct((rows, D_POOL), F32),
        compiler_params=_cparams(1),
        name="pool_prompt",
    )(v, pool_w, pool_b.reshape(1, D_POOL), pool_scale.reshape(1, D_POOL))


def _pool_sample_kernel(hist_ref, v_ref, pw_ref, pb_ref, ps_ref, y_ref):
    for gi, w in enumerate(POOL_WINDOWS):
        lanes = slice(gi * POOL_GROUP, (gi + 1) * POOL_GROUP)
        v = v_ref[:, lanes]
        s = v
        for r in range(POOL_HIST - (w - 1), POOL_HIST):
            s = s + hist_ref[r, :, lanes]
        y_ref[:, lanes] = _pool_linear(s / float(w) - v, gi, pw_ref, pb_ref, ps_ref)


def _pool_sample(hist_t, v, pool_w, pool_b, pool_scale):
    rows = v.shape[0]
    return pl.pallas_call(
        _pool_sample_kernel,
        grid=(1,),
        in_specs=[pl.BlockSpec(hist_t.shape, lambda i: (0, 0, 0)),
                  pl.BlockSpec((rows, D_POOL), lambda i: (0, 0)),
                  pl.BlockSpec(pool_w.shape, lambda i: (0, 0, 0)),
                  pl.BlockSpec((1, D_POOL), lambda i: (0, 0)),
                  pl.BlockSpec((1, D_POOL), lambda i: (0, 0))],
        out_specs=pl.BlockSpec((rows, D_POOL), lambda i: (0, 0)),
        out_shape=jax.ShapeDtypeStruct((rows, D_POOL), F32),
        compiler_params=_cparams(1),
        name="pool_sample",
    )(hist_t, v, pool_w, pool_b.reshape(1, D_POOL), pool_scale.reshape(1, D_POOL))


def _outproj_kernel(x_ref, ys_ref, yp_ref, gate_ref, w_ref, o_ref):
    tm = x_ref.shape[0]
    ys = ys_ref[...].reshape(tm, D_SSM).astype(BF16)
    yp = yp_ref[...].astype(BF16)
    mix = _dot(ys, w_ref[0:D_SSM, :]) + _dot(yp, w_ref[D_SSM:, :])
    o_ref[...] = x_ref[...] + gate_ref[...] * mix


def _outproj(x, ys, yp, mod, w_out, *, tm, per_row, n_seq):
    rows = x.shape[0]
    if per_row:
        seq_of_block = None
        ys_spec = pl.BlockSpec((tm, D_SSM), lambda i: (i, 0))
    else:
        blocks_per_seq = (rows // n_seq) // tm
        seq_of_block = lambda i: i // blocks_per_seq
        ys_spec = pl.BlockSpec((tm // T_CHUNK, None, T_CHUNK, D_SSM),
                               lambda i: (i % blocks_per_seq, i // blocks_per_seq, 0, 0))
    gate_spec = _mod_specs(per_row, tm, seq_of_block, 1)[2]
    return pl.pallas_call(
        _outproj_kernel,
        grid=(rows // tm,),
        in_specs=[pl.BlockSpec((tm, D_MODEL), lambda i: (i, 0)),
                  ys_spec,
                  pl.BlockSpec((tm, D_POOL), lambda i: (i, 0)),
                  gate_spec,
                  pl.BlockSpec((D_MODEL, D_MODEL), lambda i: (0, 0))],
        out_specs=pl.BlockSpec((tm, D_MODEL), lambda i: (i, 0)),
        out_shape=jax.ShapeDtypeStruct((rows, D_MODEL), F32),
        compiler_params=_cparams(1),
        name="outproj",
    )(x, ys, yp, mod, w_out)


def kernel(x_prompt, x_sample, state_ssm_re, state_ssm_im, state_pool, c_prompt, c_sample, ada_w, ada_b, ffn1_norm, ffn1_w_gate, ffn1_w_up, ffn1_w_down, mix_norm, w_in, ssm_lambda_re, ssm_lambda_im, ssm_log_dt, ssm_b_re, ssm_b_im, ssm_c_re, ssm_c_im, ssm_d, ssm_glu_w, ssm_glu_b, pool_w, pool_b, pool_scale, w_out, ffn2_norm, ffn2_w_gate, ffn2_w_up, ffn2_w_down, final_norm):
    n_p, seq, _ = x_prompt.shape
    n_s = x_sample.shape[0]
    G, P = N_SSM_GROUPS, SSM_STATE

    pad = (-(n_s + n_p)) % SUBLANES
    c_all = jnp.concatenate([c_sample, c_prompt, jnp.zeros((pad, D_MODEL), F32)], axis=0)
    mod = _adaln(c_all, ada_w[0], ada_b[0])
    mod_s = mod[:n_s]
    mod_p = mod[n_s:n_s + n_p].reshape(n_p, 1, N_MOD_COLS)

    w1g, w1u, w1d = (w[0].astype(BF16) for w in (ffn1_w_gate, ffn1_w_up, ffn1_w_down))
    w2g, w2u, w2d = (w[0].astype(BF16) for w in (ffn2_w_gate, ffn2_w_up, ffn2_w_down))
    win, wout = w_in[0].astype(BF16), w_out[0].astype(BF16)
    pw = pool_w[0].astype(BF16)

    ops = _s5_prep(ssm_lambda_re[0], ssm_lambda_im[0], ssm_log_dt[0], ssm_b_re[0], ssm_b_im[0],
                   ssm_c_re[0], ssm_c_im[0], ssm_glu_w[0])

    xp = x_prompt.reshape(n_p * seq, D_MODEL)
    ffn_p = functools.partial(_ffn, tm=1024, tf=512, row_chunk=256, per_row=False)
    xp = ffn_p(xp, ffn1_norm[0], mod_p, 0, w1g, w1u, w1d, final_norm, final_norm=False)
    u4, v_p = _proj(xp, mix_norm[0], mod_p, win, tm=512, per_row=False, n_seq=n_p)
    ys_p, sre_p, sim_p = _s5_prompt(u4.reshape(n_p * seq, D_SSM), ops, ssm_d[0], ssm_glu_b[0],
                                    n_seq=n_p)
    yp_p = _pool_prompt(v_p, pw, pool_b[0], pool_scale[0], n_seq=n_p)
    xp = _outproj(xp, ys_p.reshape(seq // T_CHUNK, n_p, T_CHUNK, D_SSM), yp_p, mod_p, wout,
                  tm=512, per_row=False, n_seq=n_p)
    y_prompt = ffn_p(xp, ffn2_norm[0], mod_p, 2, w2g, w2u, w2d, final_norm, final_norm=True)

    xs = x_sample.reshape(n_s, D_MODEL)
    ffn_s = functools.partial(_ffn, tm=n_s, tf=512, row_chunk=n_s, per_row=True)
    xs = ffn_s(xs, ffn1_norm[0], mod_s, 0, w1g, w1u, w1d, final_norm, final_norm=False)
    u_s, v_s = _proj(xs, mix_norm[0], mod_s, win, tm=n_s, per_row=True, n_seq=n_s)
    ys_s, sre_s, sim_s = _s5_sample(u_s, state_ssm_re[0].reshape(n_s, G * P),
                                    state_ssm_im[0].reshape(n_s, G * P), ops, ssm_d[0],
                                    ssm_glu_b[0])
    hist_t = jnp.swapaxes(state_pool[0], 0, 1)
    yp_s = _pool_sample(hist_t, v_s, pw, pool_b[0], pool_scale[0])
    xs = _outproj(xs, ys_s, yp_s, mod_s, wout, tm=n_s, per_row=True, n_seq=n_s)
    y_sample = ffn_s(xs, ffn2_norm[0], mod_s, 2, w2g, w2u, w2d, final_norm, final_norm=True)

    pool_p = v_p.reshape(n_p, seq, D_POOL)[:, seq - POOL_HIST:, :][None]
    pool_s = jnp.concatenate([state_pool[0][:, 1:, :], v_s[:, None, :]], axis=1)[None]
    return (y_prompt.reshape(n_p, seq, D_MODEL), y_sample.reshape(n_s, 1, D_MODEL),
            sre_p.reshape(1, n_p, G, P), sim_p.reshape(1, n_p, G, P), pool_p,
            sre_s.reshape(1, n_s, G, P), sim_s.reshape(1, n_s, G, P), pool_s)
```

```python
import functools

import jax
import jax.numpy as jnp
from jax import lax
from jax.experimental import pallas as pl
from jax.experimental.pallas import tpu as pltpu

F32 = jnp.float32
BF16 = jnp.bfloat16

D_MODEL = 2048
D_FF = 5632
D_SSM = 1024
D_POOL = 1024
SSM_GROUP = 16
SSM_STATE = 64
N_SSM_GROUPS = 64
POOL_WINDOWS = (2, 4, 8, 16)
POOL_GROUP = 256
POOL_HIST = 15
N_MOD_COLS = 9 * D_MODEL
N_PROMPT = 4
N_SAMPLE = 128
EPS = 1e-6
FFN_RES = 0.5

LANES = 128
SUBLANES = 8
VMEM_LIMIT = 58 * 1024 * 1024

PROMPT_MOD_BLOCK = N_SAMPLE // SUBLANES
NORM_ROWS = 32

OCT = LANES // SSM_GROUP
N_OCT = N_SSM_GROUPS // OCT
T_CHUNK = 8
OCT_STATE = OCT * SSM_STATE
CHUNK_K = T_CHUNK * LANES


def _cparams(n_axes):
    return pltpu.CompilerParams(dimension_semantics=("arbitrary",) * n_axes,
                                vmem_limit_bytes=VMEM_LIMIT)


def _dot(a, b):
    return jnp.dot(a, b, preferred_element_type=F32)


ADALN_TN = 1024


def _adaln_kernel(c_ref, w_ref, b_ref, o_ref):
    c = c_ref[...]
    sc = (c * jax.nn.sigmoid(c)).astype(BF16)
    o_ref[...] = _dot(sc, w_ref[...].astype(BF16)) + b_ref[...]


def _adaln(c_all, ada_w, ada_b):
    rows = c_all.shape[0]
    return pl.pallas_call(
        _adaln_kernel,
        grid=(N_MOD_COLS // ADALN_TN,),
        in_specs=[pl.BlockSpec((rows, D_MODEL), lambda j: (0, 0)),
                  pl.BlockSpec((D_MODEL, ADALN_TN), lambda j: (0, j)),
                  pl.BlockSpec((1, ADALN_TN), lambda j: (0, j))],
        out_specs=pl.BlockSpec((rows, ADALN_TN), lambda j: (0, j)),
        out_shape=jax.ShapeDtypeStruct((rows, N_MOD_COLS), F32),
        compiler_params=_cparams(1),
        name="adaln",
    )(c_all, ada_w, ada_b.reshape(1, N_MOD_COLS))


def _norm_mod(x, g, shift, scale):
    ms = jnp.mean(x * x, axis=-1, keepdims=True)
    y = x * lax.rsqrt(ms + EPS) * g
    return y * (1.0 + scale) + shift


def _mod_specs(per_row, tm, prompt_row_block, sub):
    specs = []
    for m in range(3):
        col = 3 * sub + m
        if per_row:
            specs.append(pl.BlockSpec((tm, D_MODEL), lambda i, *_, col=col: (i, col)))
        else:
            specs.append(pl.BlockSpec((SUBLANES, D_MODEL),
                                      lambda i, *_, col=col: (prompt_row_block, col)))
    return specs


def _mod_rows(ref, blocks_per_seq, rows=None):
    if blocks_per_seq:
        return ref[pl.ds(pl.program_id(0) // blocks_per_seq, 1), :]
    return ref[...] if rows is None else ref[rows, :]


def _ffn_kernel(x_ref, g_ref, shift_ref, scale_ref, gate_ref, wg_ref, wu_ref, wd_ref, fg_ref,
                o_ref, h_ref, *, row_chunk, final_norm, blocks_per_seq):
    j = pl.program_id(1)
    last_j = pl.num_programs(1) - 1
    tm = x_ref.shape[0]
    n_chunks = tm // row_chunk
    n_sub = row_chunk // NORM_ROWS

    def sub_rows_of(r):
        return [pl.ds(r * row_chunk + s * NORM_ROWS, NORM_ROWS) for s in range(n_sub)]

    def norm_chunk(r):
        for sr in sub_rows_of(r):
            sh = _mod_rows(shift_ref, blocks_per_seq, sr)
            sc = _mod_rows(scale_ref, blocks_per_seq, sr)
            h_ref[sr, :] = _norm_mod(x_ref[sr, :], g_ref[...], sh, sc).astype(BF16)

    def chunk(r, first, last):
        rows = pl.ds(r * row_chunk, row_chunk)
        sub_rows = sub_rows_of(r)
        if first and r + 1 < n_chunks:
            norm_chunk(r + 1)
        h = h_ref[rows, :]
        g = _dot(h, wg_ref[...])
        u = _dot(h, wu_ref[...])
        a = (g * jax.nn.sigmoid(g) * u).astype(BF16)
        d = _dot(a, wd_ref[...])
        acc = d if first else o_ref[rows, :] + d
        if not last:
            o_ref[rows, :] = acc
            return
        gt = _mod_rows(gate_ref, blocks_per_seq, rows)
        o_ref[rows, :] = x_ref[rows, :] + FFN_RES * gt * acc
        if final_norm:
            for sr in sub_rows:
                y = o_ref[sr, :]
                ms = jnp.mean(y * y, axis=-1, keepdims=True)
                o_ref[sr, :] = y * lax.rsqrt(ms + EPS) * fg_ref[...]

    def run(first, last):
        if first:
            norm_chunk(0)
        for r in range(n_chunks):
            chunk(r, first, last)

    pl.when(j == 0)(lambda: run(True, False))
    pl.when(jnp.logical_and(j > 0, j < last_j))(lambda: run(False, False))
    pl.when(j == last_j)(lambda: run(False, True))


def _ffn(x, norm_g, mod, sub, wg, wu, wd, final_g, *, tm, tf, row_chunk, per_row, final_norm,
         n_seq):
    rows = x.shape[0]
    blocks_per_seq = 0 if per_row else (rows // n_seq) // tm
    kern = functools.partial(_ffn_kernel, row_chunk=row_chunk, final_norm=final_norm,
                             blocks_per_seq=blocks_per_seq)
    return pl.pallas_call(
        kern,
        grid=(rows // tm, D_FF // tf),
        in_specs=[pl.BlockSpec((tm, D_MODEL), lambda i, j: (i, 0)),
                  pl.BlockSpec((1, D_MODEL), lambda i, j: (0, 0)),
                  *_mod_specs(per_row, tm, PROMPT_MOD_BLOCK, sub),
                  pl.BlockSpec((D_MODEL, tf), lambda i, j: (0, j)),
                  pl.BlockSpec((D_MODEL, tf), lambda i, j: (0, j)),
                  pl.BlockSpec((tf, D_MODEL), lambda i, j: (j, 0)),
                  pl.BlockSpec((1, D_MODEL), lambda i, j: (0, 0))],
        out_specs=pl.BlockSpec((tm, D_MODEL), lambda i, j: (i, 0)),
        out_shape=jax.ShapeDtypeStruct((rows, D_MODEL), F32),
        scratch_shapes=[pltpu.VMEM((tm, D_MODEL), BF16)],
        compiler_params=_cparams(2),
        name="ffn",
    )(x, norm_g.reshape(1, D_MODEL), mod, mod, mod, wg, wu, wd, final_g.reshape(1, D_MODEL))


def _proj_kernel(x_ref, g_ref, shift_ref, scale_ref, w_ref, u_ref, v_ref, *, blocks_per_seq):
    sh = _mod_rows(shift_ref, blocks_per_seq)
    sc = _mod_rows(scale_ref, blocks_per_seq)
    h = _norm_mod(x_ref[...], g_ref[...], sh, sc).astype(BF16)
    p = _dot(h, w_ref[...])
    u_ref[...] = p[:, :D_SSM].reshape(u_ref.shape)
    v_ref[...] = p[:, D_SSM:]


def _proj(x, norm_g, mod, w_in, *, tm, per_row, n_seq):
    rows = x.shape[0]
    if per_row:
        blocks_per_seq = 0
        u_shape = (rows, D_SSM)
        u_spec = pl.BlockSpec((tm, D_SSM), lambda i: (i, 0))
    else:
        seq_len = rows // n_seq
        blocks_per_seq = seq_len // tm
        u_shape = (seq_len // T_CHUNK, n_seq, T_CHUNK, D_SSM)
        u_spec = pl.BlockSpec((tm // T_CHUNK, None, T_CHUNK, D_SSM),
                              lambda i: (i % blocks_per_seq, i // blocks_per_seq, 0, 0))
    shift_spec, scale_spec, _ = _mod_specs(per_row, tm, PROMPT_MOD_BLOCK, 1)
    return pl.pallas_call(
        functools.partial(_proj_kernel, blocks_per_seq=blocks_per_seq),
        grid=(rows // tm,),
        in_specs=[pl.BlockSpec((tm, D_MODEL), lambda i: (i, 0)),
                  pl.BlockSpec((1, D_MODEL), lambda i: (0, 0)),
                  shift_spec, scale_spec,
                  pl.BlockSpec((D_MODEL, D_MODEL), lambda i: (0, 0))],
        out_specs=[u_spec, pl.BlockSpec((tm, D_POOL), lambda i: (i, 0))],
        out_shape=[jax.ShapeDtypeStruct(u_shape, F32),
                   jax.ShapeDtypeStruct((rows, D_POOL), F32)],
        compiler_params=_cparams(1),
        name="proj",
    )(x, norm_g.reshape(1, D_MODEL), mod, mod, w_in)


def _discretise(lam_re, lam_im, log_dt):
    lr = jnp.minimum(lam_re, -1e-4)
    li = lam_im
    dt = jnp.exp(log_dt)
    mag = jnp.exp(lr * dt)
    ang = li * dt
    a_re = mag * jnp.cos(ang)
    a_im = mag * jnp.sin(ang)
    den = lr * lr + li * li
    num_re = a_re - 1.0
    f_re = (num_re * lr + a_im * li) / den
    f_im = (a_im * lr - num_re * li) / den
    return a_re, a_im, f_re, f_im


def _s5_prep_kernel(lam_re1, lam_im1, ldt1, c_re1, c_im1,
                    lam_re2, lam_im2, ldt2, b_re2, b_im2, gluw_t,
                    toep_ref, wz_ref, m_ref, apow_ref, gluw_ref):
    r1 = lax.broadcasted_iota(jnp.int32, (OCT_STATE, LANES), 0) // SSM_STATE
    l1 = lax.broadcasted_iota(jnp.int32, (OCT_STATE, LANES), 1) // SSM_GROUP
    mask1 = r1 == l1
    r2 = lax.broadcasted_iota(jnp.int32, (LANES, OCT_STATE), 0) // SSM_GROUP
    l2 = lax.broadcasted_iota(jnp.int32, (LANES, OCT_STATE), 1) // SSM_STATE
    mask2 = r2 == l2

    a1_re, a1_im, _, _ = _discretise(lam_re1[...], lam_im1[...], ldt1[...])
    a2_re, a2_im, f_re, f_im = _discretise(lam_re2[...], lam_im2[...], ldt2[...])
    br, bi = b_re2[...], b_im2[...]
    bbar_re = jnp.where(mask2, f_re * br - f_im * bi, 0.0)
    bbar_im = jnp.where(mask2, f_re * bi + f_im * br, 0.0)
    cr = jnp.where(mask1, c_re1[...], 0.0)
    ci = jnp.where(mask1, c_im1[...], 0.0)

    p1_re, p1_im = jnp.ones_like(a1_re), jnp.zeros_like(a1_im)
    p2_re, p2_im = jnp.ones_like(a2_re), jnp.zeros_like(a2_im)
    kk = []
    zero_blk = jnp.zeros((LANES, LANES), F32)
    for j in range(T_CHUNK + 1):
        ca_re = cr * p1_re - ci * p1_im
        ca_im = cr * p1_im + ci * p1_re
        m_ref[0:OCT_STATE, j * LANES:(j + 1) * LANES] = ca_re.astype(BF16)
        m_ref[OCT_STATE:2 * OCT_STATE, j * LANES:(j + 1) * LANES] = (-ca_im).astype(BF16)
        if j < T_CHUNK:
            t = T_CHUNK - 1 - j
            ba_re = bbar_re * p2_re - bbar_im * p2_im
            ba_im = bbar_re * p2_im + bbar_im * p2_re
            wz_ref[t * LANES:(t + 1) * LANES, 0:OCT_STATE] = ba_re.astype(BF16)
            wz_ref[t * LANES:(t + 1) * LANES, OCT_STATE:2 * OCT_STATE] = ba_im.astype(BF16)
            kk.append(jnp.dot(bbar_re, ca_re, preferred_element_type=F32,
                              precision=lax.Precision.HIGHEST)
                      - jnp.dot(bbar_im, ca_im, preferred_element_type=F32,
                                precision=lax.Precision.HIGHEST))
        if j == 1:
            apow_ref[0:1, :] = p2_re
            apow_ref[1:2, :] = p2_im
        if j == T_CHUNK:
            apow_ref[2:3, :] = p2_re
            apow_ref[3:4, :] = p2_im
        p1_re, p1_im = p1_re * a1_re - p1_im * a1_im, p1_re * a1_im + p1_im * a1_re
        p2_re, p2_im = p2_re * a2_re - p2_im * a2_im, p2_re * a2_im + p2_im * a2_re

    for t in range(T_CHUNK):
        for t2 in range(T_CHUNK):
            blk = kk[t2 - t] if t2 >= t else zero_blk
            toep_ref[t * LANES:(t + 1) * LANES, t2 * LANES:(t2 + 1) * LANES] = blk.astype(BF16)

    rg = lax.broadcasted_iota(jnp.int32, (LANES, LANES), 0) // SSM_GROUP
    lg = lax.broadcasted_iota(jnp.int32, (LANES, LANES), 1) // SSM_GROUP
    gluw_ref[...] = jnp.where(rg == lg, gluw_t[...], 0.0).astype(BF16)


def _s5_prep(lam_re, lam_im, log_dt, b_re, b_im, c_re, c_im, glu_w):
    G, P, H = N_SSM_GROUPS, SSM_STATE, SSM_GROUP
    col = lambda a: a.reshape(N_OCT, OCT_STATE, 1)
    row = lambda a: a.reshape(N_OCT, 1, OCT_STATE)
    ldt = jnp.broadcast_to(log_dt[:, None], (G, P))

    def lay1(a_gph):
        a = a_gph.reshape(N_OCT, OCT, P, 1, H)
        return jnp.broadcast_to(a, (N_OCT, OCT, P, OCT, H)).reshape(N_OCT, OCT_STATE, LANES)

    def lay2(a_ghp):
        a = a_ghp.reshape(N_OCT, OCT, H, 1, P)
        return jnp.broadcast_to(a, (N_OCT, OCT, H, OCT, P)).reshape(N_OCT, LANES, OCT_STATE)

    c_re1 = lay1(jnp.swapaxes(c_re, 1, 2))
    c_im1 = lay1(jnp.swapaxes(c_im, 1, 2))
    b_re2 = lay2(jnp.swapaxes(b_re, 1, 2))
    b_im2 = lay2(jnp.swapaxes(b_im, 1, 2))
    gw = glu_w.reshape(N_OCT, OCT, H, 1, H)
    gluw_t = jnp.broadcast_to(gw, (N_OCT, OCT, H, OCT, H)).reshape(N_OCT, LANES, LANES)

    def spec(shape):
        return pl.BlockSpec((None,) + shape, lambda o: (o,) + (0,) * len(shape))

    m_cols = (T_CHUNK + 1) * LANES
    return pl.pallas_call(
        _s5_prep_kernel,
        grid=(N_OCT,),
        in_specs=[spec((OCT_STATE, 1))] * 3 + [spec((OCT_STATE, LANES))] * 2
                 + [spec((1, OCT_STATE))] * 3 + [spec((LANES, OCT_STATE))] * 2
                 + [spec((LANES, LANES))],
        out_specs=[spec((CHUNK_K, CHUNK_K)), spec((CHUNK_K, 2 * OCT_STATE)),
                   spec((2 * OCT_STATE, m_cols)), spec((4, OCT_STATE)), spec((LANES, LANES))],
        out_shape=[jax.ShapeDtypeStruct((N_OCT, CHUNK_K, CHUNK_K), BF16),
                   jax.ShapeDtypeStruct((N_OCT, CHUNK_K, 2 * OCT_STATE), BF16),
                   jax.ShapeDtypeStruct((N_OCT, 2 * OCT_STATE, m_cols), BF16),
                   jax.ShapeDtypeStruct((N_OCT, 4, OCT_STATE), F32),
                   jax.ShapeDtypeStruct((N_OCT, LANES, LANES), BF16)],
        compiler_params=_cparams(1),
        name="s5_prep",
    )(col(lam_re), col(lam_im), col(ldt), c_re1, c_im1,
      row(lam_re), row(lam_im), row(ldt), b_re2, b_im2, gluw_t)


def _glu_out(y, gluw, glub):
    gy = jax.nn.gelu(y, approximate=True)
    return gy * jax.nn.sigmoid(_dot(gy.astype(BF16), gluw) + glub)


def _s5_prompt_kernel(x_ref, toep_ref, wz_ref, m_ref, apow_ref, d_ref, gluw_ref, glub_ref,
                      y_ref, sre_ref, sim_ref, xr_ref, z_ref, yv_ref, *, n_seq, row_chunk):
    n_rows = x_ref.shape[0] // T_CHUNK
    n_blk = n_rows // row_chunk

    for t in range(T_CHUNK):
        xr_ref[:, t * LANES:(t + 1) * LANES] = x_ref[pl.ds(t, n_rows, stride=T_CHUNK), :].astype(BF16)

    def local(b, carry):
        rows = pl.ds(pl.multiple_of(b * row_chunk, row_chunk), row_chunk)
        xr = xr_ref[rows, :]
        z_ref[rows, :] = _dot(xr, wz_ref[...])
        yv_ref[rows, :] = _dot(xr, toep_ref[...])
        return carry
    lax.fori_loop(0, n_blk, local, 0)

    are, aim = apow_ref[2:3, :], apow_ref[3:4, :]
    lo = lax.broadcasted_iota(jnp.int32, (SUBLANES, OCT_STATE), 0) < n_seq

    def step(k, carry):
        pre, pim = carry
        rows = pl.ds(pl.multiple_of(k * SUBLANES, SUBLANES), SUBLANES)
        zre = z_ref[rows, 0:OCT_STATE]
        zim = z_ref[rows, OCT_STATE:2 * OCT_STATE]
        w1re = are * pre - aim * pim + zre
        w1im = are * pim + aim * pre + zim
        r1re = pltpu.roll(w1re, n_seq, axis=0)
        r1im = pltpu.roll(w1im, n_seq, axis=0)
        w2re = are * r1re - aim * r1im + zre
        w2im = are * r1im + aim * r1re + zim
        z_ref[rows, 0:OCT_STATE] = jnp.where(lo, pre, r1re)
        z_ref[rows, OCT_STATE:2 * OCT_STATE] = jnp.where(lo, pim, r1im)
        nre = jnp.where(lo, pltpu.roll(w2re, n_seq, axis=0), w2re)
        nim = jnp.where(lo, pltpu.roll(w2im, n_seq, axis=0), w2im)
        return nre, nim

    zeros = jnp.zeros((SUBLANES, OCT_STATE), F32)
    fre, fim = lax.fori_loop(0, n_rows // SUBLANES, step, (zeros, zeros))
    sre_ref[...] = fre[0:n_seq, :]
    sim_ref[...] = fim[0:n_seq, :]

    def readout(b, carry):
        rows = pl.ds(pl.multiple_of(b * row_chunk, row_chunk), row_chunk)
        y = yv_ref[rows, :] + _dot(z_ref[rows, :].astype(BF16), m_ref[:, LANES:])
        for t in range(T_CHUNK):
            tok = pl.ds(b * (row_chunk * T_CHUNK) + t, row_chunk, stride=T_CHUNK)
            yt = y[:, t * LANES:(t + 1) * LANES] + d_ref[...] * x_ref[tok, :]
            y_ref[tok, :] = _glu_out(yt, gluw_ref[...], glub_ref[...])
        return carry
    lax.fori_loop(0, n_blk, readout, 0)


def _s5_prompt(u_flat, ops, d_skip, glu_b, *, n_seq):
    toep, wz, m, apow, gluw = ops
    rows = u_flat.shape[0]
    n_rows = rows // T_CHUNK
    oct_spec = lambda shape: pl.BlockSpec((None,) + shape, lambda o: (o, 0, 0))
    kern = functools.partial(_s5_prompt_kernel, n_seq=n_seq, row_chunk=256)
    return pl.pallas_call(
        kern,
        grid=(N_OCT,),
        in_specs=[pl.BlockSpec((rows, LANES), lambda o: (0, o)),
                  oct_spec(toep.shape[1:]), oct_spec(wz.shape[1:]), oct_spec(m.shape[1:]),
                  oct_spec(apow.shape[1:]), oct_spec((1, LANES)), oct_spec((LANES, LANES)),
                  oct_spec((1, LANES))],
        out_specs=[pl.BlockSpec((rows, LANES), lambda o: (0, o)),
                   pl.BlockSpec((n_seq, OCT_STATE), lambda o: (0, o)),
                   pl.BlockSpec((n_seq, OCT_STATE), lambda o: (0, o))],
        out_shape=[jax.ShapeDtypeStruct((rows, D_SSM), F32),
                   jax.ShapeDtypeStruct((n_seq, N_OCT * OCT_STATE), F32),
                   jax.ShapeDtypeStruct((n_seq, N_OCT * OCT_STATE), F32)],
        scratch_shapes=[pltpu.VMEM((n_rows, CHUNK_K), BF16),
                        pltpu.VMEM((n_rows, 2 * OCT_STATE), F32),
                        pltpu.VMEM((n_rows, CHUNK_K), F32)],
        compiler_params=_cparams(1),
        name="s5_prompt",
    )(u_flat, toep, wz, m, apow, d_skip.reshape(N_OCT, 1, LANES), gluw,
      glu_b.reshape(N_OCT, 1, LANES))


def _s5_sample_kernel(u_ref, s0re_ref, s0im_ref, wz_ref, m_ref, apow_ref, d_ref, gluw_ref,
                      glub_ref, y_ref, sre_ref, sim_ref):
    u = u_ref[...]
    z = _dot(u.astype(BF16), wz_ref[...])
    are, aim = apow_ref[0:1, :], apow_ref[1:2, :]
    s0re, s0im = s0re_ref[...], s0im_ref[...]
    nre = are * s0re - aim * s0im + z[:, 0:OCT_STATE]
    nim = are * s0im + aim * s0re + z[:, OCT_STATE:2 * OCT_STATE]
    sre_ref[...] = nre
    sim_ref[...] = nim
    y = (_dot(nre.astype(BF16), m_ref[0:OCT_STATE, :])
         + _dot(nim.astype(BF16), m_ref[OCT_STATE:2 * OCT_STATE, :])
         + d_ref[...] * u)
    y_ref[...] = _glu_out(y, gluw_ref[...], glub_ref[...])


def _s5_sample(u, s0_re, s0_im, ops, d_skip, glu_b):
    _, wz, m, apow, gluw = ops
    rows = u.shape[0]
    oct_spec = lambda shape: pl.BlockSpec((None,) + shape, lambda o: (o, 0, 0))
    return pl.pallas_call(
        _s5_sample_kernel,
        grid=(N_OCT,),
        in_specs=[pl.BlockSpec((rows, LANES), lambda o: (0, o)),
                  pl.BlockSpec((rows, OCT_STATE), lambda o: (0, o)),
                  pl.BlockSpec((rows, OCT_STATE), lambda o: (0, o)),
                  pl.BlockSpec((None, LANES, 2 * OCT_STATE), lambda o: (o, T_CHUNK - 1, 0)),
                  pl.BlockSpec((None, 2 * OCT_STATE, LANES), lambda o: (o, 0, 0)),
                  oct_spec(apow.shape[1:]), oct_spec((1, LANES)), oct_spec((LANES, LANES)),
                  oct_spec((1, LANES))],
        out_specs=[pl.BlockSpec((rows, LANES), lambda o: (0, o)),
                   pl.BlockSpec((rows, OCT_STATE), lambda o: (0, o)),
                   pl.BlockSpec((rows, OCT_STATE), lambda o: (0, o))],
        out_shape=[jax.ShapeDtypeStruct((rows, D_SSM), F32),
                   jax.ShapeDtypeStruct((rows, N_OCT * OCT_STATE), F32),
                   jax.ShapeDtypeStruct((rows, N_OCT * OCT_STATE), F32)],
        compiler_params=_cparams(1),
        name="s5_sample",
    )(u, s0_re, s0_im, wz, m, apow, d_skip.reshape(N_OCT, 1, LANES), gluw,
      glu_b.reshape(N_OCT, 1, LANES))


def _pool_linear(z, gi, pw_ref, pb_ref, ps_ref):
    lanes = slice(gi * POOL_GROUP, (gi + 1) * POOL_GROUP)
    return (_dot(z.astype(BF16), pw_ref[gi]) + pb_ref[:, lanes]) * ps_ref[:, lanes]


def _pool_prompt_kernel(v_ref, pw_ref, pb_ref, ps_ref, y_ref):
    n = v_ref.shape[0]
    pos = lax.broadcasted_iota(jnp.int32, (n, POOL_GROUP), 0)
    for gi, w in enumerate(POOL_WINDOWS):
        lanes = slice(gi * POOL_GROUP, (gi + 1) * POOL_GROUP)
        v = v_ref[:, lanes]
        s, k = v, 1
        while k < w:
            s = s + jnp.where(pos >= k, pltpu.roll(s, k, axis=0), 0.0)
            k *= 2
        cnt = jnp.clip(pos + 1, 1, w).astype(F32)
        y_ref[:, lanes] = _pool_linear(s / cnt - v, gi, pw_ref, pb_ref, ps_ref)


def _pool_prompt(v, pool_w, pool_b, pool_scale, *, n_seq):
    rows = v.shape[0]
    seq_len = rows // n_seq
    return pl.pallas_call(
        _pool_prompt_kernel,
        grid=(n_seq,),
        in_specs=[pl.BlockSpec((seq_len, D_POOL), lambda n: (n, 0)),
                  pl.BlockSpec(pool_w.shape, lambda n: (0, 0, 0)),
                  pl.BlockSpec((1, D_POOL), lambda n: (0, 0)),
                  pl.BlockSpec((1, D_POOL), lambda n: (0, 0))],
        out_specs=pl.BlockSpec((seq_len, D_POOL), lambda n: (n, 0)),
        out_shape=jax.ShapeDtypeStruct((rows, D_POOL), F32),
        compiler_params=_cparams(1),
        name="pool_prompt",
    )(v, pool_w, pool_b.reshape(1, D_POOL), pool_scale.reshape(1, D_POOL))


def _pool_sample_kernel(hist_ref, v_ref, pw_ref, pb_ref, ps_ref, y_ref):
    for gi, w in enumerate(POOL_WINDOWS):
        lanes = slice(gi * POOL_GROUP, (gi + 1) * POOL_GROUP)
        v = v_ref[:, lanes]
        s = v
        for r in range(POOL_HIST - (w - 1), POOL_HIST):
            s = s + hist_ref[r, :, lanes]
        y_ref[:, lanes] = _pool_linear(s / float(w) - v, gi, pw_ref, pb_ref, ps_ref)


def _pool_sample(hist_t, v, pool_w, pool_b, pool_scale):
    rows = v.shape[0]
    return pl.pallas_call(
        _pool_sample_kernel,
        grid=(1,),
        in_specs=[pl.BlockSpec(hist_t.shape, lambda i: (0, 0, 0)),
                  pl.BlockSpec((rows, D_POOL), lambda i: (0, 0)),
                  pl.BlockSpec(pool_w.shape, lambda i: (0, 0, 0)),
                  pl.BlockSpec((1, D_POOL), lambda i: (0, 0)),
                  pl.BlockSpec((1, D_POOL), lambda i: (0, 0))],
        out_specs=pl.BlockSpec((rows, D_POOL), lambda i: (0, 0)),
        out_shape=jax.ShapeDtypeStruct((rows, D_POOL), F32),
        compiler_params=_cparams(1),
        name="pool_sample",
    )(hist_t, v, pool_w, pool_b.reshape(1, D_POOL), pool_scale.reshape(1, D_POOL))


def _outproj_kernel(x_ref, ys_ref, yp_ref, gate_ref, w_ref, o_ref, *, blocks_per_seq):
    tm = x_ref.shape[0]
    ys = ys_ref[...].reshape(tm, D_SSM).astype(BF16)
    yp = yp_ref[...].astype(BF16)
    mix = _dot(ys, w_ref[0:D_SSM, :]) + _dot(yp, w_ref[D_SSM:, :])
    o_ref[...] = x_ref[...] + _mod_rows(gate_ref, blocks_per_seq) * mix


def _outproj(x, ys, yp, mod, w_out, *, tm, per_row, n_seq):
    rows = x.shape[0]
    if per_row:
        blocks_per_seq = 0
        ys_spec = pl.BlockSpec((tm, D_SSM), lambda i: (i, 0))
    else:
        blocks_per_seq = (rows // n_seq) // tm
        ys_spec = pl.BlockSpec((tm // T_CHUNK, None, T_CHUNK, D_SSM),
                               lambda i: (i % blocks_per_seq, i // blocks_per_seq, 0, 0))
    gate_spec = _mod_specs(per_row, tm, PROMPT_MOD_BLOCK, 1)[2]
    return pl.pallas_call(
        functools.partial(_outproj_kernel, blocks_per_seq=blocks_per_seq),
        grid=(rows // tm,),
        in_specs=[pl.BlockSpec((tm, D_MODEL), lambda i: (i, 0)),
                  ys_spec,
                  pl.BlockSpec((tm, D_POOL), lambda i: (i, 0)),
                  gate_spec,
                  pl.BlockSpec((D_MODEL, D_MODEL), lambda i: (0, 0))],
        out_specs=pl.BlockSpec((tm, D_MODEL), lambda i: (i, 0)),
        out_shape=jax.ShapeDtypeStruct((rows, D_MODEL), F32),
        compiler_params=_cparams(1),
        name="outproj",
    )(x, ys, yp, mod, w_out)


def kernel(x_prompt, x_sample, state_ssm_re, state_ssm_im, state_pool, c_prompt, c_sample, ada_w, ada_b, ffn1_norm, ffn1_w_gate, ffn1_w_up, ffn1_w_down, mix_norm, w_in, ssm_lambda_re, ssm_lambda_im, ssm_log_dt, ssm_b_re, ssm_b_im, ssm_c_re, ssm_c_im, ssm_d, ssm_glu_w, ssm_glu_b, pool_w, pool_b, pool_scale, w_out, ffn2_norm, ffn2_w_gate, ffn2_w_up, ffn2_w_down, final_norm):
    n_p, seq, _ = x_prompt.shape
    n_s = x_sample.shape[0]
    assert (n_p, n_s) == (N_PROMPT, N_SAMPLE) and n_p <= SUBLANES
    G, P = N_SSM_GROUPS, SSM_STATE

    pad = SUBLANES - n_p
    c_all = jnp.concatenate([c_sample, c_prompt, jnp.zeros((pad, D_MODEL), F32)], axis=0)
    mod = _adaln(c_all, ada_w[0], ada_b[0])
    mod_s = mod_p = mod

    w1g, w1u, w1d = (w[0].astype(BF16) for w in (ffn1_w_gate, ffn1_w_up, ffn1_w_down))
    w2g, w2u, w2d = (w[0].astype(BF16) for w in (ffn2_w_gate, ffn2_w_up, ffn2_w_down))
    win, wout = w_in[0].astype(BF16), w_out[0].astype(BF16)
    pw = pool_w[0].astype(BF16)

    ops = _s5_prep(ssm_lambda_re[0], ssm_lambda_im[0], ssm_log_dt[0], ssm_b_re[0], ssm_b_im[0],
                   ssm_c_re[0], ssm_c_im[0], ssm_glu_w[0])

    xp = x_prompt.reshape(n_p * seq, D_MODEL)
    ffn_p = functools.partial(_ffn, tm=1024, tf=512, row_chunk=256, per_row=False, n_seq=n_p)
    xp = ffn_p(xp, ffn1_norm[0], mod_p, 0, w1g, w1u, w1d, final_norm, final_norm=False)
    u4, v_p = _proj(xp, mix_norm[0], mod_p, win, tm=512, per_row=False, n_seq=n_p)
    ys_p, sre_p, sim_p = _s5_prompt(u4.reshape(n_p * seq, D_SSM), ops, ssm_d[0], ssm_glu_b[0],
                                    n_seq=n_p)
    yp_p = _pool_prompt(v_p, pw, pool_b[0], pool_scale[0], n_seq=n_p)
    xp = _outproj(xp, ys_p.reshape(seq // T_CHUNK, n_p, T_CHUNK, D_SSM), yp_p, mod_p, wout,
                  tm=512, per_row=False, n_seq=n_p)
    y_prompt = ffn_p(xp, ffn2_norm[0], mod_p, 2, w2g, w2u, w2d, final_norm, final_norm=True)

    xs = x_sample.reshape(n_s, D_MODEL)
    ffn_s = functools.partial(_ffn, tm=n_s, tf=512, row_chunk=n_s, per_row=True, n_seq=n_s)
    xs = ffn_s(xs, ffn1_norm[0], mod_s, 0, w1g, w1u, w1d, final_norm, final_norm=False)
    u_s, v_s = _proj(xs, mix_norm[0], mod_s, win, tm=n_s, per_row=True, n_seq=n_s)
    ys_s, sre_s, sim_s = _s5_sample(u_s, state_ssm_re[0].reshape(n_s, G * P),
                                    state_ssm_im[0].reshape(n_s, G * P), ops, ssm_d[0],
                                    ssm_glu_b[0])
    hist_t = jnp.swapaxes(state_pool[0], 0, 1)
    yp_s = _pool_sample(hist_t, v_s, pw, pool_b[0], pool_scale[0])
    xs = _outproj(xs, ys_s, yp_s, mod_s, wout, tm=n_s, per_row=True, n_seq=n_s)
    y_sample = ffn_s(xs, ffn2_norm[0], mod_s, 2, w2g, w2u, w2d, final_norm, final_norm=True)

    pool_p = v_p.reshape(n_p, seq, D_POOL)[:, seq - POOL_HIST:, :][None]
    pool_s = jnp.concatenate([state_pool[0][:, 1:, :], v_s[:, None, :]], axis=1)[None]
    return (y_prompt.reshape(n_p, seq, D_MODEL), y_sample.reshape(n_s, 1, D_MODEL),
            sre_p.reshape(1, n_p, G, P), sim_p.reshape(1, n_p, G, P), pool_p,
            sre_s.reshape(1, n_s, G, P), sim_s.reshape(1, n_s, G, P), pool_s)
```

```python
import functools

import jax
import jax.numpy as jnp
from jax import lax
from jax.experimental import pallas as pl
from jax.experimental.pallas import tpu as pltpu

F32 = jnp.float32
BF16 = jnp.bfloat16

D_MODEL = 2048
D_FF = 5632
D_SSM = 1024
D_POOL = 1024
SSM_GROUP = 16
SSM_STATE = 64
N_SSM_GROUPS = 64
POOL_WINDOWS = (2, 4, 8, 16)
POOL_GROUP = 256
POOL_HIST = 15
N_MOD_COLS = 9 * D_MODEL
N_PROMPT = 4
N_SAMPLE = 128
EPS = 1e-6
FFN_RES = 0.5

LANES = 128
SUBLANES = 8
MXU_TILE = 256
VMEM_LIMIT = 58 * 1024 * 1024

PROMPT_MOD_BLOCK = N_SAMPLE // SUBLANES
NORM_ROWS = 32
MIX_SUBLAYER = 1

OCT = LANES // SSM_GROUP
N_OCT = N_SSM_GROUPS // OCT
T_CHUNK = 8
OCT_STATE = OCT * SSM_STATE
CHUNK_K = T_CHUNK * LANES
TOK_PER_TILE = MXU_TILE // LANES


def _cparams(n_axes):
    return pltpu.CompilerParams(dimension_semantics=("arbitrary",) * n_axes,
                                vmem_limit_bytes=VMEM_LIMIT)


def _dot(a, b):
    return jnp.dot(a, b, preferred_element_type=F32)


ADALN_TN = 1024


def _adaln_kernel(c_ref, w_ref, b_ref, o_ref):
    c = c_ref[...]
    sc = (c * jax.nn.sigmoid(c)).astype(BF16)
    o_ref[...] = _dot(sc, w_ref[...].astype(BF16)) + b_ref[...]


def _adaln(c_all, ada_w, ada_b):
    rows = c_all.shape[0]
    return pl.pallas_call(
        _adaln_kernel,
        grid=(N_MOD_COLS // ADALN_TN,),
        in_specs=[pl.BlockSpec((rows, D_MODEL), lambda j: (0, 0)),
                  pl.BlockSpec((D_MODEL, ADALN_TN), lambda j: (0, j)),
                  pl.BlockSpec((1, ADALN_TN), lambda j: (0, j))],
        out_specs=pl.BlockSpec((rows, ADALN_TN), lambda j: (0, j)),
        out_shape=jax.ShapeDtypeStruct((rows, N_MOD_COLS), F32),
        compiler_params=_cparams(1),
        name="adaln",
    )(c_all, ada_w, ada_b.reshape(1, N_MOD_COLS))


def _norm_mod(x, g, shift, scale):
    ms = jnp.mean(x * x, axis=-1, keepdims=True)
    y = x * lax.rsqrt(ms + EPS) * g
    return y * (1.0 + scale) + shift


def _mod_specs(per_row, tm, prompt_row_block, sub):
    specs = []
    for m in range(3):
        col = 3 * sub + m
        if per_row:
            specs.append(pl.BlockSpec((tm, D_MODEL), lambda i, *_, col=col: (i, col)))
        else:
            specs.append(pl.BlockSpec((SUBLANES, D_MODEL),
                                      lambda i, *_, col=col: (prompt_row_block, col)))
    return specs


def _mod_rows(ref, blocks_per_seq, rows=None):
    if blocks_per_seq:
        return ref[pl.ds(pl.program_id(0) // blocks_per_seq, 1), :]
    return ref[...] if rows is None else ref[rows, :]


def _ffn_kernel(x_ref, g_ref, shift_ref, scale_ref, gate_ref, wg_ref, wu_ref, wd_ref, fg_ref,
                o_ref, *rest, row_chunk, final_norm, blocks_per_seq):
    h_ref = rest[-1]
    j = pl.program_id(1)
    last_j = pl.num_programs(1) - 1
    tm = x_ref.shape[0]
    n_chunks = tm // row_chunk
    n_sub = row_chunk // NORM_ROWS

    if len(rest) > 1:
        bf16_refs = rest[:3]
        for w_ref, wo_ref in zip((wg_ref, wu_ref, wd_ref), bf16_refs):
            wo_ref[...] = w_ref[...].astype(BF16)
        wg_ref, wu_ref, wd_ref = bf16_refs

    def sub_rows_of(r):
        return [pl.ds(r * row_chunk + s * NORM_ROWS, NORM_ROWS) for s in range(n_sub)]

    def norm_chunk(r):
        for sr in sub_rows_of(r):
            sh = _mod_rows(shift_ref, blocks_per_seq, sr)
            sc = _mod_rows(scale_ref, blocks_per_seq, sr)
            h_ref[sr, :] = _norm_mod(x_ref[sr, :], g_ref[...], sh, sc).astype(BF16)

    def chunk(r, first, last):
        rows = pl.ds(r * row_chunk, row_chunk)
        sub_rows = sub_rows_of(r)
        if first and r + 1 < n_chunks:
            norm_chunk(r + 1)
        h = h_ref[rows, :]
        g = _dot(h, wg_ref[...])
        u = _dot(h, wu_ref[...])
        a = (g * jax.nn.sigmoid(g) * u).astype(BF16)
        d = _dot(a, wd_ref[...])
        acc = d if first else o_ref[rows, :] + d
        if not last:
            o_ref[rows, :] = acc
            return
        gt = _mod_rows(gate_ref, blocks_per_seq, rows)
        o_ref[rows, :] = x_ref[rows, :] + FFN_RES * gt * acc
        if final_norm:
            for sr in sub_rows:
                y = o_ref[sr, :]
                ms = jnp.mean(y * y, axis=-1, keepdims=True)
                o_ref[sr, :] = y * lax.rsqrt(ms + EPS) * fg_ref[...]

    def run(first, last):
        if first:
            norm_chunk(0)
        for r in range(n_chunks):
            chunk(r, first, last)

    pl.when(j == 0)(lambda: run(True, False))
    pl.when(jnp.logical_and(j > 0, j < last_j))(lambda: run(False, False))
    pl.when(j == last_j)(lambda: run(False, True))


def _ffn(x, norm_g, mod, sub, wg, wu, wd, final_g, *, tm, tf, row_chunk, per_row, final_norm,
         n_seq):
    rows = x.shape[0]
    blocks_per_seq = 0 if per_row else (rows // n_seq) // tm
    emit_bf16 = wg.dtype == F32
    assert not emit_bf16 or rows == tm
    kern = functools.partial(_ffn_kernel, row_chunk=row_chunk, final_norm=final_norm,
                             blocks_per_seq=blocks_per_seq)
    w_specs = [pl.BlockSpec((D_MODEL, tf), lambda i, j: (0, j)),
               pl.BlockSpec((D_MODEL, tf), lambda i, j: (0, j)),
               pl.BlockSpec((tf, D_MODEL), lambda i, j: (j, 0))]
    out_specs = [pl.BlockSpec((tm, D_MODEL), lambda i, j: (i, 0))]
    out_shape = [jax.ShapeDtypeStruct((rows, D_MODEL), F32)]
    if emit_bf16:
        out_specs += w_specs
        out_shape += [jax.ShapeDtypeStruct(w.shape, BF16) for w in (wg, wu, wd)]
    outs = pl.pallas_call(
        kern,
        grid=(rows // tm, D_FF // tf),
        in_specs=[pl.BlockSpec((tm, D_MODEL), lambda i, j: (i, 0)),
                  pl.BlockSpec((1, D_MODEL), lambda i, j: (0, 0)),
                  *_mod_specs(per_row, tm, PROMPT_MOD_BLOCK, sub),
                  *w_specs,
                  pl.BlockSpec((1, D_MODEL), lambda i, j: (0, 0))],
        out_specs=out_specs,
        out_shape=out_shape,
        scratch_shapes=[pltpu.VMEM((tm, D_MODEL), BF16)],
        compiler_params=_cparams(2),
        name="ffn",
    )(x, norm_g.reshape(1, D_MODEL), mod, mod, mod, wg, wu, wd, final_g.reshape(1, D_MODEL))
    return outs if emit_bf16 else outs[0]


def _proj_kernel(x_ref, g_ref, shift_ref, scale_ref, w_ref, u_ref, v_ref, *, blocks_per_seq):
    sh = _mod_rows(shift_ref, blocks_per_seq)
    sc = _mod_rows(scale_ref, blocks_per_seq)
    h = _norm_mod(x_ref[...], g_ref[...], sh, sc).astype(BF16)
    p = _dot(h, w_ref[...])
    u_ref[...] = p[:, :D_SSM].reshape(u_ref.shape)
    v_ref[...] = p[:, D_SSM:]


def _proj(x, norm_g, mod, w_in, *, tm, n_seq):
    rows = x.shape[0]
    seq_len = rows // n_seq
    blocks_per_seq = seq_len // tm
    u_shape = (seq_len // T_CHUNK, n_seq, T_CHUNK, D_SSM)
    u_spec = pl.BlockSpec((tm // T_CHUNK, None, T_CHUNK, D_SSM),
                          lambda i: (i % blocks_per_seq, i // blocks_per_seq, 0, 0))
    shift_spec, scale_spec, _ = _mod_specs(False, tm, PROMPT_MOD_BLOCK, MIX_SUBLAYER)
    return pl.pallas_call(
        functools.partial(_proj_kernel, blocks_per_seq=blocks_per_seq),
        grid=(rows // tm,),
        in_specs=[pl.BlockSpec((tm, D_MODEL), lambda i: (i, 0)),
                  pl.BlockSpec((1, D_MODEL), lambda i: (0, 0)),
                  shift_spec, scale_spec,
                  pl.BlockSpec((D_MODEL, D_MODEL), lambda i: (0, 0))],
        out_specs=[u_spec, pl.BlockSpec((tm, D_POOL), lambda i: (i, 0))],
        out_shape=[jax.ShapeDtypeStruct(u_shape, F32),
                   jax.ShapeDtypeStruct((rows, D_POOL), F32)],
        compiler_params=_cparams(1),
        name="proj",
    )(x, norm_g.reshape(1, D_MODEL), mod, mod, w_in)


SAMPLE_TN = 512


def _proj_sample_kernel(x_ref, g_ref, shift_ref, scale_ref, w_ref, p_ref, wo_ref, h_ref):
    @pl.when(pl.program_id(0) == 0)
    def _():
        h = _norm_mod(x_ref[...], g_ref[...], shift_ref[...], scale_ref[...])
        h_ref[...] = h.astype(BF16)
    wo_ref[...] = w_ref[...].astype(BF16)
    p_ref[...] = _dot(h_ref[...], wo_ref[...])


def _proj_sample(x, norm_g, mod, w_in):
    rows = x.shape[0]
    shift_col, scale_col = 3 * MIX_SUBLAYER, 3 * MIX_SUBLAYER + 1
    return pl.pallas_call(
        _proj_sample_kernel,
        grid=(D_MODEL // SAMPLE_TN,),
        in_specs=[pl.BlockSpec((rows, D_MODEL), lambda j: (0, 0)),
                  pl.BlockSpec((1, D_MODEL), lambda j: (0, 0)),
                  pl.BlockSpec((rows, D_MODEL), lambda j: (0, shift_col)),
                  pl.BlockSpec((rows, D_MODEL), lambda j: (0, scale_col)),
                  pl.BlockSpec((D_MODEL, SAMPLE_TN), lambda j: (0, j))],
        out_specs=[pl.BlockSpec((rows, SAMPLE_TN), lambda j: (0, j)),
                   pl.BlockSpec((D_MODEL, SAMPLE_TN), lambda j: (0, j))],
        out_shape=[jax.ShapeDtypeStruct((rows, D_MODEL), F32),
                   jax.ShapeDtypeStruct((D_MODEL, D_MODEL), BF16)],
        scratch_shapes=[pltpu.VMEM((rows, D_MODEL), BF16)],
        compiler_params=_cparams(1),
        name="proj_sample",
    )(x, norm_g.reshape(1, D_MODEL), mod, mod, w_in)


def _discretise(lam_re, lam_im, log_dt):
    lr = jnp.minimum(lam_re, -1e-4)
    li = lam_im
    dt = jnp.exp(log_dt)
    mag = jnp.exp(lr * dt)
    ang = li * dt
    a_re = mag * jnp.cos(ang)
    a_im = mag * jnp.sin(ang)
    den = lr * lr + li * li
    num_re = a_re - 1.0
    f_re = (num_re * lr + a_im * li) / den
    f_im = (a_im * lr - num_re * li) / den
    return a_re, a_im, f_re, f_im


def _s5_prep_kernel(lam_re1, lam_im1, ldt1, c_re1, c_im1,
                    lam_re2, lam_im2, ldt2, b_re2, b_im2, gluw_t,
                    toep_ref, wz_ref, m_ref, apow_ref, gluw_ref):
    g1 = lax.broadcasted_iota(jnp.int32, (SSM_STATE, LANES), 1) // SSM_GROUP
    g2 = lax.broadcasted_iota(jnp.int32, (SSM_GROUP, OCT_STATE), 1) // SSM_STATE

    def expand1(x):
        return jnp.concatenate([jnp.where(g1 == g, x, 0.0) for g in range(OCT)], axis=0)

    def expand2(x):
        return jnp.concatenate([jnp.where(g2 == g, x, 0.0) for g in range(OCT)], axis=0)

    def split(x):
        hi = x.astype(BF16)
        return hi, (x - hi.astype(F32)).astype(BF16)

    a1_re, a1_im, _, _ = _discretise(lam_re1[...], lam_im1[...], ldt1[...])
    a2_re, a2_im, f_re, f_im = _discretise(lam_re2[...], lam_im2[...], ldt2[...])
    br, bi = b_re2[...], b_im2[...]
    bbar_re = f_re * br - f_im * bi
    bbar_im = f_re * bi + f_im * br
    cr, ci = c_re1[...], c_im1[...]
    bre_hi, bre_lo = split(expand2(bbar_re))
    bim_hi, bim_lo = split(expand2(bbar_im))

    def dot3(x_hi, x_lo, y_hi, y_lo):
        return _dot(x_hi, y_hi) + (_dot(x_hi, y_lo) + _dot(x_lo, y_hi))

    p1_re, p1_im = jnp.ones_like(a1_re), jnp.zeros_like(a1_im)
    p2_re, p2_im = jnp.ones_like(a2_re), jnp.zeros_like(a2_im)
    kk = []
    zero_blk = jnp.zeros((LANES, LANES), F32)
    for j in range(T_CHUNK + 1):
        care_hi, care_lo = split(expand1(cr * p1_re - ci * p1_im))
        caim_hi, caim_lo = split(expand1(cr * p1_im + ci * p1_re))
        m_ref[0:OCT_STATE, j * LANES:(j + 1) * LANES] = care_hi
        m_ref[OCT_STATE:2 * OCT_STATE, j * LANES:(j + 1) * LANES] = -caim_hi
        if j < T_CHUNK:
            t = T_CHUNK - 1 - j
            ba_re = expand2(bbar_re * p2_re - bbar_im * p2_im)
            ba_im = expand2(bbar_re * p2_im + bbar_im * p2_re)
            wz_ref[t * LANES:(t + 1) * LANES, 0:OCT_STATE] = ba_re.astype(BF16)
            wz_ref[t * LANES:(t + 1) * LANES, OCT_STATE:2 * OCT_STATE] = ba_im.astype(BF16)
            kk.append(dot3(bre_hi, bre_lo, care_hi, care_lo)
                      - dot3(bim_hi, bim_lo, caim_hi, caim_lo))
        if j == 1:
            apow_ref[0:1, :] = p2_re
            apow_ref[1:2, :] = p2_im
        if j == T_CHUNK:
            apow_ref[2:3, :] = p2_re
            apow_ref[3:4, :] = p2_im
        p1_re, p1_im = p1_re * a1_re - p1_im * a1_im, p1_re * a1_im + p1_im * a1_re
        p2_re, p2_im = p2_re * a2_re - p2_im * a2_im, p2_re * a2_im + p2_im * a2_re

    for t in range(T_CHUNK):
        for t2 in range(T_CHUNK):
            blk = kk[t2 - t] if t2 >= t else zero_blk
            toep_ref[t * LANES:(t + 1) * LANES, t2 * LANES:(t2 + 1) * LANES] = blk.astype(BF16)

    rg = lax.broadcasted_iota(jnp.int32, (LANES, LANES), 0) // SSM_GROUP
    lg = lax.broadcasted_iota(jnp.int32, (LANES, LANES), 1) // SSM_GROUP
    gw = jnp.where(rg == lg, gluw_t[...], 0.0).astype(BF16)
    gz = jnp.zeros((LANES, LANES), BF16)
    for a in range(TOK_PER_TILE):
        for b in range(TOK_PER_TILE):
            gluw_ref[a * LANES:(a + 1) * LANES, b * LANES:(b + 1) * LANES] = gw if a == b else gz


def _s5_prep(lam_re, lam_im, log_dt, b_re, b_im, c_re, c_im, glu_w):
    G, P, H = N_SSM_GROUPS, SSM_STATE, SSM_GROUP
    row = lambda a: a.reshape(N_OCT, 1, OCT_STATE)
    ldt = jnp.broadcast_to(log_dt[:, None], (G, P))

    def col(a_gp):
        a = jnp.swapaxes(a_gp.reshape(N_OCT, OCT, P), 1, 2)
        return jnp.broadcast_to(a[..., None], (N_OCT, P, OCT, H)).reshape(N_OCT, P, LANES)

    lay1 = lambda a_ghp: jnp.swapaxes(a_ghp.reshape(N_OCT, LANES, P), 1, 2)
    lay2 = lambda a_gph: jnp.swapaxes(a_gph.reshape(N_OCT, OCT_STATE, H), 1, 2)
    c_re1, c_im1, b_re2, b_im2 = lay1(c_re), lay1(c_im), lay2(b_re), lay2(b_im)
    gw = glu_w.reshape(N_OCT, OCT, H, 1, H)
    gluw_t = jnp.broadcast_to(gw, (N_OCT, OCT, H, OCT, H)).reshape(N_OCT, LANES, LANES)

    def spec(shape):
        return pl.BlockSpec((None,) + shape, lambda o: (o,) + (0,) * len(shape))

    m_cols = (T_CHUNK + 1) * LANES
    return pl.pallas_call(
        _s5_prep_kernel,
        grid=(N_OCT,),
        in_specs=[spec((SSM_STATE, LANES))] * 5
                 + [spec((1, OCT_STATE))] * 3 + [spec((SSM_GROUP, OCT_STATE))] * 2
                 + [spec((LANES, LANES))],
        out_specs=[spec((CHUNK_K, CHUNK_K)), spec((CHUNK_K, 2 * OCT_STATE)),
                   spec((2 * OCT_STATE, m_cols)), spec((4, OCT_STATE)),
                   spec((MXU_TILE, MXU_TILE))],
        out_shape=[jax.ShapeDtypeStruct((N_OCT, CHUNK_K, CHUNK_K), BF16),
                   jax.ShapeDtypeStruct((N_OCT, CHUNK_K, 2 * OCT_STATE), BF16),
                   jax.ShapeDtypeStruct((N_OCT, 2 * OCT_STATE, m_cols), BF16),
                   jax.ShapeDtypeStruct((N_OCT, 4, OCT_STATE), F32),
                   jax.ShapeDtypeStruct((N_OCT, MXU_TILE, MXU_TILE), BF16)],
        compiler_params=_cparams(1),
        name="s5_prep",
    )(col(lam_re), col(lam_im), col(ldt), c_re1, c_im1,
      row(lam_re), row(lam_im), row(ldt), b_re2, b_im2, gluw_t)


def _glu_out(y, gluw, glub):
    gy = jax.nn.gelu(y, approximate=True)
    return gy * jax.nn.sigmoid(_dot(gy.astype(BF16), gluw) + glub)


def _s5_prompt_kernel(x_ref, toep_ref, wz_ref, m_ref, apow_ref, d_ref, gluw_ref, glub_ref,
                      y_ref, sre_ref, sim_ref, xr_ref, z_ref, yv_ref, *, n_seq, row_chunk):
    n_rows = x_ref.shape[0] // T_CHUNK
    n_blk = n_rows // row_chunk

    for t in range(T_CHUNK):
        xr_ref[:, t * LANES:(t + 1) * LANES] = x_ref[pl.ds(t, n_rows, stride=T_CHUNK), :].astype(BF16)

    def local(b, carry):
        rows = pl.ds(pl.multiple_of(b * row_chunk, row_chunk), row_chunk)
        xr = xr_ref[rows, :]
        z_ref[rows, :] = _dot(xr, wz_ref[...])
        for nt in range(CHUNK_K // MXU_TILE):
            k_end = (nt + 1) * MXU_TILE
            cols = slice(nt * MXU_TILE, k_end)
            yv_ref[rows, cols] = _dot(xr[:, 0:k_end], toep_ref[0:k_end, cols])
        return carry
    lax.fori_loop(0, n_blk, local, 0)

    are, aim = apow_ref[2:3, :], apow_ref[3:4, :]
    lo = lax.broadcasted_iota(jnp.int32, (SUBLANES, OCT_STATE), 0) < n_seq

    def step(k, carry):
        pre, pim = carry
        rows = pl.ds(pl.multiple_of(k * SUBLANES, SUBLANES), SUBLANES)
        zre = z_ref[rows, 0:OCT_STATE]
        zim = z_ref[rows, OCT_STATE:2 * OCT_STATE]
        w1re = are * pre - aim * pim + zre
        w1im = are * pim + aim * pre + zim
        r1re = pltpu.roll(w1re, n_seq, axis=0)
        r1im = pltpu.roll(w1im, n_seq, axis=0)
        w2re = are * r1re - aim * r1im + zre
        w2im = are * r1im + aim * r1re + zim
        z_ref[rows, 0:OCT_STATE] = jnp.where(lo, pre, r1re)
        z_ref[rows, OCT_STATE:2 * OCT_STATE] = jnp.where(lo, pim, r1im)
        nre = jnp.where(lo, pltpu.roll(w2re, n_seq, axis=0), w2re)
        nim = jnp.where(lo, pltpu.roll(w2im, n_seq, axis=0), w2im)
        return nre, nim

    zeros = jnp.zeros((SUBLANES, OCT_STATE), F32)
    fre, fim = lax.fori_loop(0, n_rows // SUBLANES, step, (zeros, zeros))
    sre_ref[...] = fre[0:n_seq, :]
    sim_ref[...] = fim[0:n_seq, :]

    d2 = jnp.concatenate([d_ref[...]] * TOK_PER_TILE, axis=1)
    glub2 = jnp.concatenate([glub_ref[...]] * TOK_PER_TILE, axis=1)

    def readout(b, carry):
        rows = pl.ds(pl.multiple_of(b * row_chunk, row_chunk), row_chunk)
        y = yv_ref[rows, :] + _dot(z_ref[rows, :].astype(BF16), m_ref[:, LANES:])
        for t0 in range(0, T_CHUNK, TOK_PER_TILE):
            toks = [pl.ds(b * (row_chunk * T_CHUNK) + t0 + i, row_chunk, stride=T_CHUNK)
                    for i in range(TOK_PER_TILE)]
            u = jnp.concatenate([x_ref[tok, :] for tok in toks], axis=1)
            yt = y[:, t0 * LANES:(t0 + TOK_PER_TILE) * LANES] + d2 * u
            out = _glu_out(yt, gluw_ref[...], glub2)
            for i, tok in enumerate(toks):
                y_ref[tok, :] = out[:, i * LANES:(i + 1) * LANES]
        return carry
    lax.fori_loop(0, n_blk, readout, 0)


def _s5_prompt(u_flat, ops, d_skip, glu_b, *, n_seq):
    toep, wz, m, apow, gluw = ops
    rows = u_flat.shape[0]
    n_rows = rows // T_CHUNK
    oct_spec = lambda shape: pl.BlockSpec((None,) + shape, lambda o: (o, 0, 0))
    kern = functools.partial(_s5_prompt_kernel, n_seq=n_seq, row_chunk=256)
    return pl.pallas_call(
        kern,
        grid=(N_OCT,),
        in_specs=[pl.BlockSpec((rows, LANES), lambda o: (0, o)),
                  oct_spec(toep.shape[1:]), oct_spec(wz.shape[1:]), oct_spec(m.shape[1:]),
                  oct_spec(apow.shape[1:]), oct_spec((1, LANES)), oct_spec(gluw.shape[1:]),
                  oct_spec((1, LANES))],
        out_specs=[pl.BlockSpec((rows, LANES), lambda o: (0, o)),
                   pl.BlockSpec((n_seq, OCT_STATE), lambda o: (0, o)),
                   pl.BlockSpec((n_seq, OCT_STATE), lambda o: (0, o))],
        out_shape=[jax.ShapeDtypeStruct((rows, D_SSM), F32),
                   jax.ShapeDtypeStruct((n_seq, N_OCT * OCT_STATE), F32),
                   jax.ShapeDtypeStruct((n_seq, N_OCT * OCT_STATE), F32)],
        scratch_shapes=[pltpu.VMEM((n_rows, CHUNK_K), BF16),
                        pltpu.VMEM((n_rows, 2 * OCT_STATE), F32),
                        pltpu.VMEM((n_rows, CHUNK_K), F32)],
        compiler_params=_cparams(1),
        name="s5_prompt",
    )(u_flat, toep, wz, m, apow, d_skip.reshape(N_OCT, 1, LANES), gluw,
      glu_b.reshape(N_OCT, 1, LANES))


def _s5_sample_kernel(u_ref, s0re_ref, s0im_ref, wz_ref, m_ref, apow_ref, d_ref, gluw_ref,
                      glub_ref, y_ref, sre_ref, sim_ref):
    u = u_ref[...]
    z = _dot(u.astype(BF16), wz_ref[...])
    are, aim = apow_ref[0:1, :], apow_ref[1:2, :]
    s0re, s0im = s0re_ref[...], s0im_ref[...]
    nre = are * s0re - aim * s0im + z[:, 0:OCT_STATE]
    nim = are * s0im + aim * s0re + z[:, OCT_STATE:2 * OCT_STATE]
    sre_ref[...] = nre
    sim_ref[...] = nim
    y = (_dot(nre.astype(BF16), m_ref[0:OCT_STATE, :])
         + _dot(nim.astype(BF16), m_ref[OCT_STATE:2 * OCT_STATE, :])
         + d_ref[...] * u)
    y_ref[...] = _glu_out(y, gluw_ref[...], glub_ref[...])


def _s5_sample(u, s0_re, s0_im, ops, d_skip, glu_b):
    _, wz, m, apow, gluw = ops
    rows = u.shape[0]
    oct_spec = lambda shape: pl.BlockSpec((None,) + shape, lambda o: (o, 0, 0))
    return pl.pallas_call(
        _s5_sample_kernel,
        grid=(N_OCT,),
        in_specs=[pl.BlockSpec((rows, LANES), lambda o: (0, o)),
                  pl.BlockSpec((rows, OCT_STATE), lambda o: (0, o)),
                  pl.BlockSpec((rows, OCT_STATE), lambda o: (0, o)),
                  pl.BlockSpec((None, LANES, 2 * OCT_STATE), lambda o: (o, T_CHUNK - 1, 0)),
                  pl.BlockSpec((None, 2 * OCT_STATE, LANES), lambda o: (o, 0, 0)),
                  oct_spec(apow.shape[1:]), oct_spec((1, LANES)), oct_spec((LANES, LANES)),
                  oct_spec((1, LANES))],
        out_specs=[pl.BlockSpec((rows, LANES), lambda o: (0, o)),
                   pl.BlockSpec((rows, OCT_STATE), lambda o: (0, o)),
                   pl.BlockSpec((rows, OCT_STATE), lambda o: (0, o))],
        out_shape=[jax.ShapeDtypeStruct((rows, D_SSM), F32),
                   jax.ShapeDtypeStruct((rows, N_OCT * OCT_STATE), F32),
                   jax.ShapeDtypeStruct((rows, N_OCT * OCT_STATE), F32)],
        compiler_params=_cparams(1),
        name="s5_sample",
    )(u, s0_re, s0_im, wz, m, apow, d_skip.reshape(N_OCT, 1, LANES), gluw,
      glu_b.reshape(N_OCT, 1, LANES))


def _pool_linear(z, gi, pw_ref, pb_ref, ps_ref):
    lanes = slice(gi * POOL_GROUP, (gi + 1) * POOL_GROUP)
    return (_dot(z.astype(BF16), pw_ref[gi]) + pb_ref[:, lanes]) * ps_ref[:, lanes]


def _pool_prompt_kernel(v_ref, pw_ref, pb_ref, ps_ref, y_ref):
    n = v_ref.shape[0]
    pos = lax.broadcasted_iota(jnp.int32, (n, POOL_GROUP), 0)
    for gi, w in enumerate(POOL_WINDOWS):
        lanes = slice(gi * POOL_GROUP, (gi + 1) * POOL_GROUP)
        v = v_ref[:, lanes]
        s, k = v, 1
        while k < w:
            s = s + jnp.where(pos >= k, pltpu.roll(s, k, axis=0), 0.0)
            k *= 2
        cnt = jnp.clip(pos + 1, 1, w).astype(F32)
        y_ref[:, lanes] = _pool_linear(s / cnt - v, gi, pw_ref, pb_ref, ps_ref)


def _pool_prompt(v, pool_w, pool_b, pool_scale, *, n_seq):
    rows = v.shape[0]
    seq_len = rows // n_seq
    return pl.pallas_call(
        _pool_prompt_kernel,
        grid=(n_seq,),
        in_specs=[pl.BlockSpec((seq_len, D_POOL), lambda n: (n, 0)),
                  pl.BlockSpec(pool_w.shape, lambda n: (0, 0, 0)),
                  pl.BlockSpec((1, D_POOL), lambda n: (0, 0)),
                  pl.BlockSpec((1, D_POOL), lambda n: (0, 0))],
        out_specs=pl.BlockSpec((seq_len, D_POOL), lambda n: (n, 0)),
        out_shape=jax.ShapeDtypeStruct((rows, D_POOL), F32),
        compiler_params=_cparams(1),
        name="pool_prompt",
    )(v, pool_w, pool_b.reshape(1, D_POOL), pool_scale.reshape(1, D_POOL))


def _pool_sample_kernel(hist_ref, v_ref, pw_ref, pb_ref, ps_ref, y_ref):
    for gi, w in enumerate(POOL_WINDOWS):
        lanes = slice(gi * POOL_GROUP, (gi + 1) * POOL_GROUP)
        v = v_ref[:, lanes]
        s = v
        for r in range(POOL_HIST - (w - 1), POOL_HIST):
            s = s + hist_ref[r, :, lanes]
        y_ref[:, lanes] = _pool_linear(s / float(w) - v, gi, pw_ref, pb_ref, ps_ref)


def _pool_sample(hist_t, proj, pool_w, pool_b, pool_scale):
    rows = proj.shape[0]
    v = proj
    return pl.pallas_call(
        _pool_sample_kernel,
        grid=(1,),
        in_specs=[pl.BlockSpec(hist_t.shape, lambda i: (0, 0, 0)),
                  pl.BlockSpec((rows, D_POOL), lambda i: (0, D_SSM // D_POOL)),
                  pl.BlockSpec(pool_w.shape, lambda i: (0, 0, 0)),
                  pl.BlockSpec((1, D_POOL), lambda i: (0, 0)),
                  pl.BlockSpec((1, D_POOL), lambda i: (0, 0))],
        out_specs=pl.BlockSpec((rows, D_POOL), lambda i: (0, 0)),
        out_shape=jax.ShapeDtypeStruct((rows, D_POOL), F32),
        compiler_params=_cparams(1),
        name="pool_sample",
    )(hist_t, v, pool_w, pool_b.reshape(1, D_POOL), pool_scale.reshape(1, D_POOL))


def _outproj_kernel(x_ref, ys_ref, yp_ref, gate_ref, w_ref, o_ref, *, blocks_per_seq):
    tm = x_ref.shape[0]
    ys = ys_ref[...].reshape(tm, D_SSM).astype(BF16)
    yp = yp_ref[...].astype(BF16)
    mix = _dot(ys, w_ref[0:D_SSM, :]) + _dot(yp, w_ref[D_SSM:, :])
    o_ref[...] = x_ref[...] + _mod_rows(gate_ref, blocks_per_seq) * mix


def _outproj(x, ys, yp, mod, w_out, *, tm, n_seq):
    rows = x.shape[0]
    blocks_per_seq = (rows // n_seq) // tm
    ys_spec = pl.BlockSpec((tm // T_CHUNK, None, T_CHUNK, D_SSM),
                           lambda i: (i % blocks_per_seq, i // blocks_per_seq, 0, 0))
    gate_spec = _mod_specs(False, tm, PROMPT_MOD_BLOCK, MIX_SUBLAYER)[2]
    return pl.pallas_call(
        functools.partial(_outproj_kernel, blocks_per_seq=blocks_per_seq),
        grid=(rows // tm,),
        in_specs=[pl.BlockSpec((tm, D_MODEL), lambda i: (i, 0)),
                  ys_spec,
                  pl.BlockSpec((tm, D_POOL), lambda i: (i, 0)),
                  gate_spec,
                  pl.BlockSpec((D_MODEL, D_MODEL), lambda i: (0, 0))],
        out_specs=pl.BlockSpec((tm, D_MODEL), lambda i: (i, 0)),
        out_shape=jax.ShapeDtypeStruct((rows, D_MODEL), F32),
        compiler_params=_cparams(1),
        name="outproj",
    )(x, ys, yp, mod, w_out)


def _outproj_sample_kernel(x_ref, ys_ref, yp_ref, gate_ref, w_ref, o_ref, wo_ref):
    wo_ref[...] = w_ref[...].astype(BF16)
    mix = (_dot(ys_ref[...].astype(BF16), wo_ref[0:D_SSM, :])
           + _dot(yp_ref[...].astype(BF16), wo_ref[D_SSM:, :]))
    o_ref[...] = x_ref[...] + gate_ref[...] * mix


def _outproj_sample(x, ys, yp, mod, w_out):
    rows = x.shape[0]
    n_col = D_MODEL // SAMPLE_TN
    gate_col = (3 * MIX_SUBLAYER + 2) * n_col
    return pl.pallas_call(
        _outproj_sample_kernel,
        grid=(n_col,),
        in_specs=[pl.BlockSpec((rows, SAMPLE_TN), lambda j: (0, j)),
                  pl.BlockSpec((rows, D_SSM), lambda j: (0, 0)),
                  pl.BlockSpec((rows, D_POOL), lambda j: (0, 0)),
                  pl.BlockSpec((rows, SAMPLE_TN), lambda j: (0, gate_col + j)),
                  pl.BlockSpec((D_MODEL, SAMPLE_TN), lambda j: (0, j))],
        out_specs=[pl.BlockSpec((rows, SAMPLE_TN), lambda j: (0, j)),
                   pl.BlockSpec((D_MODEL, SAMPLE_TN), lambda j: (0, j))],
        out_shape=[jax.ShapeDtypeStruct((rows, D_MODEL), F32),
                   jax.ShapeDtypeStruct((D_MODEL, D_MODEL), BF16)],
        compiler_params=_cparams(1),
        name="outproj_sample",
    )(x, ys, yp, mod, w_out)


def kernel(x_prompt, x_sample, state_ssm_re, state_ssm_im, state_pool, c_prompt, c_sample, ada_w, ada_b, ffn1_norm, ffn1_w_gate, ffn1_w_up, ffn1_w_down, mix_norm, w_in, ssm_lambda_re, ssm_lambda_im, ssm_log_dt, ssm_b_re, ssm_b_im, ssm_c_re, ssm_c_im, ssm_d, ssm_glu_w, ssm_glu_b, pool_w, pool_b, pool_scale, w_out, ffn2_norm, ffn2_w_gate, ffn2_w_up, ffn2_w_down, final_norm):
    n_p, seq, _ = x_prompt.shape
    n_s = x_sample.shape[0]
    assert (n_p, n_s) == (N_PROMPT, N_SAMPLE) and n_p <= SUBLANES
    G, P = N_SSM_GROUPS, SSM_STATE

    pad = SUBLANES - n_p
    c_all = jnp.concatenate([c_sample, c_prompt, jnp.zeros((pad, D_MODEL), F32)], axis=0)
    mod = _adaln(c_all, ada_w[0], ada_b[0])
    pw = pool_w[0].astype(BF16)
    ops = _s5_prep(ssm_lambda_re[0], ssm_lambda_im[0], ssm_log_dt[0], ssm_b_re[0], ssm_b_im[0],
                   ssm_c_re[0], ssm_c_im[0], ssm_glu_w[0])

    xs = x_sample.reshape(n_s, D_MODEL)
    ffn_s = functools.partial(_ffn, tm=n_s, tf=512, row_chunk=n_s, per_row=True, n_seq=n_s)
    xs, w1g, w1u, w1d = ffn_s(xs, ffn1_norm[0], mod, 0, ffn1_w_gate[0], ffn1_w_up[0],
                              ffn1_w_down[0], final_norm, final_norm=False)
    proj_s, win = _proj_sample(xs, mix_norm[0], mod, w_in[0])
    ys_s, sre_s, sim_s = _s5_sample(proj_s, state_ssm_re[0].reshape(n_s, G * P),
                                    state_ssm_im[0].reshape(n_s, G * P), ops, ssm_d[0],
                                    ssm_glu_b[0])
    hist_t = jnp.swapaxes(state_pool[0], 0, 1)
    yp_s = _pool_sample(hist_t, proj_s, pw, pool_b[0], pool_scale[0])
    xs, wout = _outproj_sample(xs, ys_s, yp_s, mod, w_out[0])
    y_sample, w2g, w2u, w2d = ffn_s(xs, ffn2_norm[0], mod, 2, ffn2_w_gate[0], ffn2_w_up[0],
                                    ffn2_w_down[0], final_norm, final_norm=True)
    v_s = proj_s[:, D_SSM:]

    xp = x_prompt.reshape(n_p * seq, D_MODEL)
    ffn_p = functools.partial(_ffn, tm=1024, tf=512, row_chunk=256, per_row=False, n_seq=n_p)
    xp = ffn_p(xp, ffn1_norm[0], mod, 0, w1g, w1u, w1d, final_norm, final_norm=False)
    u4, v_p = _proj(xp, mix_norm[0], mod, win, tm=512, n_seq=n_p)
    ys_p, sre_p, sim_p = _s5_prompt(u4.reshape(n_p * seq, D_SSM), ops, ssm_d[0], ssm_glu_b[0],
                                    n_seq=n_p)
    yp_p = _pool_prompt(v_p, pw, pool_b[0], pool_scale[0], n_seq=n_p)
    xp = _outproj(xp, ys_p.reshape(seq // T_CHUNK, n_p, T_CHUNK, D_SSM), yp_p, mod, wout,
                  tm=512, n_seq=n_p)
    y_prompt = ffn_p(xp, ffn2_norm[0], mod, 2, w2g, w2u, w2d, final_norm, final_norm=True)

    pool_p = v_p.reshape(n_p, seq, D_POOL)[:, seq - POOL_HIST:, :][None]
    pool_s = jnp.concatenate([state_pool[0][:, 1:, :], v_s[:, None, :]], axis=1)[None]
    return (y_prompt.reshape(n_p, seq, D_MODEL), y_sample.reshape(n_s, 1, D_MODEL),
            sre_p.reshape(1, n_p, G, P), sim_p.reshape(1, n_p, G, P), pool_p,
            sre_s.reshape(1, n_s, G, P), sim_s.reshape(1, n_s, G, P), pool_s)
```

```python
import functools
from typing import NamedTuple

import jax
import jax.numpy as jnp
from jax import lax
from jax.experimental import pallas as pl
from jax.experimental.pallas import tpu as pltpu

F32 = jnp.float32
BF16 = jnp.bfloat16

D_MODEL = 2048
D_FF = 5632
D_SSM = 1024
D_POOL = 1024
SSM_GROUP = 16
SSM_STATE = 64
N_SSM_GROUPS = 64
POOL_WINDOWS = (2, 4, 8, 16)
POOL_GROUP = 256
POOL_HIST = 15
N_MOD_COLS = 9 * D_MODEL
N_PROMPT = 4
N_SAMPLE = 128
EPS = 1e-6
FFN_RES = 0.5

LANES = 128
SUBLANES = 8
MXU_TILE = 256
VMEM_LIMIT = 58 * 1024 * 1024

PROMPT_MOD_BLOCK = N_SAMPLE // SUBLANES
NORM_ROWS = 32
MIX_SUBLAYER = 1

FFN_TM = 1024
FFN_ROW_CHUNK = 512
FFN_TF = 512
FFN_HEAD_TF = 256

OCT = LANES // SSM_GROUP
N_OCT = N_SSM_GROUPS // OCT
T_CHUNK = 8
OCT_STATE = OCT * SSM_STATE
CHUNK_K = T_CHUNK * LANES
TOK_PER_TILE = MXU_TILE // LANES


def _cparams(n_axes):
    return pltpu.CompilerParams(dimension_semantics=("arbitrary",) * n_axes,
                                vmem_limit_bytes=VMEM_LIMIT)


def _dot(a, b):
    return jnp.dot(a, b, preferred_element_type=F32)


ADALN_TN = 1024


def _adaln_kernel(cs_ref, cp_ref, w_ref, b_ref, o_ref):
    w = w_ref[...].astype(BF16)
    n_s, n_p = cs_ref.shape[0], cp_ref.shape[0]

    def mod_of(c):
        return _dot((c * jax.nn.sigmoid(c)).astype(BF16), w) + b_ref[...]

    o_ref[0:n_s, :] = mod_of(cs_ref[...])
    o_ref[n_s:n_s + n_p, :] = mod_of(cp_ref[...])
    o_ref[n_s + n_p:, :] = jnp.zeros((o_ref.shape[0] - n_s - n_p, o_ref.shape[1]), F32)


def _adaln(c_sample, c_prompt, ada_w, ada_b):
    n_s, n_p = c_sample.shape[0], c_prompt.shape[0]
    rows = n_s + SUBLANES
    return pl.pallas_call(
        _adaln_kernel,
        grid=(N_MOD_COLS // ADALN_TN,),
        in_specs=[pl.BlockSpec((n_s, D_MODEL), lambda j: (0, 0)),
                  pl.BlockSpec((n_p, D_MODEL), lambda j: (0, 0)),
                  pl.BlockSpec((D_MODEL, ADALN_TN), lambda j: (0, j)),
                  pl.BlockSpec((1, ADALN_TN), lambda j: (0, j))],
        out_specs=pl.BlockSpec((rows, ADALN_TN), lambda j: (0, j)),
        out_shape=jax.ShapeDtypeStruct((rows, N_MOD_COLS), F32),
        compiler_params=_cparams(1),
        name="adaln",
    )(c_sample, c_prompt, ada_w, ada_b.reshape(1, N_MOD_COLS))


def _norm_mod(x, g, shift, scale):
    ms = jnp.mean(x * x, axis=-1, keepdims=True)
    y = x * lax.rsqrt(ms + EPS) * g
    return y * (1.0 + scale) + shift


def _mod_specs(per_row, tm, prompt_row_block, sub):
    specs = []
    for m in range(3):
        col = 3 * sub + m
        if per_row:
            specs.append(pl.BlockSpec((tm, D_MODEL), lambda i, *_, col=col: (i, col)))
        else:
            specs.append(pl.BlockSpec((SUBLANES, D_MODEL),
                                      lambda i, *_, col=col: (prompt_row_block, col)))
    return specs


def _mod_rows(ref, blocks_per_seq, rows=None):
    if blocks_per_seq:
        return ref[pl.ds(pl.program_id(0) // blocks_per_seq, 1), :]
    return ref[...] if rows is None else ref[rows, :]


class _RowGroup(NamedTuple):
    rows: int
    row_chunk: int
    per_row: bool
    first_block: int
    resident: bool


def _ffn_kernel(*refs, groups, blocks_per_seq, final_norm, emit_bf16, has_alias):
    n = len(groups)
    g_ref, wg_ref, wu_ref, wd_ref, fg_ref = refs[4 * n:4 * n + 5]
    outs = refs[4 * n + 5 + int(has_alias):-1]
    h_ref = refs[-1]
    j = pl.program_id(1)
    last_j = pl.num_programs(1) - 1

    if emit_bf16:
        bf16_refs = outs[n:n + 3]
        for w_ref, wo_ref in zip((wg_ref, wu_ref, wd_ref), bf16_refs):
            wo_ref[...] = w_ref[...].astype(BF16)
        wg_ref, wu_ref, wd_ref = bf16_refs

    chunks, h_base = [], 0
    for k, grp in enumerate(groups):
        chunks += [(k, r, h_base + r) for r in range(0, grp.rows, grp.row_chunk)]
        h_base += grp.rows

    def mod_rows(k, ref, rows):
        if groups[k].per_row:
            return ref[rows, :]
        seq = (pl.program_id(0) + groups[k].first_block) // blocks_per_seq
        return ref[pl.ds(seq, 1), :]

    def norm_chunk(c):
        k, r, hr = chunks[c]
        x_ref, shift_ref, scale_ref, _ = refs[4 * k:4 * k + 4]
        for s in range(0, groups[k].row_chunk, NORM_ROWS):
            sr = pl.ds(r + s, NORM_ROWS)
            h = _norm_mod(x_ref[sr, :], g_ref[...], mod_rows(k, shift_ref, sr),
                          mod_rows(k, scale_ref, sr))
            h_ref[pl.ds(hr + s, NORM_ROWS), :] = h.astype(BF16)

    def chunk(c, first, last):
        k, r, hr = chunks[c]
        size = groups[k].row_chunk
        x_ref, _, _, gate_ref = refs[4 * k:4 * k + 4]
        o_ref = outs[k]
        rows = pl.ds(r, size)
        if first and c + 1 < len(chunks):
            norm_chunk(c + 1)
        h = h_ref[pl.ds(hr, size), :]
        g = _dot(h, wg_ref[...])
        u = _dot(h, wu_ref[...])
        a = (g * jax.nn.sigmoid(g) * u).astype(BF16)
        d = _dot(a, wd_ref[...])
        acc = d if first else o_ref[rows, :] + d
        if not last:
            o_ref[rows, :] = acc
            return
        o_ref[rows, :] = x_ref[rows, :] + FFN_RES * mod_rows(k, gate_ref, rows) * acc
        if final_norm:
            for s in range(0, size, NORM_ROWS):
                sr = pl.ds(r + s, NORM_ROWS)
                y = o_ref[sr, :]
                ms = jnp.mean(y * y, axis=-1, keepdims=True)
                o_ref[sr, :] = y * lax.rsqrt(ms + EPS) * fg_ref[...]

    def run(first, last):
        if first:
            norm_chunk(0)
        for c in range(len(chunks)):
            chunk(c, first, last)

    pl.when(j == 0)(lambda: run(True, False))
    pl.when(jnp.logical_and(j > 0, j < last_j))(lambda: run(False, False))
    pl.when(j == last_j)(lambda: run(False, True))


def _ffn(xs, groups, norm_g, mod, sub, wg, wu, wd, final_g, *, n_blocks, tf, final_norm,
         blocks_per_seq, alias_into=None):
    emit_bf16 = wg.dtype == F32
    assert not emit_bf16 or n_blocks == 1

    def row_spec(grp, col=0):
        mode = dict(pipeline_mode=pl.Buffered(1)) if grp.resident else {}
        return pl.BlockSpec((grp.rows, D_MODEL), lambda i, j: (i + grp.first_block, col), **mode)

    in_specs, operands = [], []
    for x, grp in zip(xs, groups):
        in_specs.append(row_spec(grp))
        operands.append(x)
        for m in range(3):
            col = 3 * sub + m
            if grp.per_row:
                in_specs.append(row_spec(grp, col))
            else:
                in_specs.append(pl.BlockSpec((SUBLANES, D_MODEL),
                                             lambda i, j, col=col: (PROMPT_MOD_BLOCK, col)))
            operands.append(mod)
    w_specs = [pl.BlockSpec((D_MODEL, tf), lambda i, j: (0, j)),
               pl.BlockSpec((D_MODEL, tf), lambda i, j: (0, j)),
               pl.BlockSpec((tf, D_MODEL), lambda i, j: (j, 0))]
    vec_spec = pl.BlockSpec((1, D_MODEL), lambda i, j: (0, 0))
    in_specs += [vec_spec, *w_specs, vec_spec]
    operands += [norm_g.reshape(1, D_MODEL), wg, wu, wd, final_g.reshape(1, D_MODEL)]
    aliases = {}
    if alias_into is not None:
        aliases = {len(operands): len(groups) - 1}
        in_specs.append(pl.BlockSpec(memory_space=pl.ANY))
        operands.append(alias_into)

    out_specs = [row_spec(grp) for grp in groups]
    out_shape = [jax.ShapeDtypeStruct(x.shape, F32) for x in xs]
    if emit_bf16:
        out_specs += w_specs
        out_shape += [jax.ShapeDtypeStruct(w.shape, BF16) for w in (wg, wu, wd)]
    kern = functools.partial(_ffn_kernel, groups=groups, blocks_per_seq=blocks_per_seq,
                             final_norm=final_norm, emit_bf16=emit_bf16,
                             has_alias=alias_into is not None)
    return pl.pallas_call(
        kern,
        grid=(n_blocks, D_FF // tf),
        in_specs=in_specs,
        out_specs=out_specs,
        out_shape=out_shape,
        input_output_aliases=aliases,
        scratch_shapes=[pltpu.VMEM((sum(grp.rows for grp in groups), D_MODEL), BF16)],
        compiler_params=_cparams(2),
        name="ffn",
    )(*operands)


def _proj_kernel(x_ref, g_ref, shift_ref, scale_ref, w_ref, u_ref, v_ref, *, blocks_per_seq):
    sh = _mod_rows(shift_ref, blocks_per_seq)
    sc = _mod_rows(scale_ref, blocks_per_seq)
    h = _norm_mod(x_ref[...], g_ref[...], sh, sc).astype(BF16)
    p = _dot(h, w_ref[...])
    u_ref[...] = p[:, :D_SSM].reshape(u_ref.shape)
    v_ref[...] = p[:, D_SSM:]


def _proj(x, norm_g, mod, w_in, *, tm, n_seq):
    rows = x.shape[0]
    seq_len = rows // n_seq
    blocks_per_seq = seq_len // tm
    u_shape = (seq_len // T_CHUNK, n_seq, T_CHUNK, D_SSM)
    u_spec = pl.BlockSpec((tm // T_CHUNK, None, T_CHUNK, D_SSM),
                          lambda i: (i % blocks_per_seq, i // blocks_per_seq, 0, 0))
    shift_spec, scale_spec, _ = _mod_specs(False, tm, PROMPT_MOD_BLOCK, MIX_SUBLAYER)
    return pl.pallas_call(
        functools.partial(_proj_kernel, blocks_per_seq=blocks_per_seq),
        grid=(rows // tm,),
        in_specs=[pl.BlockSpec((tm, D_MODEL), lambda i: (i, 0)),
                  pl.BlockSpec((1, D_MODEL), lambda i: (0, 0)),
                  shift_spec, scale_spec,
                  pl.BlockSpec((D_MODEL, D_MODEL), lambda i: (0, 0))],
        out_specs=[u_spec, pl.BlockSpec((tm, D_POOL), lambda i: (i, 0))],
        out_shape=[jax.ShapeDtypeStruct(u_shape, F32),
                   jax.ShapeDtypeStruct((rows, D_POOL), F32)],
        compiler_params=_cparams(1),
        name="proj",
    )(x, norm_g.reshape(1, D_MODEL), mod, mod, w_in)


SAMPLE_TN = 512


def _proj_sample_kernel(x_ref, g_ref, shift_ref, scale_ref, w_ref, p_ref, wo_ref, h_ref):
    @pl.when(pl.program_id(0) == 0)
    def _():
        h = _norm_mod(x_ref[...], g_ref[...], shift_ref[...], scale_ref[...])
        h_ref[...] = h.astype(BF16)
    wo_ref[...] = w_ref[...].astype(BF16)
    p_ref[...] = _dot(h_ref[...], wo_ref[...])


def _proj_sample(x, norm_g, mod, w_in):
    rows = x.shape[0]
    shift_col, scale_col = 3 * MIX_SUBLAYER, 3 * MIX_SUBLAYER + 1
    return pl.pallas_call(
        _proj_sample_kernel,
        grid=(D_MODEL // SAMPLE_TN,),
        in_specs=[pl.BlockSpec((rows, D_MODEL), lambda j: (0, 0)),
                  pl.BlockSpec((1, D_MODEL), lambda j: (0, 0)),
                  pl.BlockSpec((rows, D_MODEL), lambda j: (0, shift_col)),
                  pl.BlockSpec((rows, D_MODEL), lambda j: (0, scale_col)),
                  pl.BlockSpec((D_MODEL, SAMPLE_TN), lambda j: (0, j))],
        out_specs=[pl.BlockSpec((rows, SAMPLE_TN), lambda j: (0, j)),
                   pl.BlockSpec((D_MODEL, SAMPLE_TN), lambda j: (0, j))],
        out_shape=[jax.ShapeDtypeStruct((rows, D_MODEL), F32),
                   jax.ShapeDtypeStruct((D_MODEL, D_MODEL), BF16)],
        scratch_shapes=[pltpu.VMEM((rows, D_MODEL), BF16)],
        compiler_params=_cparams(1),
        name="proj_sample",
    )(x, norm_g.reshape(1, D_MODEL), mod, mod, w_in)


def _discretise(lam_re, lam_im, log_dt):
    lr = jnp.minimum(lam_re, -1e-4)
    li = lam_im
    dt = jnp.exp(log_dt)
    mag = jnp.exp(lr * dt)
    ang = li * dt
    a_re = mag * jnp.cos(ang)
    a_im = mag * jnp.sin(ang)
    den = lr * lr + li * li
    num_re = a_re - 1.0
    f_re = (num_re * lr + a_im * li) / den
    f_im = (a_im * lr - num_re * li) / den
    return a_re, a_im, f_re, f_im


def _s5_prep_kernel(lam_re1, lam_im1, ldt1, c_re1, c_im1,
                    lam_re2, lam_im2, ldt2, b_re2, b_im2, gluw_t,
                    toep_ref, wz_ref, m_ref, apow_ref, gluw_ref):
    g1 = lax.broadcasted_iota(jnp.int32, (SSM_STATE, LANES), 1) // SSM_GROUP
    g2 = lax.broadcasted_iota(jnp.int32, (SSM_GROUP, OCT_STATE), 1) // SSM_STATE

    def expand1(x):
        return jnp.concatenate([jnp.where(g1 == g, x, 0.0) for g in range(OCT)], axis=0)

    def expand2(x):
        return jnp.concatenate([jnp.where(g2 == g, x, 0.0) for g in range(OCT)], axis=0)

    def split(x):
        hi = x.astype(BF16)
        return hi, (x - hi.astype(F32)).astype(BF16)

    a1_re, a1_im, _, _ = _discretise(lam_re1[...], lam_im1[...], ldt1[...])
    a2_re, a2_im, f_re, f_im = _discretise(lam_re2[...], lam_im2[...], ldt2[...])
    br, bi = b_re2[...], b_im2[...]
    bbar_re = f_re * br - f_im * bi
    bbar_im = f_re * bi + f_im * br
    cr, ci = c_re1[...], c_im1[...]
    bre_hi, bre_lo = split(expand2(bbar_re))
    bim_hi, bim_lo = split(expand2(bbar_im))

    def dot3(x_hi, x_lo, y_hi, y_lo):
        return _dot(x_hi, y_hi) + (_dot(x_hi, y_lo) + _dot(x_lo, y_hi))

    p1_re, p1_im = jnp.ones_like(a1_re), jnp.zeros_like(a1_im)
    p2_re, p2_im = jnp.ones_like(a2_re), jnp.zeros_like(a2_im)
    kk = []
    zero_blk = jnp.zeros((LANES, LANES), F32)
    for j in range(T_CHUNK + 1):
        care_hi, care_lo = split(expand1(cr * p1_re - ci * p1_im))
        caim_hi, caim_lo = split(expand1(cr * p1_im + ci * p1_re))
        m_ref[0:OCT_STATE, j * LANES:(j + 1) * LANES] = care_hi
        m_ref[OCT_STATE:2 * OCT_STATE, j * LANES:(j + 1) * LANES] = -caim_hi
        if j < T_CHUNK:
            t = T_CHUNK - 1 - j
            ba_re = expand2(bbar_re * p2_re - bbar_im * p2_im)
            ba_im = expand2(bbar_re * p2_im + bbar_im * p2_re)
            wz_ref[t * LANES:(t + 1) * LANES, 0:OCT_STATE] = ba_re.astype(BF16)
            wz_ref[t * LANES:(t + 1) * LANES, OCT_STATE:2 * OCT_STATE] = ba_im.astype(BF16)
            kk.append(dot3(bre_hi, bre_lo, care_hi, care_lo)
                      - dot3(bim_hi, bim_lo, caim_hi, caim_lo))
        if j == 1:
            apow_ref[0:1, :] = p2_re
            apow_ref[1:2, :] = p2_im
        if j == T_CHUNK:
            apow_ref[2:3, :] = p2_re
            apow_ref[3:4, :] = p2_im
        p1_re, p1_im = p1_re * a1_re - p1_im * a1_im, p1_re * a1_im + p1_im * a1_re
        p2_re, p2_im = p2_re * a2_re - p2_im * a2_im, p2_re * a2_im + p2_im * a2_re

    for t in range(T_CHUNK):
        for t2 in range(T_CHUNK):
            blk = kk[t2 - t] if t2 >= t else zero_blk
            toep_ref[t * LANES:(t + 1) * LANES, t2 * LANES:(t2 + 1) * LANES] = blk.astype(BF16)

    rg = lax.broadcasted_iota(jnp.int32, (LANES, LANES), 0) // SSM_GROUP
    lg = lax.broadcasted_iota(jnp.int32, (LANES, LANES), 1) // SSM_GROUP
    gw = jnp.where(rg == lg, gluw_t[...], 0.0).astype(BF16)
    gz = jnp.zeros((LANES, LANES), BF16)
    for a in range(TOK_PER_TILE):
        for b in range(TOK_PER_TILE):
            gluw_ref[a * LANES:(a + 1) * LANES, b * LANES:(b + 1) * LANES] = gw if a == b else gz


def _s5_prep(lam_re, lam_im, log_dt, b_re, b_im, c_re, c_im, glu_w):
    G, P, H = N_SSM_GROUPS, SSM_STATE, SSM_GROUP
    row = lambda a: a.reshape(N_OCT, 1, OCT_STATE)
    ldt = jnp.broadcast_to(log_dt[:, None], (G, P))

    def col(a_gp):
        a = jnp.swapaxes(a_gp.reshape(N_OCT, OCT, P), 1, 2)
        return jnp.broadcast_to(a[..., None], (N_OCT, P, OCT, H)).reshape(N_OCT, P, LANES)

    lay1 = lambda a_ghp: jnp.swapaxes(a_ghp.reshape(N_OCT, LANES, P), 1, 2)
    lay2 = lambda a_gph: jnp.swapaxes(a_gph.reshape(N_OCT, OCT_STATE, H), 1, 2)
    c_re1, c_im1, b_re2, b_im2 = lay1(c_re), lay1(c_im), lay2(b_re), lay2(b_im)
    gw = glu_w.reshape(N_OCT, OCT, H, 1, H)
    gluw_t = jnp.broadcast_to(gw, (N_OCT, OCT, H, OCT, H)).reshape(N_OCT, LANES, LANES)

    def spec(shape):
        return pl.BlockSpec((None,) + shape, lambda o: (o,) + (0,) * len(shape))

    m_cols = (T_CHUNK + 1) * LANES
    return pl.pallas_call(
        _s5_prep_kernel,
        grid=(N_OCT,),
        in_specs=[spec((SSM_STATE, LANES))] * 5
                 + [spec((1, OCT_STATE))] * 3 + [spec((SSM_GROUP, OCT_STATE))] * 2
                 + [spec((LANES, LANES))],
        out_specs=[spec((CHUNK_K, CHUNK_K)), spec((CHUNK_K, 2 * OCT_STATE)),
                   spec((2 * OCT_STATE, m_cols)), spec((4, OCT_STATE)),
                   spec((MXU_TILE, MXU_TILE))],
        out_shape=[jax.ShapeDtypeStruct((N_OCT, CHUNK_K, CHUNK_K), BF16),
                   jax.ShapeDtypeStruct((N_OCT, CHUNK_K, 2 * OCT_STATE), BF16),
                   jax.ShapeDtypeStruct((N_OCT, 2 * OCT_STATE, m_cols), BF16),
                   jax.ShapeDtypeStruct((N_OCT, 4, OCT_STATE), F32),
                   jax.ShapeDtypeStruct((N_OCT, MXU_TILE, MXU_TILE), BF16)],
        compiler_params=_cparams(1),
        name="s5_prep",
    )(col(lam_re), col(lam_im), col(ldt), c_re1, c_im1,
      row(lam_re), row(lam_im), row(ldt), b_re2, b_im2, gluw_t)


def _glu_out(y, gluw, glub):
    gy = jax.nn.gelu(y, approximate=True)
    return gy * jax.nn.sigmoid(_dot(gy.astype(BF16), gluw) + glub)


def _s5_prompt_kernel(x_ref, toep_ref, wz_ref, m_ref, apow_ref, d_ref, gluw_ref, glub_ref,
                      y_ref, sre_ref, sim_ref, xr_ref, z_ref, yv_ref, *, n_seq, row_chunk):
    n_rows = x_ref.shape[0] // T_CHUNK
    n_blk = n_rows // row_chunk

    for t in range(T_CHUNK):
        xr_ref[:, t * LANES:(t + 1) * LANES] = x_ref[pl.ds(t, n_rows, stride=T_CHUNK), :].astype(BF16)

    def local(b, carry):
        rows = pl.ds(pl.multiple_of(b * row_chunk, row_chunk), row_chunk)
        xr = xr_ref[rows, :]
        z_ref[rows, :] = _dot(xr, wz_ref[...])
        for nt in range(CHUNK_K // MXU_TILE):
            k_end = (nt + 1) * MXU_TILE
            cols = slice(nt * MXU_TILE, k_end)
            yv_ref[rows, cols] = _dot(xr[:, 0:k_end], toep_ref[0:k_end, cols])
        return carry
    lax.fori_loop(0, n_blk, local, 0)

    are, aim = apow_ref[2:3, :], apow_ref[3:4, :]
    lo = lax.broadcasted_iota(jnp.int32, (SUBLANES, OCT_STATE), 0) < n_seq

    def step(k, carry):
        pre, pim = carry
        rows = pl.ds(pl.multiple_of(k * SUBLANES, SUBLANES), SUBLANES)
        zre = z_ref[rows, 0:OCT_STATE]
        zim = z_ref[rows, OCT_STATE:2 * OCT_STATE]
        w1re = are * pre - aim * pim + zre
        w1im = are * pim + aim * pre + zim
        r1re = pltpu.roll(w1re, n_seq, axis=0)
        r1im = pltpu.roll(w1im, n_seq, axis=0)
        w2re = are * r1re - aim * r1im + zre
        w2im = are * r1im + aim * r1re + zim
        z_ref[rows, 0:OCT_STATE] = jnp.where(lo, pre, r1re)
        z_ref[rows, OCT_STATE:2 * OCT_STATE] = jnp.where(lo, pim, r1im)
        nre = jnp.where(lo, pltpu.roll(w2re, n_seq, axis=0), w2re)
        nim = jnp.where(lo, pltpu.roll(w2im, n_seq, axis=0), w2im)
        return nre, nim

    zeros = jnp.zeros((SUBLANES, OCT_STATE), F32)
    fre, fim = lax.fori_loop(0, n_rows // SUBLANES, step, (zeros, zeros))
    sre_ref[...] = fre[0:n_seq, :]
    sim_ref[...] = fim[0:n_seq, :]

    d2 = jnp.concatenate([d_ref[...]] * TOK_PER_TILE, axis=1)
    glub2 = jnp.concatenate([glub_ref[...]] * TOK_PER_TILE, axis=1)

    def readout(b, carry):
        rows = pl.ds(pl.multiple_of(b * row_chunk, row_chunk), row_chunk)
        y = yv_ref[rows, :] + _dot(z_ref[rows, :].astype(BF16), m_ref[:, LANES:])
        for t0 in range(0, T_CHUNK, TOK_PER_TILE):
            toks = [pl.ds(b * (row_chunk * T_CHUNK) + t0 + i, row_chunk, stride=T_CHUNK)
                    for i in range(TOK_PER_TILE)]
            u = jnp.concatenate([x_ref[tok, :] for tok in toks], axis=1)
            yt = y[:, t0 * LANES:(t0 + TOK_PER_TILE) * LANES] + d2 * u
            out = _glu_out(yt, gluw_ref[...], glub2)
            for i, tok in enumerate(toks):
                y_ref[tok, :] = out[:, i * LANES:(i + 1) * LANES]
        return carry
    lax.fori_loop(0, n_blk, readout, 0)


def _s5_prompt(u_flat, ops, d_skip, glu_b, *, n_seq):
    toep, wz, m, apow, gluw = ops
    rows = u_flat.shape[0]
    n_rows = rows // T_CHUNK
    oct_spec = lambda shape: pl.BlockSpec((None,) + shape, lambda o: (o, 0, 0))
    kern = functools.partial(_s5_prompt_kernel, n_seq=n_seq, row_chunk=256)
    return pl.pallas_call(
        kern,
        grid=(N_OCT,),
        in_specs=[pl.BlockSpec((rows, LANES), lambda o: (0, o)),
                  oct_spec(toep.shape[1:]), oct_spec(wz.shape[1:]), oct_spec(m.shape[1:]),
                  oct_spec(apow.shape[1:]), oct_spec((1, LANES)), oct_spec(gluw.shape[1:]),
                  oct_spec((1, LANES))],
        out_specs=[pl.BlockSpec((rows, LANES), lambda o: (0, o)),
                   pl.BlockSpec((n_seq, OCT_STATE), lambda o: (0, o)),
                   pl.BlockSpec((n_seq, OCT_STATE), lambda o: (0, o))],
        out_shape=[jax.ShapeDtypeStruct((rows, D_SSM), F32),
                   jax.ShapeDtypeStruct((n_seq, N_OCT * OCT_STATE), F32),
                   jax.ShapeDtypeStruct((n_seq, N_OCT * OCT_STATE), F32)],
        scratch_shapes=[pltpu.VMEM((n_rows, CHUNK_K), BF16),
                        pltpu.VMEM((n_rows, 2 * OCT_STATE), F32),
                        pltpu.VMEM((n_rows, CHUNK_K), F32)],
        compiler_params=_cparams(1),
        name="s5_prompt",
    )(u_flat, toep, wz, m, apow, d_skip.reshape(N_OCT, 1, LANES), gluw,
      glu_b.reshape(N_OCT, 1, LANES))


def _s5_sample_kernel(u_ref, s0re_ref, s0im_ref, wz_ref, m_ref, apow_ref, d_ref, gluw_ref,
                      glub_ref, y_ref, sre_ref, sim_ref):
    u = u_ref[...]
    z = _dot(u.astype(BF16), wz_ref[...])
    are, aim = apow_ref[0:1, :], apow_ref[1:2, :]
    s0re, s0im = s0re_ref[...], s0im_ref[...]
    nre = are * s0re - aim * s0im + z[:, 0:OCT_STATE]
    nim = are * s0im + aim * s0re + z[:, OCT_STATE:2 * OCT_STATE]
    sre_ref[...] = nre
    sim_ref[...] = nim
    y = (_dot(nre.astype(BF16), m_ref[0:OCT_STATE, :])
         + _dot(nim.astype(BF16), m_ref[OCT_STATE:2 * OCT_STATE, :])
         + d_ref[...] * u)
    y_ref[...] = _glu_out(y, gluw_ref[...], glub_ref[...])


def _s5_sample(u, s0_re, s0_im, ops, d_skip, glu_b):
    _, wz, m, apow, gluw = ops
    rows = u.shape[0]
    oct_spec = lambda shape: pl.BlockSpec((None,) + shape, lambda o: (o, 0, 0))
    return pl.pallas_call(
        _s5_sample_kernel,
        grid=(N_OCT,),
        in_specs=[pl.BlockSpec((rows, LANES), lambda o: (0, o)),
                  pl.BlockSpec((rows, OCT_STATE), lambda o: (0, o)),
                  pl.BlockSpec((rows, OCT_STATE), lambda o: (0, o)),
                  pl.BlockSpec((None, LANES, 2 * OCT_STATE), lambda o: (o, T_CHUNK - 1, 0)),
                  pl.BlockSpec((None, 2 * OCT_STATE, LANES), lambda o: (o, 0, 0)),
                  oct_spec(apow.shape[1:]), oct_spec((1, LANES)), oct_spec((LANES, LANES)),
                  oct_spec((1, LANES))],
        out_specs=[pl.BlockSpec((rows, LANES), lambda o: (0, o)),
                   pl.BlockSpec((rows, OCT_STATE), lambda o: (0, o)),
                   pl.BlockSpec((rows, OCT_STATE), lambda o: (0, o))],
        out_shape=[jax.ShapeDtypeStruct((rows, D_SSM), F32),
                   jax.ShapeDtypeStruct((rows, N_OCT * OCT_STATE), F32),
                   jax.ShapeDtypeStruct((rows, N_OCT * OCT_STATE), F32)],
        compiler_params=_cparams(1),
        name="s5_sample",
    )(u, s0_re, s0_im, wz, m, apow, d_skip.reshape(N_OCT, 1, LANES), gluw,
      glu_b.reshape(N_OCT, 1, LANES))


def _pool_linear(z, gi, pw_ref, pb_ref, ps_ref):
    lanes = slice(gi * POOL_GROUP, (gi + 1) * POOL_GROUP)
    return (_dot(z.astype(BF16), pw_ref[gi]) + pb_ref[:, lanes]) * ps_ref[:, lanes]


def _pool_prompt_kernel(v_ref, pw_ref, pb_ref, ps_ref, y_ref):
    n = v_ref.shape[0]
    pos = lax.broadcasted_iota(jnp.int32, (n, POOL_GROUP), 0)
    for gi, w in enumerate(POOL_WINDOWS):
        lanes = slice(gi * POOL_GROUP, (gi + 1) * POOL_GROUP)
        v = v_ref[:, lanes]
        s, k = v, 1
        while k < w:
            s = s + jnp.where(pos >= k, pltpu.roll(s, k, axis=0), 0.0)
            k *= 2
        cnt = jnp.clip(pos + 1, 1, w).astype(F32)
        y_ref[:, lanes] = _pool_linear(s / cnt - v, gi, pw_ref, pb_ref, ps_ref)


def _pool_prompt(v, pool_w, pool_b, pool_scale, *, n_seq):
    rows = v.shape[0]
    seq_len = rows // n_seq
    return pl.pallas_call(
        _pool_prompt_kernel,
        grid=(n_seq,),
        in_specs=[pl.BlockSpec((seq_len, D_POOL), lambda n: (n, 0)),
                  pl.BlockSpec(pool_w.shape, lambda n: (0, 0, 0)),
                  pl.BlockSpec((1, D_POOL), lambda n: (0, 0)),
                  pl.BlockSpec((1, D_POOL), lambda n: (0, 0))],
        out_specs=pl.BlockSpec((seq_len, D_POOL), lambda n: (n, 0)),
        out_shape=jax.ShapeDtypeStruct((rows, D_POOL), F32),
        compiler_params=_cparams(1),
        name="pool_prompt",
    )(v, pool_w, pool_b.reshape(1, D_POOL), pool_scale.reshape(1, D_POOL))


def _pool_sample_kernel(hist_ref, v_ref, pw_ref, pb_ref, ps_ref, y_ref):
    for gi, w in enumerate(POOL_WINDOWS):
        lanes = slice(gi * POOL_GROUP, (gi + 1) * POOL_GROUP)
        v = v_ref[:, lanes]
        s = v
        for r in range(POOL_HIST - (w - 1), POOL_HIST):
            s = s + hist_ref[r, :, lanes]
        y_ref[:, lanes] = _pool_linear(s / float(w) - v, gi, pw_ref, pb_ref, ps_ref)


def _pool_sample(hist_t, proj, pool_w, pool_b, pool_scale):
    rows = proj.shape[0]
    v = proj
    return pl.pallas_call(
        _pool_sample_kernel,
        grid=(1,),
        in_specs=[pl.BlockSpec(hist_t.shape, lambda i: (0, 0, 0)),
                  pl.BlockSpec((rows, D_POOL), lambda i: (0, D_SSM // D_POOL)),
                  pl.BlockSpec(pool_w.shape, lambda i: (0, 0, 0)),
                  pl.BlockSpec((1, D_POOL), lambda i: (0, 0)),
                  pl.BlockSpec((1, D_POOL), lambda i: (0, 0))],
        out_specs=pl.BlockSpec((rows, D_POOL), lambda i: (0, 0)),
        out_shape=jax.ShapeDtypeStruct((rows, D_POOL), F32),
        compiler_params=_cparams(1),
        name="pool_sample",
    )(hist_t, v, pool_w, pool_b.reshape(1, D_POOL), pool_scale.reshape(1, D_POOL))


def _outproj_kernel(x_ref, ys_ref, yp_ref, gate_ref, w_ref, o_ref, *, blocks_per_seq):
    tm = x_ref.shape[0]
    ys = ys_ref[...].reshape(tm, D_SSM).astype(BF16)
    yp = yp_ref[...].astype(BF16)
    mix = _dot(ys, w_ref[0:D_SSM, :]) + _dot(yp, w_ref[D_SSM:, :])
    o_ref[...] = x_ref[...] + _mod_rows(gate_ref, blocks_per_seq) * mix


def _outproj(x, ys, yp, mod, w_out, *, tm, n_seq):
    rows = x.shape[0]
    blocks_per_seq = (rows // n_seq) // tm
    ys_spec = pl.BlockSpec((tm // T_CHUNK, None, T_CHUNK, D_SSM),
                           lambda i: (i % blocks_per_seq, i // blocks_per_seq, 0, 0))
    gate_spec = _mod_specs(False, tm, PROMPT_MOD_BLOCK, MIX_SUBLAYER)[2]
    return pl.pallas_call(
        functools.partial(_outproj_kernel, blocks_per_seq=blocks_per_seq),
        grid=(rows // tm,),
        in_specs=[pl.BlockSpec((tm, D_MODEL), lambda i: (i, 0)),
                  ys_spec,
                  pl.BlockSpec((tm, D_POOL), lambda i: (i, 0)),
                  gate_spec,
                  pl.BlockSpec((D_MODEL, D_MODEL), lambda i: (0, 0))],
        out_specs=pl.BlockSpec((tm, D_MODEL), lambda i: (i, 0)),
        out_shape=jax.ShapeDtypeStruct((rows, D_MODEL), F32),
        compiler_params=_cparams(1),
        name="outproj",
    )(x, ys, yp, mod, w_out)


def _outproj_sample_kernel(x_ref, ys_ref, yp_ref, gate_ref, w_ref, o_ref, wo_ref):
    wo_ref[...] = w_ref[...].astype(BF16)
    mix = (_dot(ys_ref[...].astype(BF16), wo_ref[0:D_SSM, :])
           + _dot(yp_ref[...].astype(BF16), wo_ref[D_SSM:, :]))
    o_ref[...] = x_ref[...] + gate_ref[...] * mix


def _outproj_sample(x, ys, yp, mod, w_out):
    rows = x.shape[0]
    n_col = D_MODEL // SAMPLE_TN
    gate_col = (3 * MIX_SUBLAYER + 2) * n_col
    return pl.pallas_call(
        _outproj_sample_kernel,
        grid=(n_col,),
        in_specs=[pl.BlockSpec((rows, SAMPLE_TN), lambda j: (0, j)),
                  pl.BlockSpec((rows, D_SSM), lambda j: (0, 0)),
                  pl.BlockSpec((rows, D_POOL), lambda j: (0, 0)),
                  pl.BlockSpec((rows, SAMPLE_TN), lambda j: (0, gate_col + j)),
                  pl.BlockSpec((D_MODEL, SAMPLE_TN), lambda j: (0, j))],
        out_specs=[pl.BlockSpec((rows, SAMPLE_TN), lambda j: (0, j)),
                   pl.BlockSpec((D_MODEL, SAMPLE_TN), lambda j: (0, j))],
        out_shape=[jax.ShapeDtypeStruct((rows, D_MODEL), F32),
                   jax.ShapeDtypeStruct((D_MODEL, D_MODEL), BF16)],
        compiler_params=_cparams(1),
        name="outproj_sample",
    )(x, ys, yp, mod, w_out)


def kernel(x_prompt, x_sample, state_ssm_re, state_ssm_im, state_pool, c_prompt, c_sample, ada_w, ada_b, ffn1_norm, ffn1_w_gate, ffn1_w_up, ffn1_w_down, mix_norm, w_in, ssm_lambda_re, ssm_lambda_im, ssm_log_dt, ssm_b_re, ssm_b_im, ssm_c_re, ssm_c_im, ssm_d, ssm_glu_w, ssm_glu_b, pool_w, pool_b, pool_scale, w_out, ffn2_norm, ffn2_w_gate, ffn2_w_up, ffn2_w_down, final_norm):
    n_p, seq, _ = x_prompt.shape
    n_s = x_sample.shape[0]
    assert (n_p, n_s) == (N_PROMPT, N_SAMPLE) and n_p <= SUBLANES
    G, P = N_SSM_GROUPS, SSM_STATE

    mod = _adaln(c_sample, c_prompt, ada_w[0], ada_b[0])
    pw = pool_w[0].astype(BF16)
    ops = _s5_prep(ssm_lambda_re[0], ssm_lambda_im[0], ssm_log_dt[0], ssm_b_re[0], ssm_b_im[0],
                   ssm_c_re[0], ssm_c_im[0], ssm_glu_w[0])

    xs = x_sample.reshape(n_s, D_MODEL)
    xp = x_prompt.reshape(n_p * seq, D_MODEL)
    blocks_per_seq = seq // FFN_TM
    sample_rows = _RowGroup(n_s, n_s, per_row=True, first_block=0, resident=True)
    head_rows = _RowGroup(FFN_TM, FFN_ROW_CHUNK, per_row=False, first_block=0, resident=True)
    tail_rows = _RowGroup(FFN_TM, FFN_ROW_CHUNK, per_row=False, first_block=1, resident=False)

    def ffn(xs, xp, norm_g, sub, wg, wu, wd, last):
        ys, yp, bg, bu, bd = _ffn([xs, xp], (sample_rows, head_rows), norm_g, mod, sub, wg, wu, wd,
                                  final_norm, n_blocks=1, tf=FFN_HEAD_TF, final_norm=last,
                                  blocks_per_seq=blocks_per_seq)
        (yp,) = _ffn([xp], (tail_rows,), norm_g, mod, sub, bg, bu, bd, final_norm,
                     n_blocks=n_p * blocks_per_seq - 1, tf=FFN_TF, final_norm=last,
                     blocks_per_seq=blocks_per_seq, alias_into=yp)
        return ys, yp

    xs, xp = ffn(xs, xp, ffn1_norm[0], 0, ffn1_w_gate[0], ffn1_w_up[0], ffn1_w_down[0], False)

    proj_s, win = _proj_sample(xs, mix_norm[0], mod, w_in[0])
    ys_s, sre_s, sim_s = _s5_sample(proj_s, state_ssm_re[0].reshape(n_s, G * P),
                                    state_ssm_im[0].reshape(n_s, G * P), ops, ssm_d[0],
                                    ssm_glu_b[0])
    hist_t = jnp.swapaxes(state_pool[0], 0, 1)
    yp_s = _pool_sample(hist_t, proj_s, pw, pool_b[0], pool_scale[0])
    xs, wout = _outproj_sample(xs, ys_s, yp_s, mod, w_out[0])
    v_s = proj_s[:, D_SSM:]

    u4, v_p = _proj(xp, mix_norm[0], mod, win, tm=512, n_seq=n_p)
    ys_p, sre_p, sim_p = _s5_prompt(u4.reshape(n_p * seq, D_SSM), ops, ssm_d[0], ssm_glu_b[0],
                                    n_seq=n_p)
    yp_p = _pool_prompt(v_p, pw, pool_b[0], pool_scale[0], n_seq=n_p)
    xp = _outproj(xp, ys_p.reshape(seq // T_CHUNK, n_p, T_CHUNK, D_SSM), yp_p, mod, wout,
                  tm=512, n_seq=n_p)

    y_sample, y_prompt = ffn(xs, xp, ffn2_norm[0], 2, ffn2_w_gate[0], ffn2_w_up[0],
                             ffn2_w_down[0], True)

    pool_p = v_p.reshape(n_p, seq, D_POOL)[:, seq - POOL_HIST:, :][None]
    pool_s = jnp.concatenate([state_pool[0][:, 1:, :], v_s[:, None, :]], axis=1)[None]
    return (y_prompt.reshape(n_p, seq, D_MODEL), y_sample.reshape(n_s, 1, D_MODEL),
            sre_p.reshape(1, n_p, G, P), sim_p.reshape(1, n_p, G, P), pool_p,
            sre_s.reshape(1, n_s, G, P), sim_s.reshape(1, n_s, G, P), pool_s)
```

```python
import functools
from typing import NamedTuple

import jax
import jax.numpy as jnp
from jax import lax
from jax.experimental import pallas as pl
from jax.experimental.pallas import tpu as pltpu

F32 = jnp.float32
BF16 = jnp.bfloat16

D_MODEL = 2048
D_FF = 5632
D_SSM = 1024
D_POOL = 1024
SSM_GROUP = 16
SSM_STATE = 64
N_SSM_GROUPS = 64
POOL_WINDOWS = (2, 4, 8, 16)
POOL_GROUP = 256
POOL_HIST = 15
N_MOD_COLS = 9 * D_MODEL
N_PROMPT = 4
N_SAMPLE = 128
EPS = 1e-6
FFN_RES = 0.5

LANES = 128
SUBLANES = 8
MXU_TILE = 256
VMEM_LIMIT = 58 * 1024 * 1024

PROMPT_MOD_BLOCK = N_SAMPLE // SUBLANES
NORM_ROWS = 32
MIX_SUBLAYER = 1

FFN_TM = 1024
FFN_ROW_CHUNK = 512
FFN_TF = 512
FFN_HEAD_TF = 256
MIX_TM = 512

OCT = LANES // SSM_GROUP
N_OCT = N_SSM_GROUPS // OCT
T_CHUNK = 8
OCT_STATE = OCT * SSM_STATE
CHUNK_K = T_CHUNK * LANES
TOK_PER_TILE = MXU_TILE // LANES


def _cparams(n_axes):
    return pltpu.CompilerParams(dimension_semantics=("arbitrary",) * n_axes,
                                vmem_limit_bytes=VMEM_LIMIT)


def _dot(a, b):
    return jnp.dot(a, b, preferred_element_type=F32)


ADALN_TN = 1024


def _adaln_kernel(cs_ref, cp_ref, w_ref, b_ref, o_ref):
    w = w_ref[...].astype(BF16)
    n_s, n_p = cs_ref.shape[0], cp_ref.shape[0]

    def mod_of(c):
        return _dot((c * jax.nn.sigmoid(c)).astype(BF16), w) + b_ref[...]

    o_ref[0:n_s, :] = mod_of(cs_ref[...])
    o_ref[n_s:n_s + n_p, :] = mod_of(cp_ref[...])
    o_ref[n_s + n_p:, :] = jnp.zeros((o_ref.shape[0] - n_s - n_p, o_ref.shape[1]), F32)


def _adaln(c_sample, c_prompt, ada_w, ada_b):
    n_s, n_p = c_sample.shape[0], c_prompt.shape[0]
    rows = n_s + SUBLANES
    return pl.pallas_call(
        _adaln_kernel,
        grid=(N_MOD_COLS // ADALN_TN,),
        in_specs=[pl.BlockSpec((n_s, D_MODEL), lambda j: (0, 0)),
                  pl.BlockSpec((n_p, D_MODEL), lambda j: (0, 0)),
                  pl.BlockSpec((D_MODEL, ADALN_TN), lambda j: (0, j)),
                  pl.BlockSpec((1, ADALN_TN), lambda j: (0, j))],
        out_specs=pl.BlockSpec((rows, ADALN_TN), lambda j: (0, j)),
        out_shape=jax.ShapeDtypeStruct((rows, N_MOD_COLS), F32),
        compiler_params=_cparams(1),
        name="adaln",
    )(c_sample, c_prompt, ada_w, ada_b.reshape(1, N_MOD_COLS))


def _norm_mod(x, g, shift, scale):
    ms = jnp.mean(x * x, axis=-1, keepdims=True)
    y = x * lax.rsqrt(ms + EPS) * g
    return y * (1.0 + scale) + shift


def _mod_specs(per_row, tm, prompt_row_block, sub):
    specs = []
    for m in range(3):
        col = 3 * sub + m
        if per_row:
            specs.append(pl.BlockSpec((tm, D_MODEL), lambda i, *_, col=col: (i, col)))
        else:
            specs.append(pl.BlockSpec((SUBLANES, D_MODEL),
                                      lambda i, *_, col=col: (prompt_row_block, col)))
    return specs


def _mod_rows(ref, blocks_per_seq, rows=None):
    if blocks_per_seq:
        return ref[pl.ds(pl.program_id(0) // blocks_per_seq, 1), :]
    return ref[...] if rows is None else ref[rows, :]


class _RowGroup(NamedTuple):
    rows: int
    row_chunk: int
    per_row: bool
    first_block: int
    resident: bool


def _ffn_kernel(*refs, groups, blocks_per_seq, final_norm, emit_bf16, has_alias):
    n = len(groups)
    g_ref, wg_ref, wu_ref, wd_ref, fg_ref = refs[4 * n:4 * n + 5]
    outs = refs[4 * n + 5 + int(has_alias):-1]
    h_ref = refs[-1]
    j = pl.program_id(1)
    last_j = pl.num_programs(1) - 1

    if emit_bf16:
        bf16_refs = outs[n:n + 3]
        for w_ref, wo_ref in zip((wg_ref, wu_ref, wd_ref), bf16_refs):
            wo_ref[...] = w_ref[...].astype(BF16)
        wg_ref, wu_ref, wd_ref = bf16_refs

    chunks, h_base = [], 0
    for k, grp in enumerate(groups):
        chunks += [(k, r, h_base + r) for r in range(0, grp.rows, grp.row_chunk)]
        h_base += grp.rows

    def mod_rows(k, ref, rows):
        if groups[k].per_row:
            return ref[rows, :]
        seq = (pl.program_id(0) + groups[k].first_block) // blocks_per_seq
        return ref[pl.ds(seq, 1), :]

    def norm_chunk(c):
        k, r, hr = chunks[c]
        x_ref, shift_ref, scale_ref, _ = refs[4 * k:4 * k + 4]
        for s in range(0, groups[k].row_chunk, NORM_ROWS):
            sr = pl.ds(r + s, NORM_ROWS)
            h = _norm_mod(x_ref[sr, :], g_ref[...], mod_rows(k, shift_ref, sr),
                          mod_rows(k, scale_ref, sr))
            h_ref[pl.ds(hr + s, NORM_ROWS), :] = h.astype(BF16)

    def chunk(c, first, last):
        k, r, hr = chunks[c]
        size = groups[k].row_chunk
        x_ref, _, _, gate_ref = refs[4 * k:4 * k + 4]
        o_ref = outs[k]
        rows = pl.ds(r, size)
        if first and c + 1 < len(chunks):
            norm_chunk(c + 1)
        h = h_ref[pl.ds(hr, size), :]
        g = _dot(h, wg_ref[...])
        u = _dot(h, wu_ref[...])
        a = (g * jax.nn.sigmoid(g) * u).astype(BF16)
        d = _dot(a, wd_ref[...])
        acc = d if first else o_ref[rows, :] + d
        if not last:
            o_ref[rows, :] = acc
            return
        o_ref[rows, :] = x_ref[rows, :] + FFN_RES * mod_rows(k, gate_ref, rows) * acc
        if final_norm:
            for s in range(0, size, NORM_ROWS):
                sr = pl.ds(r + s, NORM_ROWS)
                y = o_ref[sr, :]
                ms = jnp.mean(y * y, axis=-1, keepdims=True)
                o_ref[sr, :] = y * lax.rsqrt(ms + EPS) * fg_ref[...]

    def run(first, last):
        if first:
            norm_chunk(0)
        for c in range(len(chunks)):
            chunk(c, first, last)

    pl.when(j == 0)(lambda: run(True, False))
    pl.when(jnp.logical_and(j > 0, j < last_j))(lambda: run(False, False))
    pl.when(j == last_j)(lambda: run(False, True))


def _ffn(xs, groups, norm_g, mod, sub, wg, wu, wd, final_g, *, n_blocks, tf, final_norm,
         blocks_per_seq, alias_into=None):
    emit_bf16 = wg.dtype == F32
    assert not emit_bf16 or n_blocks == 1

    def row_spec(grp, col=0):
        mode = dict(pipeline_mode=pl.Buffered(1)) if grp.resident else {}
        return pl.BlockSpec((grp.rows, D_MODEL), lambda i, j: (i + grp.first_block, col), **mode)

    in_specs, operands = [], []
    for x, grp in zip(xs, groups):
        in_specs.append(row_spec(grp))
        operands.append(x)
        for m in range(3):
            col = 3 * sub + m
            if grp.per_row:
                in_specs.append(row_spec(grp, col))
            else:
                in_specs.append(pl.BlockSpec((SUBLANES, D_MODEL),
                                             lambda i, j, col=col: (PROMPT_MOD_BLOCK, col)))
            operands.append(mod)
    w_specs = [pl.BlockSpec((D_MODEL, tf), lambda i, j: (0, j)),
               pl.BlockSpec((D_MODEL, tf), lambda i, j: (0, j)),
               pl.BlockSpec((tf, D_MODEL), lambda i, j: (j, 0))]
    vec_spec = pl.BlockSpec((1, D_MODEL), lambda i, j: (0, 0))
    in_specs += [vec_spec, *w_specs, vec_spec]
    operands += [norm_g.reshape(1, D_MODEL), wg, wu, wd, final_g.reshape(1, D_MODEL)]
    aliases = {}
    if alias_into is not None:
        aliases = {len(operands): len(groups) - 1}
        in_specs.append(pl.BlockSpec(memory_space=pl.ANY))
        operands.append(alias_into)

    out_specs = [row_spec(grp) for grp in groups]
    out_shape = [jax.ShapeDtypeStruct(x.shape, F32) for x in xs]
    if emit_bf16:
        out_specs += w_specs
        out_shape += [jax.ShapeDtypeStruct(w.shape, BF16) for w in (wg, wu, wd)]
    kern = functools.partial(_ffn_kernel, groups=groups, blocks_per_seq=blocks_per_seq,
                             final_norm=final_norm, emit_bf16=emit_bf16,
                             has_alias=alias_into is not None)
    return pl.pallas_call(
        kern,
        grid=(n_blocks, D_FF // tf),
        in_specs=in_specs,
        out_specs=out_specs,
        out_shape=out_shape,
        input_output_aliases=aliases,
        scratch_shapes=[pltpu.VMEM((sum(grp.rows for grp in groups), D_MODEL), BF16)],
        compiler_params=_cparams(2),
        name="ffn",
    )(*operands)


def _proj_kernel(x_ref, g_ref, shift_ref, scale_ref, w_ref, u_ref, v_ref, *, blocks_per_seq):
    sh = _mod_rows(shift_ref, blocks_per_seq)
    sc = _mod_rows(scale_ref, blocks_per_seq)
    h = _norm_mod(x_ref[...], g_ref[...], sh, sc).astype(BF16)
    p = _dot(h, w_ref[...])
    u_ref[...] = p[:, :D_SSM].reshape(u_ref.shape)
    v_ref[...] = p[:, D_SSM:]


def _proj(x, norm_g, mod, w_in, *, tm, n_seq):
    rows = x.shape[0]
    seq_len = rows // n_seq
    blocks_per_seq = seq_len // tm
    u_shape = (seq_len // T_CHUNK, n_seq, T_CHUNK, D_SSM)
    u_spec = pl.BlockSpec((tm // T_CHUNK, None, T_CHUNK, D_SSM),
                          lambda i: (i % blocks_per_seq, i // blocks_per_seq, 0, 0))
    shift_spec, scale_spec, _ = _mod_specs(False, tm, PROMPT_MOD_BLOCK, MIX_SUBLAYER)
    return pl.pallas_call(
        functools.partial(_proj_kernel, blocks_per_seq=blocks_per_seq),
        grid=(rows // tm,),
        in_specs=[pl.BlockSpec((tm, D_MODEL), lambda i: (i, 0)),
                  pl.BlockSpec((1, D_MODEL), lambda i: (0, 0)),
                  shift_spec, scale_spec,
                  pl.BlockSpec((D_MODEL, D_MODEL), lambda i: (0, 0),
                               pipeline_mode=pl.Buffered(1))],
        out_specs=[u_spec, pl.BlockSpec((tm, D_POOL), lambda i: (i, 0))],
        out_shape=[jax.ShapeDtypeStruct(u_shape, F32),
                   jax.ShapeDtypeStruct((rows, D_POOL), F32)],
        compiler_params=_cparams(1),
        name="proj",
    )(x, norm_g.reshape(1, D_MODEL), mod, mod, w_in)


SAMPLE_TN = 512


def _proj_sample_kernel(x_ref, g_ref, shift_ref, scale_ref, w_ref, p_ref, wo_ref, h_ref):
    @pl.when(pl.program_id(0) == 0)
    def _():
        h = _norm_mod(x_ref[...], g_ref[...], shift_ref[...], scale_ref[...])
        h_ref[...] = h.astype(BF16)
    wo_ref[...] = w_ref[...].astype(BF16)
    p_ref[...] = _dot(h_ref[...], wo_ref[...])


def _proj_sample(x, norm_g, mod, w_in):
    rows = x.shape[0]
    shift_col, scale_col = 3 * MIX_SUBLAYER, 3 * MIX_SUBLAYER + 1
    return pl.pallas_call(
        _proj_sample_kernel,
        grid=(D_MODEL // SAMPLE_TN,),
        in_specs=[pl.BlockSpec((rows, D_MODEL), lambda j: (0, 0)),
                  pl.BlockSpec((1, D_MODEL), lambda j: (0, 0)),
                  pl.BlockSpec((rows, D_MODEL), lambda j: (0, shift_col)),
                  pl.BlockSpec((rows, D_MODEL), lambda j: (0, scale_col)),
                  pl.BlockSpec((D_MODEL, SAMPLE_TN), lambda j: (0, j))],
        out_specs=[pl.BlockSpec((rows, SAMPLE_TN), lambda j: (0, j)),
                   pl.BlockSpec((D_MODEL, SAMPLE_TN), lambda j: (0, j))],
        out_shape=[jax.ShapeDtypeStruct((rows, D_MODEL), F32),
                   jax.ShapeDtypeStruct((D_MODEL, D_MODEL), BF16)],
        scratch_shapes=[pltpu.VMEM((rows, D_MODEL), BF16)],
        compiler_params=_cparams(1),
        name="proj_sample",
    )(x, norm_g.reshape(1, D_MODEL), mod, mod, w_in)


def _discretise(lam_re, lam_im, log_dt):
    lr = jnp.minimum(lam_re, -1e-4)
    li = lam_im
    dt = jnp.exp(log_dt)
    mag = jnp.exp(lr * dt)
    ang = li * dt
    a_re = mag * jnp.cos(ang)
    a_im = mag * jnp.sin(ang)
    den = lr * lr + li * li
    num_re = a_re - 1.0
    f_re = (num_re * lr + a_im * li) / den
    f_im = (a_im * lr - num_re * li) / den
    return a_re, a_im, f_re, f_im


def _s5_prep_kernel(lam_re1, lam_im1, ldt1, c_re1, c_im1,
                    lam_re2, lam_im2, ldt2, b_re2, b_im2, gluw_t,
                    toep_ref, wz_ref, m_ref, apow_ref, gluw_ref):
    g1 = lax.broadcasted_iota(jnp.int32, (SSM_STATE, LANES), 1) // SSM_GROUP
    g2 = lax.broadcasted_iota(jnp.int32, (SSM_GROUP, OCT_STATE), 1) // SSM_STATE

    def expand1(x):
        return jnp.concatenate([jnp.where(g1 == g, x, 0.0) for g in range(OCT)], axis=0)

    def expand2(x):
        return jnp.concatenate([jnp.where(g2 == g, x, 0.0) for g in range(OCT)], axis=0)

    def split(x):
        hi = x.astype(BF16)
        return hi, (x - hi.astype(F32)).astype(BF16)

    a1_re, a1_im, _, _ = _discretise(lam_re1[...], lam_im1[...], ldt1[...])
    a2_re, a2_im, f_re, f_im = _discretise(lam_re2[...], lam_im2[...], ldt2[...])
    br, bi = b_re2[...], b_im2[...]
    bbar_re = f_re * br - f_im * bi
    bbar_im = f_re * bi + f_im * br
    cr, ci = c_re1[...], c_im1[...]
    bre_hi, bre_lo = split(expand2(bbar_re))
    bim_hi, bim_lo = split(expand2(bbar_im))

    def dot3(x_hi, x_lo, y_hi, y_lo):
        return _dot(x_hi, y_hi) + (_dot(x_hi, y_lo) + _dot(x_lo, y_hi))

    p1_re, p1_im = jnp.ones_like(a1_re), jnp.zeros_like(a1_im)
    p2_re, p2_im = jnp.ones_like(a2_re), jnp.zeros_like(a2_im)
    kk = []
    zero_blk = jnp.zeros((LANES, LANES), F32)
    for j in range(T_CHUNK + 1):
        care_hi, care_lo = split(expand1(cr * p1_re - ci * p1_im))
        caim_hi, caim_lo = split(expand1(cr * p1_im + ci * p1_re))
        m_ref[0:OCT_STATE, j * LANES:(j + 1) * LANES] = care_hi
        m_ref[OCT_STATE:2 * OCT_STATE, j * LANES:(j + 1) * LANES] = -caim_hi
        if j < T_CHUNK:
            t = T_CHUNK - 1 - j
            ba_re = expand2(bbar_re * p2_re - bbar_im * p2_im)
            ba_im = expand2(bbar_re * p2_im + bbar_im * p2_re)
            wz_ref[t * LANES:(t + 1) * LANES, 0:OCT_STATE] = ba_re.astype(BF16)
            wz_ref[t * LANES:(t + 1) * LANES, OCT_STATE:2 * OCT_STATE] = ba_im.astype(BF16)
            kk.append(dot3(bre_hi, bre_lo, care_hi, care_lo)
                      - dot3(bim_hi, bim_lo, caim_hi, caim_lo))
        if j == 1:
            apow_ref[0:1, :] = p2_re
            apow_ref[1:2, :] = p2_im
        if j == T_CHUNK:
            apow_ref[2:3, :] = p2_re
            apow_ref[3:4, :] = p2_im
        p1_re, p1_im = p1_re * a1_re - p1_im * a1_im, p1_re * a1_im + p1_im * a1_re
        p2_re, p2_im = p2_re * a2_re - p2_im * a2_im, p2_re * a2_im + p2_im * a2_re

    for t in range(T_CHUNK):
        for t2 in range(T_CHUNK):
            blk = kk[t2 - t] if t2 >= t else zero_blk
            toep_ref[t * LANES:(t + 1) * LANES, t2 * LANES:(t2 + 1) * LANES] = blk.astype(BF16)

    rg = lax.broadcasted_iota(jnp.int32, (LANES, LANES), 0) // SSM_GROUP
    lg = lax.broadcasted_iota(jnp.int32, (LANES, LANES), 1) // SSM_GROUP
    gw = jnp.where(rg == lg, gluw_t[...], 0.0).astype(BF16)
    gz = jnp.zeros((LANES, LANES), BF16)
    for a in range(TOK_PER_TILE):
        for b in range(TOK_PER_TILE):
            gluw_ref[a * LANES:(a + 1) * LANES, b * LANES:(b + 1) * LANES] = gw if a == b else gz


def _s5_prep(lam_re, lam_im, log_dt, b_re, b_im, c_re, c_im, glu_w):
    G, P, H = N_SSM_GROUPS, SSM_STATE, SSM_GROUP
    row = lambda a: a.reshape(N_OCT, 1, OCT_STATE)
    ldt = jnp.broadcast_to(log_dt[:, None], (G, P))

    def col(a_gp):
        a = jnp.swapaxes(a_gp.reshape(N_OCT, OCT, P), 1, 2)
        return jnp.broadcast_to(a[..., None], (N_OCT, P, OCT, H)).reshape(N_OCT, P, LANES)

    lay1 = lambda a_ghp: jnp.swapaxes(a_ghp.reshape(N_OCT, LANES, P), 1, 2)
    lay2 = lambda a_gph: jnp.swapaxes(a_gph.reshape(N_OCT, OCT_STATE, H), 1, 2)
    c_re1, c_im1, b_re2, b_im2 = lay1(c_re), lay1(c_im), lay2(b_re), lay2(b_im)
    gw = glu_w.reshape(N_OCT, OCT, H, 1, H)
    gluw_t = jnp.broadcast_to(gw, (N_OCT, OCT, H, OCT, H)).reshape(N_OCT, LANES, LANES)

    def spec(shape):
        return pl.BlockSpec((None,) + shape, lambda o: (o,) + (0,) * len(shape))

    m_cols = (T_CHUNK + 1) * LANES
    return pl.pallas_call(
        _s5_prep_kernel,
        grid=(N_OCT,),
        in_specs=[spec((SSM_STATE, LANES))] * 5
                 + [spec((1, OCT_STATE))] * 3 + [spec((SSM_GROUP, OCT_STATE))] * 2
                 + [spec((LANES, LANES))],
        out_specs=[spec((CHUNK_K, CHUNK_K)), spec((CHUNK_K, 2 * OCT_STATE)),
                   spec((2 * OCT_STATE, m_cols)), spec((4, OCT_STATE)),
                   spec((MXU_TILE, MXU_TILE))],
        out_shape=[jax.ShapeDtypeStruct((N_OCT, CHUNK_K, CHUNK_K), BF16),
                   jax.ShapeDtypeStruct((N_OCT, CHUNK_K, 2 * OCT_STATE), BF16),
                   jax.ShapeDtypeStruct((N_OCT, 2 * OCT_STATE, m_cols), BF16),
                   jax.ShapeDtypeStruct((N_OCT, 4, OCT_STATE), F32),
                   jax.ShapeDtypeStruct((N_OCT, MXU_TILE, MXU_TILE), BF16)],
        compiler_params=_cparams(1),
        name="s5_prep",
    )(col(lam_re), col(lam_im), col(ldt), c_re1, c_im1,
      row(lam_re), row(lam_im), row(ldt), b_re2, b_im2, gluw_t)


GELU_C0 = 0.7978845608028654
GELU_C1 = GELU_C0 * 0.044715


def _glu_out(y, gluw, glub):
    gy = y * (0.5 + 0.5 * jnp.tanh(y * (GELU_C0 + GELU_C1 * (y * y))))
    z = _dot(gy.astype(BF16), gluw) + glub
    return gy * (0.5 + 0.5 * jnp.tanh(0.5 * z))


def _s5_prompt_kernel(x_ref, toep_ref, wz_ref, m_ref, apow_ref, d_ref, gluw_ref, glub_ref,
                      y_ref, sre_ref, sim_ref, xr_ref, z_ref, yv_ref, *, n_seq, row_chunk):
    n_rows = x_ref.shape[0] // T_CHUNK
    n_blk = n_rows // row_chunk

    def tokens(b, t):
        return pl.ds(b * (row_chunk * T_CHUNK) + t, row_chunk, stride=T_CHUNK)

    for b in range(n_blk):
        rows = pl.ds(b * row_chunk, row_chunk)
        for t in range(T_CHUNK):
            xr_ref[rows, t * LANES:(t + 1) * LANES] = x_ref[tokens(b, t), :].astype(BF16)
        xr = xr_ref[rows, :]
        z_ref[rows, :] = _dot(xr, wz_ref[...])
        for nt in range(CHUNK_K // MXU_TILE):
            k_end = (nt + 1) * MXU_TILE
            cols = slice(nt * MXU_TILE, k_end)
            yv_ref[rows, cols] = _dot(xr[:, 0:k_end], toep_ref[0:k_end, cols])

    are, aim = apow_ref[2:3, :], apow_ref[3:4, :]
    lo = lax.broadcasted_iota(jnp.int32, (SUBLANES, OCT_STATE), 0) < n_seq

    def step(k, carry):
        pre, pim = carry
        rows = pl.ds(pl.multiple_of(k * SUBLANES, SUBLANES), SUBLANES)
        zre = z_ref[rows, 0:OCT_STATE]
        zim = z_ref[rows, OCT_STATE:2 * OCT_STATE]
        w1re = are * pre - aim * pim + zre
        w1im = are * pim + aim * pre + zim
        r1re = pltpu.roll(w1re, n_seq, axis=0)
        r1im = pltpu.roll(w1im, n_seq, axis=0)
        w2re = are * r1re - aim * r1im + zre
        w2im = are * r1im + aim * r1re + zim
        z_ref[rows, 0:OCT_STATE] = jnp.where(lo, pre, r1re)
        z_ref[rows, OCT_STATE:2 * OCT_STATE] = jnp.where(lo, pim, r1im)
        nre = jnp.where(lo, pltpu.roll(w2re, n_seq, axis=0), w2re)
        nim = jnp.where(lo, pltpu.roll(w2im, n_seq, axis=0), w2im)
        return nre, nim

    zeros = jnp.zeros((SUBLANES, OCT_STATE), F32)
    fre, fim = lax.fori_loop(0, n_rows // SUBLANES, step, (zeros, zeros))
    sre_ref[...] = fre[0:n_seq, :]
    sim_ref[...] = fim[0:n_seq, :]

    d2 = jnp.concatenate([d_ref[...]] * TOK_PER_TILE, axis=1)
    glub2 = jnp.concatenate([glub_ref[...]] * TOK_PER_TILE, axis=1)

    for b in range(n_blk):
        rows = pl.ds(b * row_chunk, row_chunk)
        y = yv_ref[rows, :] + _dot(z_ref[rows, :].astype(BF16), m_ref[:, LANES:])
        for t0 in range(0, T_CHUNK, TOK_PER_TILE):
            toks = [tokens(b, t0 + i) for i in range(TOK_PER_TILE)]
            u = jnp.concatenate([x_ref[tok, :] for tok in toks], axis=1)
            yt = y[:, t0 * LANES:(t0 + TOK_PER_TILE) * LANES] + d2 * u
            out = _glu_out(yt, gluw_ref[...], glub2)
            for i, tok in enumerate(toks):
                y_ref[tok, :] = out[:, i * LANES:(i + 1) * LANES]


def _s5_prompt(u_flat, ops, d_skip, glu_b, *, n_seq):
    toep, wz, m, apow, gluw = ops
    rows = u_flat.shape[0]
    n_rows = rows // T_CHUNK
    oct_spec = lambda shape: pl.BlockSpec((None,) + shape, lambda o: (o, 0, 0))
    kern = functools.partial(_s5_prompt_kernel, n_seq=n_seq, row_chunk=256)
    return pl.pallas_call(
        kern,
        grid=(N_OCT,),
        in_specs=[pl.BlockSpec((rows, LANES), lambda o: (0, o)),
                  oct_spec(toep.shape[1:]), oct_spec(wz.shape[1:]), oct_spec(m.shape[1:]),
                  oct_spec(apow.shape[1:]), oct_spec((1, LANES)), oct_spec(gluw.shape[1:]),
                  oct_spec((1, LANES))],
        out_specs=[pl.BlockSpec((rows, LANES), lambda o: (0, o)),
                   pl.BlockSpec((n_seq, OCT_STATE), lambda o: (0, o)),
                   pl.BlockSpec((n_seq, OCT_STATE), lambda o: (0, o))],
        out_shape=[jax.ShapeDtypeStruct((rows, D_SSM), F32),
                   jax.ShapeDtypeStruct((n_seq, N_OCT * OCT_STATE), F32),
                   jax.ShapeDtypeStruct((n_seq, N_OCT * OCT_STATE), F32)],
        scratch_shapes=[pltpu.VMEM((n_rows, CHUNK_K), BF16),
                        pltpu.VMEM((n_rows, 2 * OCT_STATE), F32),
                        pltpu.VMEM((n_rows, CHUNK_K), F32)],
        compiler_params=_cparams(1),
        name="s5_prompt",
    )(u_flat, toep, wz, m, apow, d_skip.reshape(N_OCT, 1, LANES), gluw,
      glu_b.reshape(N_OCT, 1, LANES))


def _s5_sample_kernel(u_ref, s0re_ref, s0im_ref, wz_ref, m_ref, apow_ref, d_ref, gluw_ref,
                      glub_ref, y_ref, sre_ref, sim_ref):
    u = u_ref[...]
    z = _dot(u.astype(BF16), wz_ref[...])
    are, aim = apow_ref[0:1, :], apow_ref[1:2, :]
    s0re, s0im = s0re_ref[...], s0im_ref[...]
    nre = are * s0re - aim * s0im + z[:, 0:OCT_STATE]
    nim = are * s0im + aim * s0re + z[:, OCT_STATE:2 * OCT_STATE]
    sre_ref[...] = nre
    sim_ref[...] = nim
    y = (_dot(nre.astype(BF16), m_ref[0:OCT_STATE, :])
         + _dot(nim.astype(BF16), m_ref[OCT_STATE:2 * OCT_STATE, :])
         + d_ref[...] * u)
    y_ref[...] = _glu_out(y, gluw_ref[...], glub_ref[...])


def _s5_sample(u, s0_re, s0_im, ops, d_skip, glu_b):
    _, wz, m, apow, gluw = ops
    rows = u.shape[0]
    oct_spec = lambda shape: pl.BlockSpec((None,) + shape, lambda o: (o, 0, 0))
    return pl.pallas_call(
        _s5_sample_kernel,
        grid=(N_OCT,),
        in_specs=[pl.BlockSpec((rows, LANES), lambda o: (0, o)),
                  pl.BlockSpec((rows, OCT_STATE), lambda o: (0, o)),
                  pl.BlockSpec((rows, OCT_STATE), lambda o: (0, o)),
                  pl.BlockSpec((None, LANES, 2 * OCT_STATE), lambda o: (o, T_CHUNK - 1, 0)),
                  pl.BlockSpec((None, 2 * OCT_STATE, LANES), lambda o: (o, 0, 0)),
                  oct_spec(apow.shape[1:]), oct_spec((1, LANES)), oct_spec((LANES, LANES)),
                  oct_spec((1, LANES))],
        out_specs=[pl.BlockSpec((rows, LANES), lambda o: (0, o)),
                   pl.BlockSpec((rows, OCT_STATE), lambda o: (0, o)),
                   pl.BlockSpec((rows, OCT_STATE), lambda o: (0, o))],
        out_shape=[jax.ShapeDtypeStruct((rows, D_SSM), F32),
                   jax.ShapeDtypeStruct((rows, N_OCT * OCT_STATE), F32),
                   jax.ShapeDtypeStruct((rows, N_OCT * OCT_STATE), F32)],
        compiler_params=_cparams(1),
        name="s5_sample",
    )(u, s0_re, s0_im, wz, m, apow, d_skip.reshape(N_OCT, 1, LANES), gluw,
      glu_b.reshape(N_OCT, 1, LANES))


def _pool_linear(z, gi, pw_ref, pb_ref, ps_ref):
    lanes = slice(gi * POOL_GROUP, (gi + 1) * POOL_GROUP)
    return (_dot(z.astype(BF16), pw_ref[gi]) + pb_ref[:, lanes]) * ps_ref[:, lanes]


def _pool_prompt_kernel(v_ref, pw_ref, pb_ref, ps_ref, y_ref):
    n = v_ref.shape[0]
    pos = lax.broadcasted_iota(jnp.int32, (n, POOL_GROUP), 0)
    for gi, w in enumerate(POOL_WINDOWS):
        lanes = slice(gi * POOL_GROUP, (gi + 1) * POOL_GROUP)
        v = v_ref[:, lanes]
        s, k = v, 1
        while k < w:
            s = s + jnp.where(pos >= k, pltpu.roll(s, k, axis=0), 0.0)
            k *= 2
        cnt = jnp.clip(pos + 1, 1, w).astype(F32)
        y_ref[:, lanes] = _pool_linear(s / cnt - v, gi, pw_ref, pb_ref, ps_ref)


def _pool_prompt(v, pool_w, pool_b, pool_scale, *, n_seq):
    rows = v.shape[0]
    seq_len = rows // n_seq
    return pl.pallas_call(
        _pool_prompt_kernel,
        grid=(n_seq,),
        in_specs=[pl.BlockSpec((seq_len, D_POOL), lambda n: (n, 0)),
                  pl.BlockSpec(pool_w.shape, lambda n: (0, 0, 0)),
                  pl.BlockSpec((1, D_POOL), lambda n: (0, 0)),
                  pl.BlockSpec((1, D_POOL), lambda n: (0, 0))],
        out_specs=pl.BlockSpec((seq_len, D_POOL), lambda n: (n, 0)),
        out_shape=jax.ShapeDtypeStruct((rows, D_POOL), F32),
        compiler_params=_cparams(1),
        name="pool_prompt",
    )(v, pool_w, pool_b.reshape(1, D_POOL), pool_scale.reshape(1, D_POOL))


def _pool_sample_kernel(hist_ref, v_ref, pw_ref, pb_ref, ps_ref, y_ref, new_ref, sum_ref):
    r = pl.program_id(0)

    @pl.when(r == 0)
    def _():
        sum_ref[...] = jnp.zeros(sum_ref.shape, F32)

    @pl.when(r < POOL_HIST)
    def _():
        h = hist_ref[...]
        new_ref[...] = h
        for gi, w in enumerate(POOL_WINDOWS):
            lanes = slice(gi * POOL_GROUP, (gi + 1) * POOL_GROUP)
            sum_ref[:, lanes] += jnp.where(r >= POOL_HIST - (w - 1), h[:, lanes], 0.0)

    @pl.when(r == POOL_HIST)
    def _():
        new_ref[...] = v_ref[...]
        for gi, w in enumerate(POOL_WINDOWS):
            lanes = slice(gi * POOL_GROUP, (gi + 1) * POOL_GROUP)
            v = v_ref[:, lanes]
            z = (sum_ref[:, lanes] + v) / float(w) - v
            y_ref[:, lanes] = _pool_linear(z, gi, pw_ref, pb_ref, ps_ref)


def _pool_sample(hist_t, proj, pool_w, pool_b, pool_scale):
    rows = proj.shape[0]
    return pl.pallas_call(
        _pool_sample_kernel,
        grid=(POOL_HIST + 1,),
        in_specs=[pl.BlockSpec((None, rows, D_POOL),
                               lambda r: (jnp.minimum(r, POOL_HIST - 1), 0, 0)),
                  pl.BlockSpec((rows, D_POOL), lambda r: (0, D_SSM // D_POOL)),
                  pl.BlockSpec(pool_w.shape, lambda r: (0, 0, 0)),
                  pl.BlockSpec((1, D_POOL), lambda r: (0, 0)),
                  pl.BlockSpec((1, D_POOL), lambda r: (0, 0))],
        out_specs=[pl.BlockSpec((rows, D_POOL), lambda r: (0, 0)),
                   pl.BlockSpec((None, rows, D_POOL), lambda r: (jnp.maximum(r - 1, 0), 0, 0))],
        out_shape=[jax.ShapeDtypeStruct((rows, D_POOL), F32),
                   jax.ShapeDtypeStruct(hist_t.shape, F32)],
        scratch_shapes=[pltpu.VMEM((rows, D_POOL), F32)],
        compiler_params=_cparams(1),
        name="pool_sample",
    )(hist_t, proj, pool_w, pool_b.reshape(1, D_POOL), pool_scale.reshape(1, D_POOL))


def _outproj_kernel(x_ref, ys_ref, yp_ref, gate_ref, w_ref, o_ref, *, blocks_per_seq):
    tm = x_ref.shape[0]
    ys = ys_ref[...].reshape(tm, D_SSM).astype(BF16)
    yp = yp_ref[...].astype(BF16)
    mix = _dot(ys, w_ref[0:D_SSM, :]) + _dot(yp, w_ref[D_SSM:, :])
    o_ref[...] = x_ref[...] + _mod_rows(gate_ref, blocks_per_seq) * mix


def _outproj(x, ys, yp, mod, w_out, *, tm, n_seq):
    rows = x.shape[0]
    blocks_per_seq = (rows // n_seq) // tm
    ys_spec = pl.BlockSpec((tm // T_CHUNK, None, T_CHUNK, D_SSM),
                           lambda i: (i % blocks_per_seq, i // blocks_per_seq, 0, 0))
    gate_spec = _mod_specs(False, tm, PROMPT_MOD_BLOCK, MIX_SUBLAYER)[2]
    return pl.pallas_call(
        functools.partial(_outproj_kernel, blocks_per_seq=blocks_per_seq),
        grid=(rows // tm,),
        in_specs=[pl.BlockSpec((tm, D_MODEL), lambda i: (i, 0)),
                  ys_spec,
                  pl.BlockSpec((tm, D_POOL), lambda i: (i, 0)),
                  gate_spec,
                  pl.BlockSpec((D_MODEL, D_MODEL), lambda i: (0, 0),
                               pipeline_mode=pl.Buffered(1))],
        out_specs=pl.BlockSpec((tm, D_MODEL), lambda i: (i, 0)),
        out_shape=jax.ShapeDtypeStruct((rows, D_MODEL), F32),
        compiler_params=_cparams(1),
        name="outproj",
    )(x, ys, yp, mod, w_out)


def _outproj_sample_kernel(x_ref, ys_ref, yp_ref, gate_ref, w_ref, o_ref, wo_ref):
    wo_ref[...] = w_ref[...].astype(BF16)
    mix = (_dot(ys_ref[...].astype(BF16), wo_ref[0:D_SSM, :])
           + _dot(yp_ref[...].astype(BF16), wo_ref[D_SSM:, :]))
    o_ref[...] = x_ref[...] + gate_ref[...] * mix


def _outproj_sample(x, ys, yp, mod, w_out):
    rows = x.shape[0]
    n_col = D_MODEL // SAMPLE_TN
    gate_col = (3 * MIX_SUBLAYER + 2) * n_col
    return pl.pallas_call(
        _outproj_sample_kernel,
        grid=(n_col,),
        in_specs=[pl.BlockSpec((rows, SAMPLE_TN), lambda j: (0, j)),
                  pl.BlockSpec((rows, D_SSM), lambda j: (0, 0)),
                  pl.BlockSpec((rows, D_POOL), lambda j: (0, 0)),
                  pl.BlockSpec((rows, SAMPLE_TN), lambda j: (0, gate_col + j)),
                  pl.BlockSpec((D_MODEL, SAMPLE_TN), lambda j: (0, j))],
        out_specs=[pl.BlockSpec((rows, SAMPLE_TN), lambda j: (0, j)),
                   pl.BlockSpec((D_MODEL, SAMPLE_TN), lambda j: (0, j))],
        out_shape=[jax.ShapeDtypeStruct((rows, D_MODEL), F32),
                   jax.ShapeDtypeStruct((D_MODEL, D_MODEL), BF16)],
        compiler_params=_cparams(1),
        name="outproj_sample",
    )(x, ys, yp, mod, w_out)


def kernel(x_prompt, x_sample, state_ssm_re, state_ssm_im, state_pool, c_prompt, c_sample, ada_w, ada_b, ffn1_norm, ffn1_w_gate, ffn1_w_up, ffn1_w_down, mix_norm, w_in, ssm_lambda_re, ssm_lambda_im, ssm_log_dt, ssm_b_re, ssm_b_im, ssm_c_re, ssm_c_im, ssm_d, ssm_glu_w, ssm_glu_b, pool_w, pool_b, pool_scale, w_out, ffn2_norm, ffn2_w_gate, ffn2_w_up, ffn2_w_down, final_norm):
    n_p, seq, _ = x_prompt.shape
    n_s = x_sample.shape[0]
    assert (n_p, n_s) == (N_PROMPT, N_SAMPLE) and n_p <= SUBLANES
    G, P = N_SSM_GROUPS, SSM_STATE

    mod = _adaln(c_sample, c_prompt, ada_w[0], ada_b[0])
    pw = pool_w[0].astype(BF16)
    ops = _s5_prep(ssm_lambda_re[0], ssm_lambda_im[0], ssm_log_dt[0], ssm_b_re[0], ssm_b_im[0],
                   ssm_c_re[0], ssm_c_im[0], ssm_glu_w[0])

    xs = x_sample.reshape(n_s, D_MODEL)
    xp = x_prompt.reshape(n_p * seq, D_MODEL)
    blocks_per_seq = seq // FFN_TM
    sample_rows = _RowGroup(n_s, n_s, per_row=True, first_block=0, resident=True)
    head_rows = _RowGroup(FFN_TM, FFN_ROW_CHUNK, per_row=False, first_block=0, resident=True)
    tail_rows = _RowGroup(FFN_TM, FFN_ROW_CHUNK, per_row=False, first_block=1, resident=False)

    def ffn(xs, xp, norm_g, sub, wg, wu, wd, last):
        ys, yp, bg, bu, bd = _ffn([xs, xp], (sample_rows, head_rows), norm_g, mod, sub, wg, wu, wd,
                                  final_norm, n_blocks=1, tf=FFN_HEAD_TF, final_norm=last,
                                  blocks_per_seq=blocks_per_seq)
        (yp,) = _ffn([xp], (tail_rows,), norm_g, mod, sub, bg, bu, bd, final_norm,
                     n_blocks=n_p * blocks_per_seq - 1, tf=FFN_TF, final_norm=last,
                     blocks_per_seq=blocks_per_seq, alias_into=yp)
        return ys, yp

    xs, xp = ffn(xs, xp, ffn1_norm[0], 0, ffn1_w_gate[0], ffn1_w_up[0], ffn1_w_down[0], False)

    proj_s, win = _proj_sample(xs, mix_norm[0], mod, w_in[0])
    ys_s, sre_s, sim_s = _s5_sample(proj_s, state_ssm_re[0].reshape(n_s, G * P),
                                    state_ssm_im[0].reshape(n_s, G * P), ops, ssm_d[0],
                                    ssm_glu_b[0])
    yp_s, pool_s_t = _pool_sample(jnp.swapaxes(state_pool[0], 0, 1), proj_s, pw, pool_b[0],
                                  pool_scale[0])
    pool_s = jnp.swapaxes(pool_s_t, 0, 1)
    xs, wout = _outproj_sample(xs, ys_s, yp_s, mod, w_out[0])

    u4, v_p = _proj(xp, mix_norm[0], mod, win, tm=MIX_TM, n_seq=n_p)
    ys_p, sre_p, sim_p = _s5_prompt(u4.reshape(n_p * seq, D_SSM), ops, ssm_d[0], ssm_glu_b[0],
                                    n_seq=n_p)
    yp_p = _pool_prompt(v_p, pw, pool_b[0], pool_scale[0], n_seq=n_p)
    xp = _outproj(xp, ys_p.reshape(seq // T_CHUNK, n_p, T_CHUNK, D_SSM), yp_p, mod, wout,
                  tm=MIX_TM, n_seq=n_p)

    y_sample, y_prompt = ffn(xs, xp, ffn2_norm[0], 2, ffn2_w_gate[0], ffn2_w_up[0],
                             ffn2_w_down[0], True)

    pool_p = v_p.reshape(n_p, seq, D_POOL)[:, seq - POOL_HIST:, :][None]
    return (y_prompt.reshape(n_p, seq, D_MODEL), y_sample.reshape(n_s, 1, D_MODEL),
            sre_p.reshape(1, n_p, G, P), sim_p.reshape(1, n_p, G, P), pool_p,
            sre_s.reshape(1, n_s, G, P), sim_s.reshape(1, n_s, G, P), pool_s[None])
```

```python
import functools
from typing import NamedTuple

import jax
import jax.numpy as jnp
from jax import lax
from jax.experimental import pallas as pl
from jax.experimental.pallas import tpu as pltpu

F32 = jnp.float32
BF16 = jnp.bfloat16

D_MODEL = 2048
D_FF = 5632
D_SSM = 1024
D_POOL = 1024
SSM_GROUP = 16
SSM_STATE = 64
N_SSM_GROUPS = 64
POOL_WINDOWS = (2, 4, 8, 16)
POOL_GROUP = 256
POOL_HIST = 15
N_MOD_COLS = 9 * D_MODEL
N_PROMPT = 4
N_SAMPLE = 128
EPS = 1e-6
FFN_RES = 0.5

LANES = 128
SUBLANES = 8
MXU_TILE = 256
VMEM_LIMIT = 58 * 1024 * 1024

PROMPT_MOD_BLOCK = N_SAMPLE // SUBLANES
NORM_ROWS = 32
N_EARLY_MOD_COLS = 3 * D_MODEL
MIX_SUBLAYER = 0
FFN2_SUBLAYER = 1
ADALN_SIDE_TN = 256

FFN_TM = 1024
FFN_ROW_CHUNK = 512
FFN_TF = 512
FFN_HEAD_TF = 256
MIX_TM = 512

OCT = LANES // SSM_GROUP
N_OCT = N_SSM_GROUPS // OCT
T_CHUNK = 8
OCT_STATE = OCT * SSM_STATE
CHUNK_K = T_CHUNK * LANES
TOK_PER_TILE = MXU_TILE // LANES


def _cparams(n_axes):
    return pltpu.CompilerParams(dimension_semantics=("arbitrary",) * n_axes,
                                vmem_limit_bytes=VMEM_LIMIT)


def _dot(a, b):
    return jnp.dot(a, b, preferred_element_type=F32)


ADALN_TN = 1024


def _adaln_kernel(c_ref, w_ref, b_ref, o_ref):
    c = c_ref[...]
    sc = (c * jax.nn.sigmoid(c)).astype(BF16)
    o_ref[...] = _dot(sc, w_ref[...].astype(BF16)) + b_ref[...]


def _adaln(c_all, ada_w, ada_b, n_cols):
    rows = c_all.shape[0]
    return pl.pallas_call(
        _adaln_kernel,
        grid=(n_cols // ADALN_TN,),
        in_specs=[pl.BlockSpec((rows, D_MODEL), lambda j: (0, 0)),
                  pl.BlockSpec((D_MODEL, ADALN_TN), lambda j: (0, j)),
                  pl.BlockSpec((1, ADALN_TN), lambda j: (0, j))],
        out_specs=pl.BlockSpec((rows, ADALN_TN), lambda j: (0, j)),
        out_shape=jax.ShapeDtypeStruct((rows, n_cols), F32),
        compiler_params=_cparams(1),
        name="adaln",
    )(c_all, ada_w, ada_b)


def _norm_mod(x, g, shift, scale):
    ms = jnp.mean(x * x, axis=-1, keepdims=True)
    y = x * lax.rsqrt(ms + EPS) * g
    return y * (1.0 + scale) + shift


def _mod_specs(per_row, tm, prompt_row_block, sub):
    specs = []
    for m in range(3):
        col = 3 * sub + m
        if per_row:
            specs.append(pl.BlockSpec((tm, D_MODEL), lambda i, *_, col=col: (i, col)))
        else:
            specs.append(pl.BlockSpec((SUBLANES, D_MODEL),
                                      lambda i, *_, col=col: (prompt_row_block, col)))
    return specs


def _mod_rows(ref, blocks_per_seq, rows=None):
    if blocks_per_seq:
        return ref[pl.ds(pl.program_id(0) // blocks_per_seq, 1), :]
    return ref[...] if rows is None else ref[rows, :]


class _RowGroup(NamedTuple):
    rows: int
    row_chunk: int
    per_row: bool
    first_block: int
    resident: bool


def _ffn_kernel(*refs, groups, blocks_per_seq, final_norm, emit_bf16, has_alias, has_adaln):
    n = len(groups)
    g_ref, wg_ref, wu_ref, wd_ref, fg_ref = refs[4 * n:4 * n + 5]
    n_in = 4 * n + 5 + int(has_alias) + (3 if has_adaln else 0)
    outs = refs[n_in:-1]
    h_ref = refs[-1]
    j = pl.program_id(1)
    last_j = pl.num_programs(1) - 1

    if emit_bf16:
        bf16_refs = outs[n:n + 3]
        for w_ref, wo_ref in zip((wg_ref, wu_ref, wd_ref), bf16_refs):
            wo_ref[...] = w_ref[...].astype(BF16)
        wg_ref, wu_ref, wd_ref = bf16_refs

    chunks, h_base = [], 0
    for k, grp in enumerate(groups):
        chunks += [(k, r, h_base + r) for r in range(0, grp.rows, grp.row_chunk)]
        h_base += grp.rows

    def mod_rows(k, ref, rows):
        if groups[k].per_row:
            return ref[rows, :]
        seq = (pl.program_id(0) + groups[k].first_block) // blocks_per_seq
        return ref[pl.ds(seq, 1), :]

    def norm_chunk(c):
        k, r, hr = chunks[c]
        x_ref, shift_ref, scale_ref, _ = refs[4 * k:4 * k + 4]
        for s in range(0, groups[k].row_chunk, NORM_ROWS):
            sr = pl.ds(r + s, NORM_ROWS)
            h = _norm_mod(x_ref[sr, :], g_ref[...], mod_rows(k, shift_ref, sr),
                          mod_rows(k, scale_ref, sr))
            h_ref[pl.ds(hr + s, NORM_ROWS), :] = h.astype(BF16)

    def chunk(c, first, last):
        k, r, hr = chunks[c]
        size = groups[k].row_chunk
        x_ref, _, _, gate_ref = refs[4 * k:4 * k + 4]
        o_ref = outs[k]
        rows = pl.ds(r, size)
        if first and c + 1 < len(chunks):
            norm_chunk(c + 1)
        h = h_ref[pl.ds(hr, size), :]
        g = _dot(h, wg_ref[...])
        u = _dot(h, wu_ref[...])
        a = (g * jax.nn.sigmoid(g) * u).astype(BF16)
        d = _dot(a, wd_ref[...])
        acc = d if first else o_ref[rows, :] + d
        if not last:
            o_ref[rows, :] = acc
            return
        o_ref[rows, :] = x_ref[rows, :] + FFN_RES * mod_rows(k, gate_ref, rows) * acc
        if final_norm:
            for s in range(0, size, NORM_ROWS):
                sr = pl.ds(r + s, NORM_ROWS)
                y = o_ref[sr, :]
                ms = jnp.mean(y * y, axis=-1, keepdims=True)
                o_ref[sr, :] = y * lax.rsqrt(ms + EPS) * fg_ref[...]

    def run(first, last):
        if has_adaln:
            _adaln_kernel(*refs[n_in - 3:n_in], outs[-1])
        if first:
            norm_chunk(0)
        for c in range(len(chunks)):
            chunk(c, first, last)

    pl.when(j == 0)(lambda: run(True, False))
    pl.when(jnp.logical_and(j > 0, j < last_j))(lambda: run(False, False))
    pl.when(j == last_j)(lambda: run(False, True))


def _ffn(xs, groups, norm_g, mod, sub, wg, wu, wd, final_g, *, n_blocks, tf, final_norm,
         blocks_per_seq, alias_into=None, adaln=None):
    emit_bf16 = wg.dtype == F32
    assert not emit_bf16 or n_blocks == 1
    n_j = D_FF // tf

    def row_spec(grp, col=0):
        mode = dict(pipeline_mode=pl.Buffered(1)) if grp.resident else {}
        return pl.BlockSpec((grp.rows, D_MODEL), lambda i, j: (i + grp.first_block, col), **mode)

    in_specs, operands = [], []
    for x, grp in zip(xs, groups):
        in_specs.append(row_spec(grp))
        operands.append(x)
        for m in range(3):
            col = 3 * sub + m
            if grp.per_row:
                in_specs.append(row_spec(grp, col))
            else:
                in_specs.append(pl.BlockSpec((SUBLANES, D_MODEL),
                                             lambda i, j, col=col: (PROMPT_MOD_BLOCK, col)))
            operands.append(mod)
    w_specs = [pl.BlockSpec((D_MODEL, tf), lambda i, j: (0, j)),
               pl.BlockSpec((D_MODEL, tf), lambda i, j: (0, j)),
               pl.BlockSpec((tf, D_MODEL), lambda i, j: (j, 0))]
    vec_spec = pl.BlockSpec((1, D_MODEL), lambda i, j: (0, 0))
    in_specs += [vec_spec, *w_specs, vec_spec]
    operands += [norm_g.reshape(1, D_MODEL), wg, wu, wd, final_g.reshape(1, D_MODEL)]
    aliases = {}
    if alias_into is not None:
        aliases = {len(operands): len(groups) - 1}
        in_specs.append(pl.BlockSpec(memory_space=pl.ANY))
        operands.append(alias_into)

    out_specs = [row_spec(grp) for grp in groups]
    out_shape = [jax.ShapeDtypeStruct(x.shape, F32) for x in xs]
    if emit_bf16:
        out_specs += w_specs
        out_shape += [jax.ShapeDtypeStruct(w.shape, BF16) for w in (wg, wu, wd)]
    if adaln is not None:
        c_all, ada_w, ada_b, first_col = adaln
        n_tiles = (ada_w.shape[1] - first_col) // ADALN_SIDE_TN
        assert n_tiles <= n_blocks * n_j
        tile_of = lambda i, j: jnp.minimum(i * n_j + j, n_tiles - 1)
        first_tile = first_col // ADALN_SIDE_TN
        c_rows = c_all.shape[0]
        in_specs += [pl.BlockSpec((c_rows, D_MODEL), lambda i, j: (0, 0)),
                     pl.BlockSpec((D_MODEL, ADALN_SIDE_TN),
                                  lambda i, j: (0, first_tile + tile_of(i, j))),
                     pl.BlockSpec((1, ADALN_SIDE_TN),
                                  lambda i, j: (0, first_tile + tile_of(i, j)))]
        operands += [c_all, ada_w, ada_b]
        out_specs.append(pl.BlockSpec((c_rows, ADALN_SIDE_TN), lambda i, j: (0, tile_of(i, j))))
        out_shape.append(jax.ShapeDtypeStruct((c_rows, n_tiles * ADALN_SIDE_TN), F32))
    kern = functools.partial(_ffn_kernel, groups=groups, blocks_per_seq=blocks_per_seq,
                             final_norm=final_norm, emit_bf16=emit_bf16,
                             has_alias=alias_into is not None, has_adaln=adaln is not None)
    return pl.pallas_call(
        kern,
        grid=(n_blocks, n_j),
        in_specs=in_specs,
        out_specs=out_specs,
        out_shape=out_shape,
        input_output_aliases=aliases,
        scratch_shapes=[pltpu.VMEM((sum(grp.rows for grp in groups), D_MODEL), BF16)],
        compiler_params=_cparams(2),
        name="ffn",
    )(*operands)


def _proj_kernel(x_ref, g_ref, shift_ref, scale_ref, w_ref, u_ref, v_ref, *, blocks_per_seq):
    sh = _mod_rows(shift_ref, blocks_per_seq)
    sc = _mod_rows(scale_ref, blocks_per_seq)
    h = _norm_mod(x_ref[...], g_ref[...], sh, sc).astype(BF16)
    p = _dot(h, w_ref[...])
    u_ref[...] = p[:, :D_SSM].reshape(u_ref.shape)
    v_ref[...] = p[:, D_SSM:]


def _proj(x, norm_g, mod, w_in, *, tm, n_seq):
    rows = x.shape[0]
    seq_len = rows // n_seq
    blocks_per_seq = seq_len // tm
    u_shape = (seq_len // T_CHUNK, n_seq, T_CHUNK, D_SSM)
    u_spec = pl.BlockSpec((tm // T_CHUNK, None, T_CHUNK, D_SSM),
                          lambda i: (i % blocks_per_seq, i // blocks_per_seq, 0, 0))
    shift_spec, scale_spec, _ = _mod_specs(False, tm, PROMPT_MOD_BLOCK, MIX_SUBLAYER)
    return pl.pallas_call(
        functools.partial(_proj_kernel, blocks_per_seq=blocks_per_seq),
        grid=(rows // tm,),
        in_specs=[pl.BlockSpec((tm, D_MODEL), lambda i: (i, 0)),
                  pl.BlockSpec((1, D_MODEL), lambda i: (0, 0)),
                  shift_spec, scale_spec,
                  pl.BlockSpec((D_MODEL, D_MODEL), lambda i: (0, 0),
                               pipeline_mode=pl.Buffered(1))],
        out_specs=[u_spec, pl.BlockSpec((tm, D_POOL), lambda i: (i, 0))],
        out_shape=[jax.ShapeDtypeStruct(u_shape, F32),
                   jax.ShapeDtypeStruct((rows, D_POOL), F32)],
        compiler_params=_cparams(1),
        name="proj",
    )(x, norm_g.reshape(1, D_MODEL), mod, mod, w_in)


SAMPLE_TN = 512


def _proj_sample_kernel(x_ref, g_ref, shift_ref, scale_ref, w_ref, p_ref, wo_ref, h_ref):
    @pl.when(pl.program_id(0) == 0)
    def _():
        h = _norm_mod(x_ref[...], g_ref[...], shift_ref[...], scale_ref[...])
        h_ref[...] = h.astype(BF16)
    wo_ref[...] = w_ref[...].astype(BF16)
    p_ref[...] = _dot(h_ref[...], wo_ref[...])


def _proj_sample(x, norm_g, mod, w_in):
    rows = x.shape[0]
    shift_col, scale_col = 3 * MIX_SUBLAYER, 3 * MIX_SUBLAYER + 1
    return pl.pallas_call(
        _proj_sample_kernel,
        grid=(D_MODEL // SAMPLE_TN,),
        in_specs=[pl.BlockSpec((rows, D_MODEL), lambda j: (0, 0)),
                  pl.BlockSpec((1, D_MODEL), lambda j: (0, 0)),
                  pl.BlockSpec((rows, D_MODEL), lambda j: (0, shift_col)),
                  pl.BlockSpec((rows, D_MODEL), lambda j: (0, scale_col)),
                  pl.BlockSpec((D_MODEL, SAMPLE_TN), lambda j: (0, j))],
        out_specs=[pl.BlockSpec((rows, SAMPLE_TN), lambda j: (0, j)),
                   pl.BlockSpec((D_MODEL, SAMPLE_TN), lambda j: (0, j))],
        out_shape=[jax.ShapeDtypeStruct((rows, D_MODEL), F32),
                   jax.ShapeDtypeStruct((D_MODEL, D_MODEL), BF16)],
        scratch_shapes=[pltpu.VMEM((rows, D_MODEL), BF16)],
        compiler_params=_cparams(1),
        name="proj_sample",
    )(x, norm_g.reshape(1, D_MODEL), mod, mod, w_in)


def _discretise(lam_re, lam_im, log_dt):
    lr = jnp.minimum(lam_re, -1e-4)
    li = lam_im
    dt = jnp.exp(log_dt)
    mag = jnp.exp(lr * dt)
    ang = li * dt
    a_re = mag * jnp.cos(ang)
    a_im = mag * jnp.sin(ang)
    den = lr * lr + li * li
    num_re = a_re - 1.0
    f_re = (num_re * lr + a_im * li) / den
    f_im = (a_im * lr - num_re * li) / den
    return a_re, a_im, f_re, f_im


def _s5_prep_kernel(lam_re1, lam_im1, ldt1, c_re1, c_im1,
                    lam_re2, lam_im2, ldt2, b_re2, b_im2, gluw_t,
                    toep_ref, wz_ref, m_ref, apow_ref, gluw_ref):
    g1 = lax.broadcasted_iota(jnp.int32, (SSM_STATE, LANES), 1) // SSM_GROUP
    g2 = lax.broadcasted_iota(jnp.int32, (SSM_GROUP, OCT_STATE), 1) // SSM_STATE

    def expand1(x):
        return jnp.concatenate([jnp.where(g1 == g, x, 0.0) for g in range(OCT)], axis=0)

    def expand2(x):
        return jnp.concatenate([jnp.where(g2 == g, x, 0.0) for g in range(OCT)], axis=0)

    def split(x):
        hi = x.astype(BF16)
        return hi, (x - hi.astype(F32)).astype(BF16)

    a1_re, a1_im, _, _ = _discretise(lam_re1[...], lam_im1[...], ldt1[...])
    a2_re, a2_im, f_re, f_im = _discretise(lam_re2[...], lam_im2[...], ldt2[...])
    br, bi = b_re2[...], b_im2[...]
    bbar_re = f_re * br - f_im * bi
    bbar_im = f_re * bi + f_im * br
    cr, ci = c_re1[...], c_im1[...]
    bre_hi, bre_lo = split(expand2(bbar_re))
    bim_hi, bim_lo = split(expand2(bbar_im))

    def dot3(x_hi, x_lo, y_hi, y_lo):
        return _dot(x_hi, y_hi) + (_dot(x_hi, y_lo) + _dot(x_lo, y_hi))

    p1_re, p1_im = jnp.ones_like(a1_re), jnp.zeros_like(a1_im)
    p2_re, p2_im = jnp.ones_like(a2_re), jnp.zeros_like(a2_im)
    kk = []
    zero_blk = jnp.zeros((LANES, LANES), F32)
    for j in range(T_CHUNK + 1):
        care_hi, care_lo = split(expand1(cr * p1_re - ci * p1_im))
        caim_hi, caim_lo = split(expand1(cr * p1_im + ci * p1_re))
        m_ref[0:OCT_STATE, j * LANES:(j + 1) * LANES] = care_hi
        m_ref[OCT_STATE:2 * OCT_STATE, j * LANES:(j + 1) * LANES] = -caim_hi
        if j < T_CHUNK:
            t = T_CHUNK - 1 - j
            ba_re = expand2(bbar_re * p2_re - bbar_im * p2_im)
            ba_im = expand2(bbar_re * p2_im + bbar_im * p2_re)
            wz_ref[t * LANES:(t + 1) * LANES, 0:OCT_STATE] = ba_re.astype(BF16)
            wz_ref[t * LANES:(t + 1) * LANES, OCT_STATE:2 * OCT_STATE] = ba_im.astype(BF16)
            kk.append(dot3(bre_hi, bre_lo, care_hi, care_lo)
                      - dot3(bim_hi, bim_lo, caim_hi, caim_lo))
        if j == 1:
            apow_ref[0:1, :] = p2_re
            apow_ref[1:2, :] = p2_im
        if j == T_CHUNK:
            apow_ref[2:3, :] = p2_re
            apow_ref[3:4, :] = p2_im
        p1_re, p1_im = p1_re * a1_re - p1_im * a1_im, p1_re * a1_im + p1_im * a1_re
        p2_re, p2_im = p2_re * a2_re - p2_im * a2_im, p2_re * a2_im + p2_im * a2_re

    for t in range(T_CHUNK):
        for t2 in range(T_CHUNK):
            blk = kk[t2 - t] if t2 >= t else zero_blk
            toep_ref[t * LANES:(t + 1) * LANES, t2 * LANES:(t2 + 1) * LANES] = blk.astype(BF16)

    rg = lax.broadcasted_iota(jnp.int32, (LANES, LANES), 0) // SSM_GROUP
    lg = lax.broadcasted_iota(jnp.int32, (LANES, LANES), 1) // SSM_GROUP
    gw = jnp.where(rg == lg, gluw_t[...], 0.0).astype(BF16)
    gz = jnp.zeros((LANES, LANES), BF16)
    for a in range(TOK_PER_TILE):
        for b in range(TOK_PER_TILE):
            gluw_ref[a * LANES:(a + 1) * LANES, b * LANES:(b + 1) * LANES] = gw if a == b else gz


def _s5_prep(lam_re, lam_im, log_dt, b_re, b_im, c_re, c_im, glu_w):
    G, P, H = N_SSM_GROUPS, SSM_STATE, SSM_GROUP
    row = lambda a: a.reshape(N_OCT, 1, OCT_STATE)
    ldt = jnp.broadcast_to(log_dt[:, None], (G, P))

    def col(a_gp):
        a = jnp.swapaxes(a_gp.reshape(N_OCT, OCT, P), 1, 2)
        return jnp.broadcast_to(a[..., None], (N_OCT, P, OCT, H)).reshape(N_OCT, P, LANES)

    lay1 = lambda a_ghp: jnp.swapaxes(a_ghp.reshape(N_OCT, LANES, P), 1, 2)
    lay2 = lambda a_gph: jnp.swapaxes(a_gph.reshape(N_OCT, OCT_STATE, H), 1, 2)
    c_re1, c_im1, b_re2, b_im2 = lay1(c_re), lay1(c_im), lay2(b_re), lay2(b_im)
    gw = glu_w.reshape(N_OCT, OCT, H, 1, H)
    gluw_t = jnp.broadcast_to(gw, (N_OCT, OCT, H, OCT, H)).reshape(N_OCT, LANES, LANES)

    def spec(shape):
        return pl.BlockSpec((None,) + shape, lambda o: (o,) + (0,) * len(shape))

    m_cols = (T_CHUNK + 1) * LANES
    return pl.pallas_call(
        _s5_prep_kernel,
        grid=(N_OCT,),
        in_specs=[spec((SSM_STATE, LANES))] * 5
                 + [spec((1, OCT_STATE))] * 3 + [spec((SSM_GROUP, OCT_STATE))] * 2
                 + [spec((LANES, LANES))],
        out_specs=[spec((CHUNK_K, CHUNK_K)), spec((CHUNK_K, 2 * OCT_STATE)),
                   spec((2 * OCT_STATE, m_cols)), spec((4, OCT_STATE)),
                   spec((MXU_TILE, MXU_TILE))],
        out_shape=[jax.ShapeDtypeStruct((N_OCT, CHUNK_K, CHUNK_K), BF16),
                   jax.ShapeDtypeStruct((N_OCT, CHUNK_K, 2 * OCT_STATE), BF16),
                   jax.ShapeDtypeStruct((N_OCT, 2 * OCT_STATE, m_cols), BF16),
                   jax.ShapeDtypeStruct((N_OCT, 4, OCT_STATE), F32),
                   jax.ShapeDtypeStruct((N_OCT, MXU_TILE, MXU_TILE), BF16)],
        compiler_params=_cparams(1),
        name="s5_prep",
    )(col(lam_re), col(lam_im), col(ldt), c_re1, c_im1,
      row(lam_re), row(lam_im), row(ldt), b_re2, b_im2, gluw_t)


GELU_C0 = 0.7978845608028654
GELU_C1 = GELU_C0 * 0.044715


def _glu_out(y, gluw, glub):
    gy = y * (0.5 + 0.5 * jnp.tanh(y * (GELU_C0 + GELU_C1 * (y * y))))
    z = _dot(gy.astype(BF16), gluw) + glub
    return gy * (0.5 + 0.5 * jnp.tanh(0.5 * z))


def _s5_prompt_kernel(x_ref, toep_ref, wz_ref, m_ref, apow_ref, d_ref, gluw_ref, glub_ref,
                      y_ref, sre_ref, sim_ref, xr_ref, z_ref, yv_ref, *, n_seq, row_chunk):
    n_rows = x_ref.shape[0] // T_CHUNK
    n_blk = n_rows // row_chunk

    def tokens(b, t):
        return pl.ds(b * (row_chunk * T_CHUNK) + t, row_chunk, stride=T_CHUNK)

    for b in range(n_blk):
        rows = pl.ds(b * row_chunk, row_chunk)
        for t in range(T_CHUNK):
            xr_ref[rows, t * LANES:(t + 1) * LANES] = x_ref[tokens(b, t), :].astype(BF16)
        xr = xr_ref[rows, :]
        z_ref[rows, :] = _dot(xr, wz_ref[...])
        for nt in range(CHUNK_K // MXU_TILE):
            k_end = (nt + 1) * MXU_TILE
            cols = slice(nt * MXU_TILE, k_end)
            yv_ref[rows, cols] = _dot(xr[:, 0:k_end], toep_ref[0:k_end, cols])

    are, aim = apow_ref[2:3, :], apow_ref[3:4, :]
    lo = lax.broadcasted_iota(jnp.int32, (SUBLANES, OCT_STATE), 0) < n_seq

    def step(k, carry):
        pre, pim = carry
        rows = pl.ds(pl.multiple_of(k * SUBLANES, SUBLANES), SUBLANES)
        zre = z_ref[rows, 0:OCT_STATE]
        zim = z_ref[rows, OCT_STATE:2 * OCT_STATE]
        w1re = are * pre - aim * pim + zre
        w1im = are * pim + aim * pre + zim
        r1re = pltpu.roll(w1re, n_seq, axis=0)
        r1im = pltpu.roll(w1im, n_seq, axis=0)
        w2re = are * r1re - aim * r1im + zre
        w2im = are * r1im + aim * r1re + zim
        z_ref[rows, 0:OCT_STATE] = jnp.where(lo, pre, r1re)
        z_ref[rows, OCT_STATE:2 * OCT_STATE] = jnp.where(lo, pim, r1im)
        nre = jnp.where(lo, pltpu.roll(w2re, n_seq, axis=0), w2re)
        nim = jnp.where(lo, pltpu.roll(w2im, n_seq, axis=0), w2im)
        return nre, nim

    zeros = jnp.zeros((SUBLANES, OCT_STATE), F32)
    fre, fim = lax.fori_loop(0, n_rows // SUBLANES, step, (zeros, zeros))
    sre_ref[...] = fre[0:n_seq, :]
    sim_ref[...] = fim[0:n_seq, :]

    d2 = jnp.concatenate([d_ref[...]] * TOK_PER_TILE, axis=1)
    glub2 = jnp.concatenate([glub_ref[...]] * TOK_PER_TILE, axis=1)

    for b in range(n_blk):
        rows = pl.ds(b * row_chunk, row_chunk)
        y = yv_ref[rows, :] + _dot(z_ref[rows, :].astype(BF16), m_ref[:, LANES:])
        for t0 in range(0, T_CHUNK, TOK_PER_TILE):
            toks = [tokens(b, t0 + i) for i in range(TOK_PER_TILE)]
            u = jnp.concatenate([x_ref[tok, :] for tok in toks], axis=1)
            yt = y[:, t0 * LANES:(t0 + TOK_PER_TILE) * LANES] + d2 * u
            out = _glu_out(yt, gluw_ref[...], glub2)
            for i, tok in enumerate(toks):
                y_ref[tok, :] = out[:, i * LANES:(i + 1) * LANES]


def _s5_prompt(u_flat, ops, d_skip, glu_b, *, n_seq):
    toep, wz, m, apow, gluw = ops
    rows = u_flat.shape[0]
    n_rows = rows // T_CHUNK
    oct_spec = lambda shape: pl.BlockSpec((None,) + shape, lambda o: (o, 0, 0))
    kern = functools.partial(_s5_prompt_kernel, n_seq=n_seq, row_chunk=256)
    return pl.pallas_call(
        kern,
        grid=(N_OCT,),
        in_specs=[pl.BlockSpec((rows, LANES), lambda o: (0, o)),
                  oct_spec(toep.shape[1:]), oct_spec(wz.shape[1:]), oct_spec(m.shape[1:]),
                  oct_spec(apow.shape[1:]), oct_spec((1, LANES)), oct_spec(gluw.shape[1:]),
                  oct_spec((1, LANES))],
        out_specs=[pl.BlockSpec((rows, LANES), lambda o: (0, o)),
                   pl.BlockSpec((n_seq, OCT_STATE), lambda o: (0, o)),
                   pl.BlockSpec((n_seq, OCT_STATE), lambda o: (0, o))],
        out_shape=[jax.ShapeDtypeStruct((rows, D_SSM), F32),
                   jax.ShapeDtypeStruct((n_seq, N_OCT * OCT_STATE), F32),
                   jax.ShapeDtypeStruct((n_seq, N_OCT * OCT_STATE), F32)],
        scratch_shapes=[pltpu.VMEM((n_rows, CHUNK_K), BF16),
                        pltpu.VMEM((n_rows, 2 * OCT_STATE), F32),
                        pltpu.VMEM((n_rows, CHUNK_K), F32)],
        compiler_params=_cparams(1),
        name="s5_prompt",
    )(u_flat, toep, wz, m, apow, d_skip.reshape(N_OCT, 1, LANES), gluw,
      glu_b.reshape(N_OCT, 1, LANES))


def _s5_sample_kernel(u_ref, s0re_ref, s0im_ref, wz_ref, m_ref, apow_ref, d_ref, gluw_ref,
                      glub_ref, y_ref, sre_ref, sim_ref):
    u = u_ref[...]
    z = _dot(u.astype(BF16), wz_ref[...])
    are, aim = apow_ref[0:1, :], apow_ref[1:2, :]
    s0re, s0im = s0re_ref[...], s0im_ref[...]
    nre = are * s0re - aim * s0im + z[:, 0:OCT_STATE]
    nim = are * s0im + aim * s0re + z[:, OCT_STATE:2 * OCT_STATE]
    sre_ref[...] = nre
    sim_ref[...] = nim
    y = (_dot(nre.astype(BF16), m_ref[0:OCT_STATE, :])
         + _dot(nim.astype(BF16), m_ref[OCT_STATE:2 * OCT_STATE, :])
         + d_ref[...] * u)
    y_ref[...] = _glu_out(y, gluw_ref[...], glub_ref[...])


def _s5_sample(u, s0_re, s0_im, ops, d_skip, glu_b):
    _, wz, m, apow, gluw = ops
    rows = u.shape[0]
    oct_spec = lambda shape: pl.BlockSpec((None,) + shape, lambda o: (o, 0, 0))
    return pl.pallas_call(
        _s5_sample_kernel,
        grid=(N_OCT,),
        in_specs=[pl.BlockSpec((rows, LANES), lambda o: (0, o)),
                  pl.BlockSpec((rows, OCT_STATE), lambda o: (0, o)),
                  pl.BlockSpec((rows, OCT_STATE), lambda o: (0, o)),
                  pl.BlockSpec((None, LANES, 2 * OCT_STATE), lambda o: (o, T_CHUNK - 1, 0)),
                  pl.BlockSpec((None, 2 * OCT_STATE, LANES), lambda o: (o, 0, 0)),
                  oct_spec(apow.shape[1:]), oct_spec((1, LANES)), oct_spec((LANES, LANES)),
                  oct_spec((1, LANES))],
        out_specs=[pl.BlockSpec((rows, LANES), lambda o: (0, o)),
                   pl.BlockSpec((rows, OCT_STATE), lambda o: (0, o)),
                   pl.BlockSpec((rows, OCT_STATE), lambda o: (0, o))],
        out_shape=[jax.ShapeDtypeStruct((rows, D_SSM), F32),
                   jax.ShapeDtypeStruct((rows, N_OCT * OCT_STATE), F32),
                   jax.ShapeDtypeStruct((rows, N_OCT * OCT_STATE), F32)],
        compiler_params=_cparams(1),
        name="s5_sample",
    )(u, s0_re, s0_im, wz, m, apow, d_skip.reshape(N_OCT, 1, LANES), gluw,
      glu_b.reshape(N_OCT, 1, LANES))


def _pool_linear(z, gi, pw_ref, pb_ref, ps_ref):
    lanes = slice(gi * POOL_GROUP, (gi + 1) * POOL_GROUP)
    return (_dot(z.astype(BF16), pw_ref[gi]) + pb_ref[:, lanes]) * ps_ref[:, lanes]


def _pool_prompt_kernel(v_ref, pw_ref, pb_ref, ps_ref, y_ref):
    n = v_ref.shape[0]
    pos = lax.broadcasted_iota(jnp.int32, (n, POOL_GROUP), 0)
    for gi, w in enumerate(POOL_WINDOWS):
        lanes = slice(gi * POOL_GROUP, (gi + 1) * POOL_GROUP)
        v = v_ref[:, lanes]
        s, k = v, 1
        while k < w:
            s = s + jnp.where(pos >= k, pltpu.roll(s, k, axis=0), 0.0)
            k *= 2
        cnt = jnp.clip(pos + 1, 1, w).astype(F32)
        y_ref[:, lanes] = _pool_linear(s / cnt - v, gi, pw_ref, pb_ref, ps_ref)


def _pool_prompt(v, pool_w, pool_b, pool_scale, *, n_seq):
    rows = v.shape[0]
    seq_len = rows // n_seq
    return pl.pallas_call(
        _pool_prompt_kernel,
        grid=(n_seq,),
        in_specs=[pl.BlockSpec((seq_len, D_POOL), lambda n: (n, 0)),
                  pl.BlockSpec(pool_w.shape, lambda n: (0, 0, 0)),
                  pl.BlockSpec((1, D_POOL), lambda n: (0, 0)),
                  pl.BlockSpec((1, D_POOL), lambda n: (0, 0))],
        out_specs=pl.BlockSpec((seq_len, D_POOL), lambda n: (n, 0)),
        out_shape=jax.ShapeDtypeStruct((rows, D_POOL), F32),
        compiler_params=_cparams(1),
        name="pool_prompt",
    )(v, pool_w, pool_b.reshape(1, D_POOL), pool_scale.reshape(1, D_POOL))


def _pool_sample_kernel(hist_ref, v_ref, pw_ref, pb_ref, ps_ref, y_ref, new_ref, sum_ref):
    r = pl.program_id(0)

    @pl.when(r == 0)
    def _():
        sum_ref[...] = jnp.zeros(sum_ref.shape, F32)

    @pl.when(r < POOL_HIST)
    def _():
        h = hist_ref[...]
        new_ref[...] = h
        for gi, w in enumerate(POOL_WINDOWS):
            lanes = slice(gi * POOL_GROUP, (gi + 1) * POOL_GROUP)
            sum_ref[:, lanes] += jnp.where(r >= POOL_HIST - (w - 1), h[:, lanes], 0.0)

    @pl.when(r == POOL_HIST)
    def _():
        new_ref[...] = v_ref[...]
        for gi, w in enumerate(POOL_WINDOWS):
            lanes = slice(gi * POOL_GROUP, (gi + 1) * POOL_GROUP)
            v = v_ref[:, lanes]
            z = (sum_ref[:, lanes] + v) / float(w) - v
            y_ref[:, lanes] = _pool_linear(z, gi, pw_ref, pb_ref, ps_ref)


def _pool_sample(hist_t, proj, pool_w, pool_b, pool_scale):
    rows = proj.shape[0]
    return pl.pallas_call(
        _pool_sample_kernel,
        grid=(POOL_HIST + 1,),
        in_specs=[pl.BlockSpec((None, rows, D_POOL),
                               lambda r: (jnp.minimum(r, POOL_HIST - 1), 0, 0)),
                  pl.BlockSpec((rows, D_POOL), lambda r: (0, D_SSM // D_POOL)),
                  pl.BlockSpec(pool_w.shape, lambda r: (0, 0, 0)),
                  pl.BlockSpec((1, D_POOL), lambda r: (0, 0)),
                  pl.BlockSpec((1, D_POOL), lambda r: (0, 0))],
        out_specs=[pl.BlockSpec((rows, D_POOL), lambda r: (0, 0)),
                   pl.BlockSpec((None, rows, D_POOL), lambda r: (jnp.maximum(r - 1, 0), 0, 0))],
        out_shape=[jax.ShapeDtypeStruct((rows, D_POOL), F32),
                   jax.ShapeDtypeStruct(hist_t.shape, F32)],
        scratch_shapes=[pltpu.VMEM((rows, D_POOL), F32)],
        compiler_params=_cparams(1),
        name="pool_sample",
    )(hist_t, proj, pool_w, pool_b.reshape(1, D_POOL), pool_scale.reshape(1, D_POOL))


def _outproj_kernel(x_ref, ys_ref, yp_ref, gate_ref, w_ref, o_ref, *, blocks_per_seq):
    tm = x_ref.shape[0]
    ys = ys_ref[...].reshape(tm, D_SSM).astype(BF16)
    yp = yp_ref[...].astype(BF16)
    mix = _dot(ys, w_ref[0:D_SSM, :]) + _dot(yp, w_ref[D_SSM:, :])
    o_ref[...] = x_ref[...] + _mod_rows(gate_ref, blocks_per_seq) * mix


def _outproj(x, ys, yp, mod, w_out, *, tm, n_seq):
    rows = x.shape[0]
    blocks_per_seq = (rows // n_seq) // tm
    ys_spec = pl.BlockSpec((tm // T_CHUNK, None, T_CHUNK, D_SSM),
                           lambda i: (i % blocks_per_seq, i // blocks_per_seq, 0, 0))
    gate_spec = _mod_specs(False, tm, PROMPT_MOD_BLOCK, MIX_SUBLAYER)[2]
    return pl.pallas_call(
        functools.partial(_outproj_kernel, blocks_per_seq=blocks_per_seq),
        grid=(rows // tm,),
        in_specs=[pl.BlockSpec((tm, D_MODEL), lambda i: (i, 0)),
                  ys_spec,
                  pl.BlockSpec((tm, D_POOL), lambda i: (i, 0)),
                  gate_spec,
                  pl.BlockSpec((D_MODEL, D_MODEL), lambda i: (0, 0),
                               pipeline_mode=pl.Buffered(1))],
        out_specs=pl.BlockSpec((tm, D_MODEL), lambda i: (i, 0)),
        out_shape=jax.ShapeDtypeStruct((rows, D_MODEL), F32),
        compiler_params=_cparams(1),
        name="outproj",
    )(x, ys, yp, mod, w_out)


def _outproj_sample_kernel(x_ref, ys_ref, yp_ref, gate_ref, w_ref, o_ref, wo_ref):
    wo_ref[...] = w_ref[...].astype(BF16)
    mix = (_dot(ys_ref[...].astype(BF16), wo_ref[0:D_SSM, :])
           + _dot(yp_ref[...].astype(BF16), wo_ref[D_SSM:, :]))
    o_ref[...] = x_ref[...] + gate_ref[...] * mix


def _outproj_sample(x, ys, yp, mod, w_out):
    rows = x.shape[0]
    n_col = D_MODEL // SAMPLE_TN
    gate_col = (3 * MIX_SUBLAYER + 2) * n_col
    return pl.pallas_call(
        _outproj_sample_kernel,
        grid=(n_col,),
        in_specs=[pl.BlockSpec((rows, SAMPLE_TN), lambda j: (0, j)),
                  pl.BlockSpec((rows, D_SSM), lambda j: (0, 0)),
                  pl.BlockSpec((rows, D_POOL), lambda j: (0, 0)),
                  pl.BlockSpec((rows, SAMPLE_TN), lambda j: (0, gate_col + j)),
                  pl.BlockSpec((D_MODEL, SAMPLE_TN), lambda j: (0, j))],
        out_specs=[pl.BlockSpec((rows, SAMPLE_TN), lambda j: (0, j)),
                   pl.BlockSpec((D_MODEL, SAMPLE_TN), lambda j: (0, j))],
        out_shape=[jax.ShapeDtypeStruct((rows, D_MODEL), F32),
                   jax.ShapeDtypeStruct((D_MODEL, D_MODEL), BF16)],
        compiler_params=_cparams(1),
        name="outproj_sample",
    )(x, ys, yp, mod, w_out)


def kernel(x_prompt, x_sample, state_ssm_re, state_ssm_im, state_pool, c_prompt, c_sample, ada_w, ada_b, ffn1_norm, ffn1_w_gate, ffn1_w_up, ffn1_w_down, mix_norm, w_in, ssm_lambda_re, ssm_lambda_im, ssm_log_dt, ssm_b_re, ssm_b_im, ssm_c_re, ssm_c_im, ssm_d, ssm_glu_w, ssm_glu_b, pool_w, pool_b, pool_scale, w_out, ffn2_norm, ffn2_w_gate, ffn2_w_up, ffn2_w_down, final_norm):
    n_p, seq, _ = x_prompt.shape
    n_s = x_sample.shape[0]
    assert (n_p, n_s) == (N_PROMPT, N_SAMPLE) and n_p <= SUBLANES
    G, P = N_SSM_GROUPS, SSM_STATE

    ada_b2 = ada_b.reshape(1, N_MOD_COLS)
    c_all = jnp.concatenate([c_sample, c_prompt, jnp.zeros((SUBLANES - n_p, D_MODEL), F32)], axis=0)
    mod_early = _adaln(c_all, ada_w[0], ada_b2, N_EARLY_MOD_COLS)
    late_adaln = (c_all, ada_w[0], ada_b2, N_EARLY_MOD_COLS)
    pw = pool_w[0].astype(BF16)
    ops = _s5_prep(ssm_lambda_re[0], ssm_lambda_im[0], ssm_log_dt[0], ssm_b_re[0], ssm_b_im[0],
                   ssm_c_re[0], ssm_c_im[0], ssm_glu_w[0])

    xs = x_sample.reshape(n_s, D_MODEL)
    xp = x_prompt.reshape(n_p * seq, D_MODEL)
    blocks_per_seq = seq // FFN_TM
    sample_rows = _RowGroup(n_s, n_s, per_row=True, first_block=0, resident=True)
    head_rows = _RowGroup(FFN_TM, FFN_ROW_CHUNK, per_row=False, first_block=0, resident=True)
    tail_rows = _RowGroup(FFN_TM, FFN_ROW_CHUNK, per_row=False, first_block=1, resident=False)

    def ffn(xs, xp, norm_g, mod, sub, wg, wu, wd, last, adaln=None):
        ys, yp, bg, bu, bd = _ffn([xs, xp], (sample_rows, head_rows), norm_g, mod, sub, wg, wu, wd,
                                  final_norm, n_blocks=1, tf=FFN_HEAD_TF, final_norm=last,
                                  blocks_per_seq=blocks_per_seq)
        yp, *late = _ffn([xp], (tail_rows,), norm_g, mod, sub, bg, bu, bd, final_norm,
                         n_blocks=n_p * blocks_per_seq - 1, tf=FFN_TF, final_norm=last,
                         blocks_per_seq=blocks_per_seq, alias_into=yp, adaln=adaln)
        return (ys, yp, *late)

    xs, xp, mod = ffn(xs, xp, ffn1_norm[0], mod_early, 0, ffn1_w_gate[0], ffn1_w_up[0],
                      ffn1_w_down[0], False, adaln=late_adaln)

    proj_s, win = _proj_sample(xs, mix_norm[0], mod, w_in[0])
    ys_s, sre_s, sim_s = _s5_sample(proj_s, state_ssm_re[0].reshape(n_s, G * P),
                                    state_ssm_im[0].reshape(n_s, G * P), ops, ssm_d[0],
                                    ssm_glu_b[0])
    yp_s, pool_s_t = _pool_sample(jnp.swapaxes(state_pool[0], 0, 1), proj_s, pw, pool_b[0],
                                  pool_scale[0])
    pool_s = jnp.swapaxes(pool_s_t, 0, 1)
    xs, wout = _outproj_sample(xs, ys_s, yp_s, mod, w_out[0])

    u4, v_p = _proj(xp, mix_norm[0], mod, win, tm=MIX_TM, n_seq=n_p)
    ys_p, sre_p, sim_p = _s5_prompt(u4.reshape(n_p * seq, D_SSM), ops, ssm_d[0], ssm_glu_b[0],
                                    n_seq=n_p)
    yp_p = _pool_prompt(v_p, pw, pool_b[0], pool_scale[0], n_seq=n_p)
    xp = _outproj(xp, ys_p.reshape(seq // T_CHUNK, n_p, T_CHUNK, D_SSM), yp_p, mod, wout,
                  tm=MIX_TM, n_seq=n_p)

    y_sample, y_prompt = ffn(xs, xp, ffn2_norm[0], mod, FFN2_SUBLAYER, ffn2_w_gate[0], ffn2_w_up[0],
                             ffn2_w_down[0], True)

    pool_p = v_p.reshape(n_p, seq, D_POOL)[:, seq - POOL_HIST:, :][None]
    return (y_prompt.reshape(n_p, seq, D_MODEL), y_sample.reshape(n_s, 1, D_MODEL),
            sre_p.reshape(1, n_p, G, P), sim_p.reshape(1, n_p, G, P), pool_p,
            sre_s.reshape(1, n_s, G, P), sim_s.reshape(1, n_s, G, P), pool_s[None])
```

```python
import functools
from typing import NamedTuple

import jax
import jax.numpy as jnp
from jax import lax
from jax.experimental import pallas as pl
from jax.experimental.pallas import tpu as pltpu

F32 = jnp.float32
BF16 = jnp.bfloat16

D_MODEL = 2048
D_FF = 5632
D_SSM = 1024
D_POOL = 1024
SSM_GROUP = 16
SSM_STATE = 64
N_SSM_GROUPS = 64
POOL_WINDOWS = (2, 4, 8, 16)
POOL_GROUP = 256
POOL_HIST = 15
N_MOD_COLS = 9 * D_MODEL
N_PROMPT = 4
N_SAMPLE = 128
EPS = 1e-6
FFN_RES = 0.5

LANES = 128
SUBLANES = 8
MXU_TILE = 256
VMEM_LIMIT = 58 * 1024 * 1024

PROMPT_MOD_BLOCK = N_SAMPLE // SUBLANES
NORM_ROWS = 32
FFN1_SUBLAYER, MIX_SUBLAYER, FFN2_SUBLAYER = 0, 1, 2

FFN_TM = 1024
FFN_ROW_CHUNK = 512
FFN_TF = 512
FFN_HEAD_TF = 256
MIX_TM = 512

OCT = LANES // SSM_GROUP
N_OCT = N_SSM_GROUPS // OCT
T_CHUNK = 8
OCT_STATE = OCT * SSM_STATE
CHUNK_K = T_CHUNK * LANES
TOK_PER_TILE = MXU_TILE // LANES


def _cparams(n_axes):
    return pltpu.CompilerParams(dimension_semantics=("arbitrary",) * n_axes,
                                vmem_limit_bytes=VMEM_LIMIT)


def _dot(a, b):
    return jnp.dot(a, b, preferred_element_type=F32)


ADALN_TN = 1024


def _adaln_kernel(c_ref, w_ref, b_ref, o_ref):
    c = c_ref[...]
    sc = (c * jax.nn.sigmoid(c)).astype(BF16)
    o_ref[...] = _dot(sc, w_ref[...].astype(BF16)) + b_ref[...]


def _adaln(c_all, ada_w, ada_b):
    rows, n_cols = c_all.shape[0], ada_w.shape[1]
    return pl.pallas_call(
        _adaln_kernel,
        grid=(n_cols // ADALN_TN,),
        in_specs=[pl.BlockSpec((rows, D_MODEL), lambda j: (0, 0)),
                  pl.BlockSpec((D_MODEL, ADALN_TN), lambda j: (0, j)),
                  pl.BlockSpec((1, ADALN_TN), lambda j: (0, j))],
        out_specs=pl.BlockSpec((rows, ADALN_TN), lambda j: (0, j)),
        out_shape=jax.ShapeDtypeStruct((rows, n_cols), F32),
        compiler_params=_cparams(1),
        name="adaln",
    )(c_all, ada_w, ada_b)


def _norm_mod(x, g, shift, scale):
    ms = jnp.mean(x * x, axis=-1, keepdims=True)
    y = x * lax.rsqrt(ms + EPS) * g
    return y * (1.0 + scale) + shift


def _mod_specs(per_row, tm, prompt_row_block, sub):
    specs = []
    for m in range(3):
        col = 3 * sub + m
        if per_row:
            specs.append(pl.BlockSpec((tm, D_MODEL), lambda i, *_, col=col: (i, col)))
        else:
            specs.append(pl.BlockSpec((SUBLANES, D_MODEL),
                                      lambda i, *_, col=col: (prompt_row_block, col)))
    return specs


def _mod_rows(ref, blocks_per_seq, rows=None):
    if blocks_per_seq:
        return ref[pl.ds(pl.program_id(0) // blocks_per_seq, 1), :]
    return ref[...] if rows is None else ref[rows, :]


class _RowGroup(NamedTuple):
    rows: int
    row_chunk: int
    per_row: bool
    first_block: int
    resident: bool


def _ffn_kernel(*refs, groups, blocks_per_seq, final_norm, emit_bf16, has_alias):
    n = len(groups)
    g_ref, wg_ref, wu_ref, wd_ref, fg_ref = refs[4 * n:4 * n + 5]
    outs = refs[4 * n + 5 + int(has_alias):-1]
    h_ref = refs[-1]
    j = pl.program_id(1)
    last_j = pl.num_programs(1) - 1

    if emit_bf16:
        bf16_refs = outs[n:n + 3]
        for w_ref, wo_ref in zip((wg_ref, wu_ref, wd_ref), bf16_refs):
            wo_ref[...] = w_ref[...].astype(BF16)
        wg_ref, wu_ref, wd_ref = bf16_refs

    chunks, h_base = [], 0
    for k, grp in enumerate(groups):
        chunks += [(k, r, h_base + r) for r in range(0, grp.rows, grp.row_chunk)]
        h_base += grp.rows

    def mod_rows(k, ref, rows):
        if groups[k].per_row:
            return ref[rows, :]
        seq = (pl.program_id(0) + groups[k].first_block) // blocks_per_seq
        return ref[pl.ds(seq, 1), :]

    def norm_chunk(c):
        k, r, hr = chunks[c]
        x_ref, shift_ref, scale_ref, _ = refs[4 * k:4 * k + 4]
        for s in range(0, groups[k].row_chunk, NORM_ROWS):
            sr = pl.ds(r + s, NORM_ROWS)
            h = _norm_mod(x_ref[sr, :], g_ref[...], mod_rows(k, shift_ref, sr),
                          mod_rows(k, scale_ref, sr))
            h_ref[pl.ds(hr + s, NORM_ROWS), :] = h.astype(BF16)

    def chunk(c, first, last):
        k, r, hr = chunks[c]
        size = groups[k].row_chunk
        x_ref, _, _, gate_ref = refs[4 * k:4 * k + 4]
        o_ref = outs[k]
        rows = pl.ds(r, size)
        if first and c + 1 < len(chunks):
            norm_chunk(c + 1)
        h = h_ref[pl.ds(hr, size), :]
        g = _dot(h, wg_ref[...])
        u = _dot(h, wu_ref[...])
        a = (g * jax.nn.sigmoid(g) * u).astype(BF16)
        d = _dot(a, wd_ref[...])
        acc = d if first else o_ref[rows, :] + d
        if not last:
            o_ref[rows, :] = acc
            return
        o_ref[rows, :] = x_ref[rows, :] + FFN_RES * mod_rows(k, gate_ref, rows) * acc
        if final_norm:
            for s in range(0, size, NORM_ROWS):
                sr = pl.ds(r + s, NORM_ROWS)
                y = o_ref[sr, :]
                ms = jnp.mean(y * y, axis=-1, keepdims=True)
                o_ref[sr, :] = y * lax.rsqrt(ms + EPS) * fg_ref[...]

    def run(first, last):
        if first:
            norm_chunk(0)
        for c in range(len(chunks)):
            chunk(c, first, last)

    pl.when(j == 0)(lambda: run(True, False))
    pl.when(jnp.logical_and(j > 0, j < last_j))(lambda: run(False, False))
    pl.when(j == last_j)(lambda: run(False, True))


def _ffn(xs, groups, norm_g, mod, sub, wg, wu, wd, final_g, *, n_blocks, tf, final_norm,
         blocks_per_seq, alias_into=None):
    emit_bf16 = wg.dtype == F32
    assert not emit_bf16 or n_blocks == 1

    def row_spec(grp, col=0):
        mode = dict(pipeline_mode=pl.Buffered(1)) if grp.resident else {}
        return pl.BlockSpec((grp.rows, D_MODEL), lambda i, j: (i + grp.first_block, col), **mode)

    in_specs, operands = [], []
    for x, grp in zip(xs, groups):
        in_specs.append(row_spec(grp))
        operands.append(x)
        for m in range(3):
            col = 3 * sub + m
            if grp.per_row:
                in_specs.append(row_spec(grp, col))
            else:
                in_specs.append(pl.BlockSpec((SUBLANES, D_MODEL),
                                             lambda i, j, col=col: (PROMPT_MOD_BLOCK, col)))
            operands.append(mod)
    w_specs = [pl.BlockSpec((D_MODEL, tf), lambda i, j: (0, j)),
               pl.BlockSpec((D_MODEL, tf), lambda i, j: (0, j)),
               pl.BlockSpec((tf, D_MODEL), lambda i, j: (j, 0))]
    vec_spec = pl.BlockSpec((1, D_MODEL), lambda i, j: (0, 0))
    in_specs += [vec_spec, *w_specs, vec_spec]
    operands += [norm_g.reshape(1, D_MODEL), wg, wu, wd, final_g.reshape(1, D_MODEL)]
    aliases = {}
    if alias_into is not None:
        aliases = {len(operands): len(groups) - 1}
        in_specs.append(pl.BlockSpec(memory_space=pl.ANY))
        operands.append(alias_into)

    out_specs = [row_spec(grp) for grp in groups]
    out_shape = [jax.ShapeDtypeStruct(x.shape, F32) for x in xs]
    if emit_bf16:
        out_specs += w_specs
        out_shape += [jax.ShapeDtypeStruct(w.shape, BF16) for w in (wg, wu, wd)]
    kern = functools.partial(_ffn_kernel, groups=groups, blocks_per_seq=blocks_per_seq,
                             final_norm=final_norm, emit_bf16=emit_bf16,
                             has_alias=alias_into is not None)
    return pl.pallas_call(
        kern,
        grid=(n_blocks, D_FF // tf),
        in_specs=in_specs,
        out_specs=out_specs,
        out_shape=out_shape,
        input_output_aliases=aliases,
        scratch_shapes=[pltpu.VMEM((sum(grp.rows for grp in groups), D_MODEL), BF16)],
        compiler_params=_cparams(2),
        name="ffn",
    )(*operands)


POOL_TAIL = 16


def _proj_kernel(x_ref, g_ref, shift_ref, scale_ref, w_ref, pw_ref, pb_ref, ps_ref,
                 u_ref, y_ref, tail_out_ref, tail_ref, h_ref, *, blocks_per_seq):
    tm = x_ref.shape[0]
    half = tm // 2
    blk = pl.program_id(0) % blocks_per_seq
    sh = _mod_rows(shift_ref, blocks_per_seq)
    sc = _mod_rows(scale_ref, blocks_per_seq)

    @pl.when(blk == 0)
    def _():
        tail_ref[...] = jnp.zeros(tail_ref.shape, F32)

    def norm_half(r):
        for s in range(0, half, NORM_ROWS):
            rows = pl.ds(r * half + s, NORM_ROWS)
            h_ref[rows, :] = _norm_mod(x_ref[rows, :], g_ref[...], sh, sc).astype(BF16)

    norm_half(0)
    norm_half(1)
    pv = jnp.concatenate([_dot(h_ref[pl.ds(r * half, half), :], w_ref[:, D_SSM:])
                          for r in range(2)], axis=0)
    for r in range(2):
        pu = _dot(h_ref[pl.ds(r * half, half), :], w_ref[:, 0:D_SSM])
        u_ref[pl.ds(r * half // T_CHUNK, half // T_CHUNK), :, :] = pu.reshape(
            half // T_CHUNK, T_CHUNK, D_SSM)

    row_ext = lax.broadcasted_iota(jnp.int32, (POOL_TAIL + tm, POOL_GROUP), 0)
    pos = blk * tm + lax.broadcasted_iota(jnp.int32, (tm, POOL_GROUP), 0)
    for gi, w in enumerate(POOL_WINDOWS):
        lanes = slice(gi * POOL_GROUP, (gi + 1) * POOL_GROUP)
        v = pv[:, lanes]
        s = jnp.concatenate([tail_ref[:, lanes], v], axis=0)
        k = 1
        while k < w:
            s = s + jnp.where(row_ext >= k, pltpu.roll(s, k, axis=0), 0.0)
            k *= 2
        cnt = jnp.clip(pos + 1, 1, w).astype(F32)
        z = s[POOL_TAIL:, :] / cnt - v
        y_ref[:, lanes] = _pool_linear(z, gi, pw_ref, pb_ref, ps_ref)
    tail = pv[tm - POOL_TAIL:, :]
    tail_ref[...] = tail
    tail_out_ref[...] = tail


def _proj(x, norm_g, mod, w_in, pool_w, pool_b, pool_scale, *, tm, n_seq):
    rows = x.shape[0]
    seq_len = rows // n_seq
    blocks_per_seq = seq_len // tm
    u_shape = (seq_len // T_CHUNK, n_seq, T_CHUNK, D_SSM)
    u_spec = pl.BlockSpec((tm // T_CHUNK, None, T_CHUNK, D_SSM),
                          lambda i: (i % blocks_per_seq, i // blocks_per_seq, 0, 0))
    shift_spec, scale_spec, _ = _mod_specs(False, tm, PROMPT_MOD_BLOCK, MIX_SUBLAYER)
    vec_spec = pl.BlockSpec((1, D_POOL), lambda i: (0, 0))
    return pl.pallas_call(
        functools.partial(_proj_kernel, blocks_per_seq=blocks_per_seq),
        grid=(rows // tm,),
        in_specs=[pl.BlockSpec((tm, D_MODEL), lambda i: (i, 0)),
                  pl.BlockSpec((1, D_MODEL), lambda i: (0, 0)),
                  shift_spec, scale_spec,
                  pl.BlockSpec((D_MODEL, D_MODEL), lambda i: (0, 0),
                               pipeline_mode=pl.Buffered(1)),
                  pl.BlockSpec(pool_w.shape, lambda i: (0, 0, 0)), vec_spec, vec_spec],
        out_specs=[u_spec, pl.BlockSpec((tm, D_POOL), lambda i: (i, 0)),
                   pl.BlockSpec((None, POOL_TAIL, D_POOL), lambda i: (i // blocks_per_seq, 0, 0))],
        out_shape=[jax.ShapeDtypeStruct(u_shape, F32),
                   jax.ShapeDtypeStruct((rows, D_POOL), F32),
                   jax.ShapeDtypeStruct((n_seq, POOL_TAIL, D_POOL), F32)],
        scratch_shapes=[pltpu.VMEM((POOL_TAIL, D_POOL), F32), pltpu.VMEM((tm, D_MODEL), BF16)],
        compiler_params=_cparams(1),
        name="proj",
    )(x, norm_g.reshape(1, D_MODEL), mod, mod, w_in, pool_w, pool_b.reshape(1, D_POOL),
      pool_scale.reshape(1, D_POOL))


SAMPLE_TN = 512


def _proj_sample_kernel(x_ref, g_ref, shift_ref, scale_ref, w_ref, p_ref, wo_ref, h_ref):
    @pl.when(pl.program_id(0) == 0)
    def _():
        h = _norm_mod(x_ref[...], g_ref[...], shift_ref[...], scale_ref[...])
        h_ref[...] = h.astype(BF16)
    wo_ref[...] = w_ref[...].astype(BF16)
    p_ref[...] = _dot(h_ref[...], wo_ref[...])


def _proj_sample(x, norm_g, mod, w_in):
    rows = x.shape[0]
    shift_col, scale_col = 3 * MIX_SUBLAYER, 3 * MIX_SUBLAYER + 1
    return pl.pallas_call(
        _proj_sample_kernel,
        grid=(D_MODEL // SAMPLE_TN,),
        in_specs=[pl.BlockSpec((rows, D_MODEL), lambda j: (0, 0)),
                  pl.BlockSpec((1, D_MODEL), lambda j: (0, 0)),
                  pl.BlockSpec((rows, D_MODEL), lambda j: (0, shift_col)),
                  pl.BlockSpec((rows, D_MODEL), lambda j: (0, scale_col)),
                  pl.BlockSpec((D_MODEL, SAMPLE_TN), lambda j: (0, j))],
        out_specs=[pl.BlockSpec((rows, SAMPLE_TN), lambda j: (0, j)),
                   pl.BlockSpec((D_MODEL, SAMPLE_TN), lambda j: (0, j))],
        out_shape=[jax.ShapeDtypeStruct((rows, D_MODEL), F32),
                   jax.ShapeDtypeStruct((D_MODEL, D_MODEL), BF16)],
        scratch_shapes=[pltpu.VMEM((rows, D_MODEL), BF16)],
        compiler_params=_cparams(1),
        name="proj_sample",
    )(x, norm_g.reshape(1, D_MODEL), mod, mod, w_in)


def _discretise(lam_re, lam_im, log_dt):
    lr = jnp.minimum(lam_re, -1e-4)
    li = lam_im
    dt = jnp.exp(log_dt)
    mag = jnp.exp(lr * dt)
    ang = li * dt
    a_re = mag * jnp.cos(ang)
    a_im = mag * jnp.sin(ang)
    den = lr * lr + li * li
    num_re = a_re - 1.0
    f_re = (num_re * lr + a_im * li) / den
    f_im = (a_im * lr - num_re * li) / den
    return a_re, a_im, f_re, f_im


def _s5_prep_kernel(lam_re1, lam_im1, ldt1, c_re1, c_im1,
                    lam_re2, lam_im2, ldt2, b_re2, b_im2, gluw_t,
                    toep_ref, wz_ref, m_ref, apow_ref, gluw_ref):
    g1 = lax.broadcasted_iota(jnp.int32, (SSM_STATE, LANES), 1) // SSM_GROUP
    g2 = lax.broadcasted_iota(jnp.int32, (SSM_GROUP, OCT_STATE), 1) // SSM_STATE

    def expand1(x):
        return jnp.concatenate([jnp.where(g1 == g, x, 0.0) for g in range(OCT)], axis=0)

    def expand2(x):
        return jnp.concatenate([jnp.where(g2 == g, x, 0.0) for g in range(OCT)], axis=0)

    def split(x):
        hi = x.astype(BF16)
        return hi, (x - hi.astype(F32)).astype(BF16)

    a1_re, a1_im, _, _ = _discretise(lam_re1[...], lam_im1[...], ldt1[...])
    a2_re, a2_im, f_re, f_im = _discretise(lam_re2[...], lam_im2[...], ldt2[...])
    br, bi = b_re2[...], b_im2[...]
    bbar_re = f_re * br - f_im * bi
    bbar_im = f_re * bi + f_im * br
    cr, ci = c_re1[...], c_im1[...]
    bre_hi, bre_lo = split(expand2(bbar_re))
    bim_hi, bim_lo = split(expand2(bbar_im))

    def dot3(x_hi, x_lo, y_hi, y_lo):
        return _dot(x_hi, y_hi) + (_dot(x_hi, y_lo) + _dot(x_lo, y_hi))

    p1_re, p1_im = jnp.ones_like(a1_re), jnp.zeros_like(a1_im)
    p2_re, p2_im = jnp.ones_like(a2_re), jnp.zeros_like(a2_im)
    kk = []
    zero_blk = jnp.zeros((LANES, LANES), F32)
    for j in range(T_CHUNK + 1):
        care_hi, care_lo = split(expand1(cr * p1_re - ci * p1_im))
        caim_hi, caim_lo = split(expand1(cr * p1_im + ci * p1_re))
        m_ref[0:OCT_STATE, j * LANES:(j + 1) * LANES] = care_hi
        m_ref[OCT_STATE:2 * OCT_STATE, j * LANES:(j + 1) * LANES] = -caim_hi
        if j < T_CHUNK:
            t = T_CHUNK - 1 - j
            ba_re = expand2(bbar_re * p2_re - bbar_im * p2_im)
            ba_im = expand2(bbar_re * p2_im + bbar_im * p2_re)
            wz_ref[t * LANES:(t + 1) * LANES, 0:OCT_STATE] = ba_re.astype(BF16)
            wz_ref[t * LANES:(t + 1) * LANES, OCT_STATE:2 * OCT_STATE] = ba_im.astype(BF16)
            kk.append(dot3(bre_hi, bre_lo, care_hi, care_lo)
                      - dot3(bim_hi, bim_lo, caim_hi, caim_lo))
        if j == 1:
            apow_ref[0:1, :] = p2_re
            apow_ref[1:2, :] = p2_im
        if j == T_CHUNK:
            apow_ref[2:3, :] = p2_re
            apow_ref[3:4, :] = p2_im
        p1_re, p1_im = p1_re * a1_re - p1_im * a1_im, p1_re * a1_im + p1_im * a1_re
        p2_re, p2_im = p2_re * a2_re - p2_im * a2_im, p2_re * a2_im + p2_im * a2_re

    for t in range(T_CHUNK):
        for t2 in range(T_CHUNK):
            blk = kk[t2 - t] if t2 >= t else zero_blk
            toep_ref[t * LANES:(t + 1) * LANES, t2 * LANES:(t2 + 1) * LANES] = blk.astype(BF16)

    rg = lax.broadcasted_iota(jnp.int32, (LANES, LANES), 0) // SSM_GROUP
    lg = lax.broadcasted_iota(jnp.int32, (LANES, LANES), 1) // SSM_GROUP
    gw = jnp.where(rg == lg, gluw_t[...], 0.0).astype(BF16)
    gz = jnp.zeros((LANES, LANES), BF16)
    for a in range(TOK_PER_TILE):
        for b in range(TOK_PER_TILE):
            gluw_ref[a * LANES:(a + 1) * LANES, b * LANES:(b + 1) * LANES] = gw if a == b else gz


def _s5_prep(lam_re, lam_im, log_dt, b_re, b_im, c_re, c_im, glu_w):
    G, P, H = N_SSM_GROUPS, SSM_STATE, SSM_GROUP
    row = lambda a: a.reshape(N_OCT, 1, OCT_STATE)
    ldt = jnp.broadcast_to(log_dt[:, None], (G, P))

    def col(a_gp):
        a = jnp.swapaxes(a_gp.reshape(N_OCT, OCT, P), 1, 2)
        return jnp.broadcast_to(a[..., None], (N_OCT, P, OCT, H)).reshape(N_OCT, P, LANES)

    lay1 = lambda a_ghp: jnp.swapaxes(a_ghp.reshape(N_OCT, LANES, P), 1, 2)
    lay2 = lambda a_gph: jnp.swapaxes(a_gph.reshape(N_OCT, OCT_STATE, H), 1, 2)
    c_re1, c_im1, b_re2, b_im2 = lay1(c_re), lay1(c_im), lay2(b_re), lay2(b_im)
    gw = glu_w.reshape(N_OCT, OCT, H, 1, H)
    gluw_t = jnp.broadcast_to(gw, (N_OCT, OCT, H, OCT, H)).reshape(N_OCT, LANES, LANES)

    def spec(shape):
        return pl.BlockSpec((None,) + shape, lambda o: (o,) + (0,) * len(shape))

    m_cols = (T_CHUNK + 1) * LANES
    return pl.pallas_call(
        _s5_prep_kernel,
        grid=(N_OCT,),
        in_specs=[spec((SSM_STATE, LANES))] * 5
                 + [spec((1, OCT_STATE))] * 3 + [spec((SSM_GROUP, OCT_STATE))] * 2
                 + [spec((LANES, LANES))],
        out_specs=[spec((CHUNK_K, CHUNK_K)), spec((CHUNK_K, 2 * OCT_STATE)),
                   spec((2 * OCT_STATE, m_cols)), spec((4, OCT_STATE)),
                   spec((MXU_TILE, MXU_TILE))],
        out_shape=[jax.ShapeDtypeStruct((N_OCT, CHUNK_K, CHUNK_K), BF16),
                   jax.ShapeDtypeStruct((N_OCT, CHUNK_K, 2 * OCT_STATE), BF16),
                   jax.ShapeDtypeStruct((N_OCT, 2 * OCT_STATE, m_cols), BF16),
                   jax.ShapeDtypeStruct((N_OCT, 4, OCT_STATE), F32),
                   jax.ShapeDtypeStruct((N_OCT, MXU_TILE, MXU_TILE), BF16)],
        compiler_params=_cparams(1),
        name="s5_prep",
    )(col(lam_re), col(lam_im), col(ldt), c_re1, c_im1,
      row(lam_re), row(lam_im), row(ldt), b_re2, b_im2, gluw_t)


GELU_C0 = 0.7978845608028654
GELU_C1 = GELU_C0 * 0.044715


def _glu_out(y, gluw, glub):
    gy = y * (0.5 + 0.5 * jnp.tanh(y * (GELU_C0 + GELU_C1 * (y * y))))
    z = _dot(gy.astype(BF16), gluw) + glub
    return gy * (0.5 + 0.5 * jnp.tanh(0.5 * z))


def _s5_prompt_kernel(x_ref, toep_ref, wz_ref, m_ref, apow_ref, d_ref, gluw_ref, glub_ref,
                      y_ref, sre_ref, sim_ref, xr_ref, z_ref, yv_ref, *, n_seq, row_chunk):
    n_rows = x_ref.shape[0] // T_CHUNK
    n_blk = n_rows // row_chunk

    def tokens(b, t):
        return pl.ds(b * (row_chunk * T_CHUNK) + t, row_chunk, stride=T_CHUNK)

    for b in range(n_blk):
        rows = pl.ds(b * row_chunk, row_chunk)
        for t in range(T_CHUNK):
            xr_ref[rows, t * LANES:(t + 1) * LANES] = x_ref[tokens(b, t), :].astype(BF16)
        xr = xr_ref[rows, :]
        z_ref[rows, :] = _dot(xr, wz_ref[...])
        for nt in range(CHUNK_K // MXU_TILE):
            k_end = (nt + 1) * MXU_TILE
            cols = slice(nt * MXU_TILE, k_end)
            yv_ref[rows, cols] = _dot(xr[:, 0:k_end], toep_ref[0:k_end, cols])

    are, aim = apow_ref[2:3, :], apow_ref[3:4, :]
    lo = lax.broadcasted_iota(jnp.int32, (SUBLANES, OCT_STATE), 0) < n_seq

    def step(k, carry):
        pre, pim = carry
        rows = pl.ds(pl.multiple_of(k * SUBLANES, SUBLANES), SUBLANES)
        zre = z_ref[rows, 0:OCT_STATE]
        zim = z_ref[rows, OCT_STATE:2 * OCT_STATE]
        w1re = are * pre - aim * pim + zre
        w1im = are * pim + aim * pre + zim
        r1re = pltpu.roll(w1re, n_seq, axis=0)
        r1im = pltpu.roll(w1im, n_seq, axis=0)
        w2re = are * r1re - aim * r1im + zre
        w2im = are * r1im + aim * r1re + zim
        z_ref[rows, 0:OCT_STATE] = jnp.where(lo, pre, r1re)
        z_ref[rows, OCT_STATE:2 * OCT_STATE] = jnp.where(lo, pim, r1im)
        nre = jnp.where(lo, pltpu.roll(w2re, n_seq, axis=0), w2re)
        nim = jnp.where(lo, pltpu.roll(w2im, n_seq, axis=0), w2im)
        return nre, nim

    zeros = jnp.zeros((SUBLANES, OCT_STATE), F32)
    fre, fim = lax.fori_loop(0, n_rows // SUBLANES, step, (zeros, zeros))
    sre_ref[...] = fre[0:n_seq, :]
    sim_ref[...] = fim[0:n_seq, :]

    d2 = jnp.concatenate([d_ref[...]] * TOK_PER_TILE, axis=1)
    glub2 = jnp.concatenate([glub_ref[...]] * TOK_PER_TILE, axis=1)

    for b in range(n_blk):
        rows = pl.ds(b * row_chunk, row_chunk)
        y = yv_ref[rows, :] + _dot(z_ref[rows, :].astype(BF16), m_ref[:, LANES:])
        for t0 in range(0, T_CHUNK, TOK_PER_TILE):
            toks = [tokens(b, t0 + i) for i in range(TOK_PER_TILE)]
            u = jnp.concatenate([x_ref[tok, :] for tok in toks], axis=1)
            yt = y[:, t0 * LANES:(t0 + TOK_PER_TILE) * LANES] + d2 * u
            out = _glu_out(yt, gluw_ref[...], glub2)
            for i, tok in enumerate(toks):
                y_ref[tok, :] = out[:, i * LANES:(i + 1) * LANES]


def _s5_prompt(u_flat, ops, d_skip, glu_b, *, n_seq):
    toep, wz, m, apow, gluw = ops
    rows = u_flat.shape[0]
    n_rows = rows // T_CHUNK
    oct_spec = lambda shape: pl.BlockSpec((None,) + shape, lambda o: (o, 0, 0))
    kern = functools.partial(_s5_prompt_kernel, n_seq=n_seq, row_chunk=256)
    return pl.pallas_call(
        kern,
        grid=(N_OCT,),
        in_specs=[pl.BlockSpec((rows, LANES), lambda o: (0, o)),
                  oct_spec(toep.shape[1:]), oct_spec(wz.shape[1:]), oct_spec(m.shape[1:]),
                  oct_spec(apow.shape[1:]), oct_spec((1, LANES)), oct_spec(gluw.shape[1:]),
                  oct_spec((1, LANES))],
        out_specs=[pl.BlockSpec((rows, LANES), lambda o: (0, o)),
                   pl.BlockSpec((n_seq, OCT_STATE), lambda o: (0, o)),
                   pl.BlockSpec((n_seq, OCT_STATE), lambda o: (0, o))],
        out_shape=[jax.ShapeDtypeStruct((rows, D_SSM), F32),
                   jax.ShapeDtypeStruct((n_seq, N_OCT * OCT_STATE), F32),
                   jax.ShapeDtypeStruct((n_seq, N_OCT * OCT_STATE), F32)],
        scratch_shapes=[pltpu.VMEM((n_rows, CHUNK_K), BF16),
                        pltpu.VMEM((n_rows, 2 * OCT_STATE), F32),
                        pltpu.VMEM((n_rows, CHUNK_K), F32)],
        compiler_params=_cparams(1),
        name="s5_prompt",
    )(u_flat, toep, wz, m, apow, d_skip.reshape(N_OCT, 1, LANES), gluw,
      glu_b.reshape(N_OCT, 1, LANES))


def _s5_sample_kernel(u_ref, s0re_ref, s0im_ref, wz_ref, m_ref, apow_ref, d_ref, gluw_ref,
                      glub_ref, y_ref, sre_ref, sim_ref):
    u = u_ref[...]
    z = _dot(u.astype(BF16), wz_ref[...])
    are, aim = apow_ref[0:1, :], apow_ref[1:2, :]
    s0re, s0im = s0re_ref[...], s0im_ref[...]
    nre = are * s0re - aim * s0im + z[:, 0:OCT_STATE]
    nim = are * s0im + aim * s0re + z[:, OCT_STATE:2 * OCT_STATE]
    sre_ref[...] = nre
    sim_ref[...] = nim
    y = (_dot(nre.astype(BF16), m_ref[0:OCT_STATE, :])
         + _dot(nim.astype(BF16), m_ref[OCT_STATE:2 * OCT_STATE, :])
         + d_ref[...] * u)
    y_ref[...] = _glu_out(y, gluw_ref[...], glub_ref[...])


def _s5_sample(u, s0_re, s0_im, ops, d_skip, glu_b):
    _, wz, m, apow, gluw = ops
    rows = u.shape[0]
    oct_spec = lambda shape: pl.BlockSpec((None,) + shape, lambda o: (o, 0, 0))
    return pl.pallas_call(
        _s5_sample_kernel,
        grid=(N_OCT,),
        in_specs=[pl.BlockSpec((rows, LANES), lambda o: (0, o)),
                  pl.BlockSpec((rows, OCT_STATE), lambda o: (0, o)),
                  pl.BlockSpec((rows, OCT_STATE), lambda o: (0, o)),
                  pl.BlockSpec((None, LANES, 2 * OCT_STATE), lambda o: (o, T_CHUNK - 1, 0)),
                  pl.BlockSpec((None, 2 * OCT_STATE, LANES), lambda o: (o, 0, 0)),
                  oct_spec(apow.shape[1:]), oct_spec((1, LANES)), oct_spec((LANES, LANES)),
                  oct_spec((1, LANES))],
        out_specs=[pl.BlockSpec((rows, LANES), lambda o: (0, o)),
                   pl.BlockSpec((rows, OCT_STATE), lambda o: (0, o)),
                   pl.BlockSpec((rows, OCT_STATE), lambda o: (0, o))],
        out_shape=[jax.ShapeDtypeStruct((rows, D_SSM), F32),
                   jax.ShapeDtypeStruct((rows, N_OCT * OCT_STATE), F32),
                   jax.ShapeDtypeStruct((rows, N_OCT * OCT_STATE), F32)],
        compiler_params=_cparams(1),
        name="s5_sample",
    )(u, s0_re, s0_im, wz, m, apow, d_skip.reshape(N_OCT, 1, LANES), gluw,
      glu_b.reshape(N_OCT, 1, LANES))


def _pool_linear(z, gi, pw_ref, pb_ref, ps_ref):
    lanes = slice(gi * POOL_GROUP, (gi + 1) * POOL_GROUP)
    return (_dot(z.astype(BF16), pw_ref[gi]) + pb_ref[:, lanes]) * ps_ref[:, lanes]


def _pool_sample_kernel(hist_ref, v_ref, pw_ref, pb_ref, ps_ref, y_ref, new_ref, sum_ref):
    r = pl.program_id(0)

    @pl.when(r == 0)
    def _():
        sum_ref[...] = jnp.zeros(sum_ref.shape, F32)

    @pl.when(r < POOL_HIST)
    def _():
        h = hist_ref[...]
        new_ref[...] = h
        for gi, w in enumerate(POOL_WINDOWS):
            lanes = slice(gi * POOL_GROUP, (gi + 1) * POOL_GROUP)
            sum_ref[:, lanes] += jnp.where(r >= POOL_HIST - (w - 1), h[:, lanes], 0.0)

    @pl.when(r == POOL_HIST)
    def _():
        new_ref[...] = v_ref[...]
        for gi, w in enumerate(POOL_WINDOWS):
            lanes = slice(gi * POOL_GROUP, (gi + 1) * POOL_GROUP)
            v = v_ref[:, lanes]
            z = (sum_ref[:, lanes] + v) / float(w) - v
            y_ref[:, lanes] = _pool_linear(z, gi, pw_ref, pb_ref, ps_ref)


def _pool_sample(hist_t, proj, pool_w, pool_b, pool_scale):
    rows = proj.shape[0]
    return pl.pallas_call(
        _pool_sample_kernel,
        grid=(POOL_HIST + 1,),
        in_specs=[pl.BlockSpec((None, rows, D_POOL),
                               lambda r: (jnp.minimum(r, POOL_HIST - 1), 0, 0)),
                  pl.BlockSpec((rows, D_POOL), lambda r: (0, D_SSM // D_POOL)),
                  pl.BlockSpec(pool_w.shape, lambda r: (0, 0, 0)),
                  pl.BlockSpec((1, D_POOL), lambda r: (0, 0)),
                  pl.BlockSpec((1, D_POOL), lambda r: (0, 0))],
        out_specs=[pl.BlockSpec((rows, D_POOL), lambda r: (0, 0)),
                   pl.BlockSpec((None, rows, D_POOL), lambda r: (jnp.maximum(r - 1, 0), 0, 0))],
        out_shape=[jax.ShapeDtypeStruct((rows, D_POOL), F32),
                   jax.ShapeDtypeStruct(hist_t.shape, F32)],
        scratch_shapes=[pltpu.VMEM((rows, D_POOL), F32)],
        compiler_params=_cparams(1),
        name="pool_sample",
    )(hist_t, proj, pool_w, pool_b.reshape(1, D_POOL), pool_scale.reshape(1, D_POOL))


def _outproj_kernel(x_ref, ys_ref, yp_ref, gate_ref, w_ref, o_ref, *, blocks_per_seq):
    tm = x_ref.shape[0]
    ys = ys_ref[...].reshape(tm, D_SSM).astype(BF16)
    yp = yp_ref[...].astype(BF16)
    mix = _dot(ys, w_ref[0:D_SSM, :]) + _dot(yp, w_ref[D_SSM:, :])
    o_ref[...] = x_ref[...] + _mod_rows(gate_ref, blocks_per_seq) * mix


def _outproj(x, ys, yp, mod, w_out, *, tm, n_seq):
    rows = x.shape[0]
    blocks_per_seq = (rows // n_seq) // tm
    ys_spec = pl.BlockSpec((tm // T_CHUNK, None, T_CHUNK, D_SSM),
                           lambda i: (i % blocks_per_seq, i // blocks_per_seq, 0, 0))
    gate_spec = _mod_specs(False, tm, PROMPT_MOD_BLOCK, MIX_SUBLAYER)[2]
    return pl.pallas_call(
        functools.partial(_outproj_kernel, blocks_per_seq=blocks_per_seq),
        grid=(rows // tm,),
        in_specs=[pl.BlockSpec((tm, D_MODEL), lambda i: (i, 0)),
                  ys_spec,
                  pl.BlockSpec((tm, D_POOL), lambda i: (i, 0)),
                  gate_spec,
                  pl.BlockSpec((D_MODEL, D_MODEL), lambda i: (0, 0),
                               pipeline_mode=pl.Buffered(1))],
        out_specs=pl.BlockSpec((tm, D_MODEL), lambda i: (i, 0)),
        out_shape=jax.ShapeDtypeStruct((rows, D_MODEL), F32),
        compiler_params=_cparams(1),
        name="outproj",
    )(x, ys, yp, mod, w_out)


def _outproj_sample_kernel(x_ref, ys_ref, yp_ref, gate_ref, w_ref, o_ref, wo_ref):
    wo_ref[...] = w_ref[...].astype(BF16)
    mix = (_dot(ys_ref[...].astype(BF16), wo_ref[0:D_SSM, :])
           + _dot(yp_ref[...].astype(BF16), wo_ref[D_SSM:, :]))
    o_ref[...] = x_ref[...] + gate_ref[...] * mix


def _outproj_sample(x, ys, yp, mod, w_out):
    rows = x.shape[0]
    n_col = D_MODEL // SAMPLE_TN
    gate_col = (3 * MIX_SUBLAYER + 2) * n_col
    return pl.pallas_call(
        _outproj_sample_kernel,
        grid=(n_col,),
        in_specs=[pl.BlockSpec((rows, SAMPLE_TN), lambda j: (0, j)),
                  pl.BlockSpec((rows, D_SSM), lambda j: (0, 0)),
                  pl.BlockSpec((rows, D_POOL), lambda j: (0, 0)),
                  pl.BlockSpec((rows, SAMPLE_TN), lambda j: (0, gate_col + j)),
                  pl.BlockSpec((D_MODEL, SAMPLE_TN), lambda j: (0, j))],
        out_specs=[pl.BlockSpec((rows, SAMPLE_TN), lambda j: (0, j)),
                   pl.BlockSpec((D_MODEL, SAMPLE_TN), lambda j: (0, j))],
        out_shape=[jax.ShapeDtypeStruct((rows, D_MODEL), F32),
                   jax.ShapeDtypeStruct((D_MODEL, D_MODEL), BF16)],
        compiler_params=_cparams(1),
        name="outproj_sample",
    )(x, ys, yp, mod, w_out)


def kernel(x_prompt, x_sample, state_ssm_re, state_ssm_im, state_pool, c_prompt, c_sample, ada_w, ada_b, ffn1_norm, ffn1_w_gate, ffn1_w_up, ffn1_w_down, mix_norm, w_in, ssm_lambda_re, ssm_lambda_im, ssm_log_dt, ssm_b_re, ssm_b_im, ssm_c_re, ssm_c_im, ssm_d, ssm_glu_w, ssm_glu_b, pool_w, pool_b, pool_scale, w_out, ffn2_norm, ffn2_w_gate, ffn2_w_up, ffn2_w_down, final_norm):
    n_p, seq, _ = x_prompt.shape
    n_s = x_sample.shape[0]
    assert (n_p, n_s) == (N_PROMPT, N_SAMPLE) and n_p <= SUBLANES
    G, P = N_SSM_GROUPS, SSM_STATE

    c_all = jnp.concatenate([c_sample, c_prompt, jnp.zeros((SUBLANES - n_p, D_MODEL), F32)], axis=0)
    mod = _adaln(c_all, ada_w[0], ada_b.reshape(1, N_MOD_COLS))
    pw = pool_w[0].astype(BF16)
    ops = _s5_prep(ssm_lambda_re[0], ssm_lambda_im[0], ssm_log_dt[0], ssm_b_re[0], ssm_b_im[0],
                   ssm_c_re[0], ssm_c_im[0], ssm_glu_w[0])

    xs = x_sample.reshape(n_s, D_MODEL)
    xp = x_prompt.reshape(n_p * seq, D_MODEL)
    blocks_per_seq = seq // FFN_TM
    sample_rows = _RowGroup(n_s, n_s, per_row=True, first_block=0, resident=True)
    head_rows = _RowGroup(FFN_TM, FFN_ROW_CHUNK, per_row=False, first_block=0, resident=True)
    tail_rows = _RowGroup(FFN_TM, FFN_ROW_CHUNK, per_row=False, first_block=1, resident=False)

    def ffn(xs, xp, norm_g, sub, wg, wu, wd, last):
        ys, yp, bg, bu, bd = _ffn([xs, xp], (sample_rows, head_rows), norm_g, mod, sub, wg, wu, wd,
                                  final_norm, n_blocks=1, tf=FFN_HEAD_TF, final_norm=last,
                                  blocks_per_seq=blocks_per_seq)
        (yp,) = _ffn([xp], (tail_rows,), norm_g, mod, sub, bg, bu, bd, final_norm,
                     n_blocks=n_p * blocks_per_seq - 1, tf=FFN_TF, final_norm=last,
                     blocks_per_seq=blocks_per_seq, alias_into=yp)
        return ys, yp

    xs, xp = ffn(xs, xp, ffn1_norm[0], FFN1_SUBLAYER, ffn1_w_gate[0], ffn1_w_up[0],
                 ffn1_w_down[0], False)

    proj_s, win = _proj_sample(xs, mix_norm[0], mod, w_in[0])
    ys_s, sre_s, sim_s = _s5_sample(proj_s, state_ssm_re[0].reshape(n_s, G * P),
                                    state_ssm_im[0].reshape(n_s, G * P), ops, ssm_d[0],
                                    ssm_glu_b[0])
    yp_s, pool_s_t = _pool_sample(jnp.swapaxes(state_pool[0], 0, 1), proj_s, pw, pool_b[0],
                                  pool_scale[0])
    pool_s = jnp.swapaxes(pool_s_t, 0, 1)
    xs, wout = _outproj_sample(xs, ys_s, yp_s, mod, w_out[0])

    u4, yp_p, v_tail = _proj(xp, mix_norm[0], mod, win, pw, pool_b[0], pool_scale[0],
                             tm=MIX_TM, n_seq=n_p)
    ys_p, sre_p, sim_p = _s5_prompt(u4.reshape(n_p * seq, D_SSM), ops, ssm_d[0], ssm_glu_b[0],
                                    n_seq=n_p)
    xp = _outproj(xp, ys_p.reshape(seq // T_CHUNK, n_p, T_CHUNK, D_SSM), yp_p, mod, wout,
                  tm=MIX_TM, n_seq=n_p)

    y_sample, y_prompt = ffn(xs, xp, ffn2_norm[0], FFN2_SUBLAYER, ffn2_w_gate[0], ffn2_w_up[0],
                             ffn2_w_down[0], True)

    pool_p = v_tail[:, POOL_TAIL - POOL_HIST:, :][None]
    return (y_prompt.reshape(n_p, seq, D_MODEL), y_sample.reshape(n_s, 1, D_MODEL),
            sre_p.reshape(1, n_p, G, P), sim_p.reshape(1, n_p, G, P), pool_p,
            sre_s.reshape(1, n_s, G, P), sim_s.reshape(1, n_s, G, P), pool_s[None])
```

```python
import functools
from typing import NamedTuple

import jax
import jax.numpy as jnp
from jax import lax
from jax.experimental import pallas as pl
from jax.experimental.pallas import tpu as pltpu

F32 = jnp.float32
BF16 = jnp.bfloat16

D_MODEL = 2048
D_FF = 5632
D_SSM = 1024
D_POOL = 1024
SSM_GROUP = 16
SSM_STATE = 64
N_SSM_GROUPS = 64
POOL_WINDOWS = (2, 4, 8, 16)
POOL_GROUP = 256
POOL_HIST = 15
N_MOD_COLS = 9 * D_MODEL
N_PROMPT = 4
N_SAMPLE = 128
EPS = 1e-6
FFN_RES = 0.5

LANES = 128
SUBLANES = 8
MXU_TILE = 256
VMEM_LIMIT = 60 * 1024 * 1024

PROMPT_MOD_BLOCK = N_SAMPLE // SUBLANES
NORM_ROWS = 32
FFN1_SUBLAYER, MIX_SUBLAYER, FFN2_SUBLAYER = 0, 1, 2

FFN_TM = 1024
FFN_ROW_CHUNK = 512
FFN_TF = 512
FFN_HEAD_TF = 256
PROJ_TM = 1024
OUTPROJ_TM = 512

OCT = LANES // SSM_GROUP
N_OCT = N_SSM_GROUPS // OCT
T_CHUNK = 8
OCT_STATE = OCT * SSM_STATE
CHUNK_K = T_CHUNK * LANES
TOK_PER_TILE = MXU_TILE // LANES


def _cparams(n_axes):
    return pltpu.CompilerParams(dimension_semantics=("arbitrary",) * n_axes,
                                vmem_limit_bytes=VMEM_LIMIT)


def _dot(a, b):
    return jnp.dot(a, b, preferred_element_type=F32)


ADALN_TN = 2048


def _adaln_kernel(c_ref, w_ref, b_ref, o_ref):
    c = c_ref[...]
    sc = (c * jax.nn.sigmoid(c)).astype(BF16)
    o_ref[...] = _dot(sc, w_ref[...].astype(BF16)) + b_ref[...]


def _norm_mod(x, g, shift, scale):
    ms = jnp.mean(x * x, axis=-1, keepdims=True)
    y = x * lax.rsqrt(ms + EPS) * g
    return y * (1.0 + scale) + shift


def _mod_specs(per_row, tm, prompt_row_block, sub):
    specs = []
    for m in range(3):
        col = 3 * sub + m
        if per_row:
            specs.append(pl.BlockSpec((tm, D_MODEL), lambda i, *_, col=col: (i, col)))
        else:
            specs.append(pl.BlockSpec((SUBLANES, D_MODEL),
                                      lambda i, *_, col=col: (prompt_row_block, col)))
    return specs


def _mod_rows(ref, blocks_per_seq, rows=None):
    if blocks_per_seq:
        return ref[pl.ds(pl.program_id(0) // blocks_per_seq, 1), :]
    return ref[...] if rows is None else ref[rows, :]


class _RowGroup(NamedTuple):
    rows: int
    row_chunk: int
    per_row: bool
    first_block: int
    resident: bool


def _ffn_step(s, n_j, has_copy):
    t = jnp.maximum(s - 1, 0) if has_copy else s
    return t // n_j, t % n_j


def _ffn_kernel(*refs, groups, blocks_per_seq, n_j, final_norm, emit_bf16, has_copy):
    n = len(groups)
    g_ref, wg_ref, wu_ref, wd_ref, fg_ref = refs[4 * n:4 * n + 5]
    outs = refs[4 * n + 5 + int(has_copy):-1]
    h_ref = refs[-1]
    step = pl.program_id(0)
    i, j = _ffn_step(step, n_j, has_copy)
    last_j = n_j - 1
    active = step >= int(has_copy)

    if has_copy:
        @pl.when(step == 0)
        def _():
            outs[0][...] = refs[4 * n + 5][...]

    if emit_bf16:
        bf16_refs = outs[n:n + 3]
        for w_ref, wo_ref in zip((wg_ref, wu_ref, wd_ref), bf16_refs):
            wo_ref[...] = w_ref[...].astype(BF16)
        wg_ref, wu_ref, wd_ref = bf16_refs

    chunks, h_base = [], 0
    for k, grp in enumerate(groups):
        chunks += [(k, r, h_base + r) for r in range(0, grp.rows, grp.row_chunk)]
        h_base += grp.rows

    def mod_rows(k, ref, rows):
        if groups[k].per_row:
            return ref[rows, :]
        seq = (i + groups[k].first_block) // blocks_per_seq
        return ref[pl.ds(seq, 1), :]

    def norm_chunk(c):
        k, r, hr = chunks[c]
        x_ref, shift_ref, scale_ref, _ = refs[4 * k:4 * k + 4]
        for s in range(0, groups[k].row_chunk, NORM_ROWS):
            sr = pl.ds(r + s, NORM_ROWS)
            h = _norm_mod(x_ref[sr, :], g_ref[...], mod_rows(k, shift_ref, sr),
                          mod_rows(k, scale_ref, sr))
            h_ref[pl.ds(hr + s, NORM_ROWS), :] = h.astype(BF16)

    def chunk(c, first, last):
        k, r, hr = chunks[c]
        size = groups[k].row_chunk
        x_ref, _, _, gate_ref = refs[4 * k:4 * k + 4]
        o_ref = outs[k]
        rows = pl.ds(r, size)
        if first and c + 1 < len(chunks):
            norm_chunk(c + 1)
        h = h_ref[pl.ds(hr, size), :]
        g = _dot(h, wg_ref[...])
        u = _dot(h, wu_ref[...])
        a = (g * jax.nn.sigmoid(g) * u).astype(BF16)
        d = _dot(a, wd_ref[...])
        acc = d if first else o_ref[rows, :] + d
        if not last:
            o_ref[rows, :] = acc
            return
        o_ref[rows, :] = x_ref[rows, :] + FFN_RES * mod_rows(k, gate_ref, rows) * acc
        if final_norm:
            for s in range(0, size, NORM_ROWS):
                sr = pl.ds(r + s, NORM_ROWS)
                y = o_ref[sr, :]
                ms = jnp.mean(y * y, axis=-1, keepdims=True)
                o_ref[sr, :] = y * lax.rsqrt(ms + EPS) * fg_ref[...]

    def run(first, last):
        if first:
            norm_chunk(0)
        for c in range(len(chunks)):
            chunk(c, first, last)

    pl.when(jnp.logical_and(active, j == 0))(lambda: run(True, False))
    pl.when(jnp.logical_and(j > 0, j < last_j))(lambda: run(False, False))
    pl.when(j == last_j)(lambda: run(False, True))


def _ffn(xs, groups, norm_g, mod, sub, wg, wu, wd, final_g, *, n_blocks, tf, final_norm,
         blocks_per_seq, out_rows=None, first_rows=None):
    emit_bf16 = wg.dtype == F32
    has_copy = first_rows is not None
    assert not emit_bf16 or n_blocks == 1
    assert not has_copy or len(groups) == 1
    n_j = D_FF // tf
    step = functools.partial(_ffn_step, n_j=n_j, has_copy=has_copy)

    def row_spec(grp, col=0, out=False):
        mode = dict(pipeline_mode=pl.Buffered(1)) if grp.resident else {}
        first = 0 if out and not has_copy else grp.first_block

        def index(s):
            block = step(s)[0] + first
            return (jnp.where(s == 0, 0, block) if out and has_copy else block), col
        return pl.BlockSpec((grp.rows, D_MODEL), index, **mode)

    in_specs, operands = [], []
    for x, grp in zip(xs, groups):
        in_specs.append(row_spec(grp))
        operands.append(x)
        for m in range(3):
            col = 3 * sub + m
            if grp.per_row:
                in_specs.append(row_spec(grp, col))
            else:
                in_specs.append(pl.BlockSpec((SUBLANES, D_MODEL),
                                             lambda s, col=col: (PROMPT_MOD_BLOCK, col)))
            operands.append(mod)
    w_specs = [pl.BlockSpec((D_MODEL, tf), lambda s: (0, step(s)[1])),
               pl.BlockSpec((D_MODEL, tf), lambda s: (0, step(s)[1])),
               pl.BlockSpec((tf, D_MODEL), lambda s: (step(s)[1], 0))]
    vec_spec = pl.BlockSpec((1, D_MODEL), lambda s: (0, 0))
    in_specs += [vec_spec, *w_specs, vec_spec]
    operands += [norm_g.reshape(1, D_MODEL), wg, wu, wd, final_g.reshape(1, D_MODEL)]
    if has_copy:
        in_specs.append(pl.BlockSpec(first_rows.shape, lambda s: (0, 0),
                                     pipeline_mode=pl.Buffered(1)))
        operands.append(first_rows)

    out_specs = [row_spec(grp, out=True) for grp in groups]
    out_shape = [jax.ShapeDtypeStruct((out_rows or grp.rows, D_MODEL), F32) for grp in groups]
    if emit_bf16:
        out_specs += w_specs
        out_shape += [jax.ShapeDtypeStruct(w.shape, BF16) for w in (wg, wu, wd)]
    kern = functools.partial(_ffn_kernel, groups=groups, blocks_per_seq=blocks_per_seq, n_j=n_j,
                             final_norm=final_norm, emit_bf16=emit_bf16, has_copy=has_copy)
    return pl.pallas_call(
        kern,
        grid=(int(has_copy) + n_blocks * n_j,),
        in_specs=in_specs,
        out_specs=out_specs,
        out_shape=out_shape,
        scratch_shapes=[pltpu.VMEM((sum(grp.rows for grp in groups), D_MODEL), BF16)],
        compiler_params=_cparams(1),
        name="ffn",
    )(*operands)


POOL_TAIL = 16


def _proj_kernel(x_ref, g_ref, shift_ref, scale_ref, w_ref, pw_ref, pb_ref, ps_ref,
                 u_ref, y_ref, tail_out_ref, tail_ref, h_ref, *, blocks_per_seq):
    tm = x_ref.shape[0]
    half = tm // 2
    blk = pl.program_id(0) % blocks_per_seq
    sh = _mod_rows(shift_ref, blocks_per_seq)
    sc = _mod_rows(scale_ref, blocks_per_seq)

    @pl.when(blk == 0)
    def _():
        tail_ref[...] = jnp.zeros(tail_ref.shape, F32)

    def norm_half(r):
        for s in range(0, half, NORM_ROWS):
            rows = pl.ds(r * half + s, NORM_ROWS)
            h_ref[rows, :] = _norm_mod(x_ref[rows, :], g_ref[...], sh, sc).astype(BF16)

    norm_half(0)
    norm_half(1)
    pv = jnp.concatenate([_dot(h_ref[pl.ds(r * half, half), :], w_ref[:, D_SSM:])
                          for r in range(2)], axis=0)
    for r in range(2):
        pu = _dot(h_ref[pl.ds(r * half, half), :], w_ref[:, 0:D_SSM])
        u_ref[pl.ds(r * half // T_CHUNK, half // T_CHUNK), :, :] = pu.reshape(
            half // T_CHUNK, T_CHUNK, D_SSM)

    row_ext = lax.broadcasted_iota(jnp.int32, (POOL_TAIL + tm, POOL_GROUP), 0)
    pos = blk * tm + lax.broadcasted_iota(jnp.int32, (tm, POOL_GROUP), 0)
    for gi, w in enumerate(POOL_WINDOWS):
        lanes = slice(gi * POOL_GROUP, (gi + 1) * POOL_GROUP)
        v = pv[:, lanes]
        s = jnp.concatenate([tail_ref[:, lanes], v], axis=0)
        k = 1
        while k < w:
            s = s + jnp.where(row_ext >= k, pltpu.roll(s, k, axis=0), 0.0)
            k *= 2
        cnt = jnp.clip(pos + 1, 1, w).astype(F32)
        z = s[POOL_TAIL:, :] / cnt - v
        y_ref[:, lanes] = _pool_linear(z, gi, pw_ref, pb_ref, ps_ref).astype(y_ref.dtype)
    tail = pv[tm - POOL_TAIL:, :]
    tail_ref[...] = tail
    tail_out_ref[...] = tail


def _proj(x, norm_g, mod, w_in, pool_w, pool_b, pool_scale, *, tm, n_seq):
    rows = x.shape[0]
    seq_len = rows // n_seq
    blocks_per_seq = seq_len // tm
    u_shape = (seq_len // T_CHUNK, n_seq, T_CHUNK, D_SSM)
    u_spec = pl.BlockSpec((tm // T_CHUNK, None, T_CHUNK, D_SSM),
                          lambda i: (i % blocks_per_seq, i // blocks_per_seq, 0, 0))
    shift_spec, scale_spec, _ = _mod_specs(False, tm, PROMPT_MOD_BLOCK, MIX_SUBLAYER)
    vec_spec = pl.BlockSpec((1, D_POOL), lambda i: (0, 0))
    return pl.pallas_call(
        functools.partial(_proj_kernel, blocks_per_seq=blocks_per_seq),
        grid=(rows // tm,),
        in_specs=[pl.BlockSpec((tm, D_MODEL), lambda i: (i, 0)),
                  pl.BlockSpec((1, D_MODEL), lambda i: (0, 0)),
                  shift_spec, scale_spec,
                  pl.BlockSpec((D_MODEL, D_MODEL), lambda i: (0, 0),
                               pipeline_mode=pl.Buffered(1)),
                  pl.BlockSpec(pool_w.shape, lambda i: (0, 0, 0)), vec_spec, vec_spec],
        out_specs=[u_spec, pl.BlockSpec((tm, D_POOL), lambda i: (i, 0)),
                   pl.BlockSpec((None, POOL_TAIL, D_POOL), lambda i: (i // blocks_per_seq, 0, 0))],
        out_shape=[jax.ShapeDtypeStruct(u_shape, F32),
                   jax.ShapeDtypeStruct((rows, D_POOL), BF16),
                   jax.ShapeDtypeStruct((n_seq, POOL_TAIL, D_POOL), F32)],
        scratch_shapes=[pltpu.VMEM((POOL_TAIL, D_POOL), F32), pltpu.VMEM((tm, D_MODEL), BF16)],
        compiler_params=_cparams(1),
        name="proj",
    )(x, norm_g.reshape(1, D_MODEL), mod, mod, w_in, pool_w, pool_b.reshape(1, D_POOL),
      pool_scale.reshape(1, D_POOL))


SAMPLE_TN = 512


def _proj_sample_kernel(x_ref, g_ref, shift_ref, scale_ref, w_ref, p_ref, wo_ref, h_ref):
    @pl.when(pl.program_id(0) == 0)
    def _():
        h = _norm_mod(x_ref[...], g_ref[...], shift_ref[...], scale_ref[...])
        h_ref[...] = h.astype(BF16)
    wo_ref[...] = w_ref[...].astype(BF16)
    p_ref[...] = _dot(h_ref[...], wo_ref[...])


def _proj_sample(x, norm_g, mod, w_in):
    rows = x.shape[0]
    shift_col, scale_col = 3 * MIX_SUBLAYER, 3 * MIX_SUBLAYER + 1
    return pl.pallas_call(
        _proj_sample_kernel,
        grid=(D_MODEL // SAMPLE_TN,),
        in_specs=[pl.BlockSpec((rows, D_MODEL), lambda j: (0, 0)),
                  pl.BlockSpec((1, D_MODEL), lambda j: (0, 0)),
                  pl.BlockSpec((rows, D_MODEL), lambda j: (0, shift_col)),
                  pl.BlockSpec((rows, D_MODEL), lambda j: (0, scale_col)),
                  pl.BlockSpec((D_MODEL, SAMPLE_TN), lambda j: (0, j))],
        out_specs=[pl.BlockSpec((rows, SAMPLE_TN), lambda j: (0, j)),
                   pl.BlockSpec((D_MODEL, SAMPLE_TN), lambda j: (0, j))],
        out_shape=[jax.ShapeDtypeStruct((rows, D_MODEL), F32),
                   jax.ShapeDtypeStruct((D_MODEL, D_MODEL), BF16)],
        scratch_shapes=[pltpu.VMEM((rows, D_MODEL), BF16)],
        compiler_params=_cparams(1),
        name="proj_sample",
    )(x, norm_g.reshape(1, D_MODEL), mod, mod, w_in)


def _discretise(lam_re, lam_im, log_dt):
    lr = jnp.minimum(lam_re, -1e-4)
    li = lam_im
    dt = jnp.exp(log_dt)
    mag = jnp.exp(lr * dt)
    ang = li * dt
    a_re = mag * jnp.cos(ang)
    a_im = mag * jnp.sin(ang)
    den = lr * lr + li * li
    num_re = a_re - 1.0
    f_re = (num_re * lr + a_im * li) / den
    f_im = (a_im * lr - num_re * li) / den
    return a_re, a_im, f_re, f_im


def _s5_prep_kernel(lam_re1, lam_im1, ldt1, c_re1, c_im1,
                    lam_re2, lam_im2, ldt2, b_re2, b_im2, gluw_t,
                    toep_ref, wz_ref, m_ref, apow_ref, gluw_ref):
    g1 = lax.broadcasted_iota(jnp.int32, (SSM_STATE, LANES), 1) // SSM_GROUP
    g2 = lax.broadcasted_iota(jnp.int32, (SSM_GROUP, OCT_STATE), 1) // SSM_STATE

    def expand1(x):
        return jnp.concatenate([jnp.where(g1 == g, x, 0.0) for g in range(OCT)], axis=0)

    def expand2(x):
        return jnp.concatenate([jnp.where(g2 == g, x, 0.0) for g in range(OCT)], axis=0)

    def split(x):
        hi = x.astype(BF16)
        return hi, (x - hi.astype(F32)).astype(BF16)

    a1_re, a1_im, _, _ = _discretise(lam_re1[...], lam_im1[...], ldt1[...])
    a2_re, a2_im, f_re, f_im = _discretise(lam_re2[...], lam_im2[...], ldt2[...])
    br, bi = b_re2[...], b_im2[...]
    bbar_re = f_re * br - f_im * bi
    bbar_im = f_re * bi + f_im * br
    cr, ci = c_re1[...], c_im1[...]
    bre_hi, bre_lo = split(expand2(bbar_re))
    bim_hi, bim_lo = split(expand2(bbar_im))

    def dot3(x_hi, x_lo, y_hi, y_lo):
        return _dot(x_hi, y_hi) + (_dot(x_hi, y_lo) + _dot(x_lo, y_hi))

    p1_re, p1_im = jnp.ones_like(a1_re), jnp.zeros_like(a1_im)
    p2_re, p2_im = jnp.ones_like(a2_re), jnp.zeros_like(a2_im)
    kk = []
    zero_blk = jnp.zeros((LANES, LANES), F32)
    for j in range(T_CHUNK + 1):
        care_hi, care_lo = split(expand1(cr * p1_re - ci * p1_im))
        caim_hi, caim_lo = split(expand1(cr * p1_im + ci * p1_re))
        m_ref[0:OCT_STATE, j * LANES:(j + 1) * LANES] = care_hi
        m_ref[OCT_STATE:2 * OCT_STATE, j * LANES:(j + 1) * LANES] = -caim_hi
        if j < T_CHUNK:
            t = T_CHUNK - 1 - j
            ba_re = expand2(bbar_re * p2_re - bbar_im * p2_im)
            ba_im = expand2(bbar_re * p2_im + bbar_im * p2_re)
            wz_ref[t * LANES:(t + 1) * LANES, 0:OCT_STATE] = ba_re.astype(BF16)
            wz_ref[t * LANES:(t + 1) * LANES, OCT_STATE:2 * OCT_STATE] = ba_im.astype(BF16)
            kk.append(dot3(bre_hi, bre_lo, care_hi, care_lo)
                      - dot3(bim_hi, bim_lo, caim_hi, caim_lo))
        if j == 1:
            apow_ref[0:1, :] = p2_re
            apow_ref[1:2, :] = p2_im
        if j == T_CHUNK:
            apow_ref[2:3, :] = p2_re
            apow_ref[3:4, :] = p2_im
        p1_re, p1_im = p1_re * a1_re - p1_im * a1_im, p1_re * a1_im + p1_im * a1_re
        p2_re, p2_im = p2_re * a2_re - p2_im * a2_im, p2_re * a2_im + p2_im * a2_re

    for t in range(T_CHUNK):
        for t2 in range(T_CHUNK):
            blk = kk[t2 - t] if t2 >= t else zero_blk
            toep_ref[t * LANES:(t + 1) * LANES, t2 * LANES:(t2 + 1) * LANES] = blk.astype(BF16)

    rg = lax.broadcasted_iota(jnp.int32, (LANES, LANES), 0) // SSM_GROUP
    lg = lax.broadcasted_iota(jnp.int32, (LANES, LANES), 1) // SSM_GROUP
    gw = jnp.where(rg == lg, gluw_t[...], 0.0).astype(BF16)
    gz = jnp.zeros((LANES, LANES), BF16)
    for a in range(TOK_PER_TILE):
        for b in range(TOK_PER_TILE):
            gluw_ref[a * LANES:(a + 1) * LANES, b * LANES:(b + 1) * LANES] = gw if a == b else gz


def _adaln_s5_prep_kernel(c_ref, w_ref, b_ref, *refs):
    n_prep_in = len(refs) - 6
    _adaln_kernel(c_ref, w_ref, b_ref, refs[n_prep_in])

    @pl.when(pl.program_id(0) < N_OCT)
    def _():
        _s5_prep_kernel(*refs[:n_prep_in], *refs[n_prep_in + 1:])


def _adaln_s5_prep(c_all, ada_w, ada_b, lam_re, lam_im, log_dt, b_re, b_im, c_re, c_im, glu_w):
    G, P, H = N_SSM_GROUPS, SSM_STATE, SSM_GROUP
    rows, n_cols = c_all.shape[0], ada_w.shape[1]
    n_steps = n_cols // ADALN_TN
    assert n_steps >= N_OCT
    row = lambda a: a.reshape(N_OCT, 1, OCT_STATE)
    ldt = jnp.broadcast_to(log_dt[:, None], (G, P))

    def col(a_gp):
        a = jnp.swapaxes(a_gp.reshape(N_OCT, OCT, P), 1, 2)
        return jnp.broadcast_to(a[..., None], (N_OCT, P, OCT, H)).reshape(N_OCT, P, LANES)

    lay1 = lambda a_ghp: jnp.swapaxes(a_ghp.reshape(N_OCT, LANES, P), 1, 2)
    lay2 = lambda a_gph: jnp.swapaxes(a_gph.reshape(N_OCT, OCT_STATE, H), 1, 2)
    c_re1, c_im1, b_re2, b_im2 = lay1(c_re), lay1(c_im), lay2(b_re), lay2(b_im)
    gw = glu_w.reshape(N_OCT, OCT, H, 1, H)
    gluw_t = jnp.broadcast_to(gw, (N_OCT, OCT, H, OCT, H)).reshape(N_OCT, LANES, LANES)

    def spec(shape):
        return pl.BlockSpec((None,) + shape,
                            lambda j: (jnp.minimum(j, N_OCT - 1),) + (0,) * len(shape))

    m_cols = (T_CHUNK + 1) * LANES
    mod, *ops = pl.pallas_call(
        _adaln_s5_prep_kernel,
        grid=(n_steps,),
        in_specs=[pl.BlockSpec((rows, D_MODEL), lambda j: (0, 0)),
                  pl.BlockSpec((D_MODEL, ADALN_TN), lambda j: (0, j)),
                  pl.BlockSpec((1, ADALN_TN), lambda j: (0, j))]
                 + [spec((SSM_STATE, LANES))] * 5
                 + [spec((1, OCT_STATE))] * 3 + [spec((SSM_GROUP, OCT_STATE))] * 2
                 + [spec((LANES, LANES))],
        out_specs=[pl.BlockSpec((rows, ADALN_TN), lambda j: (0, j)),
                   spec((CHUNK_K, CHUNK_K)), spec((CHUNK_K, 2 * OCT_STATE)),
                   spec((2 * OCT_STATE, m_cols)), spec((4, OCT_STATE)),
                   spec((MXU_TILE, MXU_TILE))],
        out_shape=[jax.ShapeDtypeStruct((rows, n_cols), F32),
                   jax.ShapeDtypeStruct((N_OCT, CHUNK_K, CHUNK_K), BF16),
                   jax.ShapeDtypeStruct((N_OCT, CHUNK_K, 2 * OCT_STATE), BF16),
                   jax.ShapeDtypeStruct((N_OCT, 2 * OCT_STATE, m_cols), BF16),
                   jax.ShapeDtypeStruct((N_OCT, 4, OCT_STATE), F32),
                   jax.ShapeDtypeStruct((N_OCT, MXU_TILE, MXU_TILE), BF16)],
        compiler_params=_cparams(1),
        name="adaln_s5_prep",
    )(c_all, ada_w, ada_b, col(lam_re), col(lam_im), col(ldt), c_re1, c_im1,
      row(lam_re), row(lam_im), row(ldt), b_re2, b_im2, gluw_t)
    return mod, tuple(ops)


GELU_C0 = 0.7978845608028654
GELU_C1 = GELU_C0 * 0.044715


def _glu_out(y, gluw, glub):
    gy = y * (0.5 + 0.5 * jnp.tanh(y * (GELU_C0 + GELU_C1 * (y * y))))
    z = _dot(gy.astype(BF16), gluw) + glub
    return gy * (0.5 + 0.5 * jnp.tanh(0.5 * z))


def _s5_prompt_kernel(x_ref, toep_ref, wz_ref, m_ref, apow_ref, d_ref, gluw_ref, glub_ref,
                      y_ref, sre_ref, sim_ref, xr_ref, z_ref, yv_ref, *, n_seq, row_chunk):
    n_rows = x_ref.shape[0] // T_CHUNK
    n_blk = n_rows // row_chunk

    def tokens(b, t):
        return pl.ds(b * (row_chunk * T_CHUNK) + t, row_chunk, stride=T_CHUNK)

    for b in range(n_blk):
        rows = pl.ds(b * row_chunk, row_chunk)
        for t in range(T_CHUNK):
            xr_ref[rows, t * LANES:(t + 1) * LANES] = x_ref[tokens(b, t), :].astype(BF16)
        xr = xr_ref[rows, :]
        z_ref[rows, :] = _dot(xr, wz_ref[...])
        for nt in range(CHUNK_K // MXU_TILE):
            k_end = (nt + 1) * MXU_TILE
            cols = slice(nt * MXU_TILE, k_end)
            yv_ref[rows, cols] = _dot(xr[:, 0:k_end], toep_ref[0:k_end, cols])

    are, aim = apow_ref[2:3, :], apow_ref[3:4, :]
    lo = lax.broadcasted_iota(jnp.int32, (SUBLANES, OCT_STATE), 0) < n_seq

    def step(k, carry):
        pre, pim = carry
        rows = pl.ds(pl.multiple_of(k * SUBLANES, SUBLANES), SUBLANES)
        zre = z_ref[rows, 0:OCT_STATE]
        zim = z_ref[rows, OCT_STATE:2 * OCT_STATE]
        w1re = are * pre - aim * pim + zre
        w1im = are * pim + aim * pre + zim
        r1re = pltpu.roll(w1re, n_seq, axis=0)
        r1im = pltpu.roll(w1im, n_seq, axis=0)
        w2re = are * r1re - aim * r1im + zre
        w2im = are * r1im + aim * r1re + zim
        z_ref[rows, 0:OCT_STATE] = jnp.where(lo, pre, r1re)
        z_ref[rows, OCT_STATE:2 * OCT_STATE] = jnp.where(lo, pim, r1im)
        nre = jnp.where(lo, pltpu.roll(w2re, n_seq, axis=0), w2re)
        nim = jnp.where(lo, pltpu.roll(w2im, n_seq, axis=0), w2im)
        return nre, nim

    zeros = jnp.zeros((SUBLANES, OCT_STATE), F32)
    fre, fim = lax.fori_loop(0, n_rows // SUBLANES, step, (zeros, zeros))
    sre_ref[...] = fre[0:n_seq, :]
    sim_ref[...] = fim[0:n_seq, :]

    d2 = jnp.concatenate([d_ref[...]] * TOK_PER_TILE, axis=1)
    glub2 = jnp.concatenate([glub_ref[...]] * TOK_PER_TILE, axis=1)

    for b in range(n_blk):
        rows = pl.ds(b * row_chunk, row_chunk)
        y = yv_ref[rows, :] + _dot(z_ref[rows, :].astype(BF16), m_ref[:, LANES:])
        for t0 in range(0, T_CHUNK, TOK_PER_TILE):
            toks = [tokens(b, t0 + i) for i in range(TOK_PER_TILE)]
            u = jnp.concatenate([x_ref[tok, :] for tok in toks], axis=1)
            yt = y[:, t0 * LANES:(t0 + TOK_PER_TILE) * LANES] + d2 * u
            out = _glu_out(yt, gluw_ref[...], glub2)
            for i, tok in enumerate(toks):
                y_ref[tok, :] = out[:, i * LANES:(i + 1) * LANES]


def _s5_prompt(u_flat, ops, d_skip, glu_b, *, n_seq):
    toep, wz, m, apow, gluw = ops
    assert 2 * n_seq == SUBLANES
    rows = u_flat.shape[0]
    n_rows = rows // T_CHUNK
    oct_spec = lambda shape: pl.BlockSpec((None,) + shape, lambda o: (o, 0, 0))
    kern = functools.partial(_s5_prompt_kernel, n_seq=n_seq, row_chunk=256)
    return pl.pallas_call(
        kern,
        grid=(N_OCT,),
        in_specs=[pl.BlockSpec((rows, LANES), lambda o: (0, o)),
                  oct_spec(toep.shape[1:]), oct_spec(wz.shape[1:]), oct_spec(m.shape[1:]),
                  oct_spec(apow.shape[1:]), oct_spec((1, LANES)), oct_spec(gluw.shape[1:]),
                  oct_spec((1, LANES))],
        out_specs=[pl.BlockSpec((rows, LANES), lambda o: (0, o)),
                   pl.BlockSpec((n_seq, OCT_STATE), lambda o: (0, o)),
                   pl.BlockSpec((n_seq, OCT_STATE), lambda o: (0, o))],
        out_shape=[jax.ShapeDtypeStruct((rows, D_SSM), F32),
                   jax.ShapeDtypeStruct((n_seq, N_OCT * OCT_STATE), F32),
                   jax.ShapeDtypeStruct((n_seq, N_OCT * OCT_STATE), F32)],
        scratch_shapes=[pltpu.VMEM((n_rows, CHUNK_K), BF16),
                        pltpu.VMEM((n_rows, 2 * OCT_STATE), F32),
                        pltpu.VMEM((n_rows, CHUNK_K), F32)],
        compiler_params=_cparams(1),
        name="s5_prompt",
    )(u_flat, toep, wz, m, apow, d_skip.reshape(N_OCT, 1, LANES), gluw,
      glu_b.reshape(N_OCT, 1, LANES))


def _s5_sample_kernel(u_ref, s0re_ref, s0im_ref, wz_ref, m_ref, apow_ref, d_ref, gluw_ref,
                      glub_ref, y_ref, sre_ref, sim_ref):
    for o in range(N_OCT):
        ch = slice(o * LANES, (o + 1) * LANES)
        st = slice(o * OCT_STATE, (o + 1) * OCT_STATE)
        u = u_ref[:, ch]
        z = _dot(u.astype(BF16), wz_ref[o])
        are, aim = apow_ref[o, 0:1, :], apow_ref[o, 1:2, :]
        s0re, s0im = s0re_ref[:, st], s0im_ref[:, st]
        nre = are * s0re - aim * s0im + z[:, 0:OCT_STATE]
        nim = are * s0im + aim * s0re + z[:, OCT_STATE:2 * OCT_STATE]
        sre_ref[:, st] = nre
        sim_ref[:, st] = nim
        y = (_dot(nre.astype(BF16), m_ref[o, 0:OCT_STATE, :])
             + _dot(nim.astype(BF16), m_ref[o, OCT_STATE:2 * OCT_STATE, :])
             + d_ref[o] * u)
        y_ref[:, ch] = _glu_out(y, gluw_ref[o], glub_ref[o])


def _s5_sample(proj, s0_re, s0_im, ops, d_skip, glu_b):
    _, wz, m, apow, gluw = ops
    rows = proj.shape[0]
    whole = lambda shape: pl.BlockSpec(shape, lambda i: (0,) * len(shape))
    return pl.pallas_call(
        _s5_sample_kernel,
        grid=(1,),
        in_specs=[whole((rows, D_SSM)), whole(s0_re.shape), whole(s0_im.shape),
                  pl.BlockSpec((N_OCT, LANES, 2 * OCT_STATE), lambda i: (0, T_CHUNK - 1, 0)),
                  whole((N_OCT, 2 * OCT_STATE, LANES)),
                  whole(apow.shape), whole((N_OCT, 1, LANES)), whole((N_OCT, LANES, LANES)),
                  whole((N_OCT, 1, LANES))],
        out_specs=[whole((rows, D_SSM)), whole(s0_re.shape), whole(s0_im.shape)],
        out_shape=[jax.ShapeDtypeStruct((rows, D_SSM), F32),
                   jax.ShapeDtypeStruct(s0_re.shape, F32),
                   jax.ShapeDtypeStruct(s0_im.shape, F32)],
        compiler_params=_cparams(1),
        name="s5_sample",
    )(proj, s0_re, s0_im, wz, m, apow, d_skip.reshape(N_OCT, 1, LANES), gluw,
      glu_b.reshape(N_OCT, 1, LANES))


def _pool_linear(z, gi, pw_ref, pb_ref, ps_ref):
    lanes = slice(gi * POOL_GROUP, (gi + 1) * POOL_GROUP)
    return (_dot(z.astype(BF16), pw_ref[gi]) + pb_ref[:, lanes]) * ps_ref[:, lanes]


def _pool_sample_kernel(hist_ref, v_ref, pw_ref, pb_ref, ps_ref, y_ref, new_ref):
    new_ref[0:POOL_HIST - 1] = hist_ref[1:POOL_HIST]
    new_ref[POOL_HIST - 1] = v_ref[...]
    for gi, w in enumerate(POOL_WINDOWS):
        lanes = slice(gi * POOL_GROUP, (gi + 1) * POOL_GROUP)
        v = v_ref[:, lanes]
        s = v
        for r in range(POOL_HIST - (w - 1), POOL_HIST):
            s = s + hist_ref[r, :, lanes]
        y_ref[:, lanes] = _pool_linear(s / float(w) - v, gi, pw_ref, pb_ref, ps_ref)


def _pool_sample(hist_t, proj, pool_w, pool_b, pool_scale):
    rows = proj.shape[0]
    return pl.pallas_call(
        _pool_sample_kernel,
        grid=(1,),
        in_specs=[pl.BlockSpec(hist_t.shape, lambda i: (0, 0, 0)),
                  pl.BlockSpec((rows, D_POOL), lambda i: (0, D_SSM // D_POOL)),
                  pl.BlockSpec(pool_w.shape, lambda i: (0, 0, 0)),
                  pl.BlockSpec((1, D_POOL), lambda i: (0, 0)),
                  pl.BlockSpec((1, D_POOL), lambda i: (0, 0))],
        out_specs=[pl.BlockSpec((rows, D_POOL), lambda i: (0, 0)),
                   pl.BlockSpec(hist_t.shape, lambda i: (0, 0, 0))],
        out_shape=[jax.ShapeDtypeStruct((rows, D_POOL), F32),
                   jax.ShapeDtypeStruct(hist_t.shape, F32)],
        compiler_params=_cparams(1),
        name="pool_sample",
    )(hist_t, proj, pool_w, pool_b.reshape(1, D_POOL), pool_scale.reshape(1, D_POOL))


def _outproj_kernel(x_ref, ys_ref, yp_ref, gate_ref, w_ref, o_ref, *, blocks_per_seq):
    tm = x_ref.shape[0]
    ys = ys_ref[...].reshape(tm, D_SSM).astype(BF16)
    yp = yp_ref[...].astype(BF16)
    mix = _dot(ys, w_ref[0:D_SSM, :]) + _dot(yp, w_ref[D_SSM:, :])
    o_ref[...] = x_ref[...] + _mod_rows(gate_ref, blocks_per_seq) * mix


def _outproj(x, ys, yp, mod, w_out, *, tm, n_seq):
    rows = x.shape[0]
    blocks_per_seq = (rows // n_seq) // tm
    ys_spec = pl.BlockSpec((tm // T_CHUNK, None, T_CHUNK, D_SSM),
                           lambda i: (i % blocks_per_seq, i // blocks_per_seq, 0, 0))
    gate_spec = _mod_specs(False, tm, PROMPT_MOD_BLOCK, MIX_SUBLAYER)[2]
    return pl.pallas_call(
        functools.partial(_outproj_kernel, blocks_per_seq=blocks_per_seq),
        grid=(rows // tm,),
        in_specs=[pl.BlockSpec((tm, D_MODEL), lambda i: (i, 0)),
                  ys_spec,
                  pl.BlockSpec((tm, D_POOL), lambda i: (i, 0)),
                  gate_spec,
                  pl.BlockSpec((D_MODEL, D_MODEL), lambda i: (0, 0),
                               pipeline_mode=pl.Buffered(1))],
        out_specs=pl.BlockSpec((tm, D_MODEL), lambda i: (i, 0)),
        out_shape=jax.ShapeDtypeStruct((rows, D_MODEL), F32),
        compiler_params=_cparams(1),
        name="outproj",
    )(x, ys, yp, mod, w_out)


def _outproj_sample_kernel(x_ref, ys_ref, yp_ref, gate_ref, w_ref, o_ref, wo_ref):
    wo_ref[...] = w_ref[...].astype(BF16)
    mix = (_dot(ys_ref[...].astype(BF16), wo_ref[0:D_SSM, :])
           + _dot(yp_ref[...].astype(BF16), wo_ref[D_SSM:, :]))
    o_ref[...] = x_ref[...] + gate_ref[...] * mix


def _outproj_sample(x, ys, yp, mod, w_out):
    rows = x.shape[0]
    n_col = D_MODEL // SAMPLE_TN
    gate_col = (3 * MIX_SUBLAYER + 2) * n_col
    return pl.pallas_call(
        _outproj_sample_kernel,
        grid=(n_col,),
        in_specs=[pl.BlockSpec((rows, SAMPLE_TN), lambda j: (0, j)),
                  pl.BlockSpec((rows, D_SSM), lambda j: (0, 0)),
                  pl.BlockSpec((rows, D_POOL), lambda j: (0, 0)),
                  pl.BlockSpec((rows, SAMPLE_TN), lambda j: (0, gate_col + j)),
                  pl.BlockSpec((D_MODEL, SAMPLE_TN), lambda j: (0, j))],
        out_specs=[pl.BlockSpec((rows, SAMPLE_TN), lambda j: (0, j)),
                   pl.BlockSpec((D_MODEL, SAMPLE_TN), lambda j: (0, j))],
        out_shape=[jax.ShapeDtypeStruct((rows, D_MODEL), F32),
                   jax.ShapeDtypeStruct((D_MODEL, D_MODEL), BF16)],
        compiler_params=_cparams(1),
        name="outproj_sample",
    )(x, ys, yp, mod, w_out)


def kernel(x_prompt, x_sample, state_ssm_re, state_ssm_im, state_pool, c_prompt, c_sample, ada_w, ada_b, ffn1_norm, ffn1_w_gate, ffn1_w_up, ffn1_w_down, mix_norm, w_in, ssm_lambda_re, ssm_lambda_im, ssm_log_dt, ssm_b_re, ssm_b_im, ssm_c_re, ssm_c_im, ssm_d, ssm_glu_w, ssm_glu_b, pool_w, pool_b, pool_scale, w_out, ffn2_norm, ffn2_w_gate, ffn2_w_up, ffn2_w_down, final_norm):
    n_p, seq, _ = x_prompt.shape
    n_s = x_sample.shape[0]
    assert (n_p, n_s) == (N_PROMPT, N_SAMPLE) and n_p <= SUBLANES
    G, P = N_SSM_GROUPS, SSM_STATE

    c_all = jnp.concatenate([c_sample, c_prompt, jnp.zeros((SUBLANES - n_p, D_MODEL), F32)], axis=0)
    mod, ops = _adaln_s5_prep(c_all, ada_w[0], ada_b.reshape(1, N_MOD_COLS), ssm_lambda_re[0],
                              ssm_lambda_im[0], ssm_log_dt[0], ssm_b_re[0], ssm_b_im[0],
                              ssm_c_re[0], ssm_c_im[0], ssm_glu_w[0])
    pw = pool_w[0].astype(BF16)

    xs = x_sample.reshape(n_s, D_MODEL)
    xp = x_prompt.reshape(n_p * seq, D_MODEL)
    blocks_per_seq = seq // FFN_TM
    sample_rows = _RowGroup(n_s, n_s, per_row=True, first_block=0, resident=True)
    head_rows = _RowGroup(FFN_TM, FFN_ROW_CHUNK, per_row=False, first_block=0, resident=True)
    tail_rows = _RowGroup(FFN_TM, FFN_ROW_CHUNK, per_row=False, first_block=1, resident=False)

    def ffn(xs, xp, norm_g, sub, wg, wu, wd, last):
        ys, yp0, bg, bu, bd = _ffn([xs, xp], (sample_rows, head_rows), norm_g, mod, sub, wg, wu, wd,
                                   final_norm, n_blocks=1, tf=FFN_HEAD_TF, final_norm=last,
                                   blocks_per_seq=blocks_per_seq)
        (yp,) = _ffn([xp], (tail_rows,), norm_g, mod, sub, bg, bu, bd, final_norm,
                     n_blocks=n_p * blocks_per_seq - 1, tf=FFN_TF, final_norm=last,
                     blocks_per_seq=blocks_per_seq, out_rows=xp.shape[0], first_rows=yp0)
        return ys, yp

    xs, xp = ffn(xs, xp, ffn1_norm[0], FFN1_SUBLAYER, ffn1_w_gate[0], ffn1_w_up[0],
                 ffn1_w_down[0], False)

    proj_s, win = _proj_sample(xs, mix_norm[0], mod, w_in[0])
    ys_s, sre_s, sim_s = _s5_sample(proj_s, state_ssm_re[0].reshape(n_s, G * P),
                                    state_ssm_im[0].reshape(n_s, G * P), ops, ssm_d[0],
                                    ssm_glu_b[0])
    yp_s, pool_s_t = _pool_sample(jnp.swapaxes(state_pool[0], 0, 1), proj_s, pw, pool_b[0],
                                  pool_scale[0])
    pool_s = jnp.swapaxes(pool_s_t, 0, 1)
    xs, wout = _outproj_sample(xs, ys_s, yp_s, mod, w_out[0])

    u4, yp_p, v_tail = _proj(xp, mix_norm[0], mod, win, pw, pool_b[0], pool_scale[0],
                             tm=PROJ_TM, n_seq=n_p)
    ys_p, sre_p, sim_p = _s5_prompt(u4.reshape(n_p * seq, D_SSM), ops, ssm_d[0], ssm_glu_b[0],
                                    n_seq=n_p)
    xp = _outproj(xp, ys_p.reshape(seq // T_CHUNK, n_p, T_CHUNK, D_SSM), yp_p, mod, wout,
                  tm=OUTPROJ_TM, n_seq=n_p)

    y_sample, y_prompt = ffn(xs, xp, ffn2_norm[0], FFN2_SUBLAYER, ffn2_w_gate[0], ffn2_w_up[0],
                             ffn2_w_down[0], True)

    pool_p = v_tail[:, POOL_TAIL - POOL_HIST:, :][None]
    return (y_prompt.reshape(n_p, seq, D_MODEL), y_sample.reshape(n_s, 1, D_MODEL),
            sre_p.reshape(1, n_p, G, P), sim_p.reshape(1, n_p, G, P), pool_p,
            sre_s.reshape(1, n_s, G, P), sim_s.reshape(1, n_s, G, P), pool_s[None])
```

```python
import functools
from typing import NamedTuple

import jax
import jax.numpy as jnp
from jax import lax
from jax.experimental import pallas as pl
from jax.experimental.pallas import tpu as pltpu

F32 = jnp.float32
BF16 = jnp.bfloat16

D_MODEL = 2048
D_FF = 5632
D_SSM = 1024
D_POOL = 1024
SSM_GROUP = 16
SSM_STATE = 64
N_SSM_GROUPS = 64
POOL_WINDOWS = (2, 4, 8, 16)
POOL_GROUP = 256
POOL_HIST = 15
N_MOD_COLS = 9 * D_MODEL
N_PROMPT = 4
N_SAMPLE = 128
EPS = 1e-6
FFN_RES = 0.5

LANES = 128
SUBLANES = 8
MXU_TILE = 256
VMEM_LIMIT = 60 * 1024 * 1024

PROMPT_MOD_BLOCK = N_SAMPLE // SUBLANES
NORM_ROWS = 32
FFN1_SUBLAYER, MIX_SUBLAYER, FFN2_SUBLAYER = 0, 1, 2

FFN_TM = 1024
FFN_ROW_CHUNK = 512
FFN_TF = 512
FFN_HEAD_TF = 256
CAST_TN = 256
PROJ_TM = 1024
OUTPROJ_TM = 512

OCT = LANES // SSM_GROUP
N_OCT = N_SSM_GROUPS // OCT
T_CHUNK = 8
OCT_STATE = OCT * SSM_STATE
CHUNK_K = T_CHUNK * LANES
TOK_PER_TILE = MXU_TILE // LANES


def _cparams(n_axes):
    return pltpu.CompilerParams(dimension_semantics=("arbitrary",) * n_axes,
                                vmem_limit_bytes=VMEM_LIMIT)


def _dot(a, b):
    return jnp.dot(a, b, preferred_element_type=F32)


ADALN_TN = 2048


def _adaln_kernel(c_ref, w_ref, b_ref, o_ref):
    c = c_ref[...]
    sc = (c * jax.nn.sigmoid(c)).astype(BF16)
    o_ref[...] = _dot(sc, w_ref[...].astype(BF16)) + b_ref[...]


def _norm_mod(x, g, shift, scale):
    ms = jnp.mean(x * x, axis=-1, keepdims=True)
    y = x * lax.rsqrt(ms + EPS) * g
    return y * (1.0 + scale) + shift


def _mod_specs(per_row, tm, prompt_row_block, sub):
    specs = []
    for m in range(3):
        col = 3 * sub + m
        if per_row:
            specs.append(pl.BlockSpec((tm, D_MODEL), lambda i, *_, col=col: (i, col)))
        else:
            specs.append(pl.BlockSpec((SUBLANES, D_MODEL),
                                      lambda i, *_, col=col: (prompt_row_block, col)))
    return specs


def _mod_rows(ref, blocks_per_seq, rows=None):
    if blocks_per_seq:
        return ref[pl.ds(pl.program_id(0) // blocks_per_seq, 1), :]
    return ref[...] if rows is None else ref[rows, :]


class _RowGroup(NamedTuple):
    rows: int
    row_chunk: int
    per_row: bool
    first_block: int
    resident: bool


def _ffn_step(s, n_j, has_copy):
    t = jnp.maximum(s - 1, 0) if has_copy else s
    return t // n_j, t % n_j


def _ffn_kernel(*refs, groups, blocks_per_seq, n_j, final_norm, emit_bf16, has_copy, n_cast):
    n = len(groups)
    g_ref, wg_ref, wu_ref, wd_ref, fg_ref = refs[4 * n:4 * n + 5]
    n_in = 4 * n + 5 + int(has_copy) + n_cast
    outs = refs[n_in:-1]
    h_ref = refs[-1]
    for k in range(n_cast):
        refs[-1 - n_cast + k][...] = refs[n_in - n_cast + k][...].astype(BF16)
    step = pl.program_id(0)
    i, j = _ffn_step(step, n_j, has_copy)
    last_j = n_j - 1
    active = step >= int(has_copy)

    if has_copy:
        @pl.when(step == 0)
        def _():
            outs[0][...] = refs[4 * n + 5][...]

    if emit_bf16:
        bf16_refs = outs[n:n + 3]
        for w_ref, wo_ref in zip((wg_ref, wu_ref, wd_ref), bf16_refs):
            wo_ref[...] = w_ref[...].astype(BF16)
        wg_ref, wu_ref, wd_ref = bf16_refs

    chunks, h_base = [], 0
    for k, grp in enumerate(groups):
        chunks += [(k, r, h_base + r) for r in range(0, grp.rows, grp.row_chunk)]
        h_base += grp.rows

    def mod_rows(k, ref, rows):
        if groups[k].per_row:
            return ref[rows, :]
        seq = (i + groups[k].first_block) // blocks_per_seq
        return ref[pl.ds(seq, 1), :]

    def norm_chunk(c):
        k, r, hr = chunks[c]
        x_ref, shift_ref, scale_ref, _ = refs[4 * k:4 * k + 4]
        for s in range(0, groups[k].row_chunk, NORM_ROWS):
            sr = pl.ds(r + s, NORM_ROWS)
            h = _norm_mod(x_ref[sr, :], g_ref[...], mod_rows(k, shift_ref, sr),
                          mod_rows(k, scale_ref, sr))
            h_ref[pl.ds(hr + s, NORM_ROWS), :] = h.astype(BF16)

    def chunk(c, first, last):
        k, r, hr = chunks[c]
        size = groups[k].row_chunk
        x_ref, _, _, gate_ref = refs[4 * k:4 * k + 4]
        o_ref = outs[k]
        rows = pl.ds(r, size)
        if first and c + 1 < len(chunks):
            norm_chunk(c + 1)
        h = h_ref[pl.ds(hr, size), :]
        g = _dot(h, wg_ref[...])
        u = _dot(h, wu_ref[...])
        a = (g * jax.nn.sigmoid(g) * u).astype(BF16)
        d = _dot(a, wd_ref[...])
        acc = d if first else o_ref[rows, :] + d
        if not last:
            o_ref[rows, :] = acc
            return
        o_ref[rows, :] = x_ref[rows, :] + FFN_RES * mod_rows(k, gate_ref, rows) * acc
        if final_norm:
            for s in range(0, size, NORM_ROWS):
                sr = pl.ds(r + s, NORM_ROWS)
                y = o_ref[sr, :]
                ms = jnp.mean(y * y, axis=-1, keepdims=True)
                o_ref[sr, :] = y * lax.rsqrt(ms + EPS) * fg_ref[...]

    def run(first, last):
        if first:
            norm_chunk(0)
        for c in range(len(chunks)):
            chunk(c, first, last)

    pl.when(jnp.logical_and(active, j == 0))(lambda: run(True, False))
    pl.when(jnp.logical_and(j > 0, j < last_j))(lambda: run(False, False))
    pl.when(j == last_j)(lambda: run(False, True))


def _ffn(xs, groups, norm_g, mod, sub, wg, wu, wd, final_g, *, n_blocks, tf, final_norm,
         blocks_per_seq, out_rows=None, first_rows=None, side_casts=()):
    emit_bf16 = wg.dtype == F32
    has_copy = first_rows is not None
    assert not emit_bf16 or n_blocks == 1
    assert not has_copy or len(groups) == 1
    n_j = D_FF // tf
    step = functools.partial(_ffn_step, n_j=n_j, has_copy=has_copy)

    def row_spec(grp, col=0, out=False):
        mode = dict(pipeline_mode=pl.Buffered(1)) if grp.resident else {}
        first = 0 if out and not has_copy else grp.first_block

        def index(s):
            block = step(s)[0] + first
            return (jnp.where(s == 0, 0, block) if out and has_copy else block), col
        return pl.BlockSpec((grp.rows, D_MODEL), index, **mode)

    in_specs, operands = [], []
    for x, grp in zip(xs, groups):
        in_specs.append(row_spec(grp))
        operands.append(x)
        for m in range(3):
            col = 3 * sub + m
            if grp.per_row:
                in_specs.append(row_spec(grp, col))
            else:
                in_specs.append(pl.BlockSpec((SUBLANES, D_MODEL),
                                             lambda s, col=col: (PROMPT_MOD_BLOCK, col)))
            operands.append(mod)
    w_specs = [pl.BlockSpec((D_MODEL, tf), lambda s: (0, step(s)[1])),
               pl.BlockSpec((D_MODEL, tf), lambda s: (0, step(s)[1])),
               pl.BlockSpec((tf, D_MODEL), lambda s: (step(s)[1], 0))]
    vec_spec = pl.BlockSpec((1, D_MODEL), lambda s: (0, 0))
    in_specs += [vec_spec, *w_specs, vec_spec]
    operands += [norm_g.reshape(1, D_MODEL), wg, wu, wd, final_g.reshape(1, D_MODEL)]
    if has_copy:
        in_specs.append(pl.BlockSpec(first_rows.shape, lambda s: (0, 0),
                                     pipeline_mode=pl.Buffered(1)))
        operands.append(first_rows)
    def cast_specs():
        specs = []
        for k, w in enumerate(side_casts):
            n_t = w.shape[1] // CAST_TN
            assert (k + 1) * n_t <= n_blocks * n_j
            specs.append(pl.BlockSpec(
                (w.shape[0], CAST_TN),
                lambda s, k=k, n_t=n_t: (0, jnp.clip(s - k * n_t, 0, n_t - 1))))
        return specs
    in_specs += cast_specs()
    operands += list(side_casts)

    out_specs = [row_spec(grp, out=True) for grp in groups]
    out_shape = [jax.ShapeDtypeStruct((out_rows or grp.rows, D_MODEL), F32) for grp in groups]
    if emit_bf16:
        out_specs += w_specs
        out_shape += [jax.ShapeDtypeStruct(w.shape, BF16) for w in (wg, wu, wd)]
    out_specs += cast_specs()
    out_shape +=[jax.ShapeDtypeStruct(w.shape, BF16) for w in side_casts]
    kern = functools.partial(_ffn_kernel, groups=groups, blocks_per_seq=blocks_per_seq, n_j=n_j,
                             final_norm=final_norm, emit_bf16=emit_bf16, has_copy=has_copy,
                             n_cast=len(side_casts))
    return pl.pallas_call(
        kern,
        grid=(int(has_copy) + n_blocks * n_j,),
        in_specs=in_specs,
        out_specs=out_specs,
        out_shape=out_shape,
        scratch_shapes=[pltpu.VMEM((sum(grp.rows for grp in groups), D_MODEL), BF16)],
        compiler_params=_cparams(1),
        name="ffn",
    )(*operands)


POOL_TAIL = 16


def _proj_kernel(x_ref, g_ref, shift_ref, scale_ref, w_ref, pw_ref, pb_ref, ps_ref,
                 u_ref, y_ref, tail_out_ref, tail_ref, h_ref, *, blocks_per_seq):
    tm = x_ref.shape[0]
    half = tm // 2
    blk = pl.program_id(0) % blocks_per_seq
    sh = _mod_rows(shift_ref, blocks_per_seq)
    sc = _mod_rows(scale_ref, blocks_per_seq)

    @pl.when(blk == 0)
    def _():
        tail_ref[...] = jnp.zeros(tail_ref.shape, F32)

    def norm_half(r):
        for s in range(0, half, NORM_ROWS):
            rows = pl.ds(r * half + s, NORM_ROWS)
            h_ref[rows, :] = _norm_mod(x_ref[rows, :], g_ref[...], sh, sc).astype(BF16)

    norm_half(0)
    norm_half(1)
    pv = jnp.concatenate([_dot(h_ref[pl.ds(r * half, half), :], w_ref[:, D_SSM:])
                          for r in range(2)], axis=0)
    for r in range(2):
        pu = _dot(h_ref[pl.ds(r * half, half), :], w_ref[:, 0:D_SSM])
        u_ref[pl.ds(r * half // T_CHUNK, half // T_CHUNK), :, :] = pu.reshape(
            half // T_CHUNK, T_CHUNK, D_SSM)

    row_ext = lax.broadcasted_iota(jnp.int32, (POOL_TAIL + tm, POOL_GROUP), 0)
    pos = blk * tm + lax.broadcasted_iota(jnp.int32, (tm, POOL_GROUP), 0)
    for gi, w in enumerate(POOL_WINDOWS):
        lanes = slice(gi * POOL_GROUP, (gi + 1) * POOL_GROUP)
        v = pv[:, lanes]
        s = jnp.concatenate([tail_ref[:, lanes], v], axis=0)
        k = 1
        while k < w:
            s = s + jnp.where(row_ext >= k, pltpu.roll(s, k, axis=0), 0.0)
            k *= 2
        cnt = jnp.clip(pos + 1, 1, w).astype(F32)
        z = s[POOL_TAIL:, :] / cnt - v
        y_ref[:, lanes] = _pool_linear(z, gi, pw_ref, pb_ref, ps_ref).astype(y_ref.dtype)
    tail = pv[tm - POOL_TAIL:, :]
    tail_ref[...] = tail
    tail_out_ref[...] = tail


def _proj(x, norm_g, mod, w_in, pool_w, pool_b, pool_scale, *, tm, n_seq):
    rows = x.shape[0]
    seq_len = rows // n_seq
    blocks_per_seq = seq_len // tm
    u_shape = (seq_len // T_CHUNK, n_seq, T_CHUNK, D_SSM)
    u_spec = pl.BlockSpec((tm // T_CHUNK, None, T_CHUNK, D_SSM),
                          lambda i: (i % blocks_per_seq, i // blocks_per_seq, 0, 0))
    shift_spec, scale_spec, _ = _mod_specs(False, tm, PROMPT_MOD_BLOCK, MIX_SUBLAYER)
    vec_spec = pl.BlockSpec((1, D_POOL), lambda i: (0, 0))
    return pl.pallas_call(
        functools.partial(_proj_kernel, blocks_per_seq=blocks_per_seq),
        grid=(rows // tm,),
        in_specs=[pl.BlockSpec((tm, D_MODEL), lambda i: (i, 0)),
                  pl.BlockSpec((1, D_MODEL), lambda i: (0, 0)),
                  shift_spec, scale_spec,
                  pl.BlockSpec((D_MODEL, D_MODEL), lambda i: (0, 0),
                               pipeline_mode=pl.Buffered(1)),
                  pl.BlockSpec(pool_w.shape, lambda i: (0, 0, 0)), vec_spec, vec_spec],
        out_specs=[u_spec, pl.BlockSpec((tm, D_POOL), lambda i: (i, 0)),
                   pl.BlockSpec((None, POOL_TAIL, D_POOL), lambda i: (i // blocks_per_seq, 0, 0))],
        out_shape=[jax.ShapeDtypeStruct(u_shape, F32),
                   jax.ShapeDtypeStruct((rows, D_POOL), BF16),
                   jax.ShapeDtypeStruct((n_seq, POOL_TAIL, D_POOL), F32)],
        scratch_shapes=[pltpu.VMEM((POOL_TAIL, D_POOL), F32), pltpu.VMEM((tm, D_MODEL), BF16)],
        compiler_params=_cparams(1),
        name="proj",
    )(x, norm_g.reshape(1, D_MODEL), mod, mod, w_in, pool_w, pool_b.reshape(1, D_POOL),
      pool_scale.reshape(1, D_POOL))


SAMPLE_TN = 512


def _proj_sample_kernel(x_ref, g_ref, shift_ref, scale_ref, w_ref, p_ref, h_ref):
    @pl.when(pl.program_id(0) == 0)
    def _():
        h = _norm_mod(x_ref[...], g_ref[...], shift_ref[...], scale_ref[...])
        h_ref[...] = h.astype(BF16)
    p_ref[...] = _dot(h_ref[...], w_ref[...])


def _proj_sample(x, norm_g, mod, w_in):
    rows = x.shape[0]
    shift_col, scale_col = 3 * MIX_SUBLAYER, 3 * MIX_SUBLAYER + 1
    return pl.pallas_call(
        _proj_sample_kernel,
        grid=(D_MODEL // SAMPLE_TN,),
        in_specs=[pl.BlockSpec((rows, D_MODEL), lambda j: (0, 0)),
                  pl.BlockSpec((1, D_MODEL), lambda j: (0, 0)),
                  pl.BlockSpec((rows, D_MODEL), lambda j: (0, shift_col)),
                  pl.BlockSpec((rows, D_MODEL), lambda j: (0, scale_col)),
                  pl.BlockSpec((D_MODEL, SAMPLE_TN), lambda j: (0, j))],
        out_specs=pl.BlockSpec((rows, SAMPLE_TN), lambda j: (0, j)),
        out_shape=jax.ShapeDtypeStruct((rows, D_MODEL), F32),
        scratch_shapes=[pltpu.VMEM((rows, D_MODEL), BF16)],
        compiler_params=_cparams(1),
        name="proj_sample",
    )(x, norm_g.reshape(1, D_MODEL), mod, mod, w_in)


def _discretise(lam_re, lam_im, log_dt):
    lr = jnp.minimum(lam_re, -1e-4)
    li = lam_im
    dt = jnp.exp(log_dt)
    mag = jnp.exp(lr * dt)
    ang = li * dt
    a_re = mag * jnp.cos(ang)
    a_im = mag * jnp.sin(ang)
    den = lr * lr + li * li
    num_re = a_re - 1.0
    f_re = (num_re * lr + a_im * li) / den
    f_im = (a_im * lr - num_re * li) / den
    return a_re, a_im, f_re, f_im


def _s5_prep_kernel(lam_re1, lam_im1, ldt1, c_re1, c_im1,
                    lam_re2, lam_im2, ldt2, b_re2, b_im2, gluw_t,
                    toep_ref, wz_ref, m_ref, apow_ref, gluw_ref):
    g1 = lax.broadcasted_iota(jnp.int32, (SSM_STATE, LANES), 1) // SSM_GROUP
    g2 = lax.broadcasted_iota(jnp.int32, (SSM_GROUP, OCT_STATE), 1) // SSM_STATE

    def expand1(x):
        return jnp.concatenate([jnp.where(g1 == g, x, 0.0) for g in range(OCT)], axis=0)

    def expand2(x):
        return jnp.concatenate([jnp.where(g2 == g, x, 0.0) for g in range(OCT)], axis=0)

    def split(x):
        hi = x.astype(BF16)
        return hi, (x - hi.astype(F32)).astype(BF16)

    a1_re, a1_im, _, _ = _discretise(lam_re1[...], lam_im1[...], ldt1[...])
    a2_re, a2_im, f_re, f_im = _discretise(lam_re2[...], lam_im2[...], ldt2[...])
    br, bi = b_re2[...], b_im2[...]
    bbar_re = f_re * br - f_im * bi
    bbar_im = f_re * bi + f_im * br
    cr, ci = c_re1[...], c_im1[...]
    bre_hi, bre_lo = split(expand2(bbar_re))
    bim_hi, bim_lo = split(expand2(bbar_im))

    def dot3(x_hi, x_lo, y_hi, y_lo):
        return _dot(x_hi, y_hi) + (_dot(x_hi, y_lo) + _dot(x_lo, y_hi))

    p1_re, p1_im = jnp.ones_like(a1_re), jnp.zeros_like(a1_im)
    p2_re, p2_im = jnp.ones_like(a2_re), jnp.zeros_like(a2_im)
    kk = []
    zero_blk = jnp.zeros((LANES, LANES), F32)
    for j in range(T_CHUNK + 1):
        care_hi, care_lo = split(expand1(cr * p1_re - ci * p1_im))
        caim_hi, caim_lo = split(expand1(cr * p1_im + ci * p1_re))
        m_ref[0:OCT_STATE, j * LANES:(j + 1) * LANES] = care_hi
        m_ref[OCT_STATE:2 * OCT_STATE, j * LANES:(j + 1) * LANES] = -caim_hi
        if j < T_CHUNK:
            t = T_CHUNK - 1 - j
            ba_re = expand2(bbar_re * p2_re - bbar_im * p2_im)
            ba_im = expand2(bbar_re * p2_im + bbar_im * p2_re)
            wz_ref[t * LANES:(t + 1) * LANES, 0:OCT_STATE] = ba_re.astype(BF16)
            wz_ref[t * LANES:(t + 1) * LANES, OCT_STATE:2 * OCT_STATE] = ba_im.astype(BF16)
            kk.append(dot3(bre_hi, bre_lo, care_hi, care_lo)
                      - dot3(bim_hi, bim_lo, caim_hi, caim_lo))
        if j == 1:
            apow_ref[0:1, :] = p2_re
            apow_ref[1:2, :] = p2_im
        if j == T_CHUNK:
            apow_ref[2:3, :] = p2_re
            apow_ref[3:4, :] = p2_im
        p1_re, p1_im = p1_re * a1_re - p1_im * a1_im, p1_re * a1_im + p1_im * a1_re
        p2_re, p2_im = p2_re * a2_re - p2_im * a2_im, p2_re * a2_im + p2_im * a2_re

    for t in range(T_CHUNK):
        for t2 in range(T_CHUNK):
            blk = kk[t2 - t] if t2 >= t else zero_blk
            toep_ref[t * LANES:(t + 1) * LANES, t2 * LANES:(t2 + 1) * LANES] = blk.astype(BF16)

    rg = lax.broadcasted_iota(jnp.int32, (LANES, LANES), 0) // SSM_GROUP
    lg = lax.broadcasted_iota(jnp.int32, (LANES, LANES), 1) // SSM_GROUP
    gw = jnp.where(rg == lg, gluw_t[...], 0.0).astype(BF16)
    gz = jnp.zeros((LANES, LANES), BF16)
    for a in range(TOK_PER_TILE):
        for b in range(TOK_PER_TILE):
            gluw_ref[a * LANES:(a + 1) * LANES, b * LANES:(b + 1) * LANES] = gw if a == b else gz


def _adaln_s5_prep_kernel(c_ref, w_ref, b_ref, *refs):
    n_prep_in = len(refs) - 6
    _adaln_kernel(c_ref, w_ref, b_ref, refs[n_prep_in])

    @pl.when(pl.program_id(0) < N_OCT)
    def _():
        _s5_prep_kernel(*refs[:n_prep_in], *refs[n_prep_in + 1:])


def _adaln_s5_prep(c_all, ada_w, ada_b, lam_re, lam_im, log_dt, b_re, b_im, c_re, c_im, glu_w):
    G, P, H = N_SSM_GROUPS, SSM_STATE, SSM_GROUP
    rows, n_cols = c_all.shape[0], ada_w.shape[1]
    n_steps = n_cols // ADALN_TN
    assert n_steps >= N_OCT
    row = lambda a: a.reshape(N_OCT, 1, OCT_STATE)
    ldt = jnp.broadcast_to(log_dt[:, None], (G, P))

    def col(a_gp):
        a = jnp.swapaxes(a_gp.reshape(N_OCT, OCT, P), 1, 2)
        return jnp.broadcast_to(a[..., None], (N_OCT, P, OCT, H)).reshape(N_OCT, P, LANES)

    lay1 = lambda a_ghp: jnp.swapaxes(a_ghp.reshape(N_OCT, LANES, P), 1, 2)
    lay2 = lambda a_gph: jnp.swapaxes(a_gph.reshape(N_OCT, OCT_STATE, H), 1, 2)
    c_re1, c_im1, b_re2, b_im2 = lay1(c_re), lay1(c_im), lay2(b_re), lay2(b_im)
    gw = glu_w.reshape(N_OCT, OCT, H, 1, H)
    gluw_t = jnp.broadcast_to(gw, (N_OCT, OCT, H, OCT, H)).reshape(N_OCT, LANES, LANES)

    def spec(shape):
        return pl.BlockSpec((None,) + shape,
                            lambda j: (jnp.minimum(j, N_OCT - 1),) + (0,) * len(shape))

    m_cols = (T_CHUNK + 1) * LANES
    mod, *ops = pl.pallas_call(
        _adaln_s5_prep_kernel,
        grid=(n_steps,),
        in_specs=[pl.BlockSpec((rows, D_MODEL), lambda j: (0, 0)),
                  pl.BlockSpec((D_MODEL, ADALN_TN), lambda j: (0, j)),
                  pl.BlockSpec((1, ADALN_TN), lambda j: (0, j))]
                 + [spec((SSM_STATE, LANES))] * 5
                 + [spec((1, OCT_STATE))] * 3 + [spec((SSM_GROUP, OCT_STATE))] * 2
                 + [spec((LANES, LANES))],
        out_specs=[pl.BlockSpec((rows, ADALN_TN), lambda j: (0, j)),
                   spec((CHUNK_K, CHUNK_K)), spec((CHUNK_K, 2 * OCT_STATE)),
                   spec((2 * OCT_STATE, m_cols)), spec((4, OCT_STATE)),
                   spec((MXU_TILE, MXU_TILE))],
        out_shape=[jax.ShapeDtypeStruct((rows, n_cols), F32),
                   jax.ShapeDtypeStruct((N_OCT, CHUNK_K, CHUNK_K), BF16),
                   jax.ShapeDtypeStruct((N_OCT, CHUNK_K, 2 * OCT_STATE), BF16),
                   jax.ShapeDtypeStruct((N_OCT, 2 * OCT_STATE, m_cols), BF16),
                   jax.ShapeDtypeStruct((N_OCT, 4, OCT_STATE), F32),
                   jax.ShapeDtypeStruct((N_OCT, MXU_TILE, MXU_TILE), BF16)],
        compiler_params=_cparams(1),
        name="adaln_s5_prep",
    )(c_all, ada_w, ada_b, col(lam_re), col(lam_im), col(ldt), c_re1, c_im1,
      row(lam_re), row(lam_im), row(ldt), b_re2, b_im2, gluw_t)
    return mod, tuple(ops)


GELU_C0 = 0.7978845608028654
GELU_C1 = GELU_C0 * 0.044715


def _glu_out(y, gluw, glub):
    gy = y * (0.5 + 0.5 * jnp.tanh(y * (GELU_C0 + GELU_C1 * (y * y))))
    z = _dot(gy.astype(BF16), gluw) + glub
    return gy * (0.5 + 0.5 * jnp.tanh(0.5 * z))


def _s5_prompt_kernel(x_ref, toep_ref, wz_ref, m_ref, apow_ref, d_ref, gluw_ref, glub_ref,
                      y_ref, sre_ref, sim_ref, xr_ref, z_ref, yv_ref, *, n_seq, row_chunk):
    n_rows = x_ref.shape[0] // T_CHUNK
    n_blk = n_rows // row_chunk

    def tokens(b, t):
        return pl.ds(b * (row_chunk * T_CHUNK) + t, row_chunk, stride=T_CHUNK)

    for b in range(n_blk):
        rows = pl.ds(b * row_chunk, row_chunk)
        for t in range(T_CHUNK):
            xr_ref[rows, t * LANES:(t + 1) * LANES] = x_ref[tokens(b, t), :].astype(BF16)
        xr = xr_ref[rows, :]
        z_ref[rows, :] = _dot(xr, wz_ref[...])
        for nt in range(CHUNK_K // MXU_TILE):
            k_end = (nt + 1) * MXU_TILE
            cols = slice(nt * MXU_TILE, k_end)
            yv_ref[rows, cols] = _dot(xr[:, 0:k_end], toep_ref[0:k_end, cols])

    are, aim = apow_ref[2:3, :], apow_ref[3:4, :]
    lo = lax.broadcasted_iota(jnp.int32, (SUBLANES, OCT_STATE), 0) < n_seq

    def step(k, carry):
        pre, pim = carry
        rows = pl.ds(pl.multiple_of(k * SUBLANES, SUBLANES), SUBLANES)
        zre = z_ref[rows, 0:OCT_STATE]
        zim = z_ref[rows, OCT_STATE:2 * OCT_STATE]
        w1re = are * pre - aim * pim + zre
        w1im = are * pim + aim * pre + zim
        r1re = pltpu.roll(w1re, n_seq, axis=0)
        r1im = pltpu.roll(w1im, n_seq, axis=0)
        w2re = are * r1re - aim * r1im + zre
        w2im = are * r1im + aim * r1re + zim
        z_ref[rows, 0:OCT_STATE] = jnp.where(lo, pre, r1re)
        z_ref[rows, OCT_STATE:2 * OCT_STATE] = jnp.where(lo, pim, r1im)
        nre = jnp.where(lo, pltpu.roll(w2re, n_seq, axis=0), w2re)
        nim = jnp.where(lo, pltpu.roll(w2im, n_seq, axis=0), w2im)
        return nre, nim

    zeros = jnp.zeros((SUBLANES, OCT_STATE), F32)
    fre, fim = lax.fori_loop(0, n_rows // SUBLANES, step, (zeros, zeros))
    sre_ref[...] = fre[0:n_seq, :]
    sim_ref[...] = fim[0:n_seq, :]

    d2 = jnp.concatenate([d_ref[...]] * TOK_PER_TILE, axis=1)
    glub2 = jnp.concatenate([glub_ref[...]] * TOK_PER_TILE, axis=1)

    for b in range(n_blk):
        rows = pl.ds(b * row_chunk, row_chunk)
        y = yv_ref[rows, :] + _dot(z_ref[rows, :].astype(BF16), m_ref[:, LANES:])
        for t0 in range(0, T_CHUNK, TOK_PER_TILE):
            toks = [tokens(b, t0 + i) for i in range(TOK_PER_TILE)]
            u = jnp.concatenate([x_ref[tok, :] for tok in toks], axis=1)
            yt = y[:, t0 * LANES:(t0 + TOK_PER_TILE) * LANES] + d2 * u
            out = _glu_out(yt, gluw_ref[...], glub2)
            for i, tok in enumerate(toks):
                y_ref[tok, :] = out[:, i * LANES:(i + 1) * LANES]


def _s5_prompt(u_flat, ops, d_skip, glu_b, *, n_seq):
    toep, wz, m, apow, gluw = ops
    assert 2 * n_seq == SUBLANES
    rows = u_flat.shape[0]
    n_rows = rows // T_CHUNK
    oct_spec = lambda shape: pl.BlockSpec((None,) + shape, lambda o: (o, 0, 0))
    kern = functools.partial(_s5_prompt_kernel, n_seq=n_seq, row_chunk=256)
    return pl.pallas_call(
        kern,
        grid=(N_OCT,),
        in_specs=[pl.BlockSpec((rows, LANES), lambda o: (0, o)),
                  oct_spec(toep.shape[1:]), oct_spec(wz.shape[1:]), oct_spec(m.shape[1:]),
                  oct_spec(apow.shape[1:]), oct_spec((1, LANES)), oct_spec(gluw.shape[1:]),
                  oct_spec((1, LANES))],
        out_specs=[pl.BlockSpec((rows, LANES), lambda o: (0, o)),
                   pl.BlockSpec((n_seq, OCT_STATE), lambda o: (0, o)),
                   pl.BlockSpec((n_seq, OCT_STATE), lambda o: (0, o))],
        out_shape=[jax.ShapeDtypeStruct((rows, D_SSM), F32),
                   jax.ShapeDtypeStruct((n_seq, N_OCT * OCT_STATE), F32),
                   jax.ShapeDtypeStruct((n_seq, N_OCT * OCT_STATE), F32)],
        scratch_shapes=[pltpu.VMEM((n_rows, CHUNK_K), BF16),
                        pltpu.VMEM((n_rows, 2 * OCT_STATE), F32),
                        pltpu.VMEM((n_rows, CHUNK_K), F32)],
        compiler_params=_cparams(1),
        name="s5_prompt",
    )(u_flat, toep, wz, m, apow, d_skip.reshape(N_OCT, 1, LANES), gluw,
      glu_b.reshape(N_OCT, 1, LANES))


def _s5_sample_kernel(u_ref, s0re_ref, s0im_ref, wz_ref, m_ref, apow_ref, d_ref, gluw_ref,
                      glub_ref, y_ref, sre_ref, sim_ref):
    for o in range(N_OCT):
        ch = slice(o * LANES, (o + 1) * LANES)
        st = slice(o * OCT_STATE, (o + 1) * OCT_STATE)
        u = u_ref[:, ch]
        z = _dot(u.astype(BF16), wz_ref[o])
        are, aim = apow_ref[o, 0:1, :], apow_ref[o, 1:2, :]
        s0re, s0im = s0re_ref[:, st], s0im_ref[:, st]
        nre = are * s0re - aim * s0im + z[:, 0:OCT_STATE]
        nim = are * s0im + aim * s0re + z[:, OCT_STATE:2 * OCT_STATE]
        sre_ref[:, st] = nre
        sim_ref[:, st] = nim
        y = (_dot(nre.astype(BF16), m_ref[o, 0:OCT_STATE, :])
             + _dot(nim.astype(BF16), m_ref[o, OCT_STATE:2 * OCT_STATE, :])
             + d_ref[o] * u)
        y_ref[:, ch] = _glu_out(y, gluw_ref[o], glub_ref[o])


def _s5_sample(proj, s0_re, s0_im, ops, d_skip, glu_b):
    _, wz, m, apow, gluw = ops
    rows = proj.shape[0]
    whole = lambda shape: pl.BlockSpec(shape, lambda i: (0,) * len(shape))
    return pl.pallas_call(
        _s5_sample_kernel,
        grid=(1,),
        in_specs=[whole((rows, D_SSM)), whole(s0_re.shape), whole(s0_im.shape),
                  pl.BlockSpec((N_OCT, LANES, 2 * OCT_STATE), lambda i: (0, T_CHUNK - 1, 0)),
                  whole((N_OCT, 2 * OCT_STATE, LANES)),
                  whole(apow.shape), whole((N_OCT, 1, LANES)), whole((N_OCT, LANES, LANES)),
                  whole((N_OCT, 1, LANES))],
        out_specs=[whole((rows, D_SSM)), whole(s0_re.shape), whole(s0_im.shape)],
        out_shape=[jax.ShapeDtypeStruct((rows, D_SSM), F32),
                   jax.ShapeDtypeStruct(s0_re.shape, F32),
                   jax.ShapeDtypeStruct(s0_im.shape, F32)],
        compiler_params=_cparams(1),
        name="s5_sample",
    )(proj, s0_re, s0_im, wz, m, apow, d_skip.reshape(N_OCT, 1, LANES), gluw,
      glu_b.reshape(N_OCT, 1, LANES))


def _pool_linear(z, gi, pw_ref, pb_ref, ps_ref):
    lanes = slice(gi * POOL_GROUP, (gi + 1) * POOL_GROUP)
    return (_dot(z.astype(BF16), pw_ref[gi]) + pb_ref[:, lanes]) * ps_ref[:, lanes]


def _pool_sample_kernel(hist_ref, v_ref, pw_ref, pb_ref, ps_ref, y_ref, new_ref):
    new_ref[0:POOL_HIST - 1] = hist_ref[1:POOL_HIST]
    new_ref[POOL_HIST - 1] = v_ref[...]
    for gi, w in enumerate(POOL_WINDOWS):
        lanes = slice(gi * POOL_GROUP, (gi + 1) * POOL_GROUP)
        v = v_ref[:, lanes]
        s = v
        for r in range(POOL_HIST - (w - 1), POOL_HIST):
            s = s + hist_ref[r, :, lanes]
        y_ref[:, lanes] = _pool_linear(s / float(w) - v, gi, pw_ref, pb_ref, ps_ref)


def _pool_sample(hist_t, proj, pool_w, pool_b, pool_scale):
    rows = proj.shape[0]
    return pl.pallas_call(
        _pool_sample_kernel,
        grid=(1,),
        in_specs=[pl.BlockSpec(hist_t.shape, lambda i: (0, 0, 0)),
                  pl.BlockSpec((rows, D_POOL), lambda i: (0, D_SSM // D_POOL)),
                  pl.BlockSpec(pool_w.shape, lambda i: (0, 0, 0)),
                  pl.BlockSpec((1, D_POOL), lambda i: (0, 0)),
                  pl.BlockSpec((1, D_POOL), lambda i: (0, 0))],
        out_specs=[pl.BlockSpec((rows, D_POOL), lambda i: (0, 0)),
                   pl.BlockSpec(hist_t.shape, lambda i: (0, 0, 0))],
        out_shape=[jax.ShapeDtypeStruct((rows, D_POOL), F32),
                   jax.ShapeDtypeStruct(hist_t.shape, F32)],
        compiler_params=_cparams(1),
        name="pool_sample",
    )(hist_t, proj, pool_w, pool_b.reshape(1, D_POOL), pool_scale.reshape(1, D_POOL))


def _outproj_kernel(x_ref, ys_ref, yp_ref, gate_ref, w_ref, o_ref, *, blocks_per_seq):
    tm = x_ref.shape[0]
    ys = ys_ref[...].reshape(tm, D_SSM).astype(BF16)
    yp = yp_ref[...].astype(BF16)
    mix = _dot(ys, w_ref[0:D_SSM, :]) + _dot(yp, w_ref[D_SSM:, :])
    o_ref[...] = x_ref[...] + _mod_rows(gate_ref, blocks_per_seq) * mix


def _outproj(x, ys, yp, mod, w_out, *, tm, n_seq):
    rows = x.shape[0]
    blocks_per_seq = (rows // n_seq) // tm
    ys_spec = pl.BlockSpec((tm // T_CHUNK, None, T_CHUNK, D_SSM),
                           lambda i: (i % blocks_per_seq, i // blocks_per_seq, 0, 0))
    gate_spec = _mod_specs(False, tm, PROMPT_MOD_BLOCK, MIX_SUBLAYER)[2]
    return pl.pallas_call(
        functools.partial(_outproj_kernel, blocks_per_seq=blocks_per_seq),
        grid=(rows // tm,),
        in_specs=[pl.BlockSpec((tm, D_MODEL), lambda i: (i, 0)),
                  ys_spec,
                  pl.BlockSpec((tm, D_POOL), lambda i: (i, 0)),
                  gate_spec,
                  pl.BlockSpec((D_MODEL, D_MODEL), lambda i: (0, 0),
                               pipeline_mode=pl.Buffered(1))],
        out_specs=pl.BlockSpec((tm, D_MODEL), lambda i: (i, 0)),
        out_shape=jax.ShapeDtypeStruct((rows, D_MODEL), F32),
        compiler_params=_cparams(1),
        name="outproj",
    )(x, ys, yp, mod, w_out)


def _outproj_sample_kernel(x_ref, ys_ref, yp_ref, gate_ref, w_ref, o_ref):
    mix = (_dot(ys_ref[...].astype(BF16), w_ref[0:D_SSM, :])
           + _dot(yp_ref[...].astype(BF16), w_ref[D_SSM:, :]))
    o_ref[...] = x_ref[...] + gate_ref[...] * mix


def _outproj_sample(x, ys, yp, mod, w_out):
    rows = x.shape[0]
    n_col = D_MODEL // SAMPLE_TN
    gate_col = (3 * MIX_SUBLAYER + 2) * n_col
    return pl.pallas_call(
        _outproj_sample_kernel,
        grid=(n_col,),
        in_specs=[pl.BlockSpec((rows, SAMPLE_TN), lambda j: (0, j)),
                  pl.BlockSpec((rows, D_SSM), lambda j: (0, 0)),
                  pl.BlockSpec((rows, D_POOL), lambda j: (0, 0)),
                  pl.BlockSpec((rows, SAMPLE_TN), lambda j: (0, gate_col + j)),
                  pl.BlockSpec((D_MODEL, SAMPLE_TN), lambda j: (0, j))],
        out_specs=pl.BlockSpec((rows, SAMPLE_TN), lambda j: (0, j)),
        out_shape=jax.ShapeDtypeStruct((rows, D_MODEL), F32),
        compiler_params=_cparams(1),
        name="outproj_sample",
    )(x, ys, yp, mod, w_out)


def kernel(x_prompt, x_sample, state_ssm_re, state_ssm_im, state_pool, c_prompt, c_sample, ada_w, ada_b, ffn1_norm, ffn1_w_gate, ffn1_w_up, ffn1_w_down, mix_norm, w_in, ssm_lambda_re, ssm_lambda_im, ssm_log_dt, ssm_b_re, ssm_b_im, ssm_c_re, ssm_c_im, ssm_d, ssm_glu_w, ssm_glu_b, pool_w, pool_b, pool_scale, w_out, ffn2_norm, ffn2_w_gate, ffn2_w_up, ffn2_w_down, final_norm):
    n_p, seq, _ = x_prompt.shape
    n_s = x_sample.shape[0]
    assert (n_p, n_s) == (N_PROMPT, N_SAMPLE) and n_p <= SUBLANES
    G, P = N_SSM_GROUPS, SSM_STATE

    c_all = jnp.concatenate([c_sample, c_prompt, jnp.zeros((SUBLANES - n_p, D_MODEL), F32)], axis=0)
    mod, ops = _adaln_s5_prep(c_all, ada_w[0], ada_b.reshape(1, N_MOD_COLS), ssm_lambda_re[0],
                              ssm_lambda_im[0], ssm_log_dt[0], ssm_b_re[0], ssm_b_im[0],
                              ssm_c_re[0], ssm_c_im[0], ssm_glu_w[0])
    pw = pool_w[0].astype(BF16)

    xs = x_sample.reshape(n_s, D_MODEL)
    xp = x_prompt.reshape(n_p * seq, D_MODEL)
    blocks_per_seq = seq // FFN_TM
    sample_rows = _RowGroup(n_s, n_s, per_row=True, first_block=0, resident=True)
    head_rows = _RowGroup(FFN_TM, FFN_ROW_CHUNK, per_row=False, first_block=0, resident=True)
    tail_rows = _RowGroup(FFN_TM, FFN_ROW_CHUNK, per_row=False, first_block=1, resident=False)

    def ffn(xs, xp, norm_g, sub, wg, wu, wd, last, side_casts=()):
        ys, yp0, bg, bu, bd, *cast = _ffn([xs, xp], (sample_rows, head_rows), norm_g, mod, sub,
                                          wg, wu, wd, final_norm, n_blocks=1, tf=FFN_HEAD_TF,
                                          final_norm=last, blocks_per_seq=blocks_per_seq,
                                          side_casts=side_casts)
        (yp,) = _ffn([xp], (tail_rows,), norm_g, mod, sub, bg, bu, bd, final_norm,
                     n_blocks=n_p * blocks_per_seq - 1, tf=FFN_TF, final_norm=last,
                     blocks_per_seq=blocks_per_seq, out_rows=xp.shape[0], first_rows=yp0)
        return (ys, yp, *cast)

    xs, xp, win, wout = ffn(xs, xp, ffn1_norm[0], FFN1_SUBLAYER, ffn1_w_gate[0], ffn1_w_up[0],
                            ffn1_w_down[0], False, side_casts=(w_in[0], w_out[0]))

    proj_s = _proj_sample(xs, mix_norm[0], mod, win)
    ys_s, sre_s, sim_s = _s5_sample(proj_s, state_ssm_re[0].reshape(n_s, G * P),
                                    state_ssm_im[0].reshape(n_s, G * P), ops, ssm_d[0],
                                    ssm_glu_b[0])
    yp_s, pool_s_t = _pool_sample(jnp.swapaxes(state_pool[0], 0, 1), proj_s, pw, pool_b[0],
                                  pool_scale[0])
    pool_s = jnp.swapaxes(pool_s_t, 0, 1)
    xs = _outproj_sample(xs, ys_s, yp_s, mod, wout)

    u4, yp_p, v_tail = _proj(xp, mix_norm[0], mod, win, pw, pool_b[0], pool_scale[0],
                             tm=PROJ_TM, n_seq=n_p)
    ys_p, sre_p, sim_p = _s5_prompt(u4.reshape(n_p * seq, D_SSM), ops, ssm_d[0], ssm_glu_b[0],
                                    n_seq=n_p)
    xp = _outproj(xp, ys_p.reshape(seq // T_CHUNK, n_p, T_CHUNK, D_SSM), yp_p, mod, wout,
                  tm=OUTPROJ_TM, n_seq=n_p)

    y_sample, y_prompt = ffn(xs, xp, ffn2_norm[0], FFN2_SUBLAYER, ffn2_w_gate[0], ffn2_w_up[0],
                             ffn2_w_down[0], True)

    pool_p = v_tail[:, POOL_TAIL - POOL_HIST:, :][None]
    return (y_prompt.reshape(n_p, seq, D_MODEL), y_sample.reshape(n_s, 1, D_MODEL),
            sre_p.reshape(1, n_p, G, P), sim_p.reshape(1, n_p, G, P), pool_p,
            sre_s.reshape(1, n_s, G, P), sim_s.reshape(1, n_s, G, P), pool_s[None])
```

```python
import functools
from typing import NamedTuple

import jax
import jax.numpy as jnp
from jax import lax
from jax.experimental import pallas as pl
from jax.experimental.pallas import tpu as pltpu

F32 = jnp.float32
BF16 = jnp.bfloat16

D_MODEL = 2048
D_FF = 5632
D_SSM = 1024
D_POOL = 1024
SSM_GROUP = 16
SSM_STATE = 64
N_SSM_GROUPS = 64
POOL_WINDOWS = (2, 4, 8, 16)
POOL_GROUP = 256
POOL_HIST = 15
N_MOD_COLS = 9 * D_MODEL
N_PROMPT = 4
N_SAMPLE = 128
EPS = 1e-6
FFN_RES = 0.5

LANES = 128
SUBLANES = 8
MXU_TILE = 256
VMEM_LIMIT = 60 * 1024 * 1024

PROMPT_MOD_BLOCK = N_SAMPLE // SUBLANES
NORM_ROWS = 32
FFN1_SUBLAYER, MIX_SUBLAYER, FFN2_SUBLAYER = 0, 1, 2

FFN_TM = 1024
FFN_ROW_CHUNK = 512
FFN_TF = 512
FFN_HEAD_TF = 256
PROJ_TM = 1024
OUTPROJ_TM = 512

OCT = LANES // SSM_GROUP
N_OCT = N_SSM_GROUPS // OCT
T_CHUNK = 8
OCT_STATE = OCT * SSM_STATE
CHUNK_K = T_CHUNK * LANES
TOK_PER_TILE = MXU_TILE // LANES
N_TOEP_TILES = CHUNK_K // MXU_TILE


def _toep_tile_row(nt):
    return MXU_TILE * (nt * (nt + 1) // 2)


TOEP_ROWS = _toep_tile_row(N_TOEP_TILES)


def _cparams(n_axes):
    return pltpu.CompilerParams(dimension_semantics=("arbitrary",) * n_axes,
                                vmem_limit_bytes=VMEM_LIMIT)


def _dot(a, b):
    return jnp.dot(a, b, preferred_element_type=F32)


ADALN_TN = 2048


def _adaln_kernel(c_ref, w_ref, b_ref, o_ref):
    c = c_ref[...]
    sc = (c * jax.nn.sigmoid(c)).astype(BF16)
    o_ref[...] = _dot(sc, w_ref[...].astype(BF16)) + b_ref[...]


def _norm_mod(x, g, shift, scale):
    ms = jnp.mean(x * x, axis=-1, keepdims=True)
    y = x * lax.rsqrt(ms + EPS) * g
    return y * (1.0 + scale) + shift


def _mod_specs(sub):
    return [pl.BlockSpec((SUBLANES, D_MODEL), lambda i, col=3 * sub + m: (PROMPT_MOD_BLOCK, col))
            for m in range(3)]


def _mod_rows(ref, blocks_per_seq):
    return ref[pl.ds(pl.program_id(0) // blocks_per_seq, 1), :]


class _RowGroup(NamedTuple):
    rows: int
    row_chunk: int
    per_row: bool
    first_block: int
    resident: bool


def _ffn_step(s, n_j, has_copy):
    t = jnp.maximum(s - 1, 0) if has_copy else s
    return t // n_j, t % n_j


def _ffn_kernel(*refs, groups, blocks_per_seq, n_j, final_norm, emit_bf16, has_copy):
    n = len(groups)
    g_ref, wg_ref, wu_ref, wd_ref, fg_ref = refs[4 * n:4 * n + 5]
    outs = refs[4 * n + 5 + int(has_copy):-1]
    h_ref = refs[-1]
    step = pl.program_id(0)
    i, j = _ffn_step(step, n_j, has_copy)
    last_j = n_j - 1
    active = step >= int(has_copy)

    if has_copy:
        @pl.when(step == 0)
        def _():
            outs[0][...] = refs[4 * n + 5][...]

    if emit_bf16:
        bf16_refs = outs[n:n + 3]
        for w_ref, wo_ref in zip((wg_ref, wu_ref, wd_ref), bf16_refs):
            wo_ref[...] = w_ref[...].astype(BF16)
        wg_ref, wu_ref, wd_ref = bf16_refs

    chunks, h_base = [], 0
    for k, grp in enumerate(groups):
        chunks += [(k, r, h_base + r) for r in range(0, grp.rows, grp.row_chunk)]
        h_base += grp.rows

    def mod_rows(k, ref, rows):
        if groups[k].per_row:
            return ref[rows, :]
        seq = (i + groups[k].first_block) // blocks_per_seq
        return ref[pl.ds(seq, 1), :]

    def norm_chunk(c):
        k, r, hr = chunks[c]
        x_ref, shift_ref, scale_ref, _ = refs[4 * k:4 * k + 4]
        for s in range(0, groups[k].row_chunk, NORM_ROWS):
            sr = pl.ds(r + s, NORM_ROWS)
            h = _norm_mod(x_ref[sr, :], g_ref[...], mod_rows(k, shift_ref, sr),
                          mod_rows(k, scale_ref, sr))
            h_ref[pl.ds(hr + s, NORM_ROWS), :] = h.astype(BF16)

    def chunk(c, first, last):
        k, r, hr = chunks[c]
        size = groups[k].row_chunk
        x_ref, _, _, gate_ref = refs[4 * k:4 * k + 4]
        o_ref = outs[k]
        rows = pl.ds(r, size)
        if first and c + 1 < len(chunks):
            norm_chunk(c + 1)
        h = h_ref[pl.ds(hr, size), :]
        g = _dot(h, wg_ref[...])
        u = _dot(h, wu_ref[...])
        a = (g * jax.nn.sigmoid(g) * u).astype(BF16)
        d = _dot(a, wd_ref[...])
        acc = d if first else o_ref[rows, :] + d
        if not last:
            o_ref[rows, :] = acc
            return
        o_ref[rows, :] = x_ref[rows, :] + FFN_RES * mod_rows(k, gate_ref, rows) * acc
        if final_norm:
            for s in range(0, size, NORM_ROWS):
                sr = pl.ds(r + s, NORM_ROWS)
                y = o_ref[sr, :]
                ms = jnp.mean(y * y, axis=-1, keepdims=True)
                o_ref[sr, :] = y * lax.rsqrt(ms + EPS) * fg_ref[...]

    def run(first, last):
        if first:
            norm_chunk(0)
        for c in range(len(chunks)):
            chunk(c, first, last)

    pl.when(jnp.logical_and(active, j == 0))(lambda: run(True, False))
    pl.when(jnp.logical_and(j > 0, j < last_j))(lambda: run(False, False))
    pl.when(j == last_j)(lambda: run(False, True))


def _ffn(xs, groups, norm_g, mod, sub, wg, wu, wd, final_g, *, n_blocks, tf, final_norm,
         blocks_per_seq, out_rows=None, first_rows=None):
    emit_bf16 = wg.dtype == F32
    has_copy = first_rows is not None
    assert not emit_bf16 or n_blocks == 1
    assert not has_copy or len(groups) == 1
    n_j = D_FF // tf
    step = functools.partial(_ffn_step, n_j=n_j, has_copy=has_copy)

    def row_spec(grp, col=0, out=False):
        mode = dict(pipeline_mode=pl.Buffered(1)) if grp.resident else {}
        first = 0 if out and not has_copy else grp.first_block

        def index(s):
            block = step(s)[0] + first
            return (jnp.where(s == 0, 0, block) if out and has_copy else block), col
        return pl.BlockSpec((grp.rows, D_MODEL), index, **mode)

    in_specs, operands = [], []
    for x, grp in zip(xs, groups):
        in_specs.append(row_spec(grp))
        operands.append(x)
        for m in range(3):
            col = 3 * sub + m
            if grp.per_row:
                in_specs.append(row_spec(grp, col))
            else:
                in_specs.append(pl.BlockSpec((SUBLANES, D_MODEL),
                                             lambda s, col=col: (PROMPT_MOD_BLOCK, col)))
            operands.append(mod)
    w_specs = [pl.BlockSpec((D_MODEL, tf), lambda s: (0, step(s)[1])),
               pl.BlockSpec((D_MODEL, tf), lambda s: (0, step(s)[1])),
               pl.BlockSpec((tf, D_MODEL), lambda s: (step(s)[1], 0))]
    vec_spec = pl.BlockSpec((1, D_MODEL), lambda s: (0, 0))
    in_specs += [vec_spec, *w_specs, vec_spec]
    operands += [norm_g.reshape(1, D_MODEL), wg, wu, wd, final_g.reshape(1, D_MODEL)]
    if has_copy:
        in_specs.append(pl.BlockSpec(first_rows.shape, lambda s: (0, 0),
                                     pipeline_mode=pl.Buffered(1)))
        operands.append(first_rows)

    out_specs = [row_spec(grp, out=True) for grp in groups]
    out_shape = [jax.ShapeDtypeStruct((out_rows or grp.rows, D_MODEL), F32) for grp in groups]
    if emit_bf16:
        out_specs += w_specs
        out_shape += [jax.ShapeDtypeStruct(w.shape, BF16) for w in (wg, wu, wd)]
    kern = functools.partial(_ffn_kernel, groups=groups, blocks_per_seq=blocks_per_seq, n_j=n_j,
                             final_norm=final_norm, emit_bf16=emit_bf16, has_copy=has_copy)
    return pl.pallas_call(
        kern,
        grid=(int(has_copy) + n_blocks * n_j,),
        in_specs=in_specs,
        out_specs=out_specs,
        out_shape=out_shape,
        scratch_shapes=[pltpu.VMEM((sum(grp.rows for grp in groups), D_MODEL), BF16)],
        compiler_params=_cparams(1),
        name="ffn",
    )(*operands)


POOL_TAIL = 16


def _proj_kernel(x_ref, g_ref, shift_ref, scale_ref, w_ref, pw_ref, pb_ref, ps_ref,
                 u_ref, y_ref, tail_out_ref, tail_ref, h_ref, *, blocks_per_seq):
    tm = x_ref.shape[0]
    half = tm // 2
    blk = pl.program_id(0) % blocks_per_seq
    sh = _mod_rows(shift_ref, blocks_per_seq)
    sc = _mod_rows(scale_ref, blocks_per_seq)

    @pl.when(blk == 0)
    def _():
        tail_ref[...] = jnp.zeros(tail_ref.shape, F32)

    def norm_half(r):
        for s in range(0, half, NORM_ROWS):
            rows = pl.ds(r * half + s, NORM_ROWS)
            h_ref[rows, :] = _norm_mod(x_ref[rows, :], g_ref[...], sh, sc).astype(BF16)

    norm_half(0)
    norm_half(1)
    pv = jnp.concatenate([_dot(h_ref[pl.ds(r * half, half), :], w_ref[:, D_SSM:])
                          for r in range(2)], axis=0)
    for r in range(2):
        pu = _dot(h_ref[pl.ds(r * half, half), :], w_ref[:, 0:D_SSM])
        u_ref[pl.ds(r * half // T_CHUNK, half // T_CHUNK), :, :] = pu.reshape(
            half // T_CHUNK, T_CHUNK, D_SSM)

    row_ext = lax.broadcasted_iota(jnp.int32, (POOL_TAIL + tm, POOL_GROUP), 0)
    pos = blk * tm + lax.broadcasted_iota(jnp.int32, (tm, POOL_GROUP), 0)
    for gi, w in enumerate(POOL_WINDOWS):
        lanes = slice(gi * POOL_GROUP, (gi + 1) * POOL_GROUP)
        v = pv[:, lanes]
        s = jnp.concatenate([tail_ref[:, lanes], v], axis=0)
        k = 1
        while k < w:
            s = s + jnp.where(row_ext >= k, pltpu.roll(s, k, axis=0), 0.0)
            k *= 2
        cnt = jnp.clip(pos + 1, 1, w).astype(F32)
        z = s[POOL_TAIL:, :] / cnt - v
        y_ref[:, lanes] = _pool_linear(z, gi, pw_ref, pb_ref, ps_ref).astype(y_ref.dtype)
    tail = pv[tm - POOL_TAIL:, :]
    tail_ref[...] = tail
    tail_out_ref[...] = tail


def _proj(x, norm_g, mod, w_in, pool_w, pool_b, pool_scale, *, tm, n_seq):
    rows = x.shape[0]
    seq_len = rows // n_seq
    blocks_per_seq = seq_len // tm
    u_shape = (seq_len // T_CHUNK, n_seq, T_CHUNK, D_SSM)
    u_spec = pl.BlockSpec((tm // T_CHUNK, None, T_CHUNK, D_SSM),
                          lambda i: (i % blocks_per_seq, i // blocks_per_seq, 0, 0))
    shift_spec, scale_spec, _ = _mod_specs(MIX_SUBLAYER)
    vec_spec = pl.BlockSpec((1, D_POOL), lambda i: (0, 0))
    return pl.pallas_call(
        functools.partial(_proj_kernel, blocks_per_seq=blocks_per_seq),
        grid=(rows // tm,),
        in_specs=[pl.BlockSpec((tm, D_MODEL), lambda i: (i, 0)),
                  pl.BlockSpec((1, D_MODEL), lambda i: (0, 0)),
                  shift_spec, scale_spec,
                  pl.BlockSpec((D_MODEL, D_MODEL), lambda i: (0, 0),
                               pipeline_mode=pl.Buffered(1)),
                  pl.BlockSpec(pool_w.shape, lambda i: (0, 0, 0)), vec_spec, vec_spec],
        out_specs=[u_spec, pl.BlockSpec((tm, D_POOL), lambda i: (i, 0)),
                   pl.BlockSpec((None, POOL_TAIL, D_POOL), lambda i: (i // blocks_per_seq, 0, 0))],
        out_shape=[jax.ShapeDtypeStruct(u_shape, F32),
                   jax.ShapeDtypeStruct((rows, D_POOL), BF16),
                   jax.ShapeDtypeStruct((n_seq, POOL_TAIL, D_POOL), F32)],
        scratch_shapes=[pltpu.VMEM((POOL_TAIL, D_POOL), F32), pltpu.VMEM((tm, D_MODEL), BF16)],
        compiler_params=_cparams(1),
        name="proj",
    )(x, norm_g.reshape(1, D_MODEL), mod, mod, w_in, pool_w, pool_b.reshape(1, D_POOL),
      pool_scale.reshape(1, D_POOL))


SAMPLE_TN = 512


def _proj_sample_kernel(x_ref, g_ref, shift_ref, scale_ref, w_ref, p_ref, wo_ref, h_ref):
    @pl.when(pl.program_id(0) == 0)
    def _():
        h = _norm_mod(x_ref[...], g_ref[...], shift_ref[...], scale_ref[...])
        h_ref[...] = h.astype(BF16)
    wo_ref[...] = w_ref[...].astype(BF16)
    p_ref[...] = _dot(h_ref[...], wo_ref[...])


def _proj_sample(x, norm_g, mod, w_in):
    rows = x.shape[0]
    shift_col, scale_col = 3 * MIX_SUBLAYER, 3 * MIX_SUBLAYER + 1
    return pl.pallas_call(
        _proj_sample_kernel,
        grid=(D_MODEL // SAMPLE_TN,),
        in_specs=[pl.BlockSpec((rows, D_MODEL), lambda j: (0, 0)),
                  pl.BlockSpec((1, D_MODEL), lambda j: (0, 0)),
                  pl.BlockSpec((rows, D_MODEL), lambda j: (0, shift_col)),
                  pl.BlockSpec((rows, D_MODEL), lambda j: (0, scale_col)),
                  pl.BlockSpec((D_MODEL, SAMPLE_TN), lambda j: (0, j))],
        out_specs=[pl.BlockSpec((rows, SAMPLE_TN), lambda j: (0, j)),
                   pl.BlockSpec((D_MODEL, SAMPLE_TN), lambda j: (0, j))],
        out_shape=[jax.ShapeDtypeStruct((rows, D_MODEL), F32),
                   jax.ShapeDtypeStruct((D_MODEL, D_MODEL), BF16)],
        scratch_shapes=[pltpu.VMEM((rows, D_MODEL), BF16)],
        compiler_params=_cparams(1),
        name="proj_sample",
    )(x, norm_g.reshape(1, D_MODEL), mod, mod, w_in)


def _discretise(lam_re, lam_im, log_dt):
    lr = jnp.minimum(lam_re, -1e-4)
    li = lam_im
    dt = jnp.exp(log_dt)
    mag = jnp.exp(lr * dt)
    ang = li * dt
    a_re = mag * jnp.cos(ang)
    a_im = mag * jnp.sin(ang)
    den = lr * lr + li * li
    num_re = a_re - 1.0
    f_re = (num_re * lr + a_im * li) / den
    f_im = (a_im * lr - num_re * li) / den
    return a_re, a_im, f_re, f_im


def _s5_prep_kernel(lam_re1, lam_im1, ldt1, c_re1, c_im1,
                    lam_re2, lam_im2, ldt2, b_re2, b_im2, gluw_t,
                    toep_ref, wz_ref, m_ref, apow_ref, gluw_ref):
    g1 = lax.broadcasted_iota(jnp.int32, (SSM_STATE, LANES), 1) // SSM_GROUP
    g2 = lax.broadcasted_iota(jnp.int32, (SSM_GROUP, OCT_STATE), 1) // SSM_STATE

    def expand1(x):
        return jnp.concatenate([jnp.where(g1 == g, x, 0.0) for g in range(OCT)], axis=0)

    def expand2(x):
        return jnp.concatenate([jnp.where(g2 == g, x, 0.0) for g in range(OCT)], axis=0)

    def split(x):
        hi = x.astype(BF16)
        return hi, (x - hi.astype(F32)).astype(BF16)

    a1_re, a1_im, _, _ = _discretise(lam_re1[...], lam_im1[...], ldt1[...])
    a2_re, a2_im, f_re, f_im = _discretise(lam_re2[...], lam_im2[...], ldt2[...])
    br, bi = b_re2[...], b_im2[...]
    bbar_re = f_re * br - f_im * bi
    bbar_im = f_re * bi + f_im * br
    cr, ci = c_re1[...], c_im1[...]
    bre_hi, bre_lo = split(expand2(bbar_re))
    bim_hi, bim_lo = split(expand2(bbar_im))

    def dot3(x_hi, x_lo, y_hi, y_lo):
        return _dot(x_hi, y_hi) + (_dot(x_hi, y_lo) + _dot(x_lo, y_hi))

    p1_re, p1_im = jnp.ones_like(a1_re), jnp.zeros_like(a1_im)
    p2_re, p2_im = jnp.ones_like(a2_re), jnp.zeros_like(a2_im)
    kk = []
    zero_blk = jnp.zeros((LANES, LANES), F32)
    for j in range(T_CHUNK + 1):
        care_hi, care_lo = split(expand1(cr * p1_re - ci * p1_im))
        caim_hi, caim_lo = split(expand1(cr * p1_im + ci * p1_re))
        m_ref[0:OCT_STATE, j * LANES:(j + 1) * LANES] = care_hi
        m_ref[OCT_STATE:2 * OCT_STATE, j * LANES:(j + 1) * LANES] = -caim_hi
        if j < T_CHUNK:
            t = T_CHUNK - 1 - j
            ba_re = expand2(bbar_re * p2_re - bbar_im * p2_im)
            ba_im = expand2(bbar_re * p2_im + bbar_im * p2_re)
            wz_ref[t * LANES:(t + 1) * LANES, 0:OCT_STATE] = ba_re.astype(BF16)
            wz_ref[t * LANES:(t + 1) * LANES, OCT_STATE:2 * OCT_STATE] = ba_im.astype(BF16)
            kk.append(dot3(bre_hi, bre_lo, care_hi, care_lo)
                      - dot3(bim_hi, bim_lo, caim_hi, caim_lo))
        if j == 1:
            apow_ref[0:1, :] = p2_re
            apow_ref[1:2, :] = p2_im
        if j == T_CHUNK:
            apow_ref[2:3, :] = p2_re
            apow_ref[3:4, :] = p2_im
        p1_re, p1_im = p1_re * a1_re - p1_im * a1_im, p1_re * a1_im + p1_im * a1_re
        p2_re, p2_im = p2_re * a2_re - p2_im * a2_im, p2_re * a2_im + p2_im * a2_re

    for t2 in range(T_CHUNK):
        nt, col = divmod(t2, TOK_PER_TILE)
        for t in range((nt + 1) * TOK_PER_TILE):
            blk = kk[t2 - t] if t2 >= t else zero_blk
            row = _toep_tile_row(nt) + t * LANES
            toep_ref[row:row + LANES, col * LANES:(col + 1) * LANES] = blk.astype(BF16)

    rg = lax.broadcasted_iota(jnp.int32, (LANES, LANES), 0) // SSM_GROUP
    lg = lax.broadcasted_iota(jnp.int32, (LANES, LANES), 1) // SSM_GROUP
    gw = jnp.where(rg == lg, gluw_t[...], 0.0).astype(BF16)
    gz = jnp.zeros((LANES, LANES), BF16)
    for a in range(TOK_PER_TILE):
        for b in range(TOK_PER_TILE):
            gluw_ref[a * LANES:(a + 1) * LANES, b * LANES:(b + 1) * LANES] = gw if a == b else gz


def _adaln_s5_prep_kernel(c_ref, w_ref, b_ref, *refs):
    n_prep_in = len(refs) - 6
    _adaln_kernel(c_ref, w_ref, b_ref, refs[n_prep_in])

    @pl.when(pl.program_id(0) < N_OCT)
    def _():
        _s5_prep_kernel(*refs[:n_prep_in], *refs[n_prep_in + 1:])


def _adaln_s5_prep(c_all, ada_w, ada_b, lam_re, lam_im, log_dt, b_re, b_im, c_re, c_im, glu_w):
    G, P, H = N_SSM_GROUPS, SSM_STATE, SSM_GROUP
    rows, n_cols = c_all.shape[0], ada_w.shape[1]
    n_steps = n_cols // ADALN_TN
    assert n_steps >= N_OCT
    row = lambda a: a.reshape(N_OCT, 1, OCT_STATE)
    ldt = jnp.broadcast_to(log_dt[:, None], (G, P))

    def col(a_gp):
        a = jnp.swapaxes(a_gp.reshape(N_OCT, OCT, P), 1, 2)
        return jnp.broadcast_to(a[..., None], (N_OCT, P, OCT, H)).reshape(N_OCT, P, LANES)

    lay1 = lambda a_ghp: jnp.swapaxes(a_ghp.reshape(N_OCT, LANES, P), 1, 2)
    lay2 = lambda a_gph: jnp.swapaxes(a_gph.reshape(N_OCT, OCT_STATE, H), 1, 2)
    c_re1, c_im1, b_re2, b_im2 = lay1(c_re), lay1(c_im), lay2(b_re), lay2(b_im)
    gw = glu_w.reshape(N_OCT, OCT, H, 1, H)
    gluw_t = jnp.broadcast_to(gw, (N_OCT, OCT, H, OCT, H)).reshape(N_OCT, LANES, LANES)

    def spec(shape):
        return pl.BlockSpec((None,) + shape,
                            lambda j: (jnp.minimum(j, N_OCT - 1),) + (0,) * len(shape))

    m_cols = (T_CHUNK + 1) * LANES
    mod, *ops = pl.pallas_call(
        _adaln_s5_prep_kernel,
        grid=(n_steps,),
        in_specs=[pl.BlockSpec((rows, D_MODEL), lambda j: (0, 0)),
                  pl.BlockSpec((D_MODEL, ADALN_TN), lambda j: (0, j)),
                  pl.BlockSpec((1, ADALN_TN), lambda j: (0, j))]
                 + [spec((SSM_STATE, LANES))] * 5
                 + [spec((1, OCT_STATE))] * 3 + [spec((SSM_GROUP, OCT_STATE))] * 2
                 + [spec((LANES, LANES))],
        out_specs=[pl.BlockSpec((rows, ADALN_TN), lambda j: (0, j)),
                   spec((TOEP_ROWS, MXU_TILE)), spec((CHUNK_K, 2 * OCT_STATE)),
                   spec((2 * OCT_STATE, m_cols)), spec((4, OCT_STATE)),
                   spec((MXU_TILE, MXU_TILE))],
        out_shape=[jax.ShapeDtypeStruct((rows, n_cols), F32),
                   jax.ShapeDtypeStruct((N_OCT, TOEP_ROWS, MXU_TILE), BF16),
                   jax.ShapeDtypeStruct((N_OCT, CHUNK_K, 2 * OCT_STATE), BF16),
                   jax.ShapeDtypeStruct((N_OCT, 2 * OCT_STATE, m_cols), BF16),
                   jax.ShapeDtypeStruct((N_OCT, 4, OCT_STATE), F32),
                   jax.ShapeDtypeStruct((N_OCT, MXU_TILE, MXU_TILE), BF16)],
        compiler_params=_cparams(1),
        name="adaln_s5_prep",
    )(c_all, ada_w, ada_b, col(lam_re), col(lam_im), col(ldt), c_re1, c_im1,
      row(lam_re), row(lam_im), row(ldt), b_re2, b_im2, gluw_t)
    return mod, tuple(ops)


GELU_C0 = 0.7978845608028654
GELU_C1 = GELU_C0 * 0.044715


def _glu_out(y, gluw, glub):
    gy = y * (0.5 + 0.5 * jnp.tanh(y * (GELU_C0 + GELU_C1 * (y * y))))
    z = _dot(gy.astype(BF16), gluw) + glub
    return gy * (0.5 + 0.5 * jnp.tanh(0.5 * z))


def _s5_prompt_kernel(x_ref, toep_ref, wz_ref, m_ref, apow_ref, d_ref, gluw_ref, glub_ref,
                      y_ref, sre_ref, sim_ref, xr_ref, z_ref, yv_ref, *, n_seq, row_chunk):
    n_rows = x_ref.shape[0] // T_CHUNK
    n_blk = n_rows // row_chunk

    def tokens(b, t):
        return pl.ds(b * (row_chunk * T_CHUNK) + t, row_chunk, stride=T_CHUNK)

    for b in range(n_blk):
        rows = pl.ds(b * row_chunk, row_chunk)
        for t in range(T_CHUNK):
            xr_ref[rows, t * LANES:(t + 1) * LANES] = x_ref[tokens(b, t), :].astype(BF16)
        xr = xr_ref[rows, :]
        z_ref[rows, :] = _dot(xr, wz_ref[...])
        for nt in range(N_TOEP_TILES):
            k_end = (nt + 1) * MXU_TILE
            tile = toep_ref[_toep_tile_row(nt):_toep_tile_row(nt) + k_end, :]
            yv_ref[rows, nt * MXU_TILE:k_end] = _dot(xr[:, 0:k_end], tile)

    are, aim = apow_ref[2:3, :], apow_ref[3:4, :]
    lo = lax.broadcasted_iota(jnp.int32, (SUBLANES, OCT_STATE), 0) < n_seq

    def step(k, carry):
        pre, pim = carry
        rows = pl.ds(pl.multiple_of(k * SUBLANES, SUBLANES), SUBLANES)
        zre = z_ref[rows, 0:OCT_STATE]
        zim = z_ref[rows, OCT_STATE:2 * OCT_STATE]
        w1re = are * pre - aim * pim + zre
        w1im = are * pim + aim * pre + zim
        r1re = pltpu.roll(w1re, n_seq, axis=0)
        r1im = pltpu.roll(w1im, n_seq, axis=0)
        w2re = are * r1re - aim * r1im + zre
        w2im = are * r1im + aim * r1re + zim
        z_ref[rows, 0:OCT_STATE] = jnp.where(lo, pre, r1re)
        z_ref[rows, OCT_STATE:2 * OCT_STATE] = jnp.where(lo, pim, r1im)
        nre = jnp.where(lo, pltpu.roll(w2re, n_seq, axis=0), w2re)
        nim = jnp.where(lo, pltpu.roll(w2im, n_seq, axis=0), w2im)
        return nre, nim

    zeros = jnp.zeros((SUBLANES, OCT_STATE), F32)
    fre, fim = lax.fori_loop(0, n_rows // SUBLANES, step, (zeros, zeros))
    sre_ref[...] = fre[0:n_seq, :]
    sim_ref[...] = fim[0:n_seq, :]

    d2 = jnp.concatenate([d_ref[...]] * TOK_PER_TILE, axis=1)
    glub2 = jnp.concatenate([glub_ref[...]] * TOK_PER_TILE, axis=1)

    for b in range(n_blk):
        rows = pl.ds(b * row_chunk, row_chunk)
        y = yv_ref[rows, :] + _dot(z_ref[rows, :].astype(BF16), m_ref[:, LANES:])
        for t0 in range(0, T_CHUNK, TOK_PER_TILE):
            toks = [tokens(b, t0 + i) for i in range(TOK_PER_TILE)]
            u = jnp.concatenate([x_ref[tok, :] for tok in toks], axis=1)
            yt = y[:, t0 * LANES:(t0 + TOK_PER_TILE) * LANES] + d2 * u
            out = _glu_out(yt, gluw_ref[...], glub2)
            for i, tok in enumerate(toks):
                y_ref[tok, :] = out[:, i * LANES:(i + 1) * LANES]


def _s5_prompt(u_flat, ops, d_skip, glu_b, *, n_seq):
    toep, wz, m, apow, gluw = ops
    assert 2 * n_seq == SUBLANES
    rows = u_flat.shape[0]
    n_rows = rows // T_CHUNK
    oct_spec = lambda shape: pl.BlockSpec((None,) + shape, lambda o: (o, 0, 0))
    kern = functools.partial(_s5_prompt_kernel, n_seq=n_seq, row_chunk=512)
    return pl.pallas_call(
        kern,
        grid=(N_OCT,),
        in_specs=[pl.BlockSpec((rows, LANES), lambda o: (0, o)),
                  oct_spec(toep.shape[1:]), oct_spec(wz.shape[1:]), oct_spec(m.shape[1:]),
                  oct_spec(apow.shape[1:]), oct_spec((1, LANES)), oct_spec(gluw.shape[1:]),
                  oct_spec((1, LANES))],
        out_specs=[pl.BlockSpec((rows, LANES), lambda o: (0, o)),
                   pl.BlockSpec((n_seq, OCT_STATE), lambda o: (0, o)),
                   pl.BlockSpec((n_seq, OCT_STATE), lambda o: (0, o))],
        out_shape=[jax.ShapeDtypeStruct((rows, D_SSM), F32),
                   jax.ShapeDtypeStruct((n_seq, N_OCT * OCT_STATE), F32),
                   jax.ShapeDtypeStruct((n_seq, N_OCT * OCT_STATE), F32)],
        scratch_shapes=[pltpu.VMEM((n_rows, CHUNK_K), BF16),
                        pltpu.VMEM((n_rows, 2 * OCT_STATE), F32),
                        pltpu.VMEM((n_rows, CHUNK_K), F32)],
        compiler_params=_cparams(1),
        name="s5_prompt",
    )(u_flat, toep, wz, m, apow, d_skip.reshape(N_OCT, 1, LANES), gluw,
      glu_b.reshape(N_OCT, 1, LANES))


def _s5_sample_kernel(u_ref, s0re_ref, s0im_ref, wz_ref, m_ref, apow_ref, d_ref, gluw_ref,
                      glub_ref, y_ref, sre_ref, sim_ref):
    for o in range(N_OCT):
        ch = slice(o * LANES, (o + 1) * LANES)
        st = slice(o * OCT_STATE, (o + 1) * OCT_STATE)
        u = u_ref[:, ch]
        z = _dot(u.astype(BF16), wz_ref[o])
        are, aim = apow_ref[o, 0:1, :], apow_ref[o, 1:2, :]
        s0re, s0im = s0re_ref[:, st], s0im_ref[:, st]
        nre = are * s0re - aim * s0im + z[:, 0:OCT_STATE]
        nim = are * s0im + aim * s0re + z[:, OCT_STATE:2 * OCT_STATE]
        sre_ref[:, st] = nre
        sim_ref[:, st] = nim
        y = (_dot(nre.astype(BF16), m_ref[o, 0:OCT_STATE, :])
             + _dot(nim.astype(BF16), m_ref[o, OCT_STATE:2 * OCT_STATE, :])
             + d_ref[o] * u)
        y_ref[:, ch] = _glu_out(y, gluw_ref[o], glub_ref[o])


def _s5_sample(proj, s0_re, s0_im, ops, d_skip, glu_b):
    _, wz, m, apow, gluw = ops
    rows = proj.shape[0]
    whole = lambda shape: pl.BlockSpec(shape, lambda i: (0,) * len(shape))
    return pl.pallas_call(
        _s5_sample_kernel,
        grid=(1,),
        in_specs=[whole((rows, D_SSM)), whole(s0_re.shape), whole(s0_im.shape),
                  pl.BlockSpec((N_OCT, LANES, 2 * OCT_STATE), lambda i: (0, T_CHUNK - 1, 0)),
                  whole((N_OCT, 2 * OCT_STATE, LANES)),
                  whole(apow.shape), whole((N_OCT, 1, LANES)), whole((N_OCT, LANES, LANES)),
                  whole((N_OCT, 1, LANES))],
        out_specs=[whole((rows, D_SSM)), whole(s0_re.shape), whole(s0_im.shape)],
        out_shape=[jax.ShapeDtypeStruct((rows, D_SSM), F32),
                   jax.ShapeDtypeStruct(s0_re.shape, F32),
                   jax.ShapeDtypeStruct(s0_im.shape, F32)],
        compiler_params=_cparams(1),
        name="s5_sample",
    )(proj, s0_re, s0_im, wz, m, apow, d_skip.reshape(N_OCT, 1, LANES), gluw,
      glu_b.reshape(N_OCT, 1, LANES))


def _pool_linear(z, gi, pw_ref, pb_ref, ps_ref):
    lanes = slice(gi * POOL_GROUP, (gi + 1) * POOL_GROUP)
    return (_dot(z.astype(BF16), pw_ref[gi]) + pb_ref[:, lanes]) * ps_ref[:, lanes]


def _pool_sample_kernel(hist_ref, v_ref, pw_ref, pb_ref, ps_ref, y_ref, new_ref):
    new_ref[0:POOL_HIST - 1] = hist_ref[1:POOL_HIST]
    new_ref[POOL_HIST - 1] = v_ref[...]
    for gi, w in enumerate(POOL_WINDOWS):
        lanes = slice(gi * POOL_GROUP, (gi + 1) * POOL_GROUP)
        v = v_ref[:, lanes]
        s = v
        for r in range(POOL_HIST - (w - 1), POOL_HIST):
            s = s + hist_ref[r, :, lanes]
        y_ref[:, lanes] = _pool_linear(s / float(w) - v, gi, pw_ref, pb_ref, ps_ref)


def _pool_sample(hist_t, proj, pool_w, pool_b, pool_scale):
    rows = proj.shape[0]
    return pl.pallas_call(
        _pool_sample_kernel,
        grid=(1,),
        in_specs=[pl.BlockSpec(hist_t.shape, lambda i: (0, 0, 0)),
                  pl.BlockSpec((rows, D_POOL), lambda i: (0, D_SSM // D_POOL)),
                  pl.BlockSpec(pool_w.shape, lambda i: (0, 0, 0)),
                  pl.BlockSpec((1, D_POOL), lambda i: (0, 0)),
                  pl.BlockSpec((1, D_POOL), lambda i: (0, 0))],
        out_specs=[pl.BlockSpec((rows, D_POOL), lambda i: (0, 0)),
                   pl.BlockSpec(hist_t.shape, lambda i: (0, 0, 0))],
        out_shape=[jax.ShapeDtypeStruct((rows, D_POOL), F32),
                   jax.ShapeDtypeStruct(hist_t.shape, F32)],
        compiler_params=_cparams(1),
        name="pool_sample",
    )(hist_t, proj, pool_w, pool_b.reshape(1, D_POOL), pool_scale.reshape(1, D_POOL))


def _outproj_kernel(x_ref, ys_ref, yp_ref, gate_ref, w_ref, o_ref, *, blocks_per_seq):
    tm = x_ref.shape[0]
    ys = ys_ref[...].reshape(tm, D_SSM).astype(BF16)
    yp = yp_ref[...].astype(BF16)
    mix = _dot(ys, w_ref[0:D_SSM, :]) + _dot(yp, w_ref[D_SSM:, :])
    o_ref[...] = x_ref[...] + _mod_rows(gate_ref, blocks_per_seq) * mix


def _outproj(x, ys, yp, mod, w_out, *, tm, n_seq):
    rows = x.shape[0]
    blocks_per_seq = (rows // n_seq) // tm
    ys_spec = pl.BlockSpec((tm // T_CHUNK, None, T_CHUNK, D_SSM),
                           lambda i: (i % blocks_per_seq, i // blocks_per_seq, 0, 0))
    gate_spec = _mod_specs(MIX_SUBLAYER)[2]
    return pl.pallas_call(
        functools.partial(_outproj_kernel, blocks_per_seq=blocks_per_seq),
        grid=(rows // tm,),
        in_specs=[pl.BlockSpec((tm, D_MODEL), lambda i: (i, 0)),
                  ys_spec,
                  pl.BlockSpec((tm, D_POOL), lambda i: (i, 0)),
                  gate_spec,
                  pl.BlockSpec((D_MODEL, D_MODEL), lambda i: (0, 0),
                               pipeline_mode=pl.Buffered(1))],
        out_specs=pl.BlockSpec((tm, D_MODEL), lambda i: (i, 0)),
        out_shape=jax.ShapeDtypeStruct((rows, D_MODEL), F32),
        compiler_params=_cparams(1),
        name="outproj",
    )(x, ys, yp, mod, w_out)


def _outproj_sample_kernel(x_ref, ys_ref, yp_ref, gate_ref, w_ref, o_ref, wo_ref):
    wo_ref[...] = w_ref[...].astype(BF16)
    mix = (_dot(ys_ref[...].astype(BF16), wo_ref[0:D_SSM, :])
           + _dot(yp_ref[...].astype(BF16), wo_ref[D_SSM:, :]))
    o_ref[...] = x_ref[...] + gate_ref[...] * mix


def _outproj_sample(x, ys, yp, mod, w_out):
    rows = x.shape[0]
    n_col = D_MODEL // SAMPLE_TN
    gate_col = (3 * MIX_SUBLAYER + 2) * n_col
    return pl.pallas_call(
        _outproj_sample_kernel,
        grid=(n_col,),
        in_specs=[pl.BlockSpec((rows, SAMPLE_TN), lambda j: (0, j)),
                  pl.BlockSpec((rows, D_SSM), lambda j: (0, 0)),
                  pl.BlockSpec((rows, D_POOL), lambda j: (0, 0)),
                  pl.BlockSpec((rows, SAMPLE_TN), lambda j: (0, gate_col + j)),
                  pl.BlockSpec((D_MODEL, SAMPLE_TN), lambda j: (0, j))],
        out_specs=[pl.BlockSpec((rows, SAMPLE_TN), lambda j: (0, j)),
                   pl.BlockSpec((D_MODEL, SAMPLE_TN), lambda j: (0, j))],
        out_shape=[jax.ShapeDtypeStruct((rows, D_MODEL), F32),
                   jax.ShapeDtypeStruct((D_MODEL, D_MODEL), BF16)],
        compiler_params=_cparams(1),
        name="outproj_sample",
    )(x, ys, yp, mod, w_out)


def kernel(x_prompt, x_sample, state_ssm_re, state_ssm_im, state_pool, c_prompt, c_sample, ada_w, ada_b, ffn1_norm, ffn1_w_gate, ffn1_w_up, ffn1_w_down, mix_norm, w_in, ssm_lambda_re, ssm_lambda_im, ssm_log_dt, ssm_b_re, ssm_b_im, ssm_c_re, ssm_c_im, ssm_d, ssm_glu_w, ssm_glu_b, pool_w, pool_b, pool_scale, w_out, ffn2_norm, ffn2_w_gate, ffn2_w_up, ffn2_w_down, final_norm):
    n_p, seq, _ = x_prompt.shape
    n_s = x_sample.shape[0]
    assert (n_p, n_s) == (N_PROMPT, N_SAMPLE) and n_p <= SUBLANES
    G, P = N_SSM_GROUPS, SSM_STATE

    c_all = jnp.concatenate([c_sample, c_prompt, jnp.zeros((SUBLANES - n_p, D_MODEL), F32)], axis=0)
    mod, ops = _adaln_s5_prep(c_all, ada_w[0], ada_b.reshape(1, N_MOD_COLS), ssm_lambda_re[0],
                              ssm_lambda_im[0], ssm_log_dt[0], ssm_b_re[0], ssm_b_im[0],
                              ssm_c_re[0], ssm_c_im[0], ssm_glu_w[0])
    pw = pool_w[0].astype(BF16)

    xs = x_sample.reshape(n_s, D_MODEL)
    xp = x_prompt.reshape(n_p * seq, D_MODEL)
    blocks_per_seq = seq // FFN_TM
    sample_rows = _RowGroup(n_s, n_s, per_row=True, first_block=0, resident=True)
    head_rows = _RowGroup(FFN_TM, FFN_ROW_CHUNK, per_row=False, first_block=0, resident=True)
    tail_rows = _RowGroup(FFN_TM, FFN_ROW_CHUNK, per_row=False, first_block=1, resident=False)

    def ffn(xs, xp, norm_g, sub, wg, wu, wd, last):
        ys, yp0, bg, bu, bd = _ffn([xs, xp], (sample_rows, head_rows), norm_g, mod, sub, wg, wu, wd,
                                   final_norm, n_blocks=1, tf=FFN_HEAD_TF, final_norm=last,
                                   blocks_per_seq=blocks_per_seq)
        (yp,) = _ffn([xp], (tail_rows,), norm_g, mod, sub, bg, bu, bd, final_norm,
                     n_blocks=n_p * blocks_per_seq - 1, tf=FFN_TF, final_norm=last,
                     blocks_per_seq=blocks_per_seq, out_rows=xp.shape[0], first_rows=yp0)
        return ys, yp

    xs, xp = ffn(xs, xp, ffn1_norm[0], FFN1_SUBLAYER, ffn1_w_gate[0], ffn1_w_up[0],
                 ffn1_w_down[0], False)

    proj_s, win = _proj_sample(xs, mix_norm[0], mod, w_in[0])
    ys_s, sre_s, sim_s = _s5_sample(proj_s, state_ssm_re[0].reshape(n_s, G * P),
                                    state_ssm_im[0].reshape(n_s, G * P), ops, ssm_d[0],
                                    ssm_glu_b[0])
    yp_s, pool_s_t = _pool_sample(jnp.swapaxes(state_pool[0], 0, 1), proj_s, pw, pool_b[0],
                                  pool_scale[0])
    pool_s = jnp.swapaxes(pool_s_t, 0, 1)
    xs, wout = _outproj_sample(xs, ys_s, yp_s, mod, w_out[0])

    u4, yp_p, v_tail = _proj(xp, mix_norm[0], mod, win, pw, pool_b[0], pool_scale[0],
                             tm=PROJ_TM, n_seq=n_p)
    ys_p, sre_p, sim_p = _s5_prompt(u4.reshape(n_p * seq, D_SSM), ops, ssm_d[0], ssm_glu_b[0],
                                    n_seq=n_p)
    xp = _outproj(xp, ys_p.reshape(seq // T_CHUNK, n_p, T_CHUNK, D_SSM), yp_p, mod, wout,
                  tm=OUTPROJ_TM, n_seq=n_p)

    y_sample, y_prompt = ffn(xs, xp, ffn2_norm[0], FFN2_SUBLAYER, ffn2_w_gate[0], ffn2_w_up[0],
                             ffn2_w_down[0], True)

    pool_p = v_tail[:, POOL_TAIL - POOL_HIST:, :][None]
    return (y_prompt.reshape(n_p, seq, D_MODEL), y_sample.reshape(n_s, 1, D_MODEL),
            sre_p.reshape(1, n_p, G, P), sim_p.reshape(1, n_p, G, P), pool_p,
            sre_s.reshape(1, n_s, G, P), sim_s.reshape(1, n_s, G, P), pool_s[None])
```

```python
import functools
from typing import NamedTuple

import jax
import jax.numpy as jnp
from jax import lax
from jax.experimental import pallas as pl
from jax.experimental.pallas import tpu as pltpu

F32 = jnp.float32
BF16 = jnp.bfloat16

D_MODEL = 2048
D_FF = 5632
D_SSM = 1024
D_POOL = 1024
SSM_GROUP = 16
SSM_STATE = 64
N_SSM_GROUPS = 64
POOL_WINDOWS = (2, 4, 8, 16)
POOL_GROUP = 256
POOL_HIST = 15
N_MOD_COLS = 9 * D_MODEL
N_PROMPT = 4
N_SAMPLE = 128
EPS = 1e-6
FFN_RES = 0.5

LANES = 128
SUBLANES = 8
MXU_TILE = 256
VMEM_LIMIT = 60 * 1024 * 1024

PROMPT_MOD_BLOCK = N_SAMPLE // SUBLANES
NORM_ROWS = 32
FFN1_SUBLAYER, MIX_SUBLAYER, FFN2_SUBLAYER = 0, 1, 2

FFN_TM = 1024
FFN_ROW_CHUNK = 512
FFN_TF = 512
FFN_HEAD_TF = 256
PROJ_TM = 1024
OUTPROJ_TM = 512

OCT = LANES // SSM_GROUP
N_OCT = N_SSM_GROUPS // OCT
T_CHUNK = 8
OCT_STATE = OCT * SSM_STATE
CHUNK_K = T_CHUNK * LANES
TOK_PER_TILE = MXU_TILE // LANES
N_TOEP_TILES = CHUNK_K // MXU_TILE


def _toep_tile_row(nt):
    return MXU_TILE * (nt * (nt + 1) // 2)


TOEP_ROWS = _toep_tile_row(N_TOEP_TILES)


def _cparams(n_axes):
    return pltpu.CompilerParams(dimension_semantics=("arbitrary",) * n_axes,
                                vmem_limit_bytes=VMEM_LIMIT)


def _dot(a, b):
    return jnp.dot(a, b, preferred_element_type=F32)


ADALN_TN = 2048


def _adaln_kernel(c_ref, w_ref, b_ref, o_ref):
    c = c_ref[...]
    sc = (c * jax.nn.sigmoid(c)).astype(BF16)
    o_ref[...] = _dot(sc, w_ref[...].astype(BF16)) + b_ref[...]


def _norm_mod(x, g, shift, scale):
    ms = jnp.mean(x * x, axis=-1, keepdims=True)
    y = x * lax.rsqrt(ms + EPS) * g
    return y * (1.0 + scale) + shift


def _mod_specs(sub):
    return [pl.BlockSpec((SUBLANES, D_MODEL), lambda i, col=3 * sub + m: (PROMPT_MOD_BLOCK, col))
            for m in range(3)]


def _mod_rows(ref, blocks_per_seq):
    return ref[pl.ds(pl.program_id(0) // blocks_per_seq, 1), :]


class _RowGroup(NamedTuple):
    rows: int
    row_chunk: int
    per_row: bool
    first_block: int
    resident: bool


def _ffn_step(s, n_j, has_copy):
    t = jnp.maximum(s - 1, 0) if has_copy else s
    return t // n_j, t % n_j


def _ffn_kernel(*refs, groups, blocks_per_seq, n_j, final_norm, emit_bf16, has_copy):
    n = len(groups)
    g_ref, wg_ref, wu_ref, wd_ref, fg_ref = refs[4 * n:4 * n + 5]
    outs = refs[4 * n + 5 + int(has_copy):-1]
    h_ref = refs[-1]
    step = pl.program_id(0)
    i, j = _ffn_step(step, n_j, has_copy)
    last_j = n_j - 1
    active = step >= int(has_copy)

    if has_copy:
        @pl.when(step == 0)
        def _():
            outs[0][...] = refs[4 * n + 5][...]

    if emit_bf16:
        bf16_refs = outs[n:n + 3]
        for w_ref, wo_ref in zip((wg_ref, wu_ref, wd_ref), bf16_refs):
            wo_ref[...] = w_ref[...].astype(BF16)
        wg_ref, wu_ref, wd_ref = bf16_refs

    chunks, h_base = [], 0
    for k, grp in enumerate(groups):
        chunks += [(k, r, h_base + r) for r in range(0, grp.rows, grp.row_chunk)]
        h_base += grp.rows

    def mod_rows(k, ref, rows):
        if groups[k].per_row:
            return ref[rows, :]
        seq = (i + groups[k].first_block) // blocks_per_seq
        return ref[pl.ds(seq, 1), :]

    def norm_chunk(c):
        k, r, hr = chunks[c]
        x_ref, shift_ref, scale_ref, _ = refs[4 * k:4 * k + 4]
        for s in range(0, groups[k].row_chunk, NORM_ROWS):
            sr = pl.ds(r + s, NORM_ROWS)
            h = _norm_mod(x_ref[sr, :], g_ref[...], mod_rows(k, shift_ref, sr),
                          mod_rows(k, scale_ref, sr))
            h_ref[pl.ds(hr + s, NORM_ROWS), :] = h.astype(BF16)

    def chunk(c, first, last):
        k, r, hr = chunks[c]
        size = groups[k].row_chunk
        x_ref, _, _, gate_ref = refs[4 * k:4 * k + 4]
        o_ref = outs[k]
        rows = pl.ds(r, size)
        if first and c + 1 < len(chunks):
            norm_chunk(c + 1)
        h = h_ref[pl.ds(hr, size), :]
        g = _dot(h, wg_ref[...])
        u = _dot(h, wu_ref[...])
        a = (g * jax.nn.sigmoid(g) * u).astype(BF16)
        d = _dot(a, wd_ref[...])
        acc = d if first else o_ref[rows, :] + d
        if not last:
            o_ref[rows, :] = acc
            return
        o_ref[rows, :] = x_ref[rows, :] + FFN_RES * mod_rows(k, gate_ref, rows) * acc
        if final_norm:
            for s in range(0, size, NORM_ROWS):
                sr = pl.ds(r + s, NORM_ROWS)
                y = o_ref[sr, :]
                ms = jnp.mean(y * y, axis=-1, keepdims=True)
                o_ref[sr, :] = y * lax.rsqrt(ms + EPS) * fg_ref[...]

    def run(first, last):
        if first:
            norm_chunk(0)
        for c in range(len(chunks)):
            chunk(c, first, last)

    pl.when(jnp.logical_and(active, j == 0))(lambda: run(True, False))
    pl.when(jnp.logical_and(j > 0, j < last_j))(lambda: run(False, False))
    pl.when(j == last_j)(lambda: run(False, True))


def _ffn(xs, groups, norm_g, mod, sub, wg, wu, wd, final_g, *, n_blocks, tf, final_norm,
         blocks_per_seq, out_rows=None, first_rows=None):
    emit_bf16 = wg.dtype == F32
    has_copy = first_rows is not None
    assert not emit_bf16 or n_blocks == 1
    assert not has_copy or len(groups) == 1
    n_j = D_FF // tf
    step = functools.partial(_ffn_step, n_j=n_j, has_copy=has_copy)

    def row_spec(grp, col=0, out=False):
        mode = dict(pipeline_mode=pl.Buffered(1)) if grp.resident else {}
        first = 0 if out and not has_copy else grp.first_block

        def index(s):
            block = step(s)[0] + first
            return (jnp.where(s == 0, 0, block) if out and has_copy else block), col
        return pl.BlockSpec((grp.rows, D_MODEL), index, **mode)

    in_specs, operands = [], []
    for x, grp in zip(xs, groups):
        in_specs.append(row_spec(grp))
        operands.append(x)
        for m in range(3):
            col = 3 * sub + m
            if grp.per_row:
                in_specs.append(row_spec(grp, col))
            else:
                in_specs.append(pl.BlockSpec((SUBLANES, D_MODEL),
                                             lambda s, col=col: (PROMPT_MOD_BLOCK, col)))
            operands.append(mod)
    w_specs = [pl.BlockSpec((D_MODEL, tf), lambda s: (0, step(s)[1])),
               pl.BlockSpec((D_MODEL, tf), lambda s: (0, step(s)[1])),
               pl.BlockSpec((tf, D_MODEL), lambda s: (step(s)[1], 0))]
    vec_spec = pl.BlockSpec((1, D_MODEL), lambda s: (0, 0))
    in_specs += [vec_spec, *w_specs, vec_spec]
    operands += [norm_g.reshape(1, D_MODEL), wg, wu, wd, final_g.reshape(1, D_MODEL)]
    if has_copy:
        in_specs.append(pl.BlockSpec(first_rows.shape, lambda s: (0, 0),
                                     pipeline_mode=pl.Buffered(1)))
        operands.append(first_rows)

    out_specs = [row_spec(grp, out=True) for grp in groups]
    out_shape = [jax.ShapeDtypeStruct((out_rows or grp.rows, D_MODEL), F32) for grp in groups]
    if emit_bf16:
        out_specs += w_specs
        out_shape += [jax.ShapeDtypeStruct(w.shape, BF16) for w in (wg, wu, wd)]
    kern = functools.partial(_ffn_kernel, groups=groups, blocks_per_seq=blocks_per_seq, n_j=n_j,
                             final_norm=final_norm, emit_bf16=emit_bf16, has_copy=has_copy)
    return pl.pallas_call(
        kern,
        grid=(int(has_copy) + n_blocks * n_j,),
        in_specs=in_specs,
        out_specs=out_specs,
        out_shape=out_shape,
        scratch_shapes=[pltpu.VMEM((sum(grp.rows for grp in groups), D_MODEL), BF16)],
        compiler_params=_cparams(1),
        name="ffn",
    )(*operands)


POOL_TAIL = 16


def _proj_kernel(x_ref, g_ref, shift_ref, scale_ref, w_ref, pw_ref, pb_ref, ps_ref,
                 u_ref, y_ref, tail_out_ref, tail_ref, h_ref, *, blocks_per_seq):
    tm = x_ref.shape[0]
    half = tm // 2
    blk = pl.program_id(0) % blocks_per_seq
    sh = _mod_rows(shift_ref, blocks_per_seq)
    sc = _mod_rows(scale_ref, blocks_per_seq)

    @pl.when(blk == 0)
    def _():
        tail_ref[...] = jnp.zeros(tail_ref.shape, F32)

    def norm_half(r):
        for s in range(0, half, NORM_ROWS):
            rows = pl.ds(r * half + s, NORM_ROWS)
            h_ref[rows, :] = _norm_mod(x_ref[rows, :], g_ref[...], sh, sc).astype(BF16)

    norm_half(0)
    norm_half(1)
    pv = jnp.concatenate([_dot(h_ref[pl.ds(r * half, half), :], w_ref[:, D_SSM:])
                          for r in range(2)], axis=0)
    for r in range(2):
        pu = _dot(h_ref[pl.ds(r * half, half), :], w_ref[:, 0:D_SSM])
        u_ref[pl.ds(r * half // T_CHUNK, half // T_CHUNK), :, :] = pu.reshape(
            half // T_CHUNK, T_CHUNK, D_SSM)

    row_ext = lax.broadcasted_iota(jnp.int32, (POOL_TAIL + tm, POOL_GROUP), 0)
    pos = blk * tm + lax.broadcasted_iota(jnp.int32, (tm, POOL_GROUP), 0)
    for gi, w in enumerate(POOL_WINDOWS):
        lanes = slice(gi * POOL_GROUP, (gi + 1) * POOL_GROUP)
        v = pv[:, lanes]
        s = jnp.concatenate([tail_ref[:, lanes], v], axis=0)
        k = 1
        while k < w:
            s = s + jnp.where(row_ext >= k, pltpu.roll(s, k, axis=0), 0.0)
            k *= 2
        cnt = jnp.clip(pos + 1, 1, w).astype(F32)
        z = s[POOL_TAIL:, :] / cnt - v
        y_ref[:, lanes] = _pool_linear(z, gi, pw_ref, pb_ref, ps_ref).astype(y_ref.dtype)
    tail = pv[tm - POOL_TAIL:, :]
    tail_ref[...] = tail
    tail_out_ref[...] = tail


def _proj(x, norm_g, mod, w_in, pool_w, pool_b, pool_scale, *, tm, n_seq):
    rows = x.shape[0]
    seq_len = rows // n_seq
    blocks_per_seq = seq_len // tm
    u_shape = (seq_len // T_CHUNK, n_seq, T_CHUNK, D_SSM)
    u_spec = pl.BlockSpec((tm // T_CHUNK, None, T_CHUNK, D_SSM),
                          lambda i: (i % blocks_per_seq, i // blocks_per_seq, 0, 0))
    shift_spec, scale_spec, _ = _mod_specs(MIX_SUBLAYER)
    vec_spec = pl.BlockSpec((1, D_POOL), lambda i: (0, 0))
    return pl.pallas_call(
        functools.partial(_proj_kernel, blocks_per_seq=blocks_per_seq),
        grid=(rows // tm,),
        in_specs=[pl.BlockSpec((tm, D_MODEL), lambda i: (i, 0)),
                  pl.BlockSpec((1, D_MODEL), lambda i: (0, 0)),
                  shift_spec, scale_spec,
                  pl.BlockSpec((D_MODEL, D_MODEL), lambda i: (0, 0),
                               pipeline_mode=pl.Buffered(1)),
                  pl.BlockSpec(pool_w.shape, lambda i: (0, 0, 0)), vec_spec, vec_spec],
        out_specs=[u_spec, pl.BlockSpec((tm, D_POOL), lambda i: (i, 0)),
                   pl.BlockSpec((None, POOL_TAIL, D_POOL), lambda i: (i // blocks_per_seq, 0, 0))],
        out_shape=[jax.ShapeDtypeStruct(u_shape, F32),
                   jax.ShapeDtypeStruct((rows, D_POOL), BF16),
                   jax.ShapeDtypeStruct((n_seq, POOL_TAIL, D_POOL), F32)],
        scratch_shapes=[pltpu.VMEM((POOL_TAIL, D_POOL), F32), pltpu.VMEM((tm, D_MODEL), BF16)],
        compiler_params=_cparams(1),
        name="proj",
    )(x, norm_g.reshape(1, D_MODEL), mod, mod, w_in, pool_w, pool_b.reshape(1, D_POOL),
      pool_scale.reshape(1, D_POOL))


SAMPLE_TN = 512


def _proj_sample_kernel(x_ref, g_ref, shift_ref, scale_ref, w_ref, p_ref, wo_ref, h_ref):
    @pl.when(pl.program_id(0) == 0)
    def _():
        h = _norm_mod(x_ref[...], g_ref[...], shift_ref[...], scale_ref[...])
        h_ref[...] = h.astype(BF16)
    wo_ref[...] = w_ref[...].astype(BF16)
    p_ref[...] = _dot(h_ref[...], wo_ref[...])


def _proj_sample(x, norm_g, mod, w_in):
    rows = x.shape[0]
    shift_col, scale_col = 3 * MIX_SUBLAYER, 3 * MIX_SUBLAYER + 1
    return pl.pallas_call(
        _proj_sample_kernel,
        grid=(D_MODEL // SAMPLE_TN,),
        in_specs=[pl.BlockSpec((rows, D_MODEL), lambda j: (0, 0)),
                  pl.BlockSpec((1, D_MODEL), lambda j: (0, 0)),
                  pl.BlockSpec((rows, D_MODEL), lambda j: (0, shift_col)),
                  pl.BlockSpec((rows, D_MODEL), lambda j: (0, scale_col)),
                  pl.BlockSpec((D_MODEL, SAMPLE_TN), lambda j: (0, j))],
        out_specs=[pl.BlockSpec((rows, SAMPLE_TN), lambda j: (0, j)),
                   pl.BlockSpec((D_MODEL, SAMPLE_TN), lambda j: (0, j))],
        out_shape=[jax.ShapeDtypeStruct((rows, D_MODEL), F32),
                   jax.ShapeDtypeStruct((D_MODEL, D_MODEL), BF16)],
        scratch_shapes=[pltpu.VMEM((rows, D_MODEL), BF16)],
        compiler_params=_cparams(1),
        name="proj_sample",
    )(x, norm_g.reshape(1, D_MODEL), mod, mod, w_in)


def _discretise(lam_re, lam_im, log_dt):
    lr = jnp.minimum(lam_re, -1e-4)
    li = lam_im
    dt = jnp.exp(log_dt)
    mag = jnp.exp(lr * dt)
    ang = li * dt
    a_re = mag * jnp.cos(ang)
    a_im = mag * jnp.sin(ang)
    den = lr * lr + li * li
    num_re = a_re - 1.0
    f_re = (num_re * lr + a_im * li) / den
    f_im = (a_im * lr - num_re * li) / den
    return a_re, a_im, f_re, f_im


def _s5_prep_kernel(lam_re1, lam_im1, ldt1, c_re1, c_im1,
                    lam_re2, lam_im2, ldt2, b_re2, b_im2, gluw_t,
                    toep_ref, wz_ref, m_ref, apow_ref, gluw_ref):
    g1 = lax.broadcasted_iota(jnp.int32, (SSM_STATE, LANES), 1) // SSM_GROUP
    g2 = lax.broadcasted_iota(jnp.int32, (SSM_GROUP, OCT_STATE), 1) // SSM_STATE

    def expand1(x):
        return jnp.concatenate([jnp.where(g1 == g, x, 0.0) for g in range(OCT)], axis=0)

    def expand2(x):
        return jnp.concatenate([jnp.where(g2 == g, x, 0.0) for g in range(OCT)], axis=0)

    def split(x):
        hi = x.astype(BF16)
        return hi, (x - hi.astype(F32)).astype(BF16)

    a1_re, a1_im, _, _ = _discretise(lam_re1[...], lam_im1[...], ldt1[...])
    a2_re, a2_im, f_re, f_im = _discretise(lam_re2[...], lam_im2[...], ldt2[...])
    br, bi = b_re2[...], b_im2[...]
    bbar_re = f_re * br - f_im * bi
    bbar_im = f_re * bi + f_im * br
    cr, ci = c_re1[...], c_im1[...]
    bre_hi, bre_lo = split(expand2(bbar_re))
    bim_hi, bim_lo = split(expand2(bbar_im))

    def dot3(x_hi, x_lo, y_hi, y_lo):
        return _dot(x_hi, y_hi) + (_dot(x_hi, y_lo) + _dot(x_lo, y_hi))

    p1_re, p1_im = jnp.ones_like(a1_re), jnp.zeros_like(a1_im)
    p2_re, p2_im = jnp.ones_like(a2_re), jnp.zeros_like(a2_im)
    kk = []
    zero_blk = jnp.zeros((LANES, LANES), F32)
    for j in range(T_CHUNK + 1):
        care_hi, care_lo = split(expand1(cr * p1_re - ci * p1_im))
        caim_hi, caim_lo = split(expand1(cr * p1_im + ci * p1_re))
        m_ref[0:OCT_STATE, j * LANES:(j + 1) * LANES] = care_hi
        m_ref[OCT_STATE:2 * OCT_STATE, j * LANES:(j + 1) * LANES] = -caim_hi
        if j < T_CHUNK:
            t = T_CHUNK - 1 - j
            ba_re = expand2(bbar_re * p2_re - bbar_im * p2_im)
            ba_im = expand2(bbar_re * p2_im + bbar_im * p2_re)
            wz_ref[t * LANES:(t + 1) * LANES, 0:OCT_STATE] = ba_re.astype(BF16)
            wz_ref[t * LANES:(t + 1) * LANES, OCT_STATE:2 * OCT_STATE] = ba_im.astype(BF16)
            kk.append(dot3(bre_hi, bre_lo, care_hi, care_lo)
                      - dot3(bim_hi, bim_lo, caim_hi, caim_lo))
        if j == 1:
            apow_ref[0:1, :] = p2_re
            apow_ref[1:2, :] = p2_im
        if j == T_CHUNK:
            apow_ref[2:3, :] = p2_re
            apow_ref[3:4, :] = p2_im
        p1_re, p1_im = p1_re * a1_re - p1_im * a1_im, p1_re * a1_im + p1_im * a1_re
        p2_re, p2_im = p2_re * a2_re - p2_im * a2_im, p2_re * a2_im + p2_im * a2_re

    for t2 in range(T_CHUNK):
        nt, col = divmod(t2, TOK_PER_TILE)
        for t in range((nt + 1) * TOK_PER_TILE):
            blk = kk[t2 - t] if t2 >= t else zero_blk
            row = _toep_tile_row(nt) + t * LANES
            toep_ref[row:row + LANES, col * LANES:(col + 1) * LANES] = blk.astype(BF16)

    rg = lax.broadcasted_iota(jnp.int32, (LANES, LANES), 0) // SSM_GROUP
    lg = lax.broadcasted_iota(jnp.int32, (LANES, LANES), 1) // SSM_GROUP
    gw = jnp.where(rg == lg, gluw_t[...], 0.0).astype(BF16)
    gz = jnp.zeros((LANES, LANES), BF16)
    for a in range(TOK_PER_TILE):
        for b in range(TOK_PER_TILE):
            gluw_ref[a * LANES:(a + 1) * LANES, b * LANES:(b + 1) * LANES] = gw if a == b else gz


def _adaln_s5_prep_kernel(c_ref, w_ref, b_ref, *refs):
    n_prep_in = len(refs) - 6
    _adaln_kernel(c_ref, w_ref, b_ref, refs[n_prep_in])

    @pl.when(pl.program_id(0) < N_OCT)
    def _():
        _s5_prep_kernel(*refs[:n_prep_in], *refs[n_prep_in + 1:])


def _adaln_s5_prep(c_all, ada_w, ada_b, lam_re, lam_im, log_dt, b_re, b_im, c_re, c_im, glu_w):
    G, P, H = N_SSM_GROUPS, SSM_STATE, SSM_GROUP
    rows, n_cols = c_all.shape[0], ada_w.shape[1]
    n_steps = n_cols // ADALN_TN
    assert n_steps >= N_OCT
    row = lambda a: a.reshape(N_OCT, 1, OCT_STATE)
    ldt = jnp.broadcast_to(log_dt[:, None], (G, P))

    def col(a_gp):
        a = jnp.swapaxes(a_gp.reshape(N_OCT, OCT, P), 1, 2)
        return jnp.broadcast_to(a[..., None], (N_OCT, P, OCT, H)).reshape(N_OCT, P, LANES)

    lay1 = lambda a_ghp: jnp.swapaxes(a_ghp.reshape(N_OCT, LANES, P), 1, 2)
    lay2 = lambda a_gph: jnp.swapaxes(a_gph.reshape(N_OCT, OCT_STATE, H), 1, 2)
    c_re1, c_im1, b_re2, b_im2 = lay1(c_re), lay1(c_im), lay2(b_re), lay2(b_im)
    gw = glu_w.reshape(N_OCT, OCT, H, 1, H)
    gluw_t = jnp.broadcast_to(gw, (N_OCT, OCT, H, OCT, H)).reshape(N_OCT, LANES, LANES)

    def spec(shape):
        return pl.BlockSpec((None,) + shape,
                            lambda j: (jnp.minimum(j, N_OCT - 1),) + (0,) * len(shape))

    m_cols = (T_CHUNK + 1) * LANES
    mod, *ops = pl.pallas_call(
        _adaln_s5_prep_kernel,
        grid=(n_steps,),
        in_specs=[pl.BlockSpec((rows, D_MODEL), lambda j: (0, 0)),
                  pl.BlockSpec((D_MODEL, ADALN_TN), lambda j: (0, j)),
                  pl.BlockSpec((1, ADALN_TN), lambda j: (0, j))]
                 + [spec((SSM_STATE, LANES))] * 5
                 + [spec((1, OCT_STATE))] * 3 + [spec((SSM_GROUP, OCT_STATE))] * 2
                 + [spec((LANES, LANES))],
        out_specs=[pl.BlockSpec((rows, ADALN_TN), lambda j: (0, j)),
                   spec((TOEP_ROWS, MXU_TILE)), spec((CHUNK_K, 2 * OCT_STATE)),
                   spec((2 * OCT_STATE, m_cols)), spec((4, OCT_STATE)),
                   spec((MXU_TILE, MXU_TILE))],
        out_shape=[jax.ShapeDtypeStruct((rows, n_cols), F32),
                   jax.ShapeDtypeStruct((N_OCT, TOEP_ROWS, MXU_TILE), BF16),
                   jax.ShapeDtypeStruct((N_OCT, CHUNK_K, 2 * OCT_STATE), BF16),
                   jax.ShapeDtypeStruct((N_OCT, 2 * OCT_STATE, m_cols), BF16),
                   jax.ShapeDtypeStruct((N_OCT, 4, OCT_STATE), F32),
                   jax.ShapeDtypeStruct((N_OCT, MXU_TILE, MXU_TILE), BF16)],
        compiler_params=_cparams(1),
        name="adaln_s5_prep",
    )(c_all, ada_w, ada_b, col(lam_re), col(lam_im), col(ldt), c_re1, c_im1,
      row(lam_re), row(lam_im), row(ldt), b_re2, b_im2, gluw_t)
    return mod, tuple(ops)


GELU_C0 = 0.7978845608028654
GELU_C1 = GELU_C0 * 0.044715


def _glu_out(y, gluw, glub):
    gy = y * (0.5 + 0.5 * jnp.tanh(y * (GELU_C0 + GELU_C1 * (y * y))))
    z = _dot(gy.astype(BF16), gluw) + glub
    return gy * (0.5 + 0.5 * jnp.tanh(0.5 * z))


def _s5_prompt_kernel(xa_ref, xc_ref, toep_ref, wz_ref, m_ref, apow_ref, d_ref, gluw_ref, glub_ref,
                      y_ref, sre_ref, sim_ref, xr_ref, z0_ref, z1_ref, yv0_ref, yv1_ref,
                      *, n_seq, row_chunk):
    step = pl.program_id(0)
    n_rows = xa_ref.shape[0] // T_CHUNK
    n_blk = n_rows // row_chunk
    scans_per_blk = n_rows // SUBLANES // n_blk

    def tokens(b, t):
        return pl.ds(b * (row_chunk * T_CHUNK) + t, row_chunk, stride=T_CHUNK)

    def local_block(b, z_ref, yv_ref):
        rows = pl.ds(b * row_chunk, row_chunk)
        for t in range(T_CHUNK):
            xr_ref[rows, t * LANES:(t + 1) * LANES] = xa_ref[tokens(b, t), :].astype(BF16)
        xr = xr_ref[rows, :]
        z_ref[rows, :] = _dot(xr, wz_ref[...])
        for nt in range(N_TOEP_TILES):
            k_end = (nt + 1) * MXU_TILE
            tile = toep_ref[_toep_tile_row(nt):_toep_tile_row(nt) + k_end, :]
            yv_ref[rows, nt * MXU_TILE:k_end] = _dot(xr[:, 0:k_end], tile)

    lo = lax.broadcasted_iota(jnp.int32, (SUBLANES, OCT_STATE), 0) < n_seq

    def scan_steps(k0, k1, carry, z_ref):
        are, aim = apow_ref[2:3, :], apow_ref[3:4, :]
        pre, pim = carry
        for k in range(k0, k1):
            rows = slice(k * SUBLANES, (k + 1) * SUBLANES)
            zre = z_ref[rows, 0:OCT_STATE]
            zim = z_ref[rows, OCT_STATE:2 * OCT_STATE]
            w1re = are * pre - aim * pim + zre
            w1im = are * pim + aim * pre + zim
            r1re = pltpu.roll(w1re, n_seq, axis=0)
            r1im = pltpu.roll(w1im, n_seq, axis=0)
            w2re = are * r1re - aim * r1im + zre
            w2im = are * r1im + aim * r1re + zim
            z_ref[rows, 0:OCT_STATE] = jnp.where(lo, pre, r1re)
            z_ref[rows, OCT_STATE:2 * OCT_STATE] = jnp.where(lo, pim, r1im)
            pre = jnp.where(lo, pltpu.roll(w2re, n_seq, axis=0), w2re)
            pim = jnp.where(lo, pltpu.roll(w2im, n_seq, axis=0), w2im)
        return pre, pim

    def readout(z_ref, yv_ref):
        d2 = jnp.concatenate([d_ref[...]] * TOK_PER_TILE, axis=1)
        glub2 = jnp.concatenate([glub_ref[...]] * TOK_PER_TILE, axis=1)
        for b in range(n_blk):
            rows = pl.ds(b * row_chunk, row_chunk)
            y = yv_ref[rows, :] + _dot(z_ref[rows, :].astype(BF16), m_ref[:, LANES:])
            for t0 in range(0, T_CHUNK, TOK_PER_TILE):
                toks = [tokens(b, t0 + i) for i in range(TOK_PER_TILE)]
                u = jnp.concatenate([xc_ref[tok, :] for tok in toks], axis=1)
                yt = y[:, t0 * LANES:(t0 + TOK_PER_TILE) * LANES] + d2 * u
                out = _glu_out(yt, gluw_ref[...], glub2)
                for i, tok in enumerate(toks):
                    y_ref[tok, :] = out[:, i * LANES:(i + 1) * LANES]

    def body(slot_a, slot_c):
        zeros = jnp.zeros((SUBLANES, OCT_STATE), F32)
        carry = (zeros, zeros)
        for b in range(n_blk):
            if slot_c is not None:
                carry = scan_steps(b * scans_per_blk, (b + 1) * scans_per_blk, carry, slot_c[0])
            if slot_a is not None:
                local_block(b, *slot_a)
        if slot_c is not None:
            sre_ref[...] = carry[0][0:n_seq, :]
            sim_ref[...] = carry[1][0:n_seq, :]
            readout(*slot_c)

    slots = ((z0_ref, yv0_ref), (z1_ref, yv1_ref))
    mid = jnp.logical_and(step > 0, step < N_OCT)
    even = step % 2 == 0
    pl.when(step == 0)(lambda: body(slots[0], None))
    pl.when(jnp.logical_and(mid, even))(lambda: body(slots[0], slots[1]))
    pl.when(jnp.logical_and(mid, jnp.logical_not(even)))(lambda: body(slots[1], slots[0]))
    pl.when(step == N_OCT)(lambda: body(None, slots[(N_OCT - 1) % 2]))


def _s5_prompt(u_flat, ops, d_skip, glu_b, *, n_seq):
    toep, wz, m, apow, gluw = ops
    assert 2 * n_seq == SUBLANES
    rows = u_flat.shape[0]
    n_rows = rows // T_CHUNK
    cur = lambda s: jnp.minimum(s, N_OCT - 1)
    prev = lambda s: jnp.maximum(s - 1, 0)
    a_spec = lambda shape: pl.BlockSpec((None,) + shape, lambda s: (cur(s), 0, 0))
    c_spec = lambda shape: pl.BlockSpec((None,) + shape, lambda s: (prev(s), 0, 0))
    kern = functools.partial(_s5_prompt_kernel, n_seq=n_seq, row_chunk=512)
    return pl.pallas_call(
        kern,
        grid=(N_OCT + 1,),
        in_specs=[pl.BlockSpec((rows, LANES), lambda s: (0, cur(s))),
                  pl.BlockSpec((rows, LANES), lambda s: (0, prev(s))),
                  a_spec(toep.shape[1:]), a_spec(wz.shape[1:]), c_spec(m.shape[1:]),
                  c_spec(apow.shape[1:]), c_spec((1, LANES)), c_spec(gluw.shape[1:]),
                  c_spec((1, LANES))],
        out_specs=[pl.BlockSpec((rows, LANES), lambda s: (0, prev(s))),
                   pl.BlockSpec((n_seq, OCT_STATE), lambda s: (0, prev(s))),
                   pl.BlockSpec((n_seq, OCT_STATE), lambda s: (0, prev(s)))],
        out_shape=[jax.ShapeDtypeStruct((rows, D_SSM), F32),
                   jax.ShapeDtypeStruct((n_seq, N_OCT * OCT_STATE), F32),
                   jax.ShapeDtypeStruct((n_seq, N_OCT * OCT_STATE), F32)],
        scratch_shapes=[pltpu.VMEM((n_rows, CHUNK_K), BF16)]
                       + [pltpu.VMEM((n_rows, 2 * OCT_STATE), F32)] * 2
                       + [pltpu.VMEM((n_rows, CHUNK_K), F32)] * 2,
        compiler_params=_cparams(1),
        name="s5_prompt",
    )(u_flat, u_flat, toep, wz, m, apow, d_skip.reshape(N_OCT, 1, LANES), gluw,
      glu_b.reshape(N_OCT, 1, LANES))


def _s5_sample_kernel(u_ref, s0re_ref, s0im_ref, wz_ref, m_ref, apow_ref, d_ref, gluw_ref,
                      glub_ref, y_ref, sre_ref, sim_ref):
    for o in range(N_OCT):
        ch = slice(o * LANES, (o + 1) * LANES)
        st = slice(o * OCT_STATE, (o + 1) * OCT_STATE)
        u = u_ref[:, ch]
        z = _dot(u.astype(BF16), wz_ref[o])
        are, aim = apow_ref[o, 0:1, :], apow_ref[o, 1:2, :]
        s0re, s0im = s0re_ref[:, st], s0im_ref[:, st]
        nre = are * s0re - aim * s0im + z[:, 0:OCT_STATE]
        nim = are * s0im + aim * s0re + z[:, OCT_STATE:2 * OCT_STATE]
        sre_ref[:, st] = nre
        sim_ref[:, st] = nim
        y = (_dot(nre.astype(BF16), m_ref[o, 0:OCT_STATE, :])
             + _dot(nim.astype(BF16), m_ref[o, OCT_STATE:2 * OCT_STATE, :])
             + d_ref[o] * u)
        y_ref[:, ch] = _glu_out(y, gluw_ref[o], glub_ref[o]).astype(y_ref.dtype)


def _pool_linear(z, gi, pw_ref, pb_ref, ps_ref):
    lanes = slice(gi * POOL_GROUP, (gi + 1) * POOL_GROUP)
    return (_dot(z.astype(BF16), pw_ref[gi]) + pb_ref[:, lanes]) * ps_ref[:, lanes]


def _pool_sample_kernel(hist_ref, v_ref, pw_ref, pb_ref, ps_ref, y_ref, new_ref):
    new_ref[0:POOL_HIST - 1] = hist_ref[1:POOL_HIST]
    new_ref[POOL_HIST - 1] = v_ref[...]
    for gi, w in enumerate(POOL_WINDOWS):
        lanes = slice(gi * POOL_GROUP, (gi + 1) * POOL_GROUP)
        v = v_ref[:, lanes]
        s = v
        for r in range(POOL_HIST - (w - 1), POOL_HIST):
            s = s + hist_ref[r, :, lanes]
        y = _pool_linear(s / float(w) - v, gi, pw_ref, pb_ref, ps_ref)
        y_ref[:, lanes] = y.astype(y_ref.dtype)


def _outproj_kernel(x_ref, ys_ref, yp_ref, gate_ref, w_ref, o_ref, *, blocks_per_seq):
    tm = x_ref.shape[0]
    ys = ys_ref[...].reshape(tm, D_SSM).astype(BF16)
    yp = yp_ref[...].astype(BF16)
    mix = _dot(ys, w_ref[0:D_SSM, :]) + _dot(yp, w_ref[D_SSM:, :])
    o_ref[...] = x_ref[...] + _mod_rows(gate_ref, blocks_per_seq) * mix


def _outproj(x, ys, yp, mod, w_out, *, tm, n_seq):
    rows = x.shape[0]
    blocks_per_seq = (rows // n_seq) // tm
    ys_spec = pl.BlockSpec((tm // T_CHUNK, None, T_CHUNK, D_SSM),
                           lambda i: (i % blocks_per_seq, i // blocks_per_seq, 0, 0))
    gate_spec = _mod_specs(MIX_SUBLAYER)[2]
    return pl.pallas_call(
        functools.partial(_outproj_kernel, blocks_per_seq=blocks_per_seq),
        grid=(rows // tm,),
        in_specs=[pl.BlockSpec((tm, D_MODEL), lambda i: (i, 0)),
                  ys_spec,
                  pl.BlockSpec((tm, D_POOL), lambda i: (i, 0)),
                  gate_spec,
                  pl.BlockSpec((D_MODEL, D_MODEL), lambda i: (0, 0),
                               pipeline_mode=pl.Buffered(1))],
        out_specs=pl.BlockSpec((tm, D_MODEL), lambda i: (i, 0)),
        out_shape=jax.ShapeDtypeStruct((rows, D_MODEL), F32),
        compiler_params=_cparams(1),
        name="outproj",
    )(x, ys, yp, mod, w_out)


def _mixer_sample_kernel(x_ref, gate_ref, w_ref, proj_ref, s0re_ref, s0im_ref, wz_ref, m_ref,
                         apow_ref, d_ref, gluw_ref, glub_ref, hist_ref, pw_ref, pb_ref, ps_ref,
                         o_ref, wo_ref, sre_ref, sim_ref, new_hist_ref, ys_ref, yp_ref):
    @pl.when(pl.program_id(0) == 0)
    def _():
        _s5_sample_kernel(proj_ref, s0re_ref, s0im_ref, wz_ref, m_ref, apow_ref, d_ref, gluw_ref,
                          glub_ref, ys_ref, sre_ref, sim_ref)
        _pool_sample_kernel(hist_ref, proj_ref.at[:, pl.ds(D_SSM, D_POOL)], pw_ref, pb_ref, ps_ref,
                            yp_ref, new_hist_ref)

    wo_ref[...] = w_ref[...].astype(BF16)
    mix = _dot(ys_ref[...], wo_ref[0:D_SSM, :]) + _dot(yp_ref[...], wo_ref[D_SSM:, :])
    o_ref[...] = x_ref[...] + gate_ref[...] * mix


def _mixer_sample(x, proj, s0_re, s0_im, hist_t, ops, d_skip, glu_b, pool_w, pool_b, pool_scale,
                  mod, w_out):
    _, wz, m, apow, gluw = ops
    rows = x.shape[0]
    n_col = D_MODEL // SAMPLE_TN
    gate_col = (3 * MIX_SUBLAYER + 2) * n_col

    def once(shape, index=None):
        index = index or (0,) * len(shape)
        return pl.BlockSpec(shape, lambda j: index, pipeline_mode=pl.Buffered(1))

    return pl.pallas_call(
        _mixer_sample_kernel,
        grid=(n_col,),
        in_specs=[pl.BlockSpec((rows, SAMPLE_TN), lambda j: (0, j)),
                  pl.BlockSpec((rows, SAMPLE_TN), lambda j: (0, gate_col + j)),
                  pl.BlockSpec((D_MODEL, SAMPLE_TN), lambda j: (0, j)),
                  once(proj.shape), once(s0_re.shape), once(s0_im.shape),
                  once((N_OCT, LANES, 2 * OCT_STATE), (0, T_CHUNK - 1, 0)),
                  once((N_OCT, 2 * OCT_STATE, LANES)),
                  once(apow.shape), once((N_OCT, 1, LANES)), once((N_OCT, LANES, LANES)),
                  once((N_OCT, 1, LANES)), once(hist_t.shape), once(pool_w.shape),
                  once((1, D_POOL)), once((1, D_POOL))],
        out_specs=[pl.BlockSpec((rows, SAMPLE_TN), lambda j: (0, j)),
                   pl.BlockSpec((D_MODEL, SAMPLE_TN), lambda j: (0, j)),
                   once(s0_re.shape), once(s0_im.shape), once(hist_t.shape)],
        out_shape=[jax.ShapeDtypeStruct((rows, D_MODEL), F32),
                   jax.ShapeDtypeStruct((D_MODEL, D_MODEL), BF16),
                   jax.ShapeDtypeStruct(s0_re.shape, F32),
                   jax.ShapeDtypeStruct(s0_im.shape, F32),
                   jax.ShapeDtypeStruct(hist_t.shape, F32)],
        scratch_shapes=[pltpu.VMEM((rows, D_SSM), BF16), pltpu.VMEM((rows, D_POOL), BF16)],
        compiler_params=_cparams(1),
        name="mixer_sample",
    )(x, mod, w_out, proj, s0_re, s0_im, wz, m, apow, d_skip.reshape(N_OCT, 1, LANES), gluw,
      glu_b.reshape(N_OCT, 1, LANES), hist_t, pool_w, pool_b.reshape(1, D_POOL),
      pool_scale.reshape(1, D_POOL))


def kernel(x_prompt, x_sample, state_ssm_re, state_ssm_im, state_pool, c_prompt, c_sample, ada_w, ada_b, ffn1_norm, ffn1_w_gate, ffn1_w_up, ffn1_w_down, mix_norm, w_in, ssm_lambda_re, ssm_lambda_im, ssm_log_dt, ssm_b_re, ssm_b_im, ssm_c_re, ssm_c_im, ssm_d, ssm_glu_w, ssm_glu_b, pool_w, pool_b, pool_scale, w_out, ffn2_norm, ffn2_w_gate, ffn2_w_up, ffn2_w_down, final_norm):
    n_p, seq, _ = x_prompt.shape
    n_s = x_sample.shape[0]
    assert (n_p, n_s) == (N_PROMPT, N_SAMPLE) and n_p <= SUBLANES
    G, P = N_SSM_GROUPS, SSM_STATE

    c_all = jnp.concatenate([c_sample, c_prompt, jnp.zeros((SUBLANES - n_p, D_MODEL), F32)], axis=0)
    mod, ops = _adaln_s5_prep(c_all, ada_w[0], ada_b.reshape(1, N_MOD_COLS), ssm_lambda_re[0],
                              ssm_lambda_im[0], ssm_log_dt[0], ssm_b_re[0], ssm_b_im[0],
                              ssm_c_re[0], ssm_c_im[0], ssm_glu_w[0])
    pw = pool_w[0].astype(BF16)

    xs = x_sample.reshape(n_s, D_MODEL)
    xp = x_prompt.reshape(n_p * seq, D_MODEL)
    blocks_per_seq = seq // FFN_TM
    sample_rows = _RowGroup(n_s, n_s, per_row=True, first_block=0, resident=True)
    head_rows = _RowGroup(FFN_TM, FFN_ROW_CHUNK, per_row=False, first_block=0, resident=True)
    tail_rows = _RowGroup(FFN_TM, FFN_ROW_CHUNK, per_row=False, first_block=1, resident=False)

    def ffn(xs, xp, norm_g, sub, wg, wu, wd, last):
        ys, yp0, bg, bu, bd = _ffn([xs, xp], (sample_rows, head_rows), norm_g, mod, sub, wg, wu, wd,
                                   final_norm, n_blocks=1, tf=FFN_HEAD_TF, final_norm=last,
                                   blocks_per_seq=blocks_per_seq)
        (yp,) = _ffn([xp], (tail_rows,), norm_g, mod, sub, bg, bu, bd, final_norm,
                     n_blocks=n_p * blocks_per_seq - 1, tf=FFN_TF, final_norm=last,
                     blocks_per_seq=blocks_per_seq, out_rows=xp.shape[0], first_rows=yp0)
        return ys, yp

    xs, xp = ffn(xs, xp, ffn1_norm[0], FFN1_SUBLAYER, ffn1_w_gate[0], ffn1_w_up[0],
                 ffn1_w_down[0], False)

    proj_s, win = _proj_sample(xs, mix_norm[0], mod, w_in[0])
    xs, wout, sre_s, sim_s, pool_s_t = _mixer_sample(
        xs, proj_s, state_ssm_re[0].reshape(n_s, G * P), state_ssm_im[0].reshape(n_s, G * P),
        jnp.swapaxes(state_pool[0], 0, 1), ops, ssm_d[0], ssm_glu_b[0], pw, pool_b[0],
        pool_scale[0], mod, w_out[0])
    pool_s = jnp.swapaxes(pool_s_t, 0, 1)

    u4, yp_p, v_tail = _proj(xp, mix_norm[0], mod, win, pw, pool_b[0], pool_scale[0],
                             tm=PROJ_TM, n_seq=n_p)
    ys_p, sre_p, sim_p = _s5_prompt(u4.reshape(n_p * seq, D_SSM), ops, ssm_d[0], ssm_glu_b[0],
                                    n_seq=n_p)
    xp = _outproj(xp, ys_p.reshape(seq // T_CHUNK, n_p, T_CHUNK, D_SSM), yp_p, mod, wout,
                  tm=OUTPROJ_TM, n_seq=n_p)

    y_sample, y_prompt = ffn(xs, xp, ffn2_norm[0], FFN2_SUBLAYER, ffn2_w_gate[0], ffn2_w_up[0],
                             ffn2_w_down[0], True)

    pool_p = v_tail[:, POOL_TAIL - POOL_HIST:, :][None]
    return (y_prompt.reshape(n_p, seq, D_MODEL), y_sample.reshape(n_s, 1, D_MODEL),
            sre_p.reshape(1, n_p, G, P), sim_p.reshape(1, n_p, G, P), pool_p,
            sre_s.reshape(1, n_s, G, P), sim_s.reshape(1, n_s, G, P), pool_s[None])
```

```python
import functools
from typing import NamedTuple

import jax
import jax.numpy as jnp
from jax import lax
from jax.experimental import pallas as pl
from jax.experimental.pallas import tpu as pltpu

F32 = jnp.float32
BF16 = jnp.bfloat16

D_MODEL = 2048
D_FF = 5632
D_SSM = 1024
D_POOL = 1024
SSM_GROUP = 16
SSM_STATE = 64
N_SSM_GROUPS = 64
POOL_WINDOWS = (2, 4, 8, 16)
POOL_GROUP = 256
POOL_HIST = 15
N_MOD_COLS = 9 * D_MODEL
N_PROMPT = 4
N_SAMPLE = 128
EPS = 1e-6
FFN_RES = 0.5

LANES = 128
SUBLANES = 8
MXU_TILE = 256
VMEM_LIMIT = 60 * 1024 * 1024

PROMPT_MOD_BLOCK = N_SAMPLE // SUBLANES
NORM_ROWS = 32
FFN1_SUBLAYER, MIX_SUBLAYER, FFN2_SUBLAYER = 0, 1, 2

FFN_TM = 1024
FFN_ROW_CHUNK = 512
FFN_TF = 512
FFN_HEAD_TF = 256
PROJ_TM = 1024
OUTPROJ_TM = 512

OCT = LANES // SSM_GROUP
N_OCT = N_SSM_GROUPS // OCT
T_CHUNK = 8
OCT_STATE = OCT * SSM_STATE
CHUNK_K = T_CHUNK * LANES
TOK_PER_TILE = MXU_TILE // LANES
N_TOEP_TILES = CHUNK_K // MXU_TILE


def _toep_tile_row(nt):
    return MXU_TILE * (nt * (nt + 1) // 2)


TOEP_ROWS = _toep_tile_row(N_TOEP_TILES)


def _cparams(n_axes):
    return pltpu.CompilerParams(dimension_semantics=("arbitrary",) * n_axes,
                                vmem_limit_bytes=VMEM_LIMIT)


def _dot(a, b):
    return jnp.dot(a, b, preferred_element_type=F32)


ADALN_TN = 2048


def _adaln_kernel(c_ref, w_ref, b_ref, o_ref):
    c = c_ref[...]
    sc = (c * jax.nn.sigmoid(c)).astype(BF16)
    o_ref[...] = _dot(sc, w_ref[...].astype(BF16)) + b_ref[...]


def _norm_mod(x, g, shift, scale):
    ms = jnp.mean(x * x, axis=-1, keepdims=True)
    y = x * lax.rsqrt(ms + EPS) * g
    return y * (1.0 + scale) + shift


def _mod_specs(sub):
    return [pl.BlockSpec((SUBLANES, D_MODEL), lambda i, col=3 * sub + m: (PROMPT_MOD_BLOCK, col))
            for m in range(3)]


def _mod_rows(ref, blocks_per_seq):
    return ref[pl.ds(pl.program_id(0) // blocks_per_seq, 1), :]


class _RowGroup(NamedTuple):
    rows: int
    row_chunk: int
    per_row: bool
    first_block: int
    resident: bool


def _ffn_step(s, n_j, has_copy):
    t = jnp.maximum(s - 1, 0) if has_copy else s
    return t // n_j, t % n_j


def _ffn_kernel(*refs, groups, blocks_per_seq, n_j, final_norm, emit_bf16, has_copy):
    n = len(groups)
    g_ref, wg_ref, wu_ref, wd_ref, fg_ref = refs[4 * n:4 * n + 5]
    outs = refs[4 * n + 5 + int(has_copy):-1]
    h_ref = refs[-1]
    step = pl.program_id(0)
    i, j = _ffn_step(step, n_j, has_copy)
    last_j = n_j - 1
    active = step >= int(has_copy)

    if has_copy:
        @pl.when(step == 0)
        def _():
            outs[0][...] = refs[4 * n + 5][...]

    if emit_bf16:
        bf16_refs = outs[n:n + 3]
        for w_ref, wo_ref in zip((wg_ref, wu_ref, wd_ref), bf16_refs):
            wo_ref[...] = w_ref[...].astype(BF16)
        wg_ref, wu_ref, wd_ref = bf16_refs

    chunks, h_base = [], 0
    for k, grp in enumerate(groups):
        chunks += [(k, r, h_base + r) for r in range(0, grp.rows, grp.row_chunk)]
        h_base += grp.rows

    def mod_rows(k, ref, rows):
        if groups[k].per_row:
            return ref[rows, :]
        seq = (i + groups[k].first_block) // blocks_per_seq
        return ref[pl.ds(seq, 1), :]

    def norm_chunk(c):
        k, r, hr = chunks[c]
        x_ref, shift_ref, scale_ref, _ = refs[4 * k:4 * k + 4]
        for s in range(0, groups[k].row_chunk, NORM_ROWS):
            sr = pl.ds(r + s, NORM_ROWS)
            h = _norm_mod(x_ref[sr, :], g_ref[...], mod_rows(k, shift_ref, sr),
                          mod_rows(k, scale_ref, sr))
            h_ref[pl.ds(hr + s, NORM_ROWS), :] = h.astype(BF16)

    def chunk(c, first, last):
        k, r, hr = chunks[c]
        size = groups[k].row_chunk
        x_ref, _, _, gate_ref = refs[4 * k:4 * k + 4]
        o_ref = outs[k]
        rows = pl.ds(r, size)
        if first and c + 1 < len(chunks):
            norm_chunk(c + 1)
        h = h_ref[pl.ds(hr, size), :]
        g = _dot(h, wg_ref[...])
        u = _dot(h, wu_ref[...])
        a = (g * jax.nn.sigmoid(g) * u).astype(BF16)
        d = _dot(a, wd_ref[...])
        acc = d if first else o_ref[rows, :] + d
        if not last:
            o_ref[rows, :] = acc
            return
        o_ref[rows, :] = x_ref[rows, :] + FFN_RES * mod_rows(k, gate_ref, rows) * acc
        if final_norm:
            for s in range(0, size, NORM_ROWS):
                sr = pl.ds(r + s, NORM_ROWS)
                y = o_ref[sr, :]
                ms = jnp.mean(y * y, axis=-1, keepdims=True)
                o_ref[sr, :] = y * lax.rsqrt(ms + EPS) * fg_ref[...]

    def run(first, last):
        if first:
            norm_chunk(0)
        for c in range(len(chunks)):
            chunk(c, first, last)

    pl.when(jnp.logical_and(active, j == 0))(lambda: run(True, False))
    pl.when(jnp.logical_and(j > 0, j < last_j))(lambda: run(False, False))
    pl.when(j == last_j)(lambda: run(False, True))


def _ffn(xs, groups, norm_g, mod, sub, wg, wu, wd, final_g, *, n_blocks, tf, final_norm,
         blocks_per_seq, out_rows=None, first_rows=None):
    emit_bf16 = wg.dtype == F32
    has_copy = first_rows is not None
    assert not emit_bf16 or n_blocks == 1
    assert not has_copy or len(groups) == 1
    n_j = D_FF // tf
    step = functools.partial(_ffn_step, n_j=n_j, has_copy=has_copy)

    def row_spec(grp, col=0, out=False):
        mode = dict(pipeline_mode=pl.Buffered(1)) if grp.resident else {}
        first = 0 if out and not has_copy else grp.first_block

        def index(s):
            block = step(s)[0] + first
            return (jnp.where(s == 0, 0, block) if out and has_copy else block), col
        return pl.BlockSpec((grp.rows, D_MODEL), index, **mode)

    in_specs, operands = [], []
    for x, grp in zip(xs, groups):
        in_specs.append(row_spec(grp))
        operands.append(x)
        for m in range(3):
            col = 3 * sub + m
            if grp.per_row:
                in_specs.append(row_spec(grp, col))
            else:
                in_specs.append(pl.BlockSpec((SUBLANES, D_MODEL),
                                             lambda s, col=col: (PROMPT_MOD_BLOCK, col)))
            operands.append(mod)
    w_specs = [pl.BlockSpec((D_MODEL, tf), lambda s: (0, step(s)[1])),
               pl.BlockSpec((D_MODEL, tf), lambda s: (0, step(s)[1])),
               pl.BlockSpec((tf, D_MODEL), lambda s: (step(s)[1], 0))]
    vec_spec = pl.BlockSpec((1, D_MODEL), lambda s: (0, 0))
    in_specs += [vec_spec, *w_specs, vec_spec]
    operands += [norm_g.reshape(1, D_MODEL), wg, wu, wd, final_g.reshape(1, D_MODEL)]
    if has_copy:
        in_specs.append(pl.BlockSpec(first_rows.shape, lambda s: (0, 0),
                                     pipeline_mode=pl.Buffered(1)))
        operands.append(first_rows)

    out_specs = [row_spec(grp, out=True) for grp in groups]
    out_shape = [jax.ShapeDtypeStruct((out_rows or grp.rows, D_MODEL), F32) for grp in groups]
    if emit_bf16:
        out_specs += w_specs
        out_shape += [jax.ShapeDtypeStruct(w.shape, BF16) for w in (wg, wu, wd)]
    kern = functools.partial(_ffn_kernel, groups=groups, blocks_per_seq=blocks_per_seq, n_j=n_j,
                             final_norm=final_norm, emit_bf16=emit_bf16, has_copy=has_copy)
    return pl.pallas_call(
        kern,
        grid=(int(has_copy) + n_blocks * n_j,),
        in_specs=in_specs,
        out_specs=out_specs,
        out_shape=out_shape,
        scratch_shapes=[pltpu.VMEM((sum(grp.rows for grp in groups), D_MODEL), BF16)],
        compiler_params=_cparams(1),
        name="ffn",
    )(*operands)


POOL_TAIL = 16


def _proj_kernel(x_ref, g_ref, shift_ref, scale_ref, w_ref, pw_ref, pb_ref, ps_ref,
                 u_ref, y_ref, tail_out_ref, tail_ref, h_ref, *, blocks_per_seq):
    tm = x_ref.shape[0]
    half = tm // 2
    blk = pl.program_id(0) % blocks_per_seq
    sh = _mod_rows(shift_ref, blocks_per_seq)
    sc = _mod_rows(scale_ref, blocks_per_seq)

    @pl.when(blk == 0)
    def _():
        tail_ref[...] = jnp.zeros(tail_ref.shape, F32)

    def norm_half(r):
        for s in range(0, half, NORM_ROWS):
            rows = pl.ds(r * half + s, NORM_ROWS)
            h_ref[rows, :] = _norm_mod(x_ref[rows, :], g_ref[...], sh, sc).astype(BF16)

    norm_half(0)
    norm_half(1)
    pv = jnp.concatenate([_dot(h_ref[pl.ds(r * half, half), :], w_ref[:, D_SSM:])
                          for r in range(2)], axis=0)
    for r in range(2):
        pu = _dot(h_ref[pl.ds(r * half, half), :], w_ref[:, 0:D_SSM])
        u_ref[pl.ds(r * half // T_CHUNK, half // T_CHUNK), :, :] = pu.reshape(
            half // T_CHUNK, T_CHUNK, D_SSM)

    row_ext = lax.broadcasted_iota(jnp.int32, (POOL_TAIL + tm, POOL_GROUP), 0)
    pos = blk * tm + lax.broadcasted_iota(jnp.int32, (tm, POOL_GROUP), 0)
    for gi, w in enumerate(POOL_WINDOWS):
        lanes = slice(gi * POOL_GROUP, (gi + 1) * POOL_GROUP)
        v = pv[:, lanes]
        s = jnp.concatenate([tail_ref[:, lanes], v], axis=0)
        k = 1
        while k < w:
            s = s + jnp.where(row_ext >= k, pltpu.roll(s, k, axis=0), 0.0)
            k *= 2
        cnt = jnp.clip(pos + 1, 1, w).astype(F32)
        z = s[POOL_TAIL:, :] / cnt - v
        y_ref[:, lanes] = _pool_linear(z, gi, pw_ref, pb_ref, ps_ref).astype(y_ref.dtype)
    tail = pv[tm - POOL_TAIL:, :]
    tail_ref[...] = tail
    tail_out_ref[...] = tail


def _proj(x, norm_g, mod, w_in, pool_w, pool_b, pool_scale, *, tm, n_seq):
    rows = x.shape[0]
    seq_len = rows // n_seq
    blocks_per_seq = seq_len // tm
    u_shape = (seq_len // T_CHUNK, n_seq, T_CHUNK, D_SSM)
    u_spec = pl.BlockSpec((tm // T_CHUNK, None, T_CHUNK, D_SSM),
                          lambda i: (i % blocks_per_seq, i // blocks_per_seq, 0, 0))
    shift_spec, scale_spec, _ = _mod_specs(MIX_SUBLAYER)
    vec_spec = pl.BlockSpec((1, D_POOL), lambda i: (0, 0))
    return pl.pallas_call(
        functools.partial(_proj_kernel, blocks_per_seq=blocks_per_seq),
        grid=(rows // tm,),
        in_specs=[pl.BlockSpec((tm, D_MODEL), lambda i: (i, 0)),
                  pl.BlockSpec((1, D_MODEL), lambda i: (0, 0)),
                  shift_spec, scale_spec,
                  pl.BlockSpec((D_MODEL, D_MODEL), lambda i: (0, 0),
                               pipeline_mode=pl.Buffered(1)),
                  pl.BlockSpec(pool_w.shape, lambda i: (0, 0, 0)), vec_spec, vec_spec],
        out_specs=[u_spec, pl.BlockSpec((tm, D_POOL), lambda i: (i, 0)),
                   pl.BlockSpec((None, POOL_TAIL, D_POOL), lambda i: (i // blocks_per_seq, 0, 0))],
        out_shape=[jax.ShapeDtypeStruct(u_shape, F32),
                   jax.ShapeDtypeStruct((rows, D_POOL), BF16),
                   jax.ShapeDtypeStruct((n_seq, POOL_TAIL, D_POOL), F32)],
        scratch_shapes=[pltpu.VMEM((POOL_TAIL, D_POOL), F32), pltpu.VMEM((tm, D_MODEL), BF16)],
        compiler_params=_cparams(1),
        name="proj",
    )(x, norm_g.reshape(1, D_MODEL), mod, mod, w_in, pool_w, pool_b.reshape(1, D_POOL),
      pool_scale.reshape(1, D_POOL))


SAMPLE_TN = 512


def _proj_sample_kernel(x_ref, g_ref, shift_ref, scale_ref, w_ref, p_ref, wo_ref, h_ref):
    @pl.when(pl.program_id(0) == 0)
    def _():
        h = _norm_mod(x_ref[...], g_ref[...], shift_ref[...], scale_ref[...])
        h_ref[...] = h.astype(BF16)
    wo_ref[...] = w_ref[...].astype(BF16)
    p_ref[...] = _dot(h_ref[...], wo_ref[...])


def _proj_sample(x, norm_g, mod, w_in):
    rows = x.shape[0]
    shift_col, scale_col = 3 * MIX_SUBLAYER, 3 * MIX_SUBLAYER + 1
    return pl.pallas_call(
        _proj_sample_kernel,
        grid=(D_MODEL // SAMPLE_TN,),
        in_specs=[pl.BlockSpec((rows, D_MODEL), lambda j: (0, 0)),
                  pl.BlockSpec((1, D_MODEL), lambda j: (0, 0)),
                  pl.BlockSpec((rows, D_MODEL), lambda j: (0, shift_col)),
                  pl.BlockSpec((rows, D_MODEL), lambda j: (0, scale_col)),
                  pl.BlockSpec((D_MODEL, SAMPLE_TN), lambda j: (0, j))],
        out_specs=[pl.BlockSpec((rows, SAMPLE_TN), lambda j: (0, j)),
                   pl.BlockSpec((D_MODEL, SAMPLE_TN), lambda j: (0, j))],
        out_shape=[jax.ShapeDtypeStruct((rows, D_MODEL), F32),
                   jax.ShapeDtypeStruct((D_MODEL, D_MODEL), BF16)],
        scratch_shapes=[pltpu.VMEM((rows, D_MODEL), BF16)],
        compiler_params=_cparams(1),
        name="proj_sample",
    )(x, norm_g.reshape(1, D_MODEL), mod, mod, w_in)


def _discretise(lam_re, lam_im, log_dt):
    lr = jnp.minimum(lam_re, -1e-4)
    li = lam_im
    dt = jnp.exp(log_dt)
    mag = jnp.exp(lr * dt)
    ang = li * dt
    a_re = mag * jnp.cos(ang)
    a_im = mag * jnp.sin(ang)
    den = lr * lr + li * li
    num_re = a_re - 1.0
    f_re = (num_re * lr + a_im * li) / den
    f_im = (a_im * lr - num_re * li) / den
    return a_re, a_im, f_re, f_im


def _s5_prep_kernel(lam1_ref, c1_ref, lam2_ref, b2_ref, gluw_t,
                    toep_ref, wz_ref, m_ref, apow_ref, gluw_ref):
    g1 = lax.broadcasted_iota(jnp.int32, (SSM_STATE, LANES), 1) // SSM_GROUP
    g2 = lax.broadcasted_iota(jnp.int32, (SSM_GROUP, OCT_STATE), 1) // SSM_STATE

    def expand1(x):
        return jnp.concatenate([jnp.where(g1 == g, x, 0.0) for g in range(OCT)], axis=0)

    def expand2(x):
        return jnp.concatenate([jnp.where(g2 == g, x, 0.0) for g in range(OCT)], axis=0)

    def split(x):
        hi = x.astype(BF16)
        return hi, (x - hi.astype(F32)).astype(BF16)

    a1_re, a1_im, _, _ = _discretise(lam1_ref[0], lam1_ref[1], lam1_ref[2])
    a2_re, a2_im, f_re, f_im = _discretise(lam2_ref[0], lam2_ref[1], lam2_ref[2])
    br, bi = b2_ref[0], b2_ref[1]
    bbar_re = f_re * br - f_im * bi
    bbar_im = f_re * bi + f_im * br
    cr, ci = c1_ref[0], c1_ref[1]
    bre_hi, bre_lo = split(expand2(bbar_re))
    bim_hi, bim_lo = split(expand2(bbar_im))

    def dot3(x_hi, x_lo, y_hi, y_lo):
        return _dot(x_hi, y_hi) + (_dot(x_hi, y_lo) + _dot(x_lo, y_hi))

    p1_re, p1_im = jnp.ones_like(a1_re), jnp.zeros_like(a1_im)
    p2_re, p2_im = jnp.ones_like(a2_re), jnp.zeros_like(a2_im)
    kk = []
    zero_blk = jnp.zeros((LANES, LANES), F32)
    for j in range(T_CHUNK + 1):
        care_hi, care_lo = split(expand1(cr * p1_re - ci * p1_im))
        caim_hi, caim_lo = split(expand1(cr * p1_im + ci * p1_re))
        m_ref[0:OCT_STATE, j * LANES:(j + 1) * LANES] = care_hi
        m_ref[OCT_STATE:2 * OCT_STATE, j * LANES:(j + 1) * LANES] = -caim_hi
        if j < T_CHUNK:
            t = T_CHUNK - 1 - j
            ba_re = expand2(bbar_re * p2_re - bbar_im * p2_im)
            ba_im = expand2(bbar_re * p2_im + bbar_im * p2_re)
            wz_ref[t * LANES:(t + 1) * LANES, 0:OCT_STATE] = ba_re.astype(BF16)
            wz_ref[t * LANES:(t + 1) * LANES, OCT_STATE:2 * OCT_STATE] = ba_im.astype(BF16)
            kk.append(dot3(bre_hi, bre_lo, care_hi, care_lo)
                      - dot3(bim_hi, bim_lo, caim_hi, caim_lo))
        if j == 1:
            apow_ref[0:1, :] = p2_re
            apow_ref[1:2, :] = p2_im
        if j == T_CHUNK:
            apow_ref[2:3, :] = p2_re
            apow_ref[3:4, :] = p2_im
        p1_re, p1_im = p1_re * a1_re - p1_im * a1_im, p1_re * a1_im + p1_im * a1_re
        p2_re, p2_im = p2_re * a2_re - p2_im * a2_im, p2_re * a2_im + p2_im * a2_re

    for t2 in range(T_CHUNK):
        nt, col = divmod(t2, TOK_PER_TILE)
        for t in range((nt + 1) * TOK_PER_TILE):
            blk = kk[t2 - t] if t2 >= t else zero_blk
            row = _toep_tile_row(nt) + t * LANES
            toep_ref[row:row + LANES, col * LANES:(col + 1) * LANES] = blk.astype(BF16)

    rg = lax.broadcasted_iota(jnp.int32, (LANES, LANES), 0) // SSM_GROUP
    lg = lax.broadcasted_iota(jnp.int32, (LANES, LANES), 1) // SSM_GROUP
    gw = jnp.where(rg == lg, gluw_t[...], 0.0).astype(BF16)
    gz = jnp.zeros((LANES, LANES), BF16)
    for a in range(TOK_PER_TILE):
        for b in range(TOK_PER_TILE):
            gluw_ref[a * LANES:(a + 1) * LANES, b * LANES:(b + 1) * LANES] = gw if a == b else gz


def _adaln_s5_prep_kernel(c_ref, w_ref, b_ref, *refs):
    n_prep_in = len(refs) - 6
    _adaln_kernel(c_ref, w_ref, b_ref, refs[n_prep_in])

    @pl.when(pl.program_id(0) < N_OCT)
    def _():
        _s5_prep_kernel(*refs[:n_prep_in], *refs[n_prep_in + 1:])


def _adaln_s5_prep(c_all, ada_w, ada_b, lam_re, lam_im, log_dt, b_re, b_im, c_re, c_im, glu_w):
    G, P, H = N_SSM_GROUPS, SSM_STATE, SSM_GROUP
    rows, n_cols = c_all.shape[0], ada_w.shape[1]
    n_steps = n_cols // ADALN_TN
    assert n_steps >= N_OCT
    lam = jnp.stack([lam_re, lam_im, jnp.broadcast_to(log_dt[:, None], (G, P))])
    lam2 = jnp.swapaxes(lam.reshape(3, N_OCT, 1, OCT_STATE), 0, 1)
    lam1 = jnp.transpose(lam.reshape(3, N_OCT, OCT, P), (1, 0, 3, 2))
    lam1 = jnp.broadcast_to(lam1[..., None], (N_OCT, 3, P, OCT, H)).reshape(N_OCT, 3, P, LANES)
    c1 = jnp.transpose(jnp.stack([c_re, c_im]).reshape(2, N_OCT, LANES, P), (1, 0, 3, 2))
    b2 = jnp.transpose(jnp.stack([b_re, b_im]).reshape(2, N_OCT, OCT_STATE, H), (1, 0, 3, 2))
    gw = glu_w.reshape(N_OCT, OCT, H, 1, H)
    gluw_t = jnp.broadcast_to(gw, (N_OCT, OCT, H, OCT, H)).reshape(N_OCT, LANES, LANES)

    def spec(shape):
        return pl.BlockSpec((None,) + shape,
                            lambda j: (jnp.minimum(j, N_OCT - 1),) + (0,) * len(shape))

    m_cols = (T_CHUNK + 1) * LANES
    mod, *ops = pl.pallas_call(
        _adaln_s5_prep_kernel,
        grid=(n_steps,),
        in_specs=[pl.BlockSpec((rows, D_MODEL), lambda j: (0, 0)),
                  pl.BlockSpec((D_MODEL, ADALN_TN), lambda j: (0, j)),
                  pl.BlockSpec((1, ADALN_TN), lambda j: (0, j))]
                 + [spec((3, SSM_STATE, LANES)), spec((2, SSM_STATE, LANES)),
                    spec((3, 1, OCT_STATE)), spec((2, SSM_GROUP, OCT_STATE)),
                    spec((LANES, LANES))],
        out_specs=[pl.BlockSpec((rows, ADALN_TN), lambda j: (0, j)),
                   spec((TOEP_ROWS, MXU_TILE)), spec((CHUNK_K, 2 * OCT_STATE)),
                   spec((2 * OCT_STATE, m_cols)), spec((4, OCT_STATE)),
                   spec((MXU_TILE, MXU_TILE))],
        out_shape=[jax.ShapeDtypeStruct((rows, n_cols), F32),
                   jax.ShapeDtypeStruct((N_OCT, TOEP_ROWS, MXU_TILE), BF16),
                   jax.ShapeDtypeStruct((N_OCT, CHUNK_K, 2 * OCT_STATE), BF16),
                   jax.ShapeDtypeStruct((N_OCT, 2 * OCT_STATE, m_cols), BF16),
                   jax.ShapeDtypeStruct((N_OCT, 4, OCT_STATE), F32),
                   jax.ShapeDtypeStruct((N_OCT, MXU_TILE, MXU_TILE), BF16)],
        compiler_params=_cparams(1),
        name="adaln_s5_prep",
    )(c_all, ada_w, ada_b, lam1, c1, lam2, b2, gluw_t)
    return mod, tuple(ops)


GELU_C0 = 0.7978845608028654
GELU_C1 = GELU_C0 * 0.044715


def _glu_out(y, gluw, glub):
    gy = y * (0.5 + 0.5 * jnp.tanh(y * (GELU_C0 + GELU_C1 * (y * y))))
    z = _dot(gy.astype(BF16), gluw) + glub
    return gy * (0.5 + 0.5 * jnp.tanh(0.5 * z))


def _s5_prompt_kernel(x_ref, toep_ref, wz_ref, m_ref, apow_ref, d_ref, gluw_ref, glub_ref,
                      y_ref, sre_ref, sim_ref, xr_ref, z_ref, yv_ref, *, n_seq, row_chunk):
    n_rows = x_ref.shape[0] // T_CHUNK
    n_blk = n_rows // row_chunk

    def tokens(b, t):
        return pl.ds(b * (row_chunk * T_CHUNK) + t, row_chunk, stride=T_CHUNK)

    for b in range(n_blk):
        rows = pl.ds(b * row_chunk, row_chunk)
        for t in range(T_CHUNK):
            xr_ref[rows, t * LANES:(t + 1) * LANES] = x_ref[tokens(b, t), :].astype(BF16)
        xr = xr_ref[rows, :]
        z_ref[rows, :] = _dot(xr, wz_ref[...])
        for nt in range(N_TOEP_TILES):
            k_end = (nt + 1) * MXU_TILE
            tile = toep_ref[_toep_tile_row(nt):_toep_tile_row(nt) + k_end, :]
            yv_ref[rows, nt * MXU_TILE:k_end] = _dot(xr[:, 0:k_end], tile)

    are, aim = apow_ref[2:3, :], apow_ref[3:4, :]
    lo = lax.broadcasted_iota(jnp.int32, (SUBLANES, OCT_STATE), 0) < n_seq

    def step(k, carry):
        pre, pim = carry
        rows = pl.ds(pl.multiple_of(k * SUBLANES, SUBLANES), SUBLANES)
        zre = z_ref[rows, 0:OCT_STATE]
        zim = z_ref[rows, OCT_STATE:2 * OCT_STATE]
        w1re = are * pre - aim * pim + zre
        w1im = are * pim + aim * pre + zim
        r1re = pltpu.roll(w1re, n_seq, axis=0)
        r1im = pltpu.roll(w1im, n_seq, axis=0)
        w2re = are * r1re - aim * r1im + zre
        w2im = are * r1im + aim * r1re + zim
        z_ref[rows, 0:OCT_STATE] = jnp.where(lo, pre, r1re)
        z_ref[rows, OCT_STATE:2 * OCT_STATE] = jnp.where(lo, pim, r1im)
        nre = jnp.where(lo, pltpu.roll(w2re, n_seq, axis=0), w2re)
        nim = jnp.where(lo, pltpu.roll(w2im, n_seq, axis=0), w2im)
        return nre, nim

    zeros = jnp.zeros((SUBLANES, OCT_STATE), F32)
    fre, fim = lax.fori_loop(0, n_rows // SUBLANES, step, (zeros, zeros))
    sre_ref[...] = fre[0:n_seq, :]
    sim_ref[...] = fim[0:n_seq, :]

    d2 = jnp.concatenate([d_ref[...]] * TOK_PER_TILE, axis=1)
    glub2 = jnp.concatenate([glub_ref[...]] * TOK_PER_TILE, axis=1)

    for b in range(n_blk):
        rows = pl.ds(b * row_chunk, row_chunk)
        y = yv_ref[rows, :] + _dot(z_ref[rows, :].astype(BF16), m_ref[:, LANES:])
        for t0 in range(0, T_CHUNK, TOK_PER_TILE):
            toks = [tokens(b, t0 + i) for i in range(TOK_PER_TILE)]
            u = jnp.concatenate([x_ref[tok, :] for tok in toks], axis=1)
            yt = y[:, t0 * LANES:(t0 + TOK_PER_TILE) * LANES] + d2 * u
            out = _glu_out(yt, gluw_ref[...], glub2)
            for i, tok in enumerate(toks):
                y_ref[tok, :] = out[:, i * LANES:(i + 1) * LANES]


def _s5_prompt(u_flat, ops, d_skip, glu_b, *, n_seq):
    toep, wz, m, apow, gluw = ops
    assert 2 * n_seq == SUBLANES
    rows = u_flat.shape[0]
    n_rows = rows // T_CHUNK
    oct_spec = lambda shape: pl.BlockSpec((None,) + shape, lambda o: (o, 0, 0))
    kern = functools.partial(_s5_prompt_kernel, n_seq=n_seq, row_chunk=256)
    return pl.pallas_call(
        kern,
        grid=(N_OCT,),
        in_specs=[pl.BlockSpec((rows, LANES), lambda o: (0, o)),
                  oct_spec(toep.shape[1:]), oct_spec(wz.shape[1:]), oct_spec(m.shape[1:]),
                  oct_spec(apow.shape[1:]), oct_spec((1, LANES)), oct_spec(gluw.shape[1:]),
                  oct_spec((1, LANES))],
        out_specs=[pl.BlockSpec((rows, LANES), lambda o: (0, o)),
                   pl.BlockSpec((n_seq, OCT_STATE), lambda o: (0, o)),
                   pl.BlockSpec((n_seq, OCT_STATE), lambda o: (0, o))],
        out_shape=[jax.ShapeDtypeStruct((rows, D_SSM), F32),
                   jax.ShapeDtypeStruct((n_seq, N_OCT * OCT_STATE), F32),
                   jax.ShapeDtypeStruct((n_seq, N_OCT * OCT_STATE), F32)],
        scratch_shapes=[pltpu.VMEM((n_rows, CHUNK_K), BF16),
                        pltpu.VMEM((n_rows, 2 * OCT_STATE), F32),
                        pltpu.VMEM((n_rows, CHUNK_K), F32)],
        compiler_params=_cparams(1),
        name="s5_prompt",
    )(u_flat, toep, wz, m, apow, d_skip.reshape(N_OCT, 1, LANES), gluw,
      glu_b.reshape(N_OCT, 1, LANES))


def _s5_sample_kernel(u_ref, s0re_ref, s0im_ref, wz_ref, m_ref, apow_ref, d_ref, gluw_ref,
                      glub_ref, y_ref, sre_ref, sim_ref):
    for o in range(N_OCT):
        ch = slice(o * LANES, (o + 1) * LANES)
        st = slice(o * OCT_STATE, (o + 1) * OCT_STATE)
        u = u_ref[:, ch]
        z = _dot(u.astype(BF16), wz_ref[o])
        are, aim = apow_ref[o, 0:1, :], apow_ref[o, 1:2, :]
        s0re, s0im = s0re_ref[:, st], s0im_ref[:, st]
        nre = are * s0re - aim * s0im + z[:, 0:OCT_STATE]
        nim = are * s0im + aim * s0re + z[:, OCT_STATE:2 * OCT_STATE]
        sre_ref[:, st] = nre
        sim_ref[:, st] = nim
        y = (_dot(nre.astype(BF16), m_ref[o, 0:OCT_STATE, :])
             + _dot(nim.astype(BF16), m_ref[o, OCT_STATE:2 * OCT_STATE, :])
             + d_ref[o] * u)
        y_ref[:, ch] = _glu_out(y, gluw_ref[o], glub_ref[o]).astype(y_ref.dtype)


def _pool_linear(z, gi, pw_ref, pb_ref, ps_ref):
    lanes = slice(gi * POOL_GROUP, (gi + 1) * POOL_GROUP)
    return (_dot(z.astype(BF16), pw_ref[gi]) + pb_ref[:, lanes]) * ps_ref[:, lanes]


def _pool_sample_kernel(hist_ref, v_ref, pw_ref, pb_ref, ps_ref, y_ref, new_ref):
    new_ref[0:POOL_HIST - 1] = hist_ref[1:POOL_HIST]
    new_ref[POOL_HIST - 1] = v_ref[...]
    for gi, w in enumerate(POOL_WINDOWS):
        lanes = slice(gi * POOL_GROUP, (gi + 1) * POOL_GROUP)
        v = v_ref[:, lanes]
        s = v
        for r in range(POOL_HIST - (w - 1), POOL_HIST):
            s = s + hist_ref[r, :, lanes]
        y = _pool_linear(s / float(w) - v, gi, pw_ref, pb_ref, ps_ref)
        y_ref[:, lanes] = y.astype(y_ref.dtype)


def _outproj_kernel(x_ref, ys_ref, yp_ref, gate_ref, w_ref, o_ref, *, blocks_per_seq):
    tm = x_ref.shape[0]
    ys = ys_ref[...].reshape(tm, D_SSM).astype(BF16)
    yp = yp_ref[...].astype(BF16)
    mix = _dot(ys, w_ref[0:D_SSM, :]) + _dot(yp, w_ref[D_SSM:, :])
    o_ref[...] = x_ref[...] + _mod_rows(gate_ref, blocks_per_seq) * mix


def _outproj(x, ys, yp, mod, w_out, *, tm, n_seq):
    rows = x.shape[0]
    blocks_per_seq = (rows // n_seq) // tm
    ys_spec = pl.BlockSpec((tm // T_CHUNK, None, T_CHUNK, D_SSM),
                           lambda i: (i % blocks_per_seq, i // blocks_per_seq, 0, 0))
    gate_spec = _mod_specs(MIX_SUBLAYER)[2]
    return pl.pallas_call(
        functools.partial(_outproj_kernel, blocks_per_seq=blocks_per_seq),
        grid=(rows // tm,),
        in_specs=[pl.BlockSpec((tm, D_MODEL), lambda i: (i, 0)),
                  ys_spec,
                  pl.BlockSpec((tm, D_POOL), lambda i: (i, 0)),
                  gate_spec,
                  pl.BlockSpec((D_MODEL, D_MODEL), lambda i: (0, 0),
                               pipeline_mode=pl.Buffered(1))],
        out_specs=pl.BlockSpec((tm, D_MODEL), lambda i: (i, 0)),
        out_shape=jax.ShapeDtypeStruct((rows, D_MODEL), F32),
        compiler_params=_cparams(1),
        name="outproj",
    )(x, ys, yp, mod, w_out)


def _mixer_sample_kernel(x_ref, gate_ref, w_ref, proj_ref, s0re_ref, s0im_ref, wz_ref, m_ref,
                         apow_ref, d_ref, gluw_ref, glub_ref, hist_ref, pw_ref, pb_ref, ps_ref,
                         o_ref, wo_ref, sre_ref, sim_ref, new_hist_ref, ys_ref, yp_ref):
    @pl.when(pl.program_id(0) == 0)
    def _():
        _s5_sample_kernel(proj_ref, s0re_ref, s0im_ref, wz_ref, m_ref, apow_ref, d_ref, gluw_ref,
                          glub_ref, ys_ref, sre_ref, sim_ref)
        _pool_sample_kernel(hist_ref, proj_ref.at[:, pl.ds(D_SSM, D_POOL)], pw_ref, pb_ref, ps_ref,
                            yp_ref, new_hist_ref)

    wo_ref[...] = w_ref[...].astype(BF16)
    mix = _dot(ys_ref[...], wo_ref[0:D_SSM, :]) + _dot(yp_ref[...], wo_ref[D_SSM:, :])
    o_ref[...] = x_ref[...] + gate_ref[...] * mix


def _mixer_sample(x, proj, s0_re, s0_im, hist_t, ops, d_skip, glu_b, pool_w, pool_b, pool_scale,
                  mod, w_out):
    _, wz, m, apow, gluw = ops
    rows = x.shape[0]
    n_col = D_MODEL // SAMPLE_TN
    gate_col = (3 * MIX_SUBLAYER + 2) * n_col

    def once(shape, index=None):
        index = index or (0,) * len(shape)
        return pl.BlockSpec(shape, lambda j: index, pipeline_mode=pl.Buffered(1))

    return pl.pallas_call(
        _mixer_sample_kernel,
        grid=(n_col,),
        in_specs=[pl.BlockSpec((rows, SAMPLE_TN), lambda j: (0, j)),
                  pl.BlockSpec((rows, SAMPLE_TN), lambda j: (0, gate_col + j)),
                  pl.BlockSpec((D_MODEL, SAMPLE_TN), lambda j: (0, j)),
                  once(proj.shape), once(s0_re.shape), once(s0_im.shape),
                  once((N_OCT, LANES, 2 * OCT_STATE), (0, T_CHUNK - 1, 0)),
                  once((N_OCT, 2 * OCT_STATE, LANES)),
                  once(apow.shape), once((N_OCT, 1, LANES)), once((N_OCT, LANES, LANES)),
                  once((N_OCT, 1, LANES)), once(hist_t.shape), once(pool_w.shape),
                  once((1, D_POOL)), once((1, D_POOL))],
        out_specs=[pl.BlockSpec((rows, SAMPLE_TN), lambda j: (0, j)),
                   pl.BlockSpec((D_MODEL, SAMPLE_TN), lambda j: (0, j)),
                   once(s0_re.shape), once(s0_im.shape), once(hist_t.shape)],
        out_shape=[jax.ShapeDtypeStruct((rows, D_MODEL), F32),
                   jax.ShapeDtypeStruct((D_MODEL, D_MODEL), BF16),
                   jax.ShapeDtypeStruct(s0_re.shape, F32),
                   jax.ShapeDtypeStruct(s0_im.shape, F32),
                   jax.ShapeDtypeStruct(hist_t.shape, F32)],
        scratch_shapes=[pltpu.VMEM((rows, D_SSM), BF16), pltpu.VMEM((rows, D_POOL), BF16)],
        compiler_params=_cparams(1),
        name="mixer_sample",
    )(x, mod, w_out, proj, s0_re, s0_im, wz, m, apow, d_skip.reshape(N_OCT, 1, LANES), gluw,
      glu_b.reshape(N_OCT, 1, LANES), hist_t, pool_w, pool_b.reshape(1, D_POOL),
      pool_scale.reshape(1, D_POOL))


def kernel(x_prompt, x_sample, state_ssm_re, state_ssm_im, state_pool, c_prompt, c_sample, ada_w, ada_b, ffn1_norm, ffn1_w_gate, ffn1_w_up, ffn1_w_down, mix_norm, w_in, ssm_lambda_re, ssm_lambda_im, ssm_log_dt, ssm_b_re, ssm_b_im, ssm_c_re, ssm_c_im, ssm_d, ssm_glu_w, ssm_glu_b, pool_w, pool_b, pool_scale, w_out, ffn2_norm, ffn2_w_gate, ffn2_w_up, ffn2_w_down, final_norm):
    n_p, seq, _ = x_prompt.shape
    n_s = x_sample.shape[0]
    assert (n_p, n_s) == (N_PROMPT, N_SAMPLE) and n_p <= SUBLANES
    G, P = N_SSM_GROUPS, SSM_STATE

    c_all = jnp.concatenate([c_sample, c_prompt, jnp.zeros((SUBLANES - n_p, D_MODEL), F32)], axis=0)
    mod, ops = _adaln_s5_prep(c_all, ada_w[0], ada_b.reshape(1, N_MOD_COLS), ssm_lambda_re[0],
                              ssm_lambda_im[0], ssm_log_dt[0], ssm_b_re[0], ssm_b_im[0],
                              ssm_c_re[0], ssm_c_im[0], ssm_glu_w[0])
    pw = pool_w[0].astype(BF16)

    xs = x_sample.reshape(n_s, D_MODEL)
    xp = x_prompt.reshape(n_p * seq, D_MODEL)
    blocks_per_seq = seq // FFN_TM
    sample_rows = _RowGroup(n_s, n_s, per_row=True, first_block=0, resident=True)
    head_rows = _RowGroup(FFN_TM, FFN_ROW_CHUNK, per_row=False, first_block=0, resident=True)
    tail_rows = _RowGroup(FFN_TM, FFN_ROW_CHUNK, per_row=False, first_block=1, resident=False)

    def ffn(xs, xp, norm_g, sub, wg, wu, wd, last):
        ys, yp0, bg, bu, bd = _ffn([xs, xp], (sample_rows, head_rows), norm_g, mod, sub, wg, wu, wd,
                                   final_norm, n_blocks=1, tf=FFN_HEAD_TF, final_norm=last,
                                   blocks_per_seq=blocks_per_seq)
        (yp,) = _ffn([xp], (tail_rows,), norm_g, mod, sub, bg, bu, bd, final_norm,
                     n_blocks=n_p * blocks_per_seq - 1, tf=FFN_TF, final_norm=last,
                     blocks_per_seq=blocks_per_seq, out_rows=xp.shape[0], first_rows=yp0)
        return ys, yp

    xs, xp = ffn(xs, xp, ffn1_norm[0], FFN1_SUBLAYER, ffn1_w_gate[0], ffn1_w_up[0],
                 ffn1_w_down[0], False)

    proj_s, win = _proj_sample(xs, mix_norm[0], mod, w_in[0])
    xs, wout, sre_s, sim_s, pool_s_t = _mixer_sample(
        xs, proj_s, state_ssm_re[0].reshape(n_s, G * P), state_ssm_im[0].reshape(n_s, G * P),
        jnp.swapaxes(state_pool[0], 0, 1), ops, ssm_d[0], ssm_glu_b[0], pw, pool_b[0],
        pool_scale[0], mod, w_out[0])
    pool_s = jnp.swapaxes(pool_s_t, 0, 1)

    u4, yp_p, v_tail = _proj(xp, mix_norm[0], mod, win, pw, pool_b[0], pool_scale[0],
                             tm=PROJ_TM, n_seq=n_p)
    ys_p, sre_p, sim_p = _s5_prompt(u4.reshape(n_p * seq, D_SSM), ops, ssm_d[0], ssm_glu_b[0],
                                    n_seq=n_p)
    xp = _outproj(xp, ys_p.reshape(seq // T_CHUNK, n_p, T_CHUNK, D_SSM), yp_p, mod, wout,
                  tm=OUTPROJ_TM, n_seq=n_p)

    y_sample, y_prompt = ffn(xs, xp, ffn2_norm[0], FFN2_SUBLAYER, ffn2_w_gate[0], ffn2_w_up[0],
                             ffn2_w_down[0], True)

    pool_p = v_tail[:, POOL_TAIL - POOL_HIST:, :][None]
    return (y_prompt.reshape(n_p, seq, D_MODEL), y_sample.reshape(n_s, 1, D_MODEL),
            sre_p.reshape(1, n_p, G, P), sim_p.reshape(1, n_p, G, P), pool_p,
            sre_s.reshape(1, n_s, G, P), sim_s.reshape(1, n_s, G, P), pool_s[None])
```

```python
import functools
from typing import NamedTuple

import jax
import jax.numpy as jnp
from jax import lax
from jax.experimental import pallas as pl
from jax.experimental.pallas import tpu as pltpu

F32 = jnp.float32
BF16 = jnp.bfloat16

D_MODEL = 2048
D_FF = 5632
D_SSM = 1024
D_POOL = 1024
SSM_GROUP = 16
SSM_STATE = 64
N_SSM_GROUPS = 64
POOL_WINDOWS = (2, 4, 8, 16)
POOL_GROUP = 256
POOL_HIST = 15
N_MOD_COLS = 9 * D_MODEL
N_PROMPT = 4
N_SAMPLE = 128
EPS = 1e-6
FFN_RES = 0.5

LANES = 128
SUBLANES = 8
MXU_TILE = 256
VMEM_LIMIT = 60 * 1024 * 1024

PROMPT_MOD_BLOCK = N_SAMPLE // SUBLANES
NORM_ROWS = 32
FFN1_SUBLAYER, MIX_SUBLAYER, FFN2_SUBLAYER = 0, 1, 2

FFN_TM = 1024
FFN_ROW_CHUNK = 512
FFN_TF = 512
FFN_HEAD_TF = 256
PROJ_TM = 1024
OUTPROJ_TM = 1024

OCT = LANES // SSM_GROUP
N_OCT = N_SSM_GROUPS // OCT
T_CHUNK = 8
OCT_STATE = OCT * SSM_STATE
CHUNK_K = T_CHUNK * LANES
TOK_PER_TILE = MXU_TILE // LANES
N_TOEP_TILES = CHUNK_K // MXU_TILE


def _toep_tile_row(nt):
    return MXU_TILE * (nt * (nt + 1) // 2)


TOEP_ROWS = _toep_tile_row(N_TOEP_TILES)


def _cparams(n_axes):
    return pltpu.CompilerParams(dimension_semantics=("arbitrary",) * n_axes,
                                vmem_limit_bytes=VMEM_LIMIT)


def _dot(a, b):
    return jnp.dot(a, b, preferred_element_type=F32)


ADALN_TN = 2048


def _adaln_kernel(c_ref, w_ref, b_ref, o_ref):
    c = c_ref[...]
    sc = (c * jax.nn.sigmoid(c)).astype(BF16)
    o_ref[...] = _dot(sc, w_ref[...].astype(BF16)) + b_ref[...]


def _norm_mod(x, g, shift, scale):
    ms = jnp.mean(x * x, axis=-1, keepdims=True)
    y = x * lax.rsqrt(ms + EPS) * g
    return y * (1.0 + scale) + shift


def _mod_specs(sub):
    return [pl.BlockSpec((SUBLANES, D_MODEL), lambda i, col=3 * sub + m: (PROMPT_MOD_BLOCK, col))
            for m in range(3)]


def _mod_rows(ref, blocks_per_seq):
    return ref[pl.ds(pl.program_id(0) // blocks_per_seq, 1), :]


class _RowGroup(NamedTuple):
    rows: int
    row_chunk: int
    per_row: bool
    first_block: int
    resident: bool


def _ffn_step(s, n_j, has_copy):
    t = jnp.maximum(s - 1, 0) if has_copy else s
    return t // n_j, t % n_j


def _ffn_kernel(*refs, groups, blocks_per_seq, n_j, final_norm, emit_bf16, has_copy):
    n = len(groups)
    g_ref, wg_ref, wu_ref, wd_ref, fg_ref = refs[4 * n:4 * n + 5]
    outs = refs[4 * n + 5 + int(has_copy):-1]
    h_ref = refs[-1]
    step = pl.program_id(0)
    i, j = _ffn_step(step, n_j, has_copy)
    last_j = n_j - 1
    active = step >= int(has_copy)

    if has_copy:
        @pl.when(step == 0)
        def _():
            outs[0][...] = refs[4 * n + 5][...]

    if emit_bf16:
        bf16_refs = outs[n:n + 3]
        for w_ref, wo_ref in zip((wg_ref, wu_ref, wd_ref), bf16_refs):
            wo_ref[...] = w_ref[...].astype(BF16)
        wg_ref, wu_ref, wd_ref = bf16_refs

    chunks, h_base = [], 0
    for k, grp in enumerate(groups):
        chunks += [(k, r, h_base + r) for r in range(0, grp.rows, grp.row_chunk)]
        h_base += grp.rows

    def mod_rows(k, ref, rows):
        if groups[k].per_row:
            return ref[rows, :]
        seq = (i + groups[k].first_block) // blocks_per_seq
        return ref[pl.ds(seq, 1), :]

    def norm_chunk(c):
        k, r, hr = chunks[c]
        x_ref, shift_ref, scale_ref, _ = refs[4 * k:4 * k + 4]
        for s in range(0, groups[k].row_chunk, NORM_ROWS):
            sr = pl.ds(r + s, NORM_ROWS)
            h = _norm_mod(x_ref[sr, :], g_ref[...], mod_rows(k, shift_ref, sr),
                          mod_rows(k, scale_ref, sr))
            h_ref[pl.ds(hr + s, NORM_ROWS), :] = h.astype(BF16)

    def chunk(c, first, last):
        k, r, hr = chunks[c]
        size = groups[k].row_chunk
        x_ref, _, _, gate_ref = refs[4 * k:4 * k + 4]
        o_ref = outs[k]
        rows = pl.ds(r, size)
        if first and c + 1 < len(chunks):
            norm_chunk(c + 1)
        h = h_ref[pl.ds(hr, size), :]
        g = _dot(h, wg_ref[...])
        u = _dot(h, wu_ref[...])
        a = (g * jax.nn.sigmoid(g) * u).astype(BF16)
        d = _dot(a, wd_ref[...])
        acc = d if first else o_ref[rows, :] + d
        if not last:
            o_ref[rows, :] = acc
            return
        o_ref[rows, :] = x_ref[rows, :] + FFN_RES * mod_rows(k, gate_ref, rows) * acc
        if final_norm:
            for s in range(0, size, NORM_ROWS):
                sr = pl.ds(r + s, NORM_ROWS)
                y = o_ref[sr, :]
                ms = jnp.mean(y * y, axis=-1, keepdims=True)
                o_ref[sr, :] = y * lax.rsqrt(ms + EPS) * fg_ref[...]

    def run(first, last):
        if first:
            norm_chunk(0)
        for c in range(len(chunks)):
            chunk(c, first, last)

    pl.when(jnp.logical_and(active, j == 0))(lambda: run(True, False))
    pl.when(jnp.logical_and(j > 0, j < last_j))(lambda: run(False, False))
    pl.when(j == last_j)(lambda: run(False, True))


def _ffn(xs, groups, norm_g, mod, sub, wg, wu, wd, final_g, *, n_blocks, tf, final_norm,
         blocks_per_seq, out_rows=None, first_rows=None):
    emit_bf16 = wg.dtype == F32
    has_copy = first_rows is not None
    assert not emit_bf16 or n_blocks == 1
    assert not has_copy or len(groups) == 1
    n_j = D_FF // tf
    step = functools.partial(_ffn_step, n_j=n_j, has_copy=has_copy)

    def row_spec(grp, col=0, out=False):
        mode = dict(pipeline_mode=pl.Buffered(1)) if grp.resident else {}
        first = 0 if out and not has_copy else grp.first_block

        def index(s):
            block = step(s)[0] + first
            return (jnp.where(s == 0, 0, block) if out and has_copy else block), col
        return pl.BlockSpec((grp.rows, D_MODEL), index, **mode)

    in_specs, operands = [], []
    for x, grp in zip(xs, groups):
        in_specs.append(row_spec(grp))
        operands.append(x)
        for m in range(3):
            col = 3 * sub + m
            if grp.per_row:
                in_specs.append(row_spec(grp, col))
            else:
                in_specs.append(pl.BlockSpec((SUBLANES, D_MODEL),
                                             lambda s, col=col: (PROMPT_MOD_BLOCK, col)))
            operands.append(mod)
    w_specs = [pl.BlockSpec((D_MODEL, tf), lambda s: (0, step(s)[1])),
               pl.BlockSpec((D_MODEL, tf), lambda s: (0, step(s)[1])),
               pl.BlockSpec((tf, D_MODEL), lambda s: (step(s)[1], 0))]
    vec_spec = pl.BlockSpec((1, D_MODEL), lambda s: (0, 0))
    in_specs += [vec_spec, *w_specs, vec_spec]
    operands += [norm_g.reshape(1, D_MODEL), wg, wu, wd, final_g.reshape(1, D_MODEL)]
    if has_copy:
        in_specs.append(pl.BlockSpec(first_rows.shape, lambda s: (0, 0),
                                     pipeline_mode=pl.Buffered(1)))
        operands.append(first_rows)

    out_specs = [row_spec(grp, out=True) for grp in groups]
    out_shape = [jax.ShapeDtypeStruct((out_rows or grp.rows, D_MODEL), F32) for grp in groups]
    if emit_bf16:
        out_specs += w_specs
        out_shape += [jax.ShapeDtypeStruct(w.shape, BF16) for w in (wg, wu, wd)]
    kern = functools.partial(_ffn_kernel, groups=groups, blocks_per_seq=blocks_per_seq, n_j=n_j,
                             final_norm=final_norm, emit_bf16=emit_bf16, has_copy=has_copy)
    return pl.pallas_call(
        kern,
        grid=(int(has_copy) + n_blocks * n_j,),
        in_specs=in_specs,
        out_specs=out_specs,
        out_shape=out_shape,
        scratch_shapes=[pltpu.VMEM((sum(grp.rows for grp in groups), D_MODEL), BF16)],
        compiler_params=_cparams(1),
        name="ffn",
    )(*operands)


POOL_TAIL = 16


def _proj_kernel(x_ref, g_ref, shift_ref, scale_ref, w_ref, pw_ref, pb_ref, ps_ref,
                 u_ref, y_ref, tail_out_ref, tail_ref, h_ref, *, blocks_per_seq):
    tm = x_ref.shape[0]
    half = tm // 2
    blk = pl.program_id(0) % blocks_per_seq
    sh = _mod_rows(shift_ref, blocks_per_seq)
    sc = _mod_rows(scale_ref, blocks_per_seq)

    @pl.when(blk == 0)
    def _():
        tail_ref[...] = jnp.zeros(tail_ref.shape, F32)

    def norm_half(r):
        for s in range(0, half, NORM_ROWS):
            rows = pl.ds(r * half + s, NORM_ROWS)
            h_ref[rows, :] = _norm_mod(x_ref[rows, :], g_ref[...], sh, sc).astype(BF16)

    norm_half(0)
    norm_half(1)
    pv = jnp.concatenate([_dot(h_ref[pl.ds(r * half, half), :], w_ref[:, D_SSM:])
                          for r in range(2)], axis=0)
    for r in range(2):
        pu = _dot(h_ref[pl.ds(r * half, half), :], w_ref[:, 0:D_SSM])
        u_ref[pl.ds(r * half // T_CHUNK, half // T_CHUNK), :, :] = pu.reshape(
            half // T_CHUNK, T_CHUNK, D_SSM)

    row_ext = lax.broadcasted_iota(jnp.int32, (POOL_TAIL + tm, POOL_GROUP), 0)
    pos = blk * tm + lax.broadcasted_iota(jnp.int32, (tm, POOL_GROUP), 0)
    for gi, w in enumerate(POOL_WINDOWS):
        lanes = slice(gi * POOL_GROUP, (gi + 1) * POOL_GROUP)
        v = pv[:, lanes]
        s = jnp.concatenate([tail_ref[:, lanes], v], axis=0)
        k = 1
        while k < w:
            s = s + jnp.where(row_ext >= k, pltpu.roll(s, k, axis=0), 0.0)
            k *= 2
        cnt = jnp.clip(pos + 1, 1, w).astype(F32)
        z = s[POOL_TAIL:, :] / cnt - v
        y_ref[:, lanes] = _pool_linear(z, gi, pw_ref, pb_ref, ps_ref).astype(y_ref.dtype)
    tail = pv[tm - POOL_TAIL:, :]
    tail_ref[...] = tail
    tail_out_ref[...] = tail


def _proj(x, norm_g, mod, w_in, pool_w, pool_b, pool_scale, *, tm, n_seq):
    rows = x.shape[0]
    seq_len = rows // n_seq
    blocks_per_seq = seq_len // tm
    u_shape = (seq_len // T_CHUNK, n_seq, T_CHUNK, D_SSM)
    u_spec = pl.BlockSpec((tm // T_CHUNK, None, T_CHUNK, D_SSM),
                          lambda i: (i % blocks_per_seq, i // blocks_per_seq, 0, 0))
    shift_spec, scale_spec, _ = _mod_specs(MIX_SUBLAYER)
    vec_spec = pl.BlockSpec((1, D_POOL), lambda i: (0, 0))
    return pl.pallas_call(
        functools.partial(_proj_kernel, blocks_per_seq=blocks_per_seq),
        grid=(rows // tm,),
        in_specs=[pl.BlockSpec((tm, D_MODEL), lambda i: (i, 0)),
                  pl.BlockSpec((1, D_MODEL), lambda i: (0, 0)),
                  shift_spec, scale_spec,
                  pl.BlockSpec((D_MODEL, D_MODEL), lambda i: (0, 0),
                               pipeline_mode=pl.Buffered(1)),
                  pl.BlockSpec(pool_w.shape, lambda i: (0, 0, 0)), vec_spec, vec_spec],
        out_specs=[u_spec, pl.BlockSpec((tm, D_POOL), lambda i: (i, 0)),
                   pl.BlockSpec((None, POOL_TAIL, D_POOL), lambda i: (i // blocks_per_seq, 0, 0))],
        out_shape=[jax.ShapeDtypeStruct(u_shape, F32),
                   jax.ShapeDtypeStruct((rows, D_POOL), BF16),
                   jax.ShapeDtypeStruct((n_seq, POOL_TAIL, D_POOL), F32)],
        scratch_shapes=[pltpu.VMEM((POOL_TAIL, D_POOL), F32), pltpu.VMEM((tm, D_MODEL), BF16)],
        compiler_params=_cparams(1),
        name="proj",
    )(x, norm_g.reshape(1, D_MODEL), mod, mod, w_in, pool_w, pool_b.reshape(1, D_POOL),
      pool_scale.reshape(1, D_POOL))


SAMPLE_TN = 1024


def _proj_sample_kernel(x_ref, g_ref, shift_ref, scale_ref, w_ref, p_ref, wo_ref, h_ref):
    @pl.when(pl.program_id(0) == 0)
    def _():
        h = _norm_mod(x_ref[...], g_ref[...], shift_ref[...], scale_ref[...])
        h_ref[...] = h.astype(BF16)
    wo_ref[...] = w_ref[...].astype(BF16)
    p_ref[...] = _dot(h_ref[...], wo_ref[...])


def _proj_sample(x, norm_g, mod, w_in):
    rows = x.shape[0]
    shift_col, scale_col = 3 * MIX_SUBLAYER, 3 * MIX_SUBLAYER + 1
    return pl.pallas_call(
        _proj_sample_kernel,
        grid=(D_MODEL // SAMPLE_TN,),
        in_specs=[pl.BlockSpec((rows, D_MODEL), lambda j: (0, 0)),
                  pl.BlockSpec((1, D_MODEL), lambda j: (0, 0)),
                  pl.BlockSpec((rows, D_MODEL), lambda j: (0, shift_col)),
                  pl.BlockSpec((rows, D_MODEL), lambda j: (0, scale_col)),
                  pl.BlockSpec((D_MODEL, SAMPLE_TN), lambda j: (0, j))],
        out_specs=[pl.BlockSpec((rows, SAMPLE_TN), lambda j: (0, j)),
                   pl.BlockSpec((D_MODEL, SAMPLE_TN), lambda j: (0, j))],
        out_shape=[jax.ShapeDtypeStruct((rows, D_MODEL), F32),
                   jax.ShapeDtypeStruct((D_MODEL, D_MODEL), BF16)],
        scratch_shapes=[pltpu.VMEM((rows, D_MODEL), BF16)],
        compiler_params=_cparams(1),
        name="proj_sample",
    )(x, norm_g.reshape(1, D_MODEL), mod, mod, w_in)


def _discretise(lam_re, lam_im, log_dt):
    lr = jnp.minimum(lam_re, -1e-4)
    li = lam_im
    dt = jnp.exp(log_dt)
    mag = jnp.exp(lr * dt)
    ang = li * dt
    a_re = mag * jnp.cos(ang)
    a_im = mag * jnp.sin(ang)
    den = lr * lr + li * li
    num_re = a_re - 1.0
    f_re = (num_re * lr + a_im * li) / den
    f_im = (a_im * lr - num_re * li) / den
    return a_re, a_im, f_re, f_im


def _s5_prep_kernel(lam1_ref, c1_ref, lam2_ref, b2_ref, gluw_t,
                    toep_ref, wz_ref, m_ref, apow_ref, gluw_ref):
    g1 = lax.broadcasted_iota(jnp.int32, (SSM_STATE, LANES), 1) // SSM_GROUP
    g2 = lax.broadcasted_iota(jnp.int32, (SSM_GROUP, OCT_STATE), 1) // SSM_STATE

    def expand1(x):
        return jnp.concatenate([jnp.where(g1 == g, x, 0.0) for g in range(OCT)], axis=0)

    def expand2(x):
        return jnp.concatenate([jnp.where(g2 == g, x, 0.0) for g in range(OCT)], axis=0)

    def split(x):
        hi = x.astype(BF16)
        return hi, (x - hi.astype(F32)).astype(BF16)

    a1_re, a1_im, _, _ = _discretise(lam1_ref[0], lam1_ref[1], lam1_ref[2])
    a2_re, a2_im, f_re, f_im = _discretise(lam2_ref[0], lam2_ref[1], lam2_ref[2])
    br, bi = b2_ref[0], b2_ref[1]
    bbar_re = f_re * br - f_im * bi
    bbar_im = f_re * bi + f_im * br
    cr, ci = c1_ref[0], c1_ref[1]
    bre_hi, bre_lo = split(expand2(bbar_re))
    bim_hi, bim_lo = split(expand2(bbar_im))

    def dot3(x_hi, x_lo, y_hi, y_lo):
        return _dot(x_hi, y_hi) + (_dot(x_hi, y_lo) + _dot(x_lo, y_hi))

    p1_re, p1_im = jnp.ones_like(a1_re), jnp.zeros_like(a1_im)
    p2_re, p2_im = jnp.ones_like(a2_re), jnp.zeros_like(a2_im)
    kk = []
    zero_blk = jnp.zeros((LANES, LANES), F32)
    for j in range(T_CHUNK + 1):
        care_hi, care_lo = split(expand1(cr * p1_re - ci * p1_im))
        caim_hi, caim_lo = split(expand1(cr * p1_im + ci * p1_re))
        m_ref[0:OCT_STATE, j * LANES:(j + 1) * LANES] = care_hi
        m_ref[OCT_STATE:2 * OCT_STATE, j * LANES:(j + 1) * LANES] = -caim_hi
        if j < T_CHUNK:
            t = T_CHUNK - 1 - j
            ba_re = expand2(bbar_re * p2_re - bbar_im * p2_im)
            ba_im = expand2(bbar_re * p2_im + bbar_im * p2_re)
            wz_ref[t * LANES:(t + 1) * LANES, 0:OCT_STATE] = ba_re.astype(BF16)
            wz_ref[t * LANES:(t + 1) * LANES, OCT_STATE:2 * OCT_STATE] = ba_im.astype(BF16)
            kk.append(dot3(bre_hi, bre_lo, care_hi, care_lo)
                      - dot3(bim_hi, bim_lo, caim_hi, caim_lo))
        if j == 1:
            apow_ref[0:1, :] = p2_re
            apow_ref[1:2, :] = p2_im
        if j == T_CHUNK:
            apow_ref[2:3, :] = p2_re
            apow_ref[3:4, :] = p2_im
        p1_re, p1_im = p1_re * a1_re - p1_im * a1_im, p1_re * a1_im + p1_im * a1_re
        p2_re, p2_im = p2_re * a2_re - p2_im * a2_im, p2_re * a2_im + p2_im * a2_re

    for t2 in range(T_CHUNK):
        nt, col = divmod(t2, TOK_PER_TILE)
        for t in range((nt + 1) * TOK_PER_TILE):
            blk = kk[t2 - t] if t2 >= t else zero_blk
            row = _toep_tile_row(nt) + t * LANES
            toep_ref[row:row + LANES, col * LANES:(col + 1) * LANES] = blk.astype(BF16)

    rg = lax.broadcasted_iota(jnp.int32, (LANES, LANES), 0) // SSM_GROUP
    lg = lax.broadcasted_iota(jnp.int32, (LANES, LANES), 1) // SSM_GROUP
    gw = jnp.where(rg == lg, gluw_t[...], 0.0).astype(BF16)
    gz = jnp.zeros((LANES, LANES), BF16)
    for a in range(TOK_PER_TILE):
        for b in range(TOK_PER_TILE):
            gluw_ref[a * LANES:(a + 1) * LANES, b * LANES:(b + 1) * LANES] = gw if a == b else gz


def _adaln_s5_prep_kernel(c_ref, w_ref, b_ref, *refs):
    n_prep_in = len(refs) - 6
    _adaln_kernel(c_ref, w_ref, b_ref, refs[n_prep_in])

    @pl.when(pl.program_id(0) < N_OCT)
    def _():
        _s5_prep_kernel(*refs[:n_prep_in], *refs[n_prep_in + 1:])


def _adaln_s5_prep(c_all, ada_w, ada_b, lam_re, lam_im, log_dt, b_re, b_im, c_re, c_im, glu_w):
    G, P, H = N_SSM_GROUPS, SSM_STATE, SSM_GROUP
    rows, n_cols = c_all.shape[0], ada_w.shape[1]
    n_steps = n_cols // ADALN_TN
    assert n_steps >= N_OCT
    lam = jnp.stack([lam_re, lam_im, jnp.broadcast_to(log_dt[:, None], (G, P))])
    lam2 = jnp.swapaxes(lam.reshape(3, N_OCT, 1, OCT_STATE), 0, 1)
    lam1 = jnp.transpose(lam.reshape(3, N_OCT, OCT, P), (1, 0, 3, 2))
    lam1 = jnp.broadcast_to(lam1[..., None], (N_OCT, 3, P, OCT, H)).reshape(N_OCT, 3, P, LANES)
    c1 = jnp.transpose(jnp.stack([c_re, c_im]).reshape(2, N_OCT, LANES, P), (1, 0, 3, 2))
    b2 = jnp.transpose(jnp.stack([b_re, b_im]).reshape(2, N_OCT, OCT_STATE, H), (1, 0, 3, 2))
    gw = glu_w.reshape(N_OCT, OCT, H, 1, H)
    gluw_t = jnp.broadcast_to(gw, (N_OCT, OCT, H, OCT, H)).reshape(N_OCT, LANES, LANES)

    def spec(shape):
        return pl.BlockSpec((None,) + shape,
                            lambda j: (jnp.minimum(j, N_OCT - 1),) + (0,) * len(shape))

    m_cols = (T_CHUNK + 1) * LANES
    mod, *ops = pl.pallas_call(
        _adaln_s5_prep_kernel,
        grid=(n_steps,),
        in_specs=[pl.BlockSpec((rows, D_MODEL), lambda j: (0, 0)),
                  pl.BlockSpec((D_MODEL, ADALN_TN), lambda j: (0, j)),
                  pl.BlockSpec((1, ADALN_TN), lambda j: (0, j))]
                 + [spec((3, SSM_STATE, LANES)), spec((2, SSM_STATE, LANES)),
                    spec((3, 1, OCT_STATE)), spec((2, SSM_GROUP, OCT_STATE)),
                    spec((LANES, LANES))],
        out_specs=[pl.BlockSpec((rows, ADALN_TN), lambda j: (0, j)),
                   spec((TOEP_ROWS, MXU_TILE)), spec((CHUNK_K, 2 * OCT_STATE)),
                   spec((2 * OCT_STATE, m_cols)), spec((4, OCT_STATE)),
                   spec((MXU_TILE, MXU_TILE))],
        out_shape=[jax.ShapeDtypeStruct((rows, n_cols), F32),
                   jax.ShapeDtypeStruct((N_OCT, TOEP_ROWS, MXU_TILE), BF16),
                   jax.ShapeDtypeStruct((N_OCT, CHUNK_K, 2 * OCT_STATE), BF16),
                   jax.ShapeDtypeStruct((N_OCT, 2 * OCT_STATE, m_cols), BF16),
                   jax.ShapeDtypeStruct((N_OCT, 4, OCT_STATE), F32),
                   jax.ShapeDtypeStruct((N_OCT, MXU_TILE, MXU_TILE), BF16)],
        compiler_params=_cparams(1),
        name="adaln_s5_prep",
    )(c_all, ada_w, ada_b, lam1, c1, lam2, b2, gluw_t)
    return mod, tuple(ops)


GELU_C0 = 0.7978845608028654
GELU_C1 = GELU_C0 * 0.044715


def _glu_out(y, gluw, glub):
    gy = y * (0.5 + 0.5 * jnp.tanh(y * (GELU_C0 + GELU_C1 * (y * y))))
    z = _dot(gy.astype(BF16), gluw) + glub
    return gy * (0.5 + 0.5 * jnp.tanh(0.5 * z))


def _s5_prompt_kernel(x_ref, toep_ref, wz_ref, m_ref, apow_ref, d_ref, gluw_ref, glub_ref,
                      y_ref, sre_ref, sim_ref, xr_ref, z_ref, yv_ref, *, n_seq, row_chunk):
    n_rows = x_ref.shape[0] // T_CHUNK
    n_blk = n_rows // row_chunk

    def tokens(b, t):
        return pl.ds(b * (row_chunk * T_CHUNK) + t, row_chunk, stride=T_CHUNK)

    for b in range(n_blk):
        rows = pl.ds(b * row_chunk, row_chunk)
        for t in range(T_CHUNK):
            xr_ref[rows, t * LANES:(t + 1) * LANES] = x_ref[tokens(b, t), :].astype(BF16)
        xr = xr_ref[rows, :]
        z_ref[rows, :] = _dot(xr, wz_ref[...])
        for nt in range(N_TOEP_TILES):
            k_end = (nt + 1) * MXU_TILE
            tile = toep_ref[_toep_tile_row(nt):_toep_tile_row(nt) + k_end, :]
            yv_ref[rows, nt * MXU_TILE:k_end] = _dot(xr[:, 0:k_end], tile)

    are, aim = apow_ref[2:3, :], apow_ref[3:4, :]
    lo = lax.broadcasted_iota(jnp.int32, (SUBLANES, OCT_STATE), 0) < n_seq

    def step(k, carry):
        pre, pim = carry
        rows = pl.ds(pl.multiple_of(k * SUBLANES, SUBLANES), SUBLANES)
        zre = z_ref[rows, 0:OCT_STATE]
        zim = z_ref[rows, OCT_STATE:2 * OCT_STATE]
        w1re = are * pre - aim * pim + zre
        w1im = are * pim + aim * pre + zim
        r1re = pltpu.roll(w1re, n_seq, axis=0)
        r1im = pltpu.roll(w1im, n_seq, axis=0)
        w2re = are * r1re - aim * r1im + zre
        w2im = are * r1im + aim * r1re + zim
        z_ref[rows, 0:OCT_STATE] = jnp.where(lo, pre, r1re)
        z_ref[rows, OCT_STATE:2 * OCT_STATE] = jnp.where(lo, pim, r1im)
        nre = jnp.where(lo, pltpu.roll(w2re, n_seq, axis=0), w2re)
        nim = jnp.where(lo, pltpu.roll(w2im, n_seq, axis=0), w2im)
        return nre, nim

    zeros = jnp.zeros((SUBLANES, OCT_STATE), F32)
    fre, fim = lax.fori_loop(0, n_rows // SUBLANES, step, (zeros, zeros))
    sre_ref[...] = fre[0:n_seq, :]
    sim_ref[...] = fim[0:n_seq, :]

    d2 = jnp.concatenate([d_ref[...]] * TOK_PER_TILE, axis=1)
    glub2 = jnp.concatenate([glub_ref[...]] * TOK_PER_TILE, axis=1)

    for b in range(n_blk):
        rows = pl.ds(b * row_chunk, row_chunk)
        y = yv_ref[rows, :] + _dot(z_ref[rows, :].astype(BF16), m_ref[:, LANES:])
        for t0 in range(0, T_CHUNK, TOK_PER_TILE):
            toks = [tokens(b, t0 + i) for i in range(TOK_PER_TILE)]
            u = jnp.concatenate([x_ref[tok, :] for tok in toks], axis=1)
            yt = y[:, t0 * LANES:(t0 + TOK_PER_TILE) * LANES] + d2 * u
            out = _glu_out(yt, gluw_ref[...], glub2)
            for i, tok in enumerate(toks):
                y_ref[tok, :] = out[:, i * LANES:(i + 1) * LANES]


def _s5_prompt(u_flat, ops, d_skip, glu_b, *, n_seq):
    toep, wz, m, apow, gluw = ops
    assert 2 * n_seq == SUBLANES
    rows = u_flat.shape[0]
    n_rows = rows // T_CHUNK
    oct_spec = lambda shape: pl.BlockSpec((None,) + shape, lambda o: (o, 0, 0))
    kern = functools.partial(_s5_prompt_kernel, n_seq=n_seq, row_chunk=512)
    return pl.pallas_call(
        kern,
        grid=(N_OCT,),
        in_specs=[pl.BlockSpec((rows, LANES), lambda o: (0, o)),
                  oct_spec(toep.shape[1:]), oct_spec(wz.shape[1:]), oct_spec(m.shape[1:]),
                  oct_spec(apow.shape[1:]), oct_spec((1, LANES)), oct_spec(gluw.shape[1:]),
                  oct_spec((1, LANES))],
        out_specs=[pl.BlockSpec((rows, LANES), lambda o: (0, o)),
                   pl.BlockSpec((n_seq, OCT_STATE), lambda o: (0, o)),
                   pl.BlockSpec((n_seq, OCT_STATE), lambda o: (0, o))],
        out_shape=[jax.ShapeDtypeStruct((rows, D_SSM), F32),
                   jax.ShapeDtypeStruct((n_seq, N_OCT * OCT_STATE), F32),
                   jax.ShapeDtypeStruct((n_seq, N_OCT * OCT_STATE), F32)],
        scratch_shapes=[pltpu.VMEM((n_rows, CHUNK_K), BF16),
                        pltpu.VMEM((n_rows, 2 * OCT_STATE), F32),
                        pltpu.VMEM((n_rows, CHUNK_K), F32)],
        compiler_params=_cparams(1),
        name="s5_prompt",
    )(u_flat, toep, wz, m, apow, d_skip.reshape(N_OCT, 1, LANES), gluw,
      glu_b.reshape(N_OCT, 1, LANES))


def _s5_sample_kernel(u_ref, s0re_ref, s0im_ref, wz_ref, m_ref, apow_ref, d_ref, gluw_ref,
                      glub_ref, y_ref, sre_ref, sim_ref):
    for o in range(N_OCT):
        ch = slice(o * LANES, (o + 1) * LANES)
        st = slice(o * OCT_STATE, (o + 1) * OCT_STATE)
        u = u_ref[:, ch]
        z = _dot(u.astype(BF16), wz_ref[o])
        are, aim = apow_ref[o, 0:1, :], apow_ref[o, 1:2, :]
        s0re, s0im = s0re_ref[:, st], s0im_ref[:, st]
        nre = are * s0re - aim * s0im + z[:, 0:OCT_STATE]
        nim = are * s0im + aim * s0re + z[:, OCT_STATE:2 * OCT_STATE]
        sre_ref[:, st] = nre
        sim_ref[:, st] = nim
        y = (_dot(nre.astype(BF16), m_ref[o, 0:OCT_STATE, :])
             + _dot(nim.astype(BF16), m_ref[o, OCT_STATE:2 * OCT_STATE, :])
             + d_ref[o] * u)
        y_ref[:, ch] = _glu_out(y, gluw_ref[o], glub_ref[o]).astype(y_ref.dtype)


def _pool_linear(z, gi, pw_ref, pb_ref, ps_ref):
    lanes = slice(gi * POOL_GROUP, (gi + 1) * POOL_GROUP)
    return (_dot(z.astype(BF16), pw_ref[gi]) + pb_ref[:, lanes]) * ps_ref[:, lanes]


def _pool_sample_kernel(hist_ref, v_ref, pw_ref, pb_ref, ps_ref, y_ref, new_ref):
    new_ref[0:POOL_HIST - 1] = hist_ref[1:POOL_HIST]
    new_ref[POOL_HIST - 1] = v_ref[...]
    for gi, w in enumerate(POOL_WINDOWS):
        lanes = slice(gi * POOL_GROUP, (gi + 1) * POOL_GROUP)
        v = v_ref[:, lanes]
        s = v
        for r in range(POOL_HIST - (w - 1), POOL_HIST):
            s = s + hist_ref[r, :, lanes]
        y = _pool_linear(s / float(w) - v, gi, pw_ref, pb_ref, ps_ref)
        y_ref[:, lanes] = y.astype(y_ref.dtype)


def _outproj_kernel(x_ref, ys_ref, yp_ref, gate_ref, w_ref, o_ref, *, blocks_per_seq):
    half = x_ref.shape[0] // 2
    gate = _mod_rows(gate_ref, blocks_per_seq)
    for r in range(2):
        rows = pl.ds(r * half, half)
        ys = ys_ref[pl.ds(r * half // T_CHUNK, half // T_CHUNK), :, :].reshape(half, D_SSM)
        mix = (_dot(ys.astype(BF16), w_ref[0:D_SSM, :])
               + _dot(yp_ref[rows, :].astype(BF16), w_ref[D_SSM:, :]))
        o_ref[rows, :] = x_ref[rows, :] + gate * mix


def _outproj(x, ys, yp, mod, w_out, *, tm, n_seq):
    rows = x.shape[0]
    blocks_per_seq = (rows // n_seq) // tm
    ys_spec = pl.BlockSpec((tm // T_CHUNK, None, T_CHUNK, D_SSM),
                           lambda i: (i % blocks_per_seq, i // blocks_per_seq, 0, 0))
    gate_spec = _mod_specs(MIX_SUBLAYER)[2]
    return pl.pallas_call(
        functools.partial(_outproj_kernel, blocks_per_seq=blocks_per_seq),
        grid=(rows // tm,),
        in_specs=[pl.BlockSpec((tm, D_MODEL), lambda i: (i, 0)),
                  ys_spec,
                  pl.BlockSpec((tm, D_POOL), lambda i: (i, 0)),
                  gate_spec,
                  pl.BlockSpec((D_MODEL, D_MODEL), lambda i: (0, 0),
                               pipeline_mode=pl.Buffered(1))],
        out_specs=pl.BlockSpec((tm, D_MODEL), lambda i: (i, 0)),
        out_shape=jax.ShapeDtypeStruct((rows, D_MODEL), F32),
        compiler_params=_cparams(1),
        name="outproj",
    )(x, ys, yp, mod, w_out)


def _mixer_sample_kernel(x_ref, gate_ref, w_ref, proj_ref, s0re_ref, s0im_ref, wz_ref, m_ref,
                         apow_ref, d_ref, gluw_ref, glub_ref, hist_ref, pw_ref, pb_ref, ps_ref,
                         o_ref, wo_ref, sre_ref, sim_ref, new_hist_ref, ys_ref, yp_ref):
    @pl.when(pl.program_id(0) == 0)
    def _():
        _s5_sample_kernel(proj_ref, s0re_ref, s0im_ref, wz_ref, m_ref, apow_ref, d_ref, gluw_ref,
                          glub_ref, ys_ref, sre_ref, sim_ref)
        _pool_sample_kernel(hist_ref, proj_ref.at[:, pl.ds(D_SSM, D_POOL)], pw_ref, pb_ref, ps_ref,
                            yp_ref, new_hist_ref)

    wo_ref[...] = w_ref[...].astype(BF16)
    mix = _dot(ys_ref[...], wo_ref[0:D_SSM, :]) + _dot(yp_ref[...], wo_ref[D_SSM:, :])
    o_ref[...] = x_ref[...] + gate_ref[...] * mix


def _mixer_sample(x, proj, s0_re, s0_im, hist_t, ops, d_skip, glu_b, pool_w, pool_b, pool_scale,
                  mod, w_out):
    _, wz, m, apow, gluw = ops
    rows = x.shape[0]
    n_col = D_MODEL // SAMPLE_TN
    gate_col = (3 * MIX_SUBLAYER + 2) * n_col

    def once(shape, index=None):
        index = index or (0,) * len(shape)
        return pl.BlockSpec(shape, lambda j: index, pipeline_mode=pl.Buffered(1))

    return pl.pallas_call(
        _mixer_sample_kernel,
        grid=(n_col,),
        in_specs=[pl.BlockSpec((rows, SAMPLE_TN), lambda j: (0, j)),
                  pl.BlockSpec((rows, SAMPLE_TN), lambda j: (0, gate_col + j)),
                  pl.BlockSpec((D_MODEL, SAMPLE_TN), lambda j: (0, j)),
                  once(proj.shape), once(s0_re.shape), once(s0_im.shape),
                  once((N_OCT, LANES, 2 * OCT_STATE), (0, T_CHUNK - 1, 0)),
                  once((N_OCT, 2 * OCT_STATE, LANES)),
                  once(apow.shape), once((N_OCT, 1, LANES)), once((N_OCT, LANES, LANES)),
                  once((N_OCT, 1, LANES)), once(hist_t.shape), once(pool_w.shape),
                  once((1, D_POOL)), once((1, D_POOL))],
        out_specs=[pl.BlockSpec((rows, SAMPLE_TN), lambda j: (0, j)),
                   pl.BlockSpec((D_MODEL, SAMPLE_TN), lambda j: (0, j)),
                   once(s0_re.shape), once(s0_im.shape), once(hist_t.shape)],
        out_shape=[jax.ShapeDtypeStruct((rows, D_MODEL), F32),
                   jax.ShapeDtypeStruct((D_MODEL, D_MODEL), BF16),
                   jax.ShapeDtypeStruct(s0_re.shape, F32),
                   jax.ShapeDtypeStruct(s0_im.shape, F32),
                   jax.ShapeDtypeStruct(hist_t.shape, F32)],
        scratch_shapes=[pltpu.VMEM((rows, D_SSM), BF16), pltpu.VMEM((rows, D_POOL), BF16)],
        compiler_params=_cparams(1),
        name="mixer_sample",
    )(x, mod, w_out, proj, s0_re, s0_im, wz, m, apow, d_skip.reshape(N_OCT, 1, LANES), gluw,
      glu_b.reshape(N_OCT, 1, LANES), hist_t, pool_w, pool_b.reshape(1, D_POOL),
      pool_scale.reshape(1, D_POOL))


def kernel(x_prompt, x_sample, state_ssm_re, state_ssm_im, state_pool, c_prompt, c_sample, ada_w, ada_b, ffn1_norm, ffn1_w_gate, ffn1_w_up, ffn1_w_down, mix_norm, w_in, ssm_lambda_re, ssm_lambda_im, ssm_log_dt, ssm_b_re, ssm_b_im, ssm_c_re, ssm_c_im, ssm_d, ssm_glu_w, ssm_glu_b, pool_w, pool_b, pool_scale, w_out, ffn2_norm, ffn2_w_gate, ffn2_w_up, ffn2_w_down, final_norm):
    n_p, seq, _ = x_prompt.shape
    n_s = x_sample.shape[0]
    assert (n_p, n_s) == (N_PROMPT, N_SAMPLE) and n_p <= SUBLANES
    G, P = N_SSM_GROUPS, SSM_STATE

    c_all = jnp.concatenate([c_sample, c_prompt, jnp.zeros((SUBLANES - n_p, D_MODEL), F32)], axis=0)
    mod, ops = _adaln_s5_prep(c_all, ada_w[0], ada_b.reshape(1, N_MOD_COLS), ssm_lambda_re[0],
                              ssm_lambda_im[0], ssm_log_dt[0], ssm_b_re[0], ssm_b_im[0],
                              ssm_c_re[0], ssm_c_im[0], ssm_glu_w[0])
    pw = pool_w[0].astype(BF16)

    xs = x_sample.reshape(n_s, D_MODEL)
    xp = x_prompt.reshape(n_p * seq, D_MODEL)
    blocks_per_seq = seq // FFN_TM
    sample_rows = _RowGroup(n_s, n_s, per_row=True, first_block=0, resident=True)
    head_rows = _RowGroup(FFN_TM, FFN_ROW_CHUNK, per_row=False, first_block=0, resident=True)
    tail_rows = _RowGroup(FFN_TM, FFN_ROW_CHUNK, per_row=False, first_block=1, resident=False)

    def ffn(xs, xp, norm_g, sub, wg, wu, wd, last):
        ys, yp0, bg, bu, bd = _ffn([xs, xp], (sample_rows, head_rows), norm_g, mod, sub, wg, wu, wd,
                                   final_norm, n_blocks=1, tf=FFN_HEAD_TF, final_norm=last,
                                   blocks_per_seq=blocks_per_seq)
        (yp,) = _ffn([xp], (tail_rows,), norm_g, mod, sub, bg, bu, bd, final_norm,
                     n_blocks=n_p * blocks_per_seq - 1, tf=FFN_TF, final_norm=last,
                     blocks_per_seq=blocks_per_seq, out_rows=xp.shape[0], first_rows=yp0)
        return ys, yp

    xs, xp = ffn(xs, xp, ffn1_norm[0], FFN1_SUBLAYER, ffn1_w_gate[0], ffn1_w_up[0],
                 ffn1_w_down[0], False)

    proj_s, win = _proj_sample(xs, mix_norm[0], mod, w_in[0])
    xs, wout, sre_s, sim_s, pool_s_t = _mixer_sample(
        xs, proj_s, state_ssm_re[0].reshape(n_s, G * P), state_ssm_im[0].reshape(n_s, G * P),
        jnp.swapaxes(state_pool[0], 0, 1), ops, ssm_d[0], ssm_glu_b[0], pw, pool_b[0],
        pool_scale[0], mod, w_out[0])
    pool_s = jnp.swapaxes(pool_s_t, 0, 1)

    u4, yp_p, v_tail = _proj(xp, mix_norm[0], mod, win, pw, pool_b[0], pool_scale[0],
                             tm=PROJ_TM, n_seq=n_p)
    ys_p, sre_p, sim_p = _s5_prompt(u4.reshape(n_p * seq, D_SSM), ops, ssm_d[0], ssm_glu_b[0],
                                    n_seq=n_p)
    xp = _outproj(xp, ys_p.reshape(seq // T_CHUNK, n_p, T_CHUNK, D_SSM), yp_p, mod, wout,
                  tm=OUTPROJ_TM, n_seq=n_p)

    y_sample, y_prompt = ffn(xs, xp, ffn2_norm[0], FFN2_SUBLAYER, ffn2_w_gate[0], ffn2_w_up[0],
                             ffn2_w_down[0], True)

    pool_p = v_tail[:, POOL_TAIL - POOL_HIST:, :][None]
    return (y_prompt.reshape(n_p, seq, D_MODEL), y_sample.reshape(n_s, 1, D_MODEL),
            sre_p.reshape(1, n_p, G, P), sim_p.reshape(1, n_p, G, P), pool_p,
            sre_s.reshape(1, n_s, G, P), sim_s.reshape(1, n_s, G, P), pool_s[None])
```

```python
import functools
from typing import NamedTuple

import jax
import jax.numpy as jnp
from jax import lax
from jax.experimental import pallas as pl
from jax.experimental.pallas import tpu as pltpu

F32 = jnp.float32
BF16 = jnp.bfloat16

D_MODEL = 2048
D_FF = 5632
D_SSM = 1024
D_POOL = 1024
SSM_GROUP = 16
SSM_STATE = 64
N_SSM_GROUPS = 64
POOL_WINDOWS = (2, 4, 8, 16)
POOL_GROUP = 256
POOL_HIST = 15
N_MOD_COLS = 9 * D_MODEL
N_PROMPT = 4
N_SAMPLE = 128
EPS = 1e-6
FFN_RES = 0.5

LANES = 128
SUBLANES = 8
MXU_TILE = 256
VMEM_LIMIT = 60 * 1024 * 1024

PROMPT_MOD_BLOCK = N_SAMPLE // SUBLANES
NORM_ROWS = 32
FFN1_SUBLAYER, MIX_SUBLAYER, FFN2_SUBLAYER = 0, 1, 2

FFN_TM = 1024
FFN_ROW_CHUNK = 512
FFN_TF = 512
FFN_HEAD_TF = 256
PROJ_TM = 1024
OUTPROJ_TM = 1024

OCT = LANES // SSM_GROUP
N_OCT = N_SSM_GROUPS // OCT
T_CHUNK = 8
OCT_STATE = OCT * SSM_STATE
CHUNK_K = T_CHUNK * LANES
TOK_PER_TILE = MXU_TILE // LANES
N_TOEP_TILES = CHUNK_K // MXU_TILE


def _toep_tile_row(nt):
    return MXU_TILE * (nt * (nt + 1) // 2)


TOEP_ROWS = _toep_tile_row(N_TOEP_TILES)


def _cparams(n_axes, independent=False):
    semantics = "parallel" if independent else "arbitrary"
    return pltpu.CompilerParams(dimension_semantics=(semantics,) * n_axes,
                                vmem_limit_bytes=VMEM_LIMIT)


def _dot(a, b):
    return jnp.dot(a, b, preferred_element_type=F32)


ADALN_TN = 2048


def _adaln_kernel(c_ref, w_ref, b_ref, o_ref):
    c = c_ref[...]
    sc = (c * jax.nn.sigmoid(c)).astype(BF16)
    o_ref[...] = _dot(sc, w_ref[...].astype(BF16)) + b_ref[...]


def _norm_mod(x, g, shift, scale):
    ms = jnp.mean(x * x, axis=-1, keepdims=True)
    y = x * lax.rsqrt(ms + EPS) * g
    return y * (1.0 + scale) + shift


def _mod_specs(sub):
    return [pl.BlockSpec((SUBLANES, D_MODEL), lambda i, col=3 * sub + m: (PROMPT_MOD_BLOCK, col))
            for m in range(3)]


def _mod_rows(ref, blocks_per_seq):
    return ref[pl.ds(pl.program_id(0) // blocks_per_seq, 1), :]


class _RowGroup(NamedTuple):
    rows: int
    row_chunk: int
    per_row: bool
    first_block: int
    resident: bool


def _ffn_step(s, n_j, has_copy):
    t = jnp.maximum(s - 1, 0) if has_copy else s
    return t // n_j, t % n_j


def _ffn_kernel(*refs, groups, blocks_per_seq, n_j, final_norm, emit_bf16, has_copy):
    n = len(groups)
    g_ref, wg_ref, wu_ref, wd_ref, fg_ref = refs[4 * n:4 * n + 5]
    outs = refs[4 * n + 5 + int(has_copy):-1]
    h_ref = refs[-1]
    step = pl.program_id(0)
    i, j = _ffn_step(step, n_j, has_copy)
    last_j = n_j - 1
    active = step >= int(has_copy)

    if has_copy:
        @pl.when(step == 0)
        def _():
            outs[0][...] = refs[4 * n + 5][...]

    if emit_bf16:
        bf16_refs = outs[n:n + 3]
        for w_ref, wo_ref in zip((wg_ref, wu_ref, wd_ref), bf16_refs):
            wo_ref[...] = w_ref[...].astype(BF16)
        wg_ref, wu_ref, wd_ref = bf16_refs

    chunks, h_base = [], 0
    for k, grp in enumerate(groups):
        chunks += [(k, r, h_base + r) for r in range(0, grp.rows, grp.row_chunk)]
        h_base += grp.rows

    def mod_rows(k, ref, rows):
        if groups[k].per_row:
            return ref[rows, :]
        seq = (i + groups[k].first_block) // blocks_per_seq
        return ref[pl.ds(seq, 1), :]

    def norm_chunk(c):
        k, r, hr = chunks[c]
        x_ref, shift_ref, scale_ref, _ = refs[4 * k:4 * k + 4]
        for s in range(0, groups[k].row_chunk, NORM_ROWS):
            sr = pl.ds(r + s, NORM_ROWS)
            h = _norm_mod(x_ref[sr, :], g_ref[...], mod_rows(k, shift_ref, sr),
                          mod_rows(k, scale_ref, sr))
            h_ref[pl.ds(hr + s, NORM_ROWS), :] = h.astype(BF16)

    def chunk(c, first, last):
        k, r, hr = chunks[c]
        size = groups[k].row_chunk
        x_ref, _, _, gate_ref = refs[4 * k:4 * k + 4]
        o_ref = outs[k]
        rows = pl.ds(r, size)
        if first and c + 1 < len(chunks):
            norm_chunk(c + 1)
        h = h_ref[pl.ds(hr, size), :]
        g = _dot(h, wg_ref[...])
        u = _dot(h, wu_ref[...])
        a = (g * jax.nn.sigmoid(g) * u).astype(BF16)
        d = _dot(a, wd_ref[...])
        acc = d if first else o_ref[rows, :] + d
        if not last:
            o_ref[rows, :] = acc
            return
        o_ref[rows, :] = x_ref[rows, :] + FFN_RES * mod_rows(k, gate_ref, rows) * acc
        if final_norm:
            for s in range(0, size, NORM_ROWS):
                sr = pl.ds(r + s, NORM_ROWS)
                y = o_ref[sr, :]
                ms = jnp.mean(y * y, axis=-1, keepdims=True)
                o_ref[sr, :] = y * lax.rsqrt(ms + EPS) * fg_ref[...]

    def run(first, last):
        if first:
            norm_chunk(0)
        for c in range(len(chunks)):
            chunk(c, first, last)

    pl.when(jnp.logical_and(active, j == 0))(lambda: run(True, False))
    pl.when(jnp.logical_and(j > 0, j < last_j))(lambda: run(False, False))
    pl.when(j == last_j)(lambda: run(False, True))


def _ffn(xs, groups, norm_g, mod, sub, wg, wu, wd, final_g, *, n_blocks, tf, final_norm,
         blocks_per_seq, out_rows=None, first_rows=None):
    emit_bf16 = wg.dtype == F32
    has_copy = first_rows is not None
    assert not emit_bf16 or n_blocks == 1
    assert not has_copy or len(groups) == 1
    n_j = D_FF // tf
    step = functools.partial(_ffn_step, n_j=n_j, has_copy=has_copy)

    def row_spec(grp, col=0, out=False):
        mode = dict(pipeline_mode=pl.Buffered(1)) if grp.resident else {}
        first = 0 if out and not has_copy else grp.first_block

        def index(s):
            block = step(s)[0] + first
            return (jnp.where(s == 0, 0, block) if out and has_copy else block), col
        return pl.BlockSpec((grp.rows, D_MODEL), index, **mode)

    in_specs, operands = [], []
    for x, grp in zip(xs, groups):
        in_specs.append(row_spec(grp))
        operands.append(x)
        for m in range(3):
            col = 3 * sub + m
            if grp.per_row:
                in_specs.append(row_spec(grp, col))
            else:
                in_specs.append(pl.BlockSpec((SUBLANES, D_MODEL),
                                             lambda s, col=col: (PROMPT_MOD_BLOCK, col)))
            operands.append(mod)
    w_specs = [pl.BlockSpec((D_MODEL, tf), lambda s: (0, step(s)[1])),
               pl.BlockSpec((D_MODEL, tf), lambda s: (0, step(s)[1])),
               pl.BlockSpec((tf, D_MODEL), lambda s: (step(s)[1], 0))]
    vec_spec = pl.BlockSpec((1, D_MODEL), lambda s: (0, 0))
    in_specs += [vec_spec, *w_specs, vec_spec]
    operands += [norm_g.reshape(1, D_MODEL), wg, wu, wd, final_g.reshape(1, D_MODEL)]
    if has_copy:
        in_specs.append(pl.BlockSpec(first_rows.shape, lambda s: (0, 0),
                                     pipeline_mode=pl.Buffered(1)))
        operands.append(first_rows)

    out_specs = [row_spec(grp, out=True) for grp in groups]
    out_shape = [jax.ShapeDtypeStruct((out_rows or grp.rows, D_MODEL), F32) for grp in groups]
    if emit_bf16:
        out_specs += w_specs
        out_shape += [jax.ShapeDtypeStruct(w.shape, BF16) for w in (wg, wu, wd)]
    kern = functools.partial(_ffn_kernel, groups=groups, blocks_per_seq=blocks_per_seq, n_j=n_j,
                             final_norm=final_norm, emit_bf16=emit_bf16, has_copy=has_copy)
    return pl.pallas_call(
        kern,
        grid=(int(has_copy) + n_blocks * n_j,),
        in_specs=in_specs,
        out_specs=out_specs,
        out_shape=out_shape,
        scratch_shapes=[pltpu.VMEM((sum(grp.rows for grp in groups), D_MODEL), BF16)],
        compiler_params=_cparams(1),
        name="ffn",
    )(*operands)


POOL_TAIL = 16


def _proj_kernel(x_ref, g_ref, shift_ref, scale_ref, w_ref, pw_ref, pb_ref, ps_ref,
                 u_ref, y_ref, tail_out_ref, tail_ref, h_ref, *, blocks_per_seq):
    tm = x_ref.shape[0]
    half = tm // 2
    blk = pl.program_id(0) % blocks_per_seq
    sh = _mod_rows(shift_ref, blocks_per_seq)
    sc = _mod_rows(scale_ref, blocks_per_seq)

    @pl.when(blk == 0)
    def _():
        tail_ref[...] = jnp.zeros(tail_ref.shape, F32)

    def norm_half(r):
        for s in range(0, half, NORM_ROWS):
            rows = pl.ds(r * half + s, NORM_ROWS)
            h_ref[rows, :] = _norm_mod(x_ref[rows, :], g_ref[...], sh, sc).astype(BF16)

    norm_half(0)
    norm_half(1)
    pv = jnp.concatenate([_dot(h_ref[pl.ds(r * half, half), :], w_ref[:, D_SSM:])
                          for r in range(2)], axis=0)
    for r in range(2):
        pu = _dot(h_ref[pl.ds(r * half, half), :], w_ref[:, 0:D_SSM])
        u_ref[pl.ds(r * half // T_CHUNK, half // T_CHUNK), :, :] = pu.reshape(
            half // T_CHUNK, T_CHUNK, D_SSM)

    row_ext = lax.broadcasted_iota(jnp.int32, (POOL_TAIL + tm, POOL_GROUP), 0)
    pos = blk * tm + lax.broadcasted_iota(jnp.int32, (tm, POOL_GROUP), 0)
    for gi, w in enumerate(POOL_WINDOWS):
        lanes = slice(gi * POOL_GROUP, (gi + 1) * POOL_GROUP)
        v = pv[:, lanes]
        s = jnp.concatenate([tail_ref[:, lanes], v], axis=0)
        k = 1
        while k < w:
            s = s + jnp.where(row_ext >= k, pltpu.roll(s, k, axis=0), 0.0)
            k *= 2
        cnt = jnp.clip(pos + 1, 1, w).astype(F32)
        z = s[POOL_TAIL:, :] / cnt - v
        y_ref[:, lanes] = _pool_linear(z, gi, pw_ref, pb_ref, ps_ref).astype(y_ref.dtype)
    tail = pv[tm - POOL_TAIL:, :]
    tail_ref[...] = tail
    tail_out_ref[...] = tail


def _proj(x, norm_g, mod, w_in, pool_w, pool_b, pool_scale, *, tm, n_seq):
    rows = x.shape[0]
    seq_len = rows // n_seq
    blocks_per_seq = seq_len // tm
    u_shape = (seq_len // T_CHUNK, n_seq, T_CHUNK, D_SSM)
    u_spec = pl.BlockSpec((tm // T_CHUNK, None, T_CHUNK, D_SSM),
                          lambda i: (i % blocks_per_seq, i // blocks_per_seq, 0, 0))
    shift_spec, scale_spec, _ = _mod_specs(MIX_SUBLAYER)
    vec_spec = pl.BlockSpec((1, D_POOL), lambda i: (0, 0))
    return pl.pallas_call(
        functools.partial(_proj_kernel, blocks_per_seq=blocks_per_seq),
        grid=(rows // tm,),
        in_specs=[pl.BlockSpec((tm, D_MODEL), lambda i: (i, 0)),
                  pl.BlockSpec((1, D_MODEL), lambda i: (0, 0)),
                  shift_spec, scale_spec,
                  pl.BlockSpec((D_MODEL, D_MODEL), lambda i: (0, 0),
                               pipeline_mode=pl.Buffered(1)),
                  pl.BlockSpec(pool_w.shape, lambda i: (0, 0, 0)), vec_spec, vec_spec],
        out_specs=[u_spec, pl.BlockSpec((tm, D_POOL), lambda i: (i, 0)),
                   pl.BlockSpec((None, POOL_TAIL, D_POOL), lambda i: (i // blocks_per_seq, 0, 0))],
        out_shape=[jax.ShapeDtypeStruct(u_shape, F32),
                   jax.ShapeDtypeStruct((rows, D_POOL), BF16),
                   jax.ShapeDtypeStruct((n_seq, POOL_TAIL, D_POOL), F32)],
        scratch_shapes=[pltpu.VMEM((POOL_TAIL, D_POOL), F32), pltpu.VMEM((tm, D_MODEL), BF16)],
        compiler_params=_cparams(1),
        name="proj",
    )(x, norm_g.reshape(1, D_MODEL), mod, mod, w_in, pool_w, pool_b.reshape(1, D_POOL),
      pool_scale.reshape(1, D_POOL))


SAMPLE_TN = 1024


def _proj_sample_kernel(x_ref, g_ref, shift_ref, scale_ref, w_ref, p_ref, wo_ref, h_ref):
    @pl.when(pl.program_id(0) == 0)
    def _():
        h = _norm_mod(x_ref[...], g_ref[...], shift_ref[...], scale_ref[...])
        h_ref[...] = h.astype(BF16)
    wo_ref[...] = w_ref[...].astype(BF16)
    p_ref[...] = _dot(h_ref[...], wo_ref[...])


def _proj_sample(x, norm_g, mod, w_in):
    rows = x.shape[0]
    shift_col, scale_col = 3 * MIX_SUBLAYER, 3 * MIX_SUBLAYER + 1
    return pl.pallas_call(
        _proj_sample_kernel,
        grid=(D_MODEL // SAMPLE_TN,),
        in_specs=[pl.BlockSpec((rows, D_MODEL), lambda j: (0, 0)),
                  pl.BlockSpec((1, D_MODEL), lambda j: (0, 0)),
                  pl.BlockSpec((rows, D_MODEL), lambda j: (0, shift_col)),
                  pl.BlockSpec((rows, D_MODEL), lambda j: (0, scale_col)),
                  pl.BlockSpec((D_MODEL, SAMPLE_TN), lambda j: (0, j))],
        out_specs=[pl.BlockSpec((rows, SAMPLE_TN), lambda j: (0, j)),
                   pl.BlockSpec((D_MODEL, SAMPLE_TN), lambda j: (0, j))],
        out_shape=[jax.ShapeDtypeStruct((rows, D_MODEL), F32),
                   jax.ShapeDtypeStruct((D_MODEL, D_MODEL), BF16)],
        scratch_shapes=[pltpu.VMEM((rows, D_MODEL), BF16)],
        compiler_params=_cparams(1),
        name="proj_sample",
    )(x, norm_g.reshape(1, D_MODEL), mod, mod, w_in)


def _discretise(lam_re, lam_im, log_dt):
    lr = jnp.minimum(lam_re, -1e-4)
    li = lam_im
    dt = jnp.exp(log_dt)
    mag = jnp.exp(lr * dt)
    ang = li * dt
    a_re = mag * jnp.cos(ang)
    a_im = mag * jnp.sin(ang)
    den = lr * lr + li * li
    num_re = a_re - 1.0
    f_re = (num_re * lr + a_im * li) / den
    f_im = (a_im * lr - num_re * li) / den
    return a_re, a_im, f_re, f_im


def _s5_prep_kernel(lam1_ref, c1_ref, lam2_ref, b2_ref, gluw_t,
                    toep_ref, wz_ref, m_ref, apow_ref, gluw_ref):
    g1 = lax.broadcasted_iota(jnp.int32, (SSM_STATE, LANES), 1) // SSM_GROUP
    g2 = lax.broadcasted_iota(jnp.int32, (SSM_GROUP, OCT_STATE), 1) // SSM_STATE

    def expand1(x):
        return jnp.concatenate([jnp.where(g1 == g, x, 0.0) for g in range(OCT)], axis=0)

    def expand2(x):
        return jnp.concatenate([jnp.where(g2 == g, x, 0.0) for g in range(OCT)], axis=0)

    def split(x):
        hi = x.astype(BF16)
        return hi, (x - hi.astype(F32)).astype(BF16)

    a1_re, a1_im, _, _ = _discretise(lam1_ref[0], lam1_ref[1], lam1_ref[2])
    a2_re, a2_im, f_re, f_im = _discretise(lam2_ref[0], lam2_ref[1], lam2_ref[2])
    br, bi = b2_ref[0], b2_ref[1]
    bbar_re = f_re * br - f_im * bi
    bbar_im = f_re * bi + f_im * br
    cr, ci = c1_ref[0], c1_ref[1]
    bre_hi, bre_lo = split(expand2(bbar_re))
    bim_hi, bim_lo = split(expand2(bbar_im))

    def dot3(x_hi, x_lo, y_hi, y_lo):
        return _dot(x_hi, y_hi) + (_dot(x_hi, y_lo) + _dot(x_lo, y_hi))

    p1_re, p1_im = jnp.ones_like(a1_re), jnp.zeros_like(a1_im)
    p2_re, p2_im = jnp.ones_like(a2_re), jnp.zeros_like(a2_im)
    kk = []
    zero_blk = jnp.zeros((LANES, LANES), F32)
    for j in range(T_CHUNK + 1):
        care_hi, care_lo = split(expand1(cr * p1_re - ci * p1_im))
        caim_hi, caim_lo = split(expand1(cr * p1_im + ci * p1_re))
        m_ref[0:OCT_STATE, j * LANES:(j + 1) * LANES] = care_hi
        m_ref[OCT_STATE:2 * OCT_STATE, j * LANES:(j + 1) * LANES] = -caim_hi
        if j < T_CHUNK:
            t = T_CHUNK - 1 - j
            ba_re = expand2(bbar_re * p2_re - bbar_im * p2_im)
            ba_im = expand2(bbar_re * p2_im + bbar_im * p2_re)
            wz_ref[t * LANES:(t + 1) * LANES, 0:OCT_STATE] = ba_re.astype(BF16)
            wz_ref[t * LANES:(t + 1) * LANES, OCT_STATE:2 * OCT_STATE] = ba_im.astype(BF16)
            kk.append(dot3(bre_hi, bre_lo, care_hi, care_lo)
                      - dot3(bim_hi, bim_lo, caim_hi, caim_lo))
        if j == 1:
            apow_ref[0:1, :] = p2_re
            apow_ref[1:2, :] = p2_im
        if j == T_CHUNK:
            apow_ref[2:3, :] = p2_re
            apow_ref[3:4, :] = p2_im
        p1_re, p1_im = p1_re * a1_re - p1_im * a1_im, p1_re * a1_im + p1_im * a1_re
        p2_re, p2_im = p2_re * a2_re - p2_im * a2_im, p2_re * a2_im + p2_im * a2_re

    for t2 in range(T_CHUNK):
        nt, col = divmod(t2, TOK_PER_TILE)
        for t in range((nt + 1) * TOK_PER_TILE):
            blk = kk[t2 - t] if t2 >= t else zero_blk
            row = _toep_tile_row(nt) + t * LANES
            toep_ref[row:row + LANES, col * LANES:(col + 1) * LANES] = blk.astype(BF16)

    rg = lax.broadcasted_iota(jnp.int32, (LANES, LANES), 0) // SSM_GROUP
    lg = lax.broadcasted_iota(jnp.int32, (LANES, LANES), 1) // SSM_GROUP
    gw = jnp.where(rg == lg, gluw_t[...], 0.0).astype(BF16)
    gz = jnp.zeros((LANES, LANES), BF16)
    for a in range(TOK_PER_TILE):
        for b in range(TOK_PER_TILE):
            gluw_ref[a * LANES:(a + 1) * LANES, b * LANES:(b + 1) * LANES] = gw if a == b else gz


def _adaln_s5_prep_kernel(c_ref, w_ref, b_ref, *refs):
    n_prep_in = len(refs) - 6
    _adaln_kernel(c_ref, w_ref, b_ref, refs[n_prep_in])

    @pl.when(pl.program_id(0) < N_OCT)
    def _():
        _s5_prep_kernel(*refs[:n_prep_in], *refs[n_prep_in + 1:])


def _adaln_s5_prep(c_all, ada_w, ada_b, lam_re, lam_im, log_dt, b_re, b_im, c_re, c_im, glu_w):
    G, P, H = N_SSM_GROUPS, SSM_STATE, SSM_GROUP
    rows, n_cols = c_all.shape[0], ada_w.shape[1]
    n_steps = n_cols // ADALN_TN
    assert n_steps >= N_OCT
    lam = jnp.stack([lam_re, lam_im, jnp.broadcast_to(log_dt[:, None], (G, P))])
    lam2 = jnp.swapaxes(lam.reshape(3, N_OCT, 1, OCT_STATE), 0, 1)
    lam1 = jnp.transpose(lam.reshape(3, N_OCT, OCT, P), (1, 0, 3, 2))
    lam1 = jnp.broadcast_to(lam1[..., None], (N_OCT, 3, P, OCT, H)).reshape(N_OCT, 3, P, LANES)
    c1 = jnp.transpose(jnp.stack([c_re, c_im]).reshape(2, N_OCT, LANES, P), (1, 0, 3, 2))
    b2 = jnp.transpose(jnp.stack([b_re, b_im]).reshape(2, N_OCT, OCT_STATE, H), (1, 0, 3, 2))
    gw = glu_w.reshape(N_OCT, OCT, H, 1, H)
    gluw_t = jnp.broadcast_to(gw, (N_OCT, OCT, H, OCT, H)).reshape(N_OCT, LANES, LANES)

    def spec(shape):
        return pl.BlockSpec((None,) + shape,
                            lambda j: (jnp.minimum(j, N_OCT - 1),) + (0,) * len(shape))

    m_cols = (T_CHUNK + 1) * LANES
    mod, *ops = pl.pallas_call(
        _adaln_s5_prep_kernel,
        grid=(n_steps,),
        in_specs=[pl.BlockSpec((rows, D_MODEL), lambda j: (0, 0)),
                  pl.BlockSpec((D_MODEL, ADALN_TN), lambda j: (0, j)),
                  pl.BlockSpec((1, ADALN_TN), lambda j: (0, j))]
                 + [spec((3, SSM_STATE, LANES)), spec((2, SSM_STATE, LANES)),
                    spec((3, 1, OCT_STATE)), spec((2, SSM_GROUP, OCT_STATE)),
                    spec((LANES, LANES))],
        out_specs=[pl.BlockSpec((rows, ADALN_TN), lambda j: (0, j)),
                   spec((TOEP_ROWS, MXU_TILE)), spec((CHUNK_K, 2 * OCT_STATE)),
                   spec((2 * OCT_STATE, m_cols)), spec((4, OCT_STATE)),
                   spec((MXU_TILE, MXU_TILE))],
        out_shape=[jax.ShapeDtypeStruct((rows, n_cols), F32),
                   jax.ShapeDtypeStruct((N_OCT, TOEP_ROWS, MXU_TILE), BF16),
                   jax.ShapeDtypeStruct((N_OCT, CHUNK_K, 2 * OCT_STATE), BF16),
                   jax.ShapeDtypeStruct((N_OCT, 2 * OCT_STATE, m_cols), BF16),
                   jax.ShapeDtypeStruct((N_OCT, 4, OCT_STATE), F32),
                   jax.ShapeDtypeStruct((N_OCT, MXU_TILE, MXU_TILE), BF16)],
        compiler_params=_cparams(1),
        name="adaln_s5_prep",
    )(c_all, ada_w, ada_b, lam1, c1, lam2, b2, gluw_t)
    return mod, tuple(ops)


GELU_C0 = 0.7978845608028654
GELU_C1 = GELU_C0 * 0.044715


def _glu_out(y, gluw, glub):
    gy = y * (0.5 + 0.5 * jnp.tanh(y * (GELU_C0 + GELU_C1 * (y * y))))
    z = _dot(gy.astype(BF16), gluw) + glub
    return gy * (0.5 + 0.5 * jnp.tanh(0.5 * z))


def _s5_prompt_kernel(x_ref, toep_ref, wz_ref, m_ref, apow_ref, d_ref, gluw_ref, glub_ref,
                      y_ref, sre_ref, sim_ref, xr_ref, z_ref, yv_ref, *, n_seq, row_chunk):
    n_rows = x_ref.shape[0] // T_CHUNK
    n_blk = n_rows // row_chunk

    def tokens(b, t):
        return pl.ds(b * (row_chunk * T_CHUNK) + t, row_chunk, stride=T_CHUNK)

    for b in range(n_blk):
        rows = pl.ds(b * row_chunk, row_chunk)
        for t in range(T_CHUNK):
            xr_ref[rows, t * LANES:(t + 1) * LANES] = x_ref[tokens(b, t), :].astype(BF16)
        xr = xr_ref[rows, :]
        z_ref[rows, :] = _dot(xr, wz_ref[...])
        for nt in range(N_TOEP_TILES):
            k_end = (nt + 1) * MXU_TILE
            tile = toep_ref[_toep_tile_row(nt):_toep_tile_row(nt) + k_end, :]
            yv_ref[rows, nt * MXU_TILE:k_end] = _dot(xr[:, 0:k_end], tile)

    are, aim = apow_ref[2:3, :], apow_ref[3:4, :]
    lo = lax.broadcasted_iota(jnp.int32, (SUBLANES, OCT_STATE), 0) < n_seq

    def step(k, carry):
        pre, pim = carry
        rows = pl.ds(pl.multiple_of(k * SUBLANES, SUBLANES), SUBLANES)
        zre = z_ref[rows, 0:OCT_STATE]
        zim = z_ref[rows, OCT_STATE:2 * OCT_STATE]
        w1re = are * pre - aim * pim + zre
        w1im = are * pim + aim * pre + zim
        r1re = pltpu.roll(w1re, n_seq, axis=0)
        r1im = pltpu.roll(w1im, n_seq, axis=0)
        w2re = are * r1re - aim * r1im + zre
        w2im = are * r1im + aim * r1re + zim
        z_ref[rows, 0:OCT_STATE] = jnp.where(lo, pre, r1re)
        z_ref[rows, OCT_STATE:2 * OCT_STATE] = jnp.where(lo, pim, r1im)
        nre = jnp.where(lo, pltpu.roll(w2re, n_seq, axis=0), w2re)
        nim = jnp.where(lo, pltpu.roll(w2im, n_seq, axis=0), w2im)
        return nre, nim

    zeros = jnp.zeros((SUBLANES, OCT_STATE), F32)
    fre, fim = lax.fori_loop(0, n_rows // SUBLANES, step, (zeros, zeros))
    sre_ref[...] = fre[0:n_seq, :]
    sim_ref[...] = fim[0:n_seq, :]

    d2 = jnp.concatenate([d_ref[...]] * TOK_PER_TILE, axis=1)
    glub2 = jnp.concatenate([glub_ref[...]] * TOK_PER_TILE, axis=1)

    for b in range(n_blk):
        rows = pl.ds(b * row_chunk, row_chunk)
        y = yv_ref[rows, :] + _dot(z_ref[rows, :].astype(BF16), m_ref[:, LANES:])
        for t0 in range(0, T_CHUNK, TOK_PER_TILE):
            toks = [tokens(b, t0 + i) for i in range(TOK_PER_TILE)]
            u = jnp.concatenate([x_ref[tok, :] for tok in toks], axis=1)
            yt = y[:, t0 * LANES:(t0 + TOK_PER_TILE) * LANES] + d2 * u
            out = _glu_out(yt, gluw_ref[...], glub2)
            for i, tok in enumerate(toks):
                y_ref[tok, :] = out[:, i * LANES:(i + 1) * LANES]


def _s5_prompt(u_flat, ops, d_skip, glu_b, *, n_seq):
    toep, wz, m, apow, gluw = ops
    assert 2 * n_seq == SUBLANES
    rows = u_flat.shape[0]
    n_rows = rows // T_CHUNK
    oct_spec = lambda shape: pl.BlockSpec((None,) + shape, lambda o: (o, 0, 0))
    kern = functools.partial(_s5_prompt_kernel, n_seq=n_seq, row_chunk=512)
    return pl.pallas_call(
        kern,
        grid=(N_OCT,),
        in_specs=[pl.BlockSpec((rows, LANES), lambda o: (0, o)),
                  oct_spec(toep.shape[1:]), oct_spec(wz.shape[1:]), oct_spec(m.shape[1:]),
                  oct_spec(apow.shape[1:]), oct_spec((1, LANES)), oct_spec(gluw.shape[1:]),
                  oct_spec((1, LANES))],
        out_specs=[pl.BlockSpec((rows, LANES), lambda o: (0, o)),
                   pl.BlockSpec((n_seq, OCT_STATE), lambda o: (0, o)),
                   pl.BlockSpec((n_seq, OCT_STATE), lambda o: (0, o))],
        out_shape=[jax.ShapeDtypeStruct((rows, D_SSM), F32),
                   jax.ShapeDtypeStruct((n_seq, N_OCT * OCT_STATE), F32),
                   jax.ShapeDtypeStruct((n_seq, N_OCT * OCT_STATE), F32)],
        scratch_shapes=[pltpu.VMEM((n_rows, CHUNK_K), BF16),
                        pltpu.VMEM((n_rows, 2 * OCT_STATE), F32),
                        pltpu.VMEM((n_rows, CHUNK_K), F32)],
        compiler_params=_cparams(1, independent=True),
        name="s5_prompt",
    )(u_flat, toep, wz, m, apow, d_skip.reshape(N_OCT, 1, LANES), gluw,
      glu_b.reshape(N_OCT, 1, LANES))


def _s5_sample_kernel(u_ref, s0re_ref, s0im_ref, wz_ref, m_ref, apow_ref, d_ref, gluw_ref,
                      glub_ref, y_ref, sre_ref, sim_ref):
    for o in range(N_OCT):
        ch = slice(o * LANES, (o + 1) * LANES)
        st = slice(o * OCT_STATE, (o + 1) * OCT_STATE)
        u = u_ref[:, ch]
        z = _dot(u.astype(BF16), wz_ref[o])
        are, aim = apow_ref[o, 0:1, :], apow_ref[o, 1:2, :]
        s0re, s0im = s0re_ref[:, st], s0im_ref[:, st]
        nre = are * s0re - aim * s0im + z[:, 0:OCT_STATE]
        nim = are * s0im + aim * s0re + z[:, OCT_STATE:2 * OCT_STATE]
        sre_ref[:, st] = nre
        sim_ref[:, st] = nim
        y = (_dot(nre.astype(BF16), m_ref[o, 0:OCT_STATE, :])
             + _dot(nim.astype(BF16), m_ref[o, OCT_STATE:2 * OCT_STATE, :])
             + d_ref[o] * u)
        y_ref[:, ch] = _glu_out(y, gluw_ref[o], glub_ref[o]).astype(y_ref.dtype)


def _pool_linear(z, gi, pw_ref, pb_ref, ps_ref):
    lanes = slice(gi * POOL_GROUP, (gi + 1) * POOL_GROUP)
    return (_dot(z.astype(BF16), pw_ref[gi]) + pb_ref[:, lanes]) * ps_ref[:, lanes]


def _pool_sample_kernel(hist_ref, v_ref, pw_ref, pb_ref, ps_ref, y_ref, new_ref):
    new_ref[0:POOL_HIST - 1] = hist_ref[1:POOL_HIST]
    new_ref[POOL_HIST - 1] = v_ref[...]
    for gi, w in enumerate(POOL_WINDOWS):
        lanes = slice(gi * POOL_GROUP, (gi + 1) * POOL_GROUP)
        v = v_ref[:, lanes]
        s = v
        for r in range(POOL_HIST - (w - 1), POOL_HIST):
            s = s + hist_ref[r, :, lanes]
        y = _pool_linear(s / float(w) - v, gi, pw_ref, pb_ref, ps_ref)
        y_ref[:, lanes] = y.astype(y_ref.dtype)


def _outproj_kernel(x_ref, ys_ref, yp_ref, gate_ref, w_ref, o_ref, *, blocks_per_seq):
    half = x_ref.shape[0] // 2
    gate = _mod_rows(gate_ref, blocks_per_seq)
    for r in range(2):
        rows = pl.ds(r * half, half)
        ys = ys_ref[pl.ds(r * half // T_CHUNK, half // T_CHUNK), :, :].reshape(half, D_SSM)
        mix = (_dot(ys.astype(BF16), w_ref[0:D_SSM, :])
               + _dot(yp_ref[rows, :].astype(BF16), w_ref[D_SSM:, :]))
        o_ref[rows, :] = x_ref[rows, :] + gate * mix


def _outproj(x, ys, yp, mod, w_out, *, tm, n_seq):
    rows = x.shape[0]
    blocks_per_seq = (rows // n_seq) // tm
    ys_spec = pl.BlockSpec((tm // T_CHUNK, None, T_CHUNK, D_SSM),
                           lambda i: (i % blocks_per_seq, i // blocks_per_seq, 0, 0))
    gate_spec = _mod_specs(MIX_SUBLAYER)[2]
    return pl.pallas_call(
        functools.partial(_outproj_kernel, blocks_per_seq=blocks_per_seq),
        grid=(rows // tm,),
        in_specs=[pl.BlockSpec((tm, D_MODEL), lambda i: (i, 0)),
                  ys_spec,
                  pl.BlockSpec((tm, D_POOL), lambda i: (i, 0)),
                  gate_spec,
                  pl.BlockSpec((D_MODEL, D_MODEL), lambda i: (0, 0),
                               pipeline_mode=pl.Buffered(1))],
        out_specs=pl.BlockSpec((tm, D_MODEL), lambda i: (i, 0)),
        out_shape=jax.ShapeDtypeStruct((rows, D_MODEL), F32),
        compiler_params=_cparams(1, independent=True),
        name="outproj",
    )(x, ys, yp, mod, w_out)


def _mixer_sample_kernel(x_ref, gate_ref, w_ref, proj_ref, s0re_ref, s0im_ref, wz_ref, m_ref,
                         apow_ref, d_ref, gluw_ref, glub_ref, hist_ref, pw_ref, pb_ref, ps_ref,
                         o_ref, wo_ref, sre_ref, sim_ref, new_hist_ref, ys_ref, yp_ref):
    @pl.when(pl.program_id(0) == 0)
    def _():
        _s5_sample_kernel(proj_ref, s0re_ref, s0im_ref, wz_ref, m_ref, apow_ref, d_ref, gluw_ref,
                          glub_ref, ys_ref, sre_ref, sim_ref)
        _pool_sample_kernel(hist_ref, proj_ref.at[:, pl.ds(D_SSM, D_POOL)], pw_ref, pb_ref, ps_ref,
                            yp_ref, new_hist_ref)

    wo_ref[...] = w_ref[...].astype(BF16)
    mix = _dot(ys_ref[...], wo_ref[0:D_SSM, :]) + _dot(yp_ref[...], wo_ref[D_SSM:, :])
    o_ref[...] = x_ref[...] + gate_ref[...] * mix


def _mixer_sample(x, proj, s0_re, s0_im, hist_t, ops, d_skip, glu_b, pool_w, pool_b, pool_scale,
                  mod, w_out):
    _, wz, m, apow, gluw = ops
    rows = x.shape[0]
    n_col = D_MODEL // SAMPLE_TN
    gate_col = (3 * MIX_SUBLAYER + 2) * n_col

    def once(shape, index=None):
        index = index or (0,) * len(shape)
        return pl.BlockSpec(shape, lambda j: index, pipeline_mode=pl.Buffered(1))

    return pl.pallas_call(
        _mixer_sample_kernel,
        grid=(n_col,),
        in_specs=[pl.BlockSpec((rows, SAMPLE_TN), lambda j: (0, j)),
                  pl.BlockSpec((rows, SAMPLE_TN), lambda j: (0, gate_col + j)),
                  pl.BlockSpec((D_MODEL, SAMPLE_TN), lambda j: (0, j)),
                  once(proj.shape), once(s0_re.shape), once(s0_im.shape),
                  once((N_OCT, LANES, 2 * OCT_STATE), (0, T_CHUNK - 1, 0)),
                  once((N_OCT, 2 * OCT_STATE, LANES)),
                  once(apow.shape), once((N_OCT, 1, LANES)), once((N_OCT, LANES, LANES)),
                  once((N_OCT, 1, LANES)), once(hist_t.shape), once(pool_w.shape),
                  once((1, D_POOL)), once((1, D_POOL))],
        out_specs=[pl.BlockSpec((rows, SAMPLE_TN), lambda j: (0, j)),
                   pl.BlockSpec((D_MODEL, SAMPLE_TN), lambda j: (0, j)),
                   once(s0_re.shape), once(s0_im.shape), once(hist_t.shape)],
        out_shape=[jax.ShapeDtypeStruct((rows, D_MODEL), F32),
                   jax.ShapeDtypeStruct((D_MODEL, D_MODEL), BF16),
                   jax.ShapeDtypeStruct(s0_re.shape, F32),
                   jax.ShapeDtypeStruct(s0_im.shape, F32),
                   jax.ShapeDtypeStruct(hist_t.shape, F32)],
        scratch_shapes=[pltpu.VMEM((rows, D_SSM), BF16), pltpu.VMEM((rows, D_POOL), BF16)],
        compiler_params=_cparams(1),
        name="mixer_sample",
    )(x, mod, w_out, proj, s0_re, s0_im, wz, m, apow, d_skip.reshape(N_OCT, 1, LANES), gluw,
      glu_b.reshape(N_OCT, 1, LANES), hist_t, pool_w, pool_b.reshape(1, D_POOL),
      pool_scale.reshape(1, D_POOL))


def kernel(x_prompt, x_sample, state_ssm_re, state_ssm_im, state_pool, c_prompt, c_sample, ada_w, ada_b, ffn1_norm, ffn1_w_gate, ffn1_w_up, ffn1_w_down, mix_norm, w_in, ssm_lambda_re, ssm_lambda_im, ssm_log_dt, ssm_b_re, ssm_b_im, ssm_c_re, ssm_c_im, ssm_d, ssm_glu_w, ssm_glu_b, pool_w, pool_b, pool_scale, w_out, ffn2_norm, ffn2_w_gate, ffn2_w_up, ffn2_w_down, final_norm):
    n_p, seq, _ = x_prompt.shape
    n_s = x_sample.shape[0]
    assert (n_p, n_s) == (N_PROMPT, N_SAMPLE) and n_p <= SUBLANES
    G, P = N_SSM_GROUPS, SSM_STATE

    c_all = jnp.concatenate([c_sample, c_prompt, jnp.zeros((SUBLANES - n_p, D_MODEL), F32)], axis=0)
    mod, ops = _adaln_s5_prep(c_all, ada_w[0], ada_b.reshape(1, N_MOD_COLS), ssm_lambda_re[0],
                              ssm_lambda_im[0], ssm_log_dt[0], ssm_b_re[0], ssm_b_im[0],
                              ssm_c_re[0], ssm_c_im[0], ssm_glu_w[0])
    pw = pool_w[0].astype(BF16)

    xs = x_sample.reshape(n_s, D_MODEL)
    xp = x_prompt.reshape(n_p * seq, D_MODEL)
    blocks_per_seq = seq // FFN_TM
    sample_rows = _RowGroup(n_s, n_s, per_row=True, first_block=0, resident=True)
    head_rows = _RowGroup(FFN_TM, FFN_ROW_CHUNK, per_row=False, first_block=0, resident=True)
    tail_rows = _RowGroup(FFN_TM, FFN_ROW_CHUNK, per_row=False, first_block=1, resident=False)

    def ffn(xs, xp, norm_g, sub, wg, wu, wd, last):
        ys, yp0, bg, bu, bd = _ffn([xs, xp], (sample_rows, head_rows), norm_g, mod, sub, wg, wu, wd,
                                   final_norm, n_blocks=1, tf=FFN_HEAD_TF, final_norm=last,
                                   blocks_per_seq=blocks_per_seq)
        (yp,) = _ffn([xp], (tail_rows,), norm_g, mod, sub, bg, bu, bd, final_norm,
                     n_blocks=n_p * blocks_per_seq - 1, tf=FFN_TF, final_norm=last,
                     blocks_per_seq=blocks_per_seq, out_rows=xp.shape[0], first_rows=yp0)
        return ys, yp

    xs, xp = ffn(xs, xp, ffn1_norm[0], FFN1_SUBLAYER, ffn1_w_gate[0], ffn1_w_up[0],
                 ffn1_w_down[0], False)

    proj_s, win = _proj_sample(xs, mix_norm[0], mod, w_in[0])
    xs, wout, sre_s, sim_s, pool_s_t = _mixer_sample(
        xs, proj_s, state_ssm_re[0].reshape(n_s, G * P), state_ssm_im[0].reshape(n_s, G * P),
        jnp.swapaxes(state_pool[0], 0, 1), ops, ssm_d[0], ssm_glu_b[0], pw, pool_b[0],
        pool_scale[0], mod, w_out[0])
    pool_s = jnp.swapaxes(pool_s_t, 0, 1)

    u4, yp_p, v_tail = _proj(xp, mix_norm[0], mod, win, pw, pool_b[0], pool_scale[0],
                             tm=PROJ_TM, n_seq=n_p)
    ys_p, sre_p, sim_p = _s5_prompt(u4.reshape(n_p * seq, D_SSM), ops, ssm_d[0], ssm_glu_b[0],
                                    n_seq=n_p)
    xp = _outproj(xp, ys_p.reshape(seq // T_CHUNK, n_p, T_CHUNK, D_SSM), yp_p, mod, wout,
                  tm=OUTPROJ_TM, n_seq=n_p)

    y_sample, y_prompt = ffn(xs, xp, ffn2_norm[0], FFN2_SUBLAYER, ffn2_w_gate[0], ffn2_w_up[0],
                             ffn2_w_down[0], True)

    pool_p = v_tail[:, POOL_TAIL - POOL_HIST:, :][None]
    return (y_prompt.reshape(n_p, seq, D_MODEL), y_sample.reshape(n_s, 1, D_MODEL),
            sre_p.reshape(1, n_p, G, P), sim_p.reshape(1, n_p, G, P), pool_p,
            sre_s.reshape(1, n_s, G, P), sim_s.reshape(1, n_s, G, P), pool_s[None])
```
